```python
import math
import jax
import jax.numpy as jnp
from jax import lax
import numpy as np

D_MODEL = 1024
BATCH = 4
SEQ = 8192
DEPTH = 4

GRID_W = 64
CTX_LEN = 256
HEAD_DIM = 64
EPS = 1e-6
NEG = -1e30
N_MOD = 6
SCALE = HEAD_DIM ** -0.5
ROPE_HALF = HEAD_DIM // 2
ROPE_AXIS_FREQS = ROPE_HALF // 2
ROPE_THETA = 10000.0
SWA_Q_HEADS = 8
SWA_KV_HEADS = 2
SWA_GROUP = SWA_Q_HEADS // SWA_KV_HEADS
WINDOW = 128
BLOCK = 128
SWA_WIDTH = SWA_Q_HEADS * HEAD_DIM
SWA_KV_WIDTH = SWA_KV_HEADS * HEAD_DIM
DIFF_HEADS = 4
DIFF_V_DIM = 2 * HEAD_DIM
DIFF_QK_WIDTH = DIFF_HEADS * 2 * HEAD_DIM
DIFF_WIDTH = DIFF_HEADS * DIFF_V_DIM
ATTN_IN = SWA_WIDTH + 2 * SWA_KV_WIDTH + 2 * DIFF_QK_WIDTH + DIFF_WIDTH
POOL_WINDOWS = (2, 4, 8, 16)
POOL_GROUPS = 4
POOL_WIDTH = D_MODEL // 2
POOL_GDIM = POOL_WIDTH // POOL_GROUPS
LRU_WIDTH = D_MODEL // 2
LRU_BLOCKS = 8
LRU_BDIM = LRU_WIDTH // LRU_BLOCKS
CONV_W = 4
CONV_LEFT = CONV_W // 2
LRU_C = 8.0
REC_IN = POOL_WIDTH + 2 * LRU_WIDTH
MIX_WIDTH = SWA_WIDTH + DIFF_WIDTH
PEER_HEADS = 8
PEER_NKEYS = 128
PEER_EXPERTS = PEER_NKEYS * PEER_NKEYS
PEER_QDIM = 256
PEER_HALF = PEER_QDIM // 2
PEER_TOPK = 16
PEER_CHUNK = 128
N_EVEN = (DEPTH + 1) // 2
N_ODD = DEPTH // 2

kernel_name = 'hybrid_prefix_diffusion_trunk'


def rmsnorm(x, w):
    xf = x.astype(jnp.float32)
    y = xf * lax.rsqrt(jnp.mean(xf * xf, axis=-1, keepdims=True) + EPS)
    return (y * w.astype(jnp.float32)).astype(x.dtype)


def modulate(h, shift, scale):
    return h * (1 + scale) + shift


def lambda_init(layer):
    return 0.8 - 0.6 * math.exp(-0.3 * layer)


def axial_rope(rows):
    row = jnp.repeat(jnp.arange(rows), GRID_W).astype(jnp.float32)
    col = jnp.tile(jnp.arange(GRID_W), rows).astype(jnp.float32)
    inv = ROPE_THETA ** (-jnp.arange(ROPE_AXIS_FREQS, dtype=jnp.float32) / ROPE_AXIS_FREQS)
    ang = jnp.concatenate([row[:, None] * inv, col[:, None] * inv], axis=-1)
    return jnp.cos(ang), jnp.sin(ang)


def apply_rope(x, cos, sin):
    shape = (1, x.shape[1]) + (1,) * (x.ndim - 3) + (ROPE_HALF,)
    c = cos.reshape(shape)
    s = sin.reshape(shape)
    xf = x.astype(jnp.float32)
    x1, x2 = xf[..., :ROPE_HALF], xf[..., ROPE_HALF:]
    return jnp.concatenate([x1 * c - x2 * s, x2 * c + x1 * s], axis=-1).astype(x.dtype)


def softmax_with_sink(logits, sink):
    sk = jnp.broadcast_to(sink.astype(jnp.float32).reshape(SWA_KV_HEADS, SWA_GROUP, 1, 1),
                          logits.shape[:-1] + (1,))
    p = jax.nn.softmax(jnp.concatenate([logits, sk], axis=-1), axis=-1)
    return p[..., :-1]


def swa_latent(q, k, v, kc, vc, sink):
    B, S = q.shape[0], q.shape[1]
    nb = S // BLOCK
    qb = q.reshape(B, nb, BLOCK, SWA_KV_HEADS, SWA_GROUP, HEAD_DIM)
    pad = ((0, 0), (BLOCK, BLOCK), (0, 0), (0, 0))
    kp = jnp.pad(k, pad).reshape(B, nb + 2, BLOCK, SWA_KV_HEADS, HEAD_DIM)
    vp = jnp.pad(v, pad).reshape(B, nb + 2, BLOCK, SWA_KV_HEADS, HEAD_DIM)
    kband = jnp.concatenate([kp[:, :-2], kp[:, 1:-1], kp[:, 2:]], axis=2)
    vband = jnp.concatenate([vp[:, :-2], vp[:, 1:-1], vp[:, 2:]], axis=2)
    s_loc = jnp.einsum('bnqhgd,bnkhd->bnhgqk', qb, kband).astype(jnp.float32) * SCALE
    s_ctx = jnp.einsum('bnqhgd,bchd->bnhgqc', qb, kc).astype(jnp.float32) * SCALE
    blk = jnp.arange(nb)[:, None]
    qpos = blk * BLOCK + jnp.arange(BLOCK)[None]
    kpos = blk * BLOCK - BLOCK + jnp.arange(3 * BLOCK)[None]
    kp3 = kpos[:, None, :]
    valid = (kp3 >= 0) & (kp3 < S) & (jnp.abs(qpos[:, :, None] - kp3) <= WINDOW)
    s_loc = jnp.where(valid[None, :, None, None], s_loc, NEG)
    p = softmax_with_sink(jnp.concatenate([s_loc, s_ctx], axis=-1), sink)
    kb = 3 * BLOCK
    o = (jnp.einsum('bnhgqk,bnkhd->bnqhgd', p[..., :kb].astype(v.dtype), vband)
         + jnp.einsum('bnhgqc,bchd->bnqhgd', p[..., kb:].astype(vc.dtype), vc))
    return o.reshape(B, S, SWA_WIDTH)


def swa_context(qc, kc, vc, sink):
    B, L = qc.shape[0], qc.shape[1]
    q5 = qc.reshape(B, L, SWA_KV_HEADS, SWA_GROUP, HEAD_DIM)
    s = jnp.einsum('bqhgd,bkhd->bhgqk', q5, kc).astype(jnp.float32) * SCALE
    p = softmax_with_sink(s, sink)
    o = jnp.einsum('bhgqk,bkhd->bqhgd', p.astype(vc.dtype), vc)
    return o.reshape(B, L, SWA_WIDTH)


def diff_core(q, k, v, lam):
    s = jnp.einsum('bqhmd,bkhmd->bhmqk', q, k).astype(jnp.float32) * SCALE
    p = jax.nn.softmax(s, axis=-1)
    w = p[:, :, 0] - lam * p[:, :, 1]
    return jnp.einsum('bhqk,bkhe->bqhe', w.astype(v.dtype), v)


def diff_post(o, subln_w, lam_init):
    B, T = o.shape[0], o.shape[1]
    return (rmsnorm(o, subln_w) * (1 - lam_init)).reshape(B, T, DIFF_WIDTH)


def diff_latent(q, k_all, v_all, lam):
    B, S = q.shape[0], q.shape[1]
    nb = S // BLOCK
    qb = q.reshape(B, nb, BLOCK, DIFF_HEADS, 2, HEAD_DIM).transpose(1, 0, 2, 3, 4, 5)
    o = lax.map(lambda qblk: diff_core(qblk, k_all, v_all, lam), qb)
    return o.transpose(1, 0, 2, 3, 4).reshape(B, S, DIFF_HEADS, DIFF_V_DIM)


def split_attn(p):
    B, T = p.shape[0], p.shape[1]
    o1 = SWA_WIDTH
    o2 = o1 + SWA_KV_WIDTH
    o3 = o2 + SWA_KV_WIDTH
    o4 = o3 + DIFF_QK_WIDTH
    o5 = o4 + DIFF_QK_WIDTH
    q, k, v, dq, dk, dv = jnp.split(p, [o1, o2, o3, o4, o5], axis=-1)
    return (q.reshape(B, T, SWA_Q_HEADS, HEAD_DIM),
            k.reshape(B, T, SWA_KV_HEADS, HEAD_DIM),
            v.reshape(B, T, SWA_KV_HEADS, HEAD_DIM),
            dq.reshape(B, T, DIFF_HEADS, 2, HEAD_DIM),
            dk.reshape(B, T, DIFF_HEADS, 2, HEAD_DIM),
            dv.reshape(B, T, DIFF_HEADS, DIFF_V_DIM))


def attn_mixer(hl, hc, w_in, w_out, sink, lam_vecs, subln_w, lam_init, cos, sin, ctx_out):
    ql, kl, vl, dql, dkl, dvl = split_attn(hl @ w_in)
    qc, kc, vc, dqc, dkc, dvc = split_attn(hc @ w_in)
    ql, kl = apply_rope(ql, cos, sin), apply_rope(kl, cos, sin)
    dql, dkl = apply_rope(dql, cos, sin), apply_rope(dkl, cos, sin)
    lv = lam_vecs.astype(jnp.float32)
    lam = jnp.exp(jnp.sum(lv[0] * lv[1])) - jnp.exp(jnp.sum(lv[2] * lv[3])) + lam_init
    a_l = swa_latent(ql, kl, vl, kc, vc, sink)
    k_all = jnp.concatenate([dkc, dkl], axis=1)
    v_all = jnp.concatenate([dvc, dvl], axis=1)
    b_l = diff_post(diff_latent(dql, k_all, v_all, lam), subln_w, lam_init)
    yl = jnp.concatenate([a_l, b_l], axis=-1) @ w_out
    yc = None
    if ctx_out:
        a_c = swa_context(qc, kc, vc, sink)
        b_c = diff_post(diff_core(dqc, dkc, dvc, lam), subln_w, lam_init)
        yc = jnp.concatenate([a_c, b_c], axis=-1) @ w_out
    return yl, yc


def pool_mix(x, pool_w, pool_scale):
    B, T = x.shape[0], x.shape[1]
    xf = x.astype(jnp.float32)
    cs = jnp.pad(jnp.cumsum(xf, axis=1), ((0, 0), (1, 0), (0, 0)))
    t = jnp.arange(T)
    outs = []
    for g, w in enumerate(POOL_WINDOWS):
        lo = jnp.clip(t - w // 2, 0, T)
        hi = jnp.clip(t - w // 2 + w, 0, T)
        csg = cs[..., g * POOL_GDIM:(g + 1) * POOL_GDIM]
        cnt = (hi - lo).astype(jnp.float32)[None, :, None]
        outs.append((csg[:, hi] - csg[:, lo]) / cnt - xf[..., g * POOL_GDIM:(g + 1) * POOL_GDIM])
    d = jnp.stack(outs, axis=2)
    y = jnp.einsum('btgd,gde->btge', d, pool_w.astype(jnp.float32)).reshape(B, T, POOL_WIDTH)
    return (y * pool_scale.astype(jnp.float32)).astype(x.dtype)


def centred_dwconv(x, w, b):
    T = x.shape[1]
    xp = jnp.pad(x, ((0, 0), (CONV_LEFT, CONV_W - 1 - CONV_LEFT), (0, 0)))
    y = b
    for k in range(CONV_W):
        y = y + w[k] * xp[:, k:k + T]
    return y


def rglru_coeffs(u, wa, ba, wx, bx, lam):
    B, T = u.shape[0], u.shape[1]
    uf = u.astype(jnp.float32)
    ub = uf.reshape(B, T, LRU_BLOCKS, LRU_BDIM)
    r = jax.nn.sigmoid(jnp.einsum('btnd,nde->btne', ub, wa.astype(jnp.float32)).reshape(B, T, LRU_WIDTH) + ba)
    i = jax.nn.sigmoid(jnp.einsum('btnd,nde->btne', ub, wx.astype(jnp.float32)).reshape(B, T, LRU_WIDTH) + bx)
    log_a = -LRU_C * r * jax.nn.softplus(-lam.astype(jnp.float32))
    a = jnp.exp(log_a)
    mult = jnp.sqrt(-jnp.expm1(2.0 * log_a))
    return a, mult * (i * uf)


def linear_scan(a, b, h0, reverse):
    if h0 is not None:
        idx = -1 if reverse else 0
        b = b.at[:, idx].add(a[:, idx] * h0)

    def comb(l, r):
        return l[0] * r[0], r[0] * l[1] + r[1]

    _, h = lax.associative_scan(comb, (a, b), reverse=reverse, axis=1)
    return h


def rec_mixer(hl, hc, w_in, w_out, pool_w, pool_scale, conv_w, conv_b, wa, ba, wx, bx, lam, ctx_out):
    splits = [POOL_WIDTH, POOL_WIDTH + LRU_WIDTH]
    xpl, xrl, gl = jnp.split(hl @ w_in, splits, axis=-1)
    xpc, xrc, gc = jnp.split(hc @ w_in, splits, axis=-1)
    ul = centred_dwconv(xrl, conv_w, conv_b)
    uc = centred_dwconv(xrc, conv_w, conv_b)
    h_lat, h_ctx = [], []
    for d, rev in enumerate((False, True)):
        ac, bc = rglru_coeffs(uc, wa[d], ba[d], wx[d], bx[d], lam[d])
        hcd = linear_scan(ac, bc, None, rev)
        h_end = hcd[:, 0] if rev else hcd[:, -1]
        al, bl = rglru_coeffs(ul, wa[d], ba[d], wx[d], bx[d], lam[d])
        h_lat.append(linear_scan(al, bl, h_end, rev))
        h_ctx.append(hcd)
    rec_l = (h_lat[0] + h_lat[1]).astype(hl.dtype) * jax.nn.gelu(gl, approximate=False)
    yl = jnp.concatenate([pool_mix(xpl, pool_w, pool_scale), rec_l], axis=-1) @ w_out
    yc = None
    if ctx_out:
        rec_c = (h_ctx[0] + h_ctx[1]).astype(hc.dtype) * jax.nn.gelu(gc, approximate=False)
        yc = jnp.concatenate([pool_mix(xpc, pool_w, pool_scale), rec_c], axis=-1) @ w_out
    return yl, yc


def peer_ffn(h, wq, subkeys, u, v):
    B, T, D = h.shape
    tok = h.reshape(-1, PEER_CHUNK, D)

    def chunk(xc):
        q = (xc @ wq).astype(jnp.float32).reshape(PEER_CHUNK, PEER_HEADS, 2, PEER_HALF)
        s = jnp.einsum('thpd,hpkd->thpk', q, subkeys.astype(jnp.float32))
        s1, i1 = lax.top_k(s[:, :, 0], PEER_TOPK)
        s2, i2 = lax.top_k(s[:, :, 1], PEER_TOPK)
        cand = (s1[..., :, None] + s2[..., None, :]).reshape(PEER_CHUNK, PEER_HEADS, PEER_TOPK * PEER_TOPK)
        cidx = (i1[..., :, None] * PEER_NKEYS + i2[..., None, :]).reshape(PEER_CHUNK, PEER_HEADS, PEER_TOPK * PEER_TOPK)
        top_s, pos = lax.top_k(cand, PEER_TOPK)
        eidx = jnp.take_along_axis(cidx, pos, axis=-1)
        g = jax.nn.softmax(top_s, axis=-1)
        act = jax.nn.gelu(jnp.einsum('td,thkd->thk', xc, u[eidx]).astype(jnp.float32), approximate=False)
        coef = (g * act).astype(xc.dtype)
        return jnp.einsum('thk,thkd->td', coef, v[eidx])

    return lax.map(chunk, tok).reshape(B, T, D)


def setup_inputs(seed: int = 0) -> dict:
    key = jax.random.key(seed)
    ks = jax.random.split(key, 32)
    f32 = jnp.float32

    def nrm(k, shape, std):
        return jax.random.normal(k, shape, f32) * std

    lru_u = jax.random.uniform(ks[22], (N_ODD, 2, LRU_WIDTH), f32, 0.9, 0.999)
    lru_p = lru_u ** (1.0 / LRU_C)
    return {
        'x': nrm(ks[0], (BATCH, SEQ, D_MODEL), 1.0),
        'c': nrm(ks[1], (BATCH, D_MODEL), 1.0),
        'ctx': nrm(ks[2], (BATCH, CTX_LEN, D_MODEL), 1.0),
        'c_ctx': nrm(ks[3], (D_MODEL,), 1.0),
        'w_mod': nrm(ks[4], (DEPTH, D_MODEL, N_MOD * D_MODEL), 0.5 * D_MODEL ** -0.5),
        'b_mod': nrm(ks[5], (DEPTH, N_MOD * D_MODEL), 0.02),
        'norm_mix': 1.0 + nrm(ks[6], (DEPTH, D_MODEL), 0.05),
        'norm_ffn': 1.0 + nrm(ks[7], (DEPTH, D_MODEL), 0.05),
        'w_out': nrm(ks[8], (DEPTH, MIX_WIDTH, D_MODEL), MIX_WIDTH ** -0.5),
        'attn_w_in': nrm(ks[9], (N_EVEN, D_MODEL, ATTN_IN), D_MODEL ** -0.5),
        'swa_sink': nrm(ks[10], (N_EVEN, SWA_Q_HEADS), 0.5),
        'diff_lambda': nrm(ks[11], (N_EVEN, 4, HEAD_DIM), 0.1),
        'diff_subln': 1.0 + nrm(ks[12], (N_EVEN, DIFF_V_DIM), 0.05),
        'rec_w_in': nrm(ks[13], (N_ODD, D_MODEL, REC_IN), D_MODEL ** -0.5),
        'pool_w': nrm(ks[14], (N_ODD, POOL_GROUPS, POOL_GDIM, POOL_GDIM), POOL_GDIM ** -0.5),
        'pool_scale': 1.0 + nrm(ks[15], (N_ODD, POOL_WIDTH), 0.05),
        'lru_conv_w': nrm(ks[16], (N_ODD, CONV_W, LRU_WIDTH), CONV_W ** -0.5),
        'lru_conv_b': nrm(ks[17], (N_ODD, LRU_WIDTH), 0.02),
        'lru_wa': nrm(ks[18], (N_ODD, 2, LRU_BLOCKS, LRU_BDIM, LRU_BDIM), LRU_BDIM ** -0.5),
        'lru_ba': nrm(ks[19], (N_ODD, 2, LRU_WIDTH), 0.02),
        'lru_wx': nrm(ks[20], (N_ODD, 2, LRU_BLOCKS, LRU_BDIM, LRU_BDIM), LRU_BDIM ** -0.5),
        'lru_bx': nrm(ks[21], (N_ODD, 2, LRU_WIDTH), 0.02),
        'lru_lambda': jnp.log(lru_p) - jnp.log1p(-lru_p),
        'peer_wq': nrm(ks[23], (DEPTH, D_MODEL, PEER_HEADS * PEER_QDIM), D_MODEL ** -0.5),
        'peer_subkeys': nrm(ks[24], (DEPTH, PEER_HEADS, 2, PEER_NKEYS, PEER_HALF), PEER_HALF ** -0.5),
        'peer_u': nrm(ks[25], (DEPTH, PEER_EXPERTS, D_MODEL), D_MODEL ** -0.5),
        'peer_v': nrm(ks[26], (DEPTH, PEER_EXPERTS, D_MODEL), PEER_HEADS ** -0.5),
        'final_norm': 1.0 + nrm(ks[27], (D_MODEL,), 0.05),
    }


def reference(x, c, ctx, c_ctx, w_mod, b_mod, norm_mix, norm_ffn, w_out, attn_w_in, swa_sink,
              diff_lambda, diff_subln, rec_w_in, pool_w, pool_scale, lru_conv_w, lru_conv_b,
              lru_wa, lru_ba, lru_wx, lru_bx, lru_lambda, peer_wq, peer_subkeys, peer_u, peer_v,
              final_norm):
    B, S, _ = x.shape
    rows = S // GRID_W
    cos, sin = axial_rope(rows)
    sc = jax.nn.silu(c)
    scc = jax.nn.silu(c_ctx)
    xl, xc = x, ctx
    for l in range(DEPTH):
        j = l // 2
        ctx_out = l < DEPTH - 1
        ml = (sc @ w_mod[l] + b_mod[l]).reshape(B, 1, N_MOD, D_MODEL)
        mc = (scc @ w_mod[l] + b_mod[l]).reshape(1, 1, N_MOD, D_MODEL)
        hl = modulate(rmsnorm(xl, norm_mix[l]), ml[:, :, 0], ml[:, :, 1])
        hc = modulate(rmsnorm(xc, norm_mix[l]), mc[:, :, 0], mc[:, :, 1])
        if l % 2 == 0:
            yl, yc = attn_mixer(hl, hc, attn_w_in[j], w_out[l], swa_sink[j], diff_lambda[j],
                                diff_subln[j], lambda_init(l), cos, sin, ctx_out)
        else:
            yl, yc = rec_mixer(hl, hc, rec_w_in[j], w_out[l], pool_w[j], pool_scale[j],
                               lru_conv_w[j], lru_conv_b[j], lru_wa[j], lru_ba[j], lru_wx[j],
                               lru_bx[j], lru_lambda[j], ctx_out)
        xl = xl + ml[:, :, 2] * yl
        hl = modulate(rmsnorm(xl, norm_ffn[l]), ml[:, :, 3], ml[:, :, 4])
        xl = xl + ml[:, :, 5] * peer_ffn(hl, peer_wq[l], peer_subkeys[l], peer_u[l], peer_v[l])
        if ctx_out:
            xc = xc + mc[:, :, 2] * yc
            hc = modulate(rmsnorm(xc, norm_ffn[l]), mc[:, :, 3], mc[:, :, 4])
            xc = xc + mc[:, :, 5] * peer_ffn(hc, peer_wq[l], peer_subkeys[l], peer_u[l], peer_v[l])
    return rmsnorm(xl, final_norm)
```

```python
import functools
import math

import jax
import jax.numpy as jnp
from jax import lax
from jax.experimental import pallas as pl
from jax.experimental.pallas import tpu as pltpu

F32 = jnp.float32
BF16 = jnp.bfloat16
HIGHEST = lax.Precision.HIGHEST

D_MODEL = 1024
HEAD_DIM = 64
EPS = 1e-6
NEG = -1e30
N_MOD = 6
SCALE = HEAD_DIM ** -0.5
ROPE_HALF = HEAD_DIM // 2
ROPE_AXIS_FREQS = ROPE_HALF // 2
ROPE_THETA = 10000.0
GRID_W = 64
SWA_Q_HEADS = 8
SWA_KV_HEADS = 2
SWA_GROUP = SWA_Q_HEADS // SWA_KV_HEADS
WINDOW = 128
BLOCK = 128
SWA_WIDTH = SWA_Q_HEADS * HEAD_DIM
SWA_KV_WIDTH = SWA_KV_HEADS * HEAD_DIM
DIFF_HEADS = 4
DIFF_V_DIM = 2 * HEAD_DIM
DIFF_QK_WIDTH = DIFF_HEADS * 2 * HEAD_DIM
DIFF_WIDTH = DIFF_HEADS * DIFF_V_DIM
POOL_WINDOWS = (2, 4, 8, 16)
POOL_GROUPS = 4
POOL_WIDTH = D_MODEL // 2
POOL_GDIM = POOL_WIDTH // POOL_GROUPS
LRU_WIDTH = D_MODEL // 2
LRU_BLOCKS = 8
CONV_W = 4
CONV_LEFT = CONV_W // 2
LRU_C = 8.0
PEER_HEADS = 8
PEER_NKEYS = 128
PEER_EXPERTS = PEER_NKEYS * PEER_NKEYS
PEER_QDIM = 256
PEER_HALF = PEER_QDIM // 2
PEER_TOPK = 16

LANES = 128
SUBLANES = 8
VMEM_LIMIT = 56 * 1024 * 1024

TM = 256
SEQ_T = 256
HALO = 8
PEER_TT = 512
PEER_NA = 8
PEER_RB = 64


def _cparams(sem):
    return pltpu.CompilerParams(dimension_semantics=sem, vmem_limit_bytes=VMEM_LIMIT)


def _nt_dot(a, b):
    return lax.dot_general(a, b, (((1,), (1,)), ((), ())), preferred_element_type=F32)


def _rmsnorm_mod(x, w, shift, scale):
    y = x * lax.rsqrt(jnp.mean(x * x, axis=-1, keepdims=True) + EPS) * w
    return y * (1.0 + scale) + shift


def _gelu(x):
    return 0.5 * x * (1.0 + lax.erf(x * (2.0 ** -0.5)))


def _mod_kernel(cc_ref, w_ref, b_ref, o_ref):
    cc = cc_ref[...]
    sc = cc * jax.nn.sigmoid(cc)
    o_ref[...] = jnp.dot(sc, w_ref[...], precision=HIGHEST, preferred_element_type=F32) + b_ref[...]


def _modulation(cc, w_mod, b_mod):
    depth = w_mod.shape[0]
    rows = cc.shape[0]
    out = pl.pallas_call(
        _mod_kernel,
        grid=(depth, N_MOD),
        in_specs=[
            pl.BlockSpec((rows, D_MODEL), lambda l, j: (0, 0)),
            pl.BlockSpec((None, D_MODEL, D_MODEL), lambda l, j: (l, 0, j)),
            pl.BlockSpec((None, 1, D_MODEL), lambda l, j: (l, 0, j)),
        ],
        out_specs=pl.BlockSpec((None, rows, D_MODEL), lambda l, j: (l, 0, j)),
        out_shape=jax.ShapeDtypeStruct((depth, rows, N_MOD * D_MODEL), F32),
        compiler_params=_cparams(("parallel", "parallel")),
        name="modulation",
    )(cc, w_mod, b_mod.reshape(depth, 1, N_MOD * D_MODEL))
    return out.reshape(depth, rows, N_MOD, D_MODEL)


def _rope128(x, cos, sin_signed, first_half):
    partner = jnp.where(first_half, pltpu.roll(x, LANES - ROPE_HALF, 1), pltpu.roll(x, ROPE_HALF, 1))
    return x * cos + partner * sin_signed


def _pre_attn_kernel(x_ref, mod_ref, nw_ref, w_ref, cos_ref, sin_ref,
                     q_ref, k_ref, v_ref, dq_ref, dk_ref, dv_ref):
    h = _rmsnorm_mod(x_ref[...], nw_ref[...], mod_ref[0:1, :], mod_ref[1:2, :])
    p = jnp.dot(h.astype(BF16), w_ref[...], preferred_element_type=F32)
    cos = cos_ref[...]
    sin = sin_ref[...]
    lane = lax.broadcasted_iota(jnp.int32, (1, LANES), 1)
    first_half = (lane % HEAD_DIM) < ROPE_HALF

    def roped(lo, width, scale):
        outs = []
        for c in range(width // LANES):
            xc = p[:, lo + c * LANES: lo + (c + 1) * LANES]
            outs.append((_rope128(xc, cos, sin, first_half) * scale).astype(BF16))
        return outs

    o1 = SWA_WIDTH
    o2 = o1 + SWA_KV_WIDTH
    o3 = o2 + SWA_KV_WIDTH
    o4 = o3 + DIFF_QK_WIDTH
    o5 = o4 + DIFF_QK_WIDTH
    for c, val in enumerate(roped(0, SWA_WIDTH, SCALE)):
        q_ref[:, c * LANES:(c + 1) * LANES] = val
    for c, val in enumerate(roped(o1, SWA_KV_WIDTH, 1.0)):
        k_ref[:, c * LANES:(c + 1) * LANES] = val
    v_ref[...] = p[:, o2:o3].astype(BF16)
    for c, val in enumerate(roped(o3, DIFF_QK_WIDTH, SCALE)):
        dq_ref[:, c * LANES:(c + 1) * LANES] = val
    for c, val in enumerate(roped(o4, DIFF_QK_WIDTH, 1.0)):
        dk_ref[:, c * LANES:(c + 1) * LANES] = val
    dv_ref[...] = p[:, o5:].astype(BF16)


def _pre_attn(x, mods_l, norm_w, w_in, cos_t, sin_t, dims):
    B, S, L = dims
    T = x.shape[0]
    n_lat = B * S // TM
    s_tiles = S // TM

    def mod_idx(g):
        return (jnp.where(g < n_lat, g // s_tiles, B), 0, 0)

    def rope_idx(g):
        return (jnp.where(g < n_lat, g % s_tiles, s_tiles), 0)

    widths = (SWA_WIDTH, SWA_KV_WIDTH, SWA_KV_WIDTH, DIFF_QK_WIDTH, DIFF_QK_WIDTH, DIFF_WIDTH)
    return pl.pallas_call(
        _pre_attn_kernel,
        grid=(T // TM,),
        in_specs=[
            pl.BlockSpec((TM, D_MODEL), lambda g: (g, 0)),
            pl.BlockSpec((None, N_MOD, D_MODEL), mod_idx),
            pl.BlockSpec((1, D_MODEL), lambda g: (0, 0)),
            pl.BlockSpec(w_in.shape, lambda g: (0, 0)),
            pl.BlockSpec((TM, LANES), rope_idx),
            pl.BlockSpec((TM, LANES), rope_idx),
        ],
        out_specs=[pl.BlockSpec((TM, w), lambda g: (g, 0)) for w in widths],
        out_shape=[jax.ShapeDtypeStruct((T, w), BF16) for w in widths],
        compiler_params=_cparams(("parallel",)),
        name="pre_attn",
    )(x, mods_l, norm_w, w_in, cos_t, sin_t)


def _swa_kernel(q_ref, kl_ref, vl_ref, kc_ref, vc_ref, sink_ref, o_ref, *, n_lat_blocks, seq):
    j = pl.program_id(1)
    is_ctx = j >= n_lat_blocks
    jl = jnp.minimum(j, n_lat_blocks - 1)
    band = 3 * BLOCK
    bs = pl.multiple_of(jnp.clip((jl - 1) * BLOCK, 0, seq - band), BLOCK)
    kb = kl_ref[pl.ds(bs, band), :]
    vb = vl_ref[pl.ds(bs, band), :]
    kc = kc_ref[...]
    vc = vc_ref[...]
    qpos = jl * BLOCK + lax.broadcasted_iota(jnp.int32, (BLOCK, 1), 0)
    kpos = bs + lax.broadcasted_iota(jnp.int32, (1, band), 1)
    valid = jnp.logical_and(jnp.abs(qpos - kpos) <= WINDOW, jnp.logical_not(is_ctx))
    q = q_ref[...]
    for hk in range(SWA_KV_HEADS):
        ks = slice(hk * HEAD_DIM, (hk + 1) * HEAD_DIM)
        for g in range(SWA_GROUP):
            h = hk * SWA_GROUP + g
            qh = q[:, h * HEAD_DIM:(h + 1) * HEAD_DIM]
            sl = jnp.where(valid, _nt_dot(qh, kb[:, ks]), NEG)
            sc = _nt_dot(qh, kc[:, ks])
            sink = sink_ref[h]
            m = jnp.maximum(jnp.maximum(jnp.max(sl, axis=-1, keepdims=True),
                                        jnp.max(sc, axis=-1, keepdims=True)), sink)
            pl_ = jnp.exp(sl - m)
            pc = jnp.exp(sc - m)
            den = (jnp.sum(pl_, axis=-1, keepdims=True) + jnp.sum(pc, axis=-1, keepdims=True)
                   + jnp.exp(sink - m))
            o = (jnp.dot(pl_.astype(BF16), vb[:, ks], preferred_element_type=F32)
                 + jnp.dot(pc.astype(BF16), vc[:, ks], preferred_element_type=F32))
            o_ref[:, h * HEAD_DIM:(h + 1) * HEAD_DIM] = (o / den).astype(BF16)


def _swa(q, k, v, sink, dims):
    B, S, L = dims
    T = q.shape[0]
    n_lat_blocks = S // BLOCK
    n_ctx_blocks = L // BLOCK
    lat_rows = B * S

    def q_idx(b, j):
        return (jnp.where(j < n_lat_blocks, b * n_lat_blocks + j,
                          lat_rows // BLOCK + b * n_ctx_blocks + (j - n_lat_blocks)), 0)

    lat_spec = pl.BlockSpec((S, SWA_KV_WIDTH), lambda b, j: (b, 0))
    ctx_spec = pl.BlockSpec((L, SWA_KV_WIDTH), lambda b, j: (lat_rows // L + b, 0))
    return pl.pallas_call(
        functools.partial(_swa_kernel, n_lat_blocks=n_lat_blocks, seq=S),
        grid=(B, n_lat_blocks + n_ctx_blocks),
        in_specs=[
            pl.BlockSpec((BLOCK, SWA_WIDTH), q_idx),
            lat_spec, lat_spec, ctx_spec, ctx_spec,
            pl.BlockSpec(memory_space=pltpu.SMEM),
        ],
        out_specs=pl.BlockSpec((BLOCK, SWA_WIDTH), q_idx),
        out_shape=jax.ShapeDtypeStruct((T, SWA_WIDTH), BF16),
        compiler_params=_cparams(("parallel", "arbitrary")),
        name="swa",
    )(q, k, v, k, v, sink)


def _diff_kernel(q_ref, kl_ref, vl_ref, kc_ref, vc_ref, lam_ref, subln_ref, o_ref, m_scr, l_scr, acc_scr, *,
                 n_lat_tiles, n_lat_chunks, ctx_len, lam_init):
    j = pl.program_id(1)
    lv = lam_ref[...]
    lam = (jnp.exp(jnp.sum(lv[0:1] * lv[1:2], axis=-1, keepdims=True))
           - jnp.exp(jnp.sum(lv[2:3] * lv[3:4], axis=-1, keepdims=True)) + lam_init)

    def step(m, qm, kblk, vblk):
        s = _nt_dot(qm, kblk)
        m_old = m_scr[m]
        m_new = jnp.maximum(m_old, jnp.max(s, axis=-1, keepdims=True))
        alpha = jnp.exp(m_old - m_new)
        p = jnp.exp(s - m_new)
        m_scr[m] = m_new
        l_scr[m] = alpha * l_scr[m] + jnp.sum(p, axis=-1, keepdims=True)
        acc_scr[m] = alpha * acc_scr[m] + jnp.dot(p.astype(BF16), vblk, preferred_element_type=F32)

    for h in range(DIFF_HEADS):
        vs = slice(h * DIFF_V_DIM, (h + 1) * DIFF_V_DIM)
        qh = q_ref[:, vs]
        qs = [qh[:, m * HEAD_DIM:(m + 1) * HEAD_DIM] for m in range(2)]
        m_scr[...] = jnp.full(m_scr.shape, -jnp.inf, F32)
        l_scr[...] = jnp.zeros(l_scr.shape, F32)
        acc_scr[...] = jnp.zeros(acc_scr.shape, F32)

        def update(kblk, vblk):
            for m in range(2):
                step(m, qs[m], kblk[:, m * HEAD_DIM:(m + 1) * HEAD_DIM], vblk)

        for c0 in range(0, ctx_len, SEQ_T):
            update(kc_ref[c0:c0 + SEQ_T, vs], vc_ref[c0:c0 + SEQ_T, vs])

        @pl.when(j < n_lat_tiles)
        def _():
            def body(c, _):
                r0 = pl.multiple_of(c * SEQ_T, SEQ_T)
                update(kl_ref[pl.ds(r0, SEQ_T), vs], vl_ref[pl.ds(r0, SEQ_T), vs])
                return 0

            lax.fori_loop(0, n_lat_chunks, body, 0)

        o = acc_scr[0] / l_scr[0] - lam * (acc_scr[1] / l_scr[1])
        y = o * lax.rsqrt(jnp.mean(o * o, axis=-1, keepdims=True) + EPS) * subln_ref[...]
        o_ref[:, vs] = (y * (1.0 - lam_init)).astype(BF16)


def _diff_attn(dq, dk, dv, lam_vecs, subln, lam_init, dims):
    B, S, L = dims
    T = dq.shape[0]
    n_lat_tiles = S // SEQ_T
    n_ctx_tiles = L // SEQ_T
    lat_rows = B * S

    def q_idx(b, j):
        return (jnp.where(j < n_lat_tiles, b * n_lat_tiles + j,
                          lat_rows // SEQ_T + b * n_ctx_tiles + (j - n_lat_tiles)), 0)

    lat_spec = pl.BlockSpec((S, DIFF_QK_WIDTH), lambda b, j: (b, 0))
    ctx_spec = pl.BlockSpec((L, DIFF_QK_WIDTH), lambda b, j: (lat_rows // L + b, 0))
    return pl.pallas_call(
        functools.partial(_diff_kernel, n_lat_tiles=n_lat_tiles, n_lat_chunks=S // SEQ_T,
                          ctx_len=L, lam_init=lam_init),
        grid=(B, n_lat_tiles + n_ctx_tiles),
        in_specs=[
            pl.BlockSpec((SEQ_T, DIFF_QK_WIDTH), q_idx),
            lat_spec, lat_spec, ctx_spec, ctx_spec,
            pl.BlockSpec(lam_vecs.shape, lambda b, j: (0, 0)),
            pl.BlockSpec((1, DIFF_V_DIM), lambda b, j: (0, 0)),
        ],
        out_specs=pl.BlockSpec((SEQ_T, DIFF_WIDTH), q_idx),
        out_shape=jax.ShapeDtypeStruct((T, DIFF_WIDTH), BF16),
        scratch_shapes=[
            pltpu.VMEM((2, SEQ_T, 1), F32),
            pltpu.VMEM((2, SEQ_T, 1), F32),
            pltpu.VMEM((2, SEQ_T, DIFF_V_DIM), F32),
        ],
        compiler_params=_cparams(("parallel", "arbitrary")),
        name="diff_attn",
    )(dq, dk, dv, dk, dv, lam_vecs, subln)


def _pre_rec_kernel(x_ref, mod_ref, nw_ref, w_ref, xp_ref, xr_ref, g_ref):
    h = _rmsnorm_mod(x_ref[...], nw_ref[...], mod_ref[0:1, :], mod_ref[1:2, :])
    p = jnp.dot(h.astype(BF16), w_ref[...], preferred_element_type=F32)
    xp_ref[...] = p[:, :POOL_WIDTH]
    xr_ref[...] = p[:, POOL_WIDTH:POOL_WIDTH + LRU_WIDTH]
    g_ref[...] = p[:, POOL_WIDTH + LRU_WIDTH:]


def _pre_rec(x, mods_l, norm_w, w_in, dims):
    B, S, L = dims
    T = x.shape[0]
    n_lat = B * S // TM
    s_tiles = S // TM

    def mod_idx(g):
        return (jnp.where(g < n_lat, g // s_tiles, B), 0, 0)

    widths = (POOL_WIDTH, LRU_WIDTH, LRU_WIDTH)
    return pl.pallas_call(
        _pre_rec_kernel,
        grid=(T // TM,),
        in_specs=[
            pl.BlockSpec((TM, D_MODEL), lambda g: (g, 0)),
            pl.BlockSpec((None, N_MOD, D_MODEL), mod_idx),
            pl.BlockSpec((1, D_MODEL), lambda g: (0, 0)),
            pl.BlockSpec(w_in.shape, lambda g: (0, 0)),
        ],
        out_specs=[pl.BlockSpec((TM, w), lambda g: (g, 0)) for w in widths],
        out_shape=[jax.ShapeDtypeStruct((T, w), F32) for w in widths],
        compiler_params=_cparams(("parallel",)),
        name="pre_rec",
    )(x, mods_l, norm_w, w_in)


def _rec_mid_kernel(xp_p, xp_c, xp_n, xr_p, xr_c, xr_n, cw_ref, cb_ref, wa_ref, ba_ref, wx_ref, bx_ref,
                    lam_ref, pw_ref, ps_ref, pool_ref, a_ref, b_ref, *, n_lat_chunks, lat_chunks_per_seq,
                    ctx_chunks_per_seq, lat_len, ctx_len):
    g = pl.program_id(0)
    is_lat = g < n_lat_chunks
    cps = jnp.where(is_lat, lat_chunks_per_seq, ctx_chunks_per_seq)
    within = jnp.where(is_lat, g % lat_chunks_per_seq, (g - n_lat_chunks) % ctx_chunks_per_seq)
    has_prev = within > 0
    has_next = within < cps - 1
    seg_len = jnp.where(is_lat, lat_len, ctx_len)
    ext_rows = SEQ_T + 2 * HALO

    def extended(prev_ref, cur_ref, next_ref):
        prev = jnp.where(has_prev, prev_ref[SEQ_T - HALO:, :], 0.0)
        nxt = jnp.where(has_next, next_ref[:HALO, :], 0.0)
        return jnp.concatenate([prev, cur_ref[...], nxt], axis=0)

    def shifted(ext, off):
        return pltpu.roll(ext, (-off) % ext_rows, 0)[HALO:HALO + SEQ_T, :]

    xp_ext = extended(xp_p, xp_c, xp_n)
    t = within * SEQ_T + lax.broadcasted_iota(jnp.int32, (SEQ_T, 1), 0)
    for gi, w in enumerate(POOL_WINDOWS):
        cols = slice(gi * POOL_GDIM, (gi + 1) * POOL_GDIM)
        eg = xp_ext[:, cols]
        tot = shifted(eg, -(w // 2))
        for off in range(-(w // 2) + 1, w - w // 2):
            tot = tot + shifted(eg, off)
        lo = jnp.clip(t - w // 2, 0, seg_len)
        hi = jnp.clip(t - w // 2 + w, 0, seg_len)
        cnt = (hi - lo).astype(F32)
        d = tot / cnt - xp_c[:, cols]
        y = jnp.dot(d.astype(BF16), pw_ref[gi], preferred_element_type=F32)
        pool_ref[:, cols] = (y * ps_ref[:, cols]).astype(BF16)

    xr_ext = extended(xr_p, xr_c, xr_n)
    u = cb_ref[...] + cw_ref[0:1, :] * shifted(xr_ext, -CONV_LEFT)
    for k in range(1, CONV_W):
        u = u + cw_ref[k:k + 1, :] * shifted(xr_ext, k - CONV_LEFT)
    ub = u.astype(BF16)
    for d in range(2):
        r = jax.nn.sigmoid(jnp.dot(ub, wa_ref[d], preferred_element_type=F32) + ba_ref[d:d + 1, :])
        i = jax.nn.sigmoid(jnp.dot(ub, wx_ref[d], preferred_element_type=F32) + bx_ref[d:d + 1, :])
        nl = -lam_ref[d:d + 1, :]
        softplus = jnp.maximum(nl, 0.0) + jnp.log1p(jnp.exp(-jnp.abs(nl)))
        log_a = -LRU_C * r * softplus
        a_ref[d] = jnp.exp(log_a)
        th = jnp.tanh(log_a)
        b_ref[d] = jnp.sqrt(-2.0 * th / (1.0 - th)) * (i * u)


def _rec_mid(xp, xr, conv_w, conv_b, wa_bd, ba, wx_bd, bx, lam, pool_w, pool_scale, dims):
    B, S, L = dims
    T = xp.shape[0]
    n_chunks = T // SEQ_T

    def cur(g):
        return (g, 0)

    def prv(g):
        return (jnp.maximum(g - 1, 0), 0)

    def nxt(g):
        return (jnp.minimum(g + 1, n_chunks - 1), 0)

    tile = lambda idx: pl.BlockSpec((SEQ_T, LRU_WIDTH), idx)
    full = lambda arr: pl.BlockSpec(arr.shape, lambda g: (0,) * arr.ndim)
    kern = functools.partial(_rec_mid_kernel, n_lat_chunks=B * S // SEQ_T, lat_chunks_per_seq=S // SEQ_T,
                             ctx_chunks_per_seq=L // SEQ_T, lat_len=S, ctx_len=L)
    return pl.pallas_call(
        kern,
        grid=(n_chunks,),
        in_specs=[tile(prv), tile(cur), tile(nxt), tile(prv), tile(cur), tile(nxt),
                  full(conv_w), full(conv_b), full(wa_bd), full(ba), full(wx_bd), full(bx), full(lam),
                  full(pool_w), full(pool_scale)],
        out_specs=[pl.BlockSpec((SEQ_T, POOL_WIDTH), cur),
                   pl.BlockSpec((2, SEQ_T, LRU_WIDTH), lambda g: (0, g, 0)),
                   pl.BlockSpec((2, SEQ_T, LRU_WIDTH), lambda g: (0, g, 0))],
        out_shape=[jax.ShapeDtypeStruct((T, POOL_WIDTH), BF16),
                   jax.ShapeDtypeStruct((2, T, LRU_WIDTH), F32),
                   jax.ShapeDtypeStruct((2, T, LRU_WIDTH), F32)],
        compiler_params=_cparams(("parallel",)),
        name="rec_mid",
    )(xp, xp, xp, xr, xr, xr, conv_w, conv_b, wa_bd, ba, wx_bd, bx, lam, pool_w, pool_scale)


def _scan_kernel(a_ref, b_ref, h_ref, carry_ref):
    d = pl.program_id(1)
    s = pl.program_id(2)

    @pl.when(s == 0)
    def _():
        carry_ref[...] = jnp.zeros_like(carry_ref)

    def run(reverse):
        def body(i, h):
            t = (SEQ_T - 1 - i) if reverse else i
            h = a_ref[pl.ds(t, 1), :] * h + b_ref[pl.ds(t, 1), :]
            h_ref[pl.ds(t, 1), :] = h
            return h

        carry_ref[...] = lax.fori_loop(0, SEQ_T, body, carry_ref[...], unroll=8)

    @pl.when(d == 0)
    def _():
        run(False)

    @pl.when(d == 1)
    def _():
        run(True)


def _scan(a, b, dims):
    B, S, L = dims
    T = a.shape[1]
    lat = S // SEQ_T
    ctx = L // SEQ_T
    lat_base = 0
    ctx_base = B * S // SEQ_T

    def idx(bi, d, s):
        fwd = jnp.where(s < ctx, ctx_base + bi * ctx + s, lat_base + bi * lat + (s - ctx))
        rev = jnp.where(s < ctx, ctx_base + bi * ctx + (ctx - 1 - s), lat_base + bi * lat + (lat - 1 - (s - ctx)))
        return (d, jnp.where(d == 0, fwd, rev), 0)

    spec = pl.BlockSpec((None, SEQ_T, LRU_WIDTH), idx)
    return pl.pallas_call(
        _scan_kernel,
        grid=(B, 2, lat + ctx),
        in_specs=[spec, spec],
        out_specs=spec,
        out_shape=jax.ShapeDtypeStruct((2, T, LRU_WIDTH), F32),
        scratch_shapes=[pltpu.VMEM((1, LRU_WIDTH), F32)],
        compiler_params=_cparams(("parallel", "parallel", "arbitrary")),
        name="lru_scan",
    )(a, b)


def _post_common(y, x_ref, mod_ref, nw_ref, xo_ref, h2_ref):
    x_new = x_ref[...] + mod_ref[2:3, :] * y
    xo_ref[...] = x_new
    h2_ref[...] = _rmsnorm_mod(x_new, nw_ref[...], mod_ref[3:4, :], mod_ref[4:5, :]).astype(BF16)


def _post_attn_kernel(a_ref, b_ref, x_ref, mod_ref, nw_ref, w_ref, xo_ref, h2_ref):
    half = a_ref.shape[1]
    y = (jnp.dot(a_ref[...], w_ref[:half, :], preferred_element_type=F32)
         + jnp.dot(b_ref[...], w_ref[half:, :], preferred_element_type=F32))
    _post_common(y, x_ref, mod_ref, nw_ref, xo_ref, h2_ref)


def _post_rec_kernel(pool_ref, hs_ref, g_ref, x_ref, mod_ref, nw_ref, w_ref, xo_ref, h2_ref):
    half = pool_ref.shape[1]
    rec = ((hs_ref[0] + hs_ref[1]) * _gelu(g_ref[...])).astype(BF16)
    y = (jnp.dot(pool_ref[...], w_ref[:half, :], preferred_element_type=F32)
         + jnp.dot(rec, w_ref[half:, :], preferred_element_type=F32))
    _post_common(y, x_ref, mod_ref, nw_ref, xo_ref, h2_ref)


def _post_mixer(kind, parts, x, mods_l, norm_w, w_out, dims, n_tiles):
    B, S, L = dims
    T = x.shape[0]
    n_lat = B * S // TM
    s_tiles = S // TM

    def mod_idx(g):
        return (jnp.where(g < n_lat, g // s_tiles, B), 0, 0)

    row = lambda w: pl.BlockSpec((TM, w), lambda g: (g, 0))
    if kind == "attn":
        kern = _post_attn_kernel
        part_specs = [row(SWA_WIDTH), row(DIFF_WIDTH)]
    else:
        kern = _post_rec_kernel
        part_specs = [row(POOL_WIDTH), pl.BlockSpec((2, TM, LRU_WIDTH), lambda g: (0, g, 0)), row(LRU_WIDTH)]
    n_in = len(parts)
    return pl.pallas_call(
        kern,
        grid=(n_tiles,),
        in_specs=part_specs + [
            row(D_MODEL),
            pl.BlockSpec((None, N_MOD, D_MODEL), mod_idx),
            pl.BlockSpec((1, D_MODEL), lambda g: (0, 0)),
            pl.BlockSpec(w_out.shape, lambda g: (0, 0)),
        ],
        out_specs=[row(D_MODEL), row(D_MODEL)],
        out_shape=[jax.ShapeDtypeStruct((T, D_MODEL), F32), jax.ShapeDtypeStruct((T, D_MODEL), BF16)],
        input_output_aliases={n_in: 0},
        compiler_params=_cparams(("parallel",)),
        name="post_" + kind,
    )(*parts, x, mods_l, norm_w, w_out)


def _peer_cand_blocks():
    return [(i, PEER_TOPK // (i + 1)) for i in range(1, SUBLANES)]


def _peer_stats_kernel(h2_ref, wq_ref, sk_ref, s1_ref, s2_ref, c_ref, e2_ref, tau_ref,
                       q_scr, cur_scr, top_scr, cand_scr):
    tt = h2_ref.shape[0]
    q_scr[...] = _nt_dot(wq_ref[...], h2_ref[...])
    neg_inf = -jnp.inf
    n_cand = cand_scr.shape[0]

    def extract(src_ref, dst_ref, count):
        def body(k, _):
            cur = src_ref[...]
            m = jnp.max(cur, axis=0, keepdims=True)
            dst_ref[pl.ds(k, 1), :] = m
            src_ref[...] = jnp.where(cur == m, neg_inf, cur)
            return 0

        lax.fori_loop(0, count, body, 0)

    def head_body(h, _):
        for p, s_ref in enumerate((s1_ref, s2_ref)):
            r0 = pl.multiple_of((h * 2 + p) * PEER_HALF, PEER_HALF)
            s = jnp.dot(sk_ref[h * 2 + p], q_scr[pl.ds(r0, PEER_HALF), :], precision=HIGHEST,
                        preferred_element_type=F32)
            s_ref[h] = s
            cur_scr[...] = s
            extract(cur_scr, top_scr.at[p], PEER_TOPK)
        v1 = top_scr[0]
        v2 = top_scr[1]
        cand_scr[0:PEER_TOPK, :] = v1[0:1, :] + v2
        row = lax.broadcasted_iota(jnp.int32, (SUBLANES, 1), 0)
        for i, n_i in _peer_cand_blocks():
            blk = jnp.where(row < n_i, v1[i:i + 1, :] + v2[0:SUBLANES, :], neg_inf)
            cand_scr[PEER_TOPK + (i - 1) * SUBLANES: PEER_TOPK + i * SUBLANES, :] = blk
        cand_scr[n_cand - SUBLANES:, :] = v1[SUBLANES:, :] + v2[0:1, :]
        cand = cand_scr[...]
        extract(cand_scr, top_scr.at[2], PEER_TOPK)
        tau = top_scr[2, PEER_TOPK - 1:PEER_TOPK, :]
        top = v1[0:1, :] + v2[0:1, :]
        z = jnp.sum(jnp.where(cand >= tau, jnp.exp(cand - top), 0.0), axis=0, keepdims=True)
        tau_ref[pl.ds(h, 1), :] = tau
        c_ref[h] = jnp.exp(s1_ref[h] - v1[0:1, :]) / z
        e2_ref[h] = jnp.exp(s2_ref[h] - v2[0:1, :])
        return 0

    lax.fori_loop(0, PEER_HEADS, head_body, 0)


def _peer_stats(h2, wq_t, subkeys, n_tiles):
    T = h2.shape[0]
    tt = PEER_TT
    n_cand = PEER_TOPK + SUBLANES * SUBLANES
    key_spec = pl.BlockSpec((PEER_HEADS, PEER_NKEYS, tt), lambda i: (0, 0, i))
    key_shape = jax.ShapeDtypeStruct((PEER_HEADS, PEER_NKEYS, T), F32)
    return pl.pallas_call(
        _peer_stats_kernel,
        grid=(n_tiles,),
        in_specs=[
            pl.BlockSpec((tt, D_MODEL), lambda i: (i, 0)),
            pl.BlockSpec(wq_t.shape, lambda i: (0, 0)),
            pl.BlockSpec(subkeys.shape, lambda i: (0, 0, 0)),
        ],
        out_specs=[key_spec, key_spec, key_spec, key_spec, pl.BlockSpec((PEER_HEADS, tt), lambda i: (0, i))],
        out_shape=[key_shape, key_shape, key_shape, key_shape, jax.ShapeDtypeStruct((PEER_HEADS, T), F32)],
        scratch_shapes=[
            pltpu.VMEM((PEER_HEADS * PEER_QDIM, tt), F32),
            pltpu.VMEM((PEER_NKEYS, tt), F32),
            pltpu.VMEM((3, PEER_TOPK, tt), F32),
            pltpu.VMEM((n_cand, tt), F32),
        ],
        compiler_params=_cparams(("parallel",)),
        name="peer_stats",
    )(h2, wq_t, subkeys)


def _peer_main_kernel(h2_ref, u_ref, vt_ref, s1_ref, s2_ref, c_ref, e2_ref, tau_ref, x_ref, mod_ref,
                      xo_ref, act_scr, coef_scr, acc_scr):
    j = pl.program_id(1)
    tt = h2_ref.shape[0]

    @pl.when(j == 0)
    def _():
        acc_scr[...] = jnp.zeros_like(acc_scr)

    act_scr[...] = _nt_dot(u_ref[...], h2_ref[...])

    def a_body(al, _):
        a = j * PEER_NA + al
        s1full = [s1_ref[h, pl.ds(a, 1), :] for h in range(PEER_HEADS)]
        cfull = [c_ref[h, pl.ds(a, 1), :] for h in range(PEER_HEADS)]
        for tc in range(tt // LANES):
            ls = slice(tc * LANES, (tc + 1) * LANES)
            taus = [tau_ref[h:h + 1, ls] for h in range(PEER_HEADS)]
            s1rows = [r[:, ls] for r in s1full]
            crows = [r[:, ls] for r in cfull]
            for rb in range(PEER_NKEYS // PEER_RB):
                rs = slice(rb * PEER_RB, (rb + 1) * PEER_RB)
                w = jnp.zeros((PEER_RB, LANES), F32)
                for h in range(PEER_HEADS):
                    ssum = s1rows[h] + s2_ref[h, rs, ls]
                    w = w + jnp.where(ssum >= taus[h], crows[h] * e2_ref[h, rs, ls], 0.0)
                r0 = pl.multiple_of(al * PEER_NKEYS + rb * PEER_RB, PEER_RB)
                act = act_scr[pl.ds(r0, PEER_RB), ls]
                coef_scr[pl.ds(r0, PEER_RB), ls] = (w * _gelu(act)).astype(BF16)
        return 0

    lax.fori_loop(0, PEER_NA, a_body, 0)
    acc_scr[...] += jnp.dot(vt_ref[...], coef_scr[...], preferred_element_type=F32)

    @pl.when(j == pl.num_programs(1) - 1)
    def _():
        xo_ref[...] = x_ref[...] + mod_ref[5:6, :] * acc_scr[...].T


def _peer_main(h2, u, v_t, stats, x, mods_l, dims, n_tiles):
    B, S, L = dims
    T = x.shape[0]
    tt = PEER_TT
    ne = PEER_NA * PEER_NKEYS
    n_lat = B * S // tt
    s_tiles = S // tt
    s1, s2, c, e2, tau = stats

    def mod_idx(i, j):
        return (jnp.where(i < n_lat, i // s_tiles, B), 0, 0)

    key_spec = pl.BlockSpec((PEER_HEADS, PEER_NKEYS, tt), lambda i, j: (0, 0, i))
    return pl.pallas_call(
        _peer_main_kernel,
        grid=(n_tiles, PEER_EXPERTS // ne),
        in_specs=[
            pl.BlockSpec((tt, D_MODEL), lambda i, j: (i, 0)),
            pl.BlockSpec((ne, D_MODEL), lambda i, j: (j, 0)),
            pl.BlockSpec((D_MODEL, ne), lambda i, j: (0, j)),
            key_spec, key_spec, key_spec, key_spec,
            pl.BlockSpec((PEER_HEADS, tt), lambda i, j: (0, i)),
            pl.BlockSpec((tt, D_MODEL), lambda i, j: (i, 0)),
            pl.BlockSpec((None, N_MOD, D_MODEL), mod_idx),
        ],
        out_specs=pl.BlockSpec((tt, D_MODEL), lambda i, j: (i, 0)),
        out_shape=jax.ShapeDtypeStruct((T, D_MODEL), F32),
        scratch_shapes=[
            pltpu.VMEM((ne, tt), F32),
            pltpu.VMEM((ne, tt), BF16),
            pltpu.VMEM((D_MODEL, tt), F32),
        ],
        input_output_aliases={8: 0},
        compiler_params=_cparams(("parallel", "arbitrary")),
        name="peer_main",
    )(h2, u, v_t, s1, s2, c, e2, tau, x, mods_l)


def _final_norm_kernel(x_ref, w_ref, o_ref):
    x = x_ref[...]
    o_ref[...] = x * lax.rsqrt(jnp.mean(x * x, axis=-1, keepdims=True) + EPS) * w_ref[...]


def _final_norm(x, w, n_rows):
    return pl.pallas_call(
        _final_norm_kernel,
        grid=(n_rows // TM,),
        in_specs=[pl.BlockSpec((TM, D_MODEL), lambda g: (g, 0)), pl.BlockSpec((1, D_MODEL), lambda g: (0, 0))],
        out_specs=pl.BlockSpec((TM, D_MODEL), lambda g: (g, 0)),
        out_shape=jax.ShapeDtypeStruct((n_rows, D_MODEL), F32),
        compiler_params=_cparams(("parallel",)),
        name="final_norm",
    )(x, w)


def _lambda_init(layer):
    return 0.8 - 0.6 * math.exp(-0.3 * layer)


def _rope_tables(S):
    rows = S // GRID_W
    row = jnp.repeat(jnp.arange(rows), GRID_W).astype(F32)
    col = jnp.tile(jnp.arange(GRID_W), rows).astype(F32)
    inv = ROPE_THETA ** (-jnp.arange(ROPE_AXIS_FREQS, dtype=F32) / ROPE_AXIS_FREQS)
    ang = jnp.concatenate([row[:, None] * inv, col[:, None] * inv], axis=-1)
    cos, sin = jnp.cos(ang), jnp.sin(ang)
    cos_t = jnp.tile(cos, (1, LANES // ROPE_HALF))
    sin_t = jnp.tile(jnp.concatenate([-sin, sin], axis=-1), (1, LANES // HEAD_DIM))
    cos_t = jnp.concatenate([cos_t, jnp.ones((TM, LANES), F32)], axis=0)
    sin_t = jnp.concatenate([sin_t, jnp.zeros((TM, LANES), F32)], axis=0)
    return cos_t, sin_t


def _block_diag(w):
    nd, nb, bd, _ = w.shape
    eye = jnp.eye(nb, dtype=w.dtype)
    return jnp.einsum("dnij,nm->dnimj", w, eye).reshape(nd, nb * bd, nb * bd)


def kernel(x, c, ctx, c_ctx, w_mod, b_mod, norm_mix, norm_ffn, w_out, attn_w_in, swa_sink, diff_lambda, diff_subln, rec_w_in, pool_w, pool_scale, lru_conv_w, lru_conv_b, lru_wa, lru_ba, lru_wx, lru_bx, lru_lambda, peer_wq, peer_subkeys, peer_u, peer_v, final_norm):
    B, S, D = x.shape
    L = ctx.shape[1]
    depth = w_mod.shape[0]
    dims = (B, S, L)
    assert D == D_MODEL and S % PEER_TT == 0 and (B * L) % PEER_TT == 0 and L % SEQ_T == 0
    assert S % GRID_W == 0 and S >= 3 * BLOCK and B + 1 <= SUBLANES
    lat_rows = B * S
    T = lat_rows + B * L

    xs = jnp.concatenate([x.reshape(lat_rows, D), ctx.reshape(B * L, D)], axis=0)
    cc = jnp.zeros((SUBLANES, D), F32).at[:B].set(c).at[B].set(c_ctx)
    mods = _modulation(cc, w_mod, b_mod)
    cos_t, sin_t = _rope_tables(S)

    for l in range(depth):
        jl = l // 2
        ctx_out = l < depth - 1
        n_rows = T if ctx_out else lat_rows
        mods_l = mods[l]
        if l % 2 == 0:
            q, k, v, dq, dk, dv = _pre_attn(xs, mods_l, norm_mix[l][None], attn_w_in[jl].astype(BF16),
                                            cos_t, sin_t, dims)
            a = _swa(q, k, v, swa_sink[jl], dims)
            bd = _diff_attn(dq, dk, dv, diff_lambda[jl], diff_subln[jl][None], _lambda_init(l), dims)
            xs, h2 = _post_mixer("attn", (a, bd), xs, mods_l, norm_ffn[l][None], w_out[l].astype(BF16),
                                 dims, n_rows // TM)
        else:
            xp, xr, g = _pre_rec(xs, mods_l, norm_mix[l][None], rec_w_in[jl].astype(BF16), dims)
            pool, a_co, b_co = _rec_mid(xp, xr, lru_conv_w[jl], lru_conv_b[jl][None],
                                        _block_diag(lru_wa[jl]).astype(BF16), lru_ba[jl],
                                        _block_diag(lru_wx[jl]).astype(BF16), lru_bx[jl], lru_lambda[jl],
                                        pool_w[jl].astype(BF16), pool_scale[jl][None], dims)
            hs = _scan(a_co, b_co, dims)
            xs, h2 = _post_mixer("rec", (pool, hs, g), xs, mods_l, norm_ffn[l][None], w_out[l].astype(BF16),
                                 dims, n_rows // TM)
        n_peer = n_rows // PEER_TT
        wq_t = peer_wq[l].astype(BF16).T
        sk = peer_subkeys[l].reshape(PEER_HEADS * 2, PEER_NKEYS, PEER_HALF)
        stats = _peer_stats(h2, wq_t, sk, n_peer)
        xs = _peer_main(h2, peer_u[l].astype(BF16), peer_v[l].astype(BF16).T, stats, xs, mods_l, dims, n_peer)

    return _final_norm(xs, final_norm[None], lat_rows).reshape(B, S, D)
```

```python
import functools
import math

import jax
import jax.numpy as jnp
from jax import lax
from jax.experimental import pallas as pl
from jax.experimental.pallas import tpu as pltpu

F32 = jnp.float32
BF16 = jnp.bfloat16
HIGHEST = lax.Precision.HIGHEST

D_MODEL = 1024
HEAD_DIM = 64
EPS = 1e-6
NEG = -1e30
N_MOD = 6
SCALE = HEAD_DIM ** -0.5
ROPE_HALF = HEAD_DIM // 2
ROPE_AXIS_FREQS = ROPE_HALF // 2
ROPE_THETA = 10000.0
GRID_W = 64
SWA_Q_HEADS = 8
SWA_KV_HEADS = 2
SWA_GROUP = SWA_Q_HEADS // SWA_KV_HEADS
WINDOW = 128
BLOCK = 128
SWA_WIDTH = SWA_Q_HEADS * HEAD_DIM
SWA_KV_WIDTH = SWA_KV_HEADS * HEAD_DIM
DIFF_HEADS = 4
DIFF_V_DIM = 2 * HEAD_DIM
DIFF_QK_WIDTH = DIFF_HEADS * 2 * HEAD_DIM
DIFF_WIDTH = DIFF_HEADS * DIFF_V_DIM
POOL_WINDOWS = (2, 4, 8, 16)
POOL_GROUPS = 4
POOL_WIDTH = D_MODEL // 2
POOL_GDIM = POOL_WIDTH // POOL_GROUPS
LRU_WIDTH = D_MODEL // 2
LRU_BLOCKS = 8
CONV_W = 4
CONV_LEFT = CONV_W // 2
LRU_C = 8.0
PEER_HEADS = 8
PEER_NKEYS = 128
PEER_EXPERTS = PEER_NKEYS * PEER_NKEYS
PEER_QDIM = 256
PEER_HALF = PEER_QDIM // 2
PEER_TOPK = 16

LANES = 128
SUBLANES = 8
VMEM_LIMIT = 56 * 1024 * 1024

TM = 256
SEQ_T = 256
DIFF_GROUP = 4
HALO = 8
PEER_TT = 512
PEER_NA = 8
PEER_RB = 64


def _cparams(sem):
    return pltpu.CompilerParams(dimension_semantics=sem, vmem_limit_bytes=VMEM_LIMIT)


def _nt_dot(a, b):
    return lax.dot_general(a, b, (((1,), (1,)), ((), ())), preferred_element_type=F32)


def _rmsnorm_mod(x, w, shift, scale):
    y = x * lax.rsqrt(jnp.mean(x * x, axis=-1, keepdims=True) + EPS) * w
    return y * (1.0 + scale) + shift


def _gelu(x):
    return 0.5 * x * (1.0 + lax.erf(x * (2.0 ** -0.5)))


def _mod_kernel(cc_ref, w_ref, b_ref, o_ref):
    cc = cc_ref[...]
    sc = cc * jax.nn.sigmoid(cc)
    o_ref[...] = jnp.dot(sc, w_ref[...], precision=HIGHEST, preferred_element_type=F32) + b_ref[...]


def _modulation(cc, w_mod, b_mod):
    depth = w_mod.shape[0]
    rows = cc.shape[0]
    out = pl.pallas_call(
        _mod_kernel,
        grid=(depth, N_MOD),
        in_specs=[
            pl.BlockSpec((rows, D_MODEL), lambda l, j: (0, 0)),
            pl.BlockSpec((None, D_MODEL, D_MODEL), lambda l, j: (l, 0, j)),
            pl.BlockSpec((None, 1, D_MODEL), lambda l, j: (l, 0, j)),
        ],
        out_specs=pl.BlockSpec((None, rows, D_MODEL), lambda l, j: (l, 0, j)),
        out_shape=jax.ShapeDtypeStruct((depth, rows, N_MOD * D_MODEL), F32),
        compiler_params=_cparams(("parallel", "parallel")),
        name="modulation",
    )(cc, w_mod, b_mod.reshape(depth, 1, N_MOD * D_MODEL))
    return out.reshape(depth, rows, N_MOD, D_MODEL)


def _rope128(x, cos, sin_signed, first_half):
    partner = jnp.where(first_half, pltpu.roll(x, LANES - ROPE_HALF, 1), pltpu.roll(x, ROPE_HALF, 1))
    return x * cos + partner * sin_signed


def _pre_attn_kernel(x_ref, mod_ref, nw_ref, w_ref, cos_ref, sin_ref,
                     q_ref, k_ref, v_ref, dq_ref, dk_ref, dv_ref):
    h = _rmsnorm_mod(x_ref[...], nw_ref[...], mod_ref[0:1, :], mod_ref[1:2, :])
    p = jnp.dot(h.astype(BF16), w_ref[...], preferred_element_type=F32)
    cos = cos_ref[...]
    sin = sin_ref[...]
    lane = lax.broadcasted_iota(jnp.int32, (1, LANES), 1)
    first_half = (lane % HEAD_DIM) < ROPE_HALF

    def roped(lo, width, scale):
        outs = []
        for c in range(width // LANES):
            xc = p[:, lo + c * LANES: lo + (c + 1) * LANES]
            outs.append((_rope128(xc, cos, sin, first_half) * scale).astype(BF16))
        return outs

    o1 = SWA_WIDTH
    o2 = o1 + SWA_KV_WIDTH
    o3 = o2 + SWA_KV_WIDTH
    o4 = o3 + DIFF_QK_WIDTH
    o5 = o4 + DIFF_QK_WIDTH
    for c, val in enumerate(roped(0, SWA_WIDTH, SCALE)):
        q_ref[:, c * LANES:(c + 1) * LANES] = val
    for c, val in enumerate(roped(o1, SWA_KV_WIDTH, 1.0)):
        k_ref[:, c * LANES:(c + 1) * LANES] = val
    v_ref[...] = p[:, o2:o3].astype(BF16)
    for c, val in enumerate(roped(o3, DIFF_QK_WIDTH, SCALE)):
        dq_ref[:, c * LANES:(c + 1) * LANES] = val
    for c, val in enumerate(roped(o4, DIFF_QK_WIDTH, 1.0)):
        dk_ref[:, c * LANES:(c + 1) * LANES] = val
    dv_ref[...] = p[:, o5:].T.astype(BF16)


def _pre_attn(x, mods_l, norm_w, w_in, cos_t, sin_t, dims):
    B, S, L = dims
    T = x.shape[0]
    n_lat = B * S // TM
    s_tiles = S // TM

    def mod_idx(g):
        return (jnp.where(g < n_lat, g // s_tiles, B), 0, 0)

    def rope_idx(g):
        return (jnp.where(g < n_lat, g % s_tiles, s_tiles), 0)

    widths = (SWA_WIDTH, SWA_KV_WIDTH, SWA_KV_WIDTH, DIFF_QK_WIDTH, DIFF_QK_WIDTH)
    return pl.pallas_call(
        _pre_attn_kernel,
        grid=(T // TM,),
        in_specs=[
            pl.BlockSpec((TM, D_MODEL), lambda g: (g, 0)),
            pl.BlockSpec((None, N_MOD, D_MODEL), mod_idx),
            pl.BlockSpec((1, D_MODEL), lambda g: (0, 0)),
            pl.BlockSpec(w_in.shape, lambda g: (0, 0)),
            pl.BlockSpec((TM, LANES), rope_idx),
            pl.BlockSpec((TM, LANES), rope_idx),
        ],
        out_specs=[pl.BlockSpec((TM, w), lambda g: (g, 0)) for w in widths]
        + [pl.BlockSpec((None, DIFF_WIDTH, TM), lambda g: (g, 0, 0))],
        out_shape=[jax.ShapeDtypeStruct((T, w), BF16) for w in widths]
        + [jax.ShapeDtypeStruct((T // TM, DIFF_WIDTH, TM), BF16)],
        compiler_params=_cparams(("parallel",)),
        name="pre_attn",
    )(x, mods_l, norm_w, w_in, cos_t, sin_t)


def _swa_kernel(q_ref, kl_ref, vl_ref, kc_ref, vc_ref, sink_ref, o_ref, *, n_lat_blocks, seq):
    j = pl.program_id(1)
    is_ctx = j >= n_lat_blocks
    jl = jnp.minimum(j, n_lat_blocks - 1)
    band = 3 * BLOCK
    bs = pl.multiple_of(jnp.clip((jl - 1) * BLOCK, 0, seq - band), BLOCK)
    kb = kl_ref[pl.ds(bs, band), :]
    vb = vl_ref[pl.ds(bs, band), :]
    kc = kc_ref[...]
    vc = vc_ref[...]
    qpos = jl * BLOCK + lax.broadcasted_iota(jnp.int32, (BLOCK, 1), 0)
    kpos = bs + lax.broadcasted_iota(jnp.int32, (1, band), 1)
    valid = jnp.logical_and(jnp.abs(qpos - kpos) <= WINDOW, jnp.logical_not(is_ctx))
    q = q_ref[...]
    for hk in range(SWA_KV_HEADS):
        ks = slice(hk * HEAD_DIM, (hk + 1) * HEAD_DIM)
        for g in range(SWA_GROUP):
            h = hk * SWA_GROUP + g
            qh = q[:, h * HEAD_DIM:(h + 1) * HEAD_DIM]
            sl = jnp.where(valid, _nt_dot(qh, kb[:, ks]), NEG)
            sc = _nt_dot(qh, kc[:, ks])
            sink = sink_ref[h]
            m = jnp.maximum(jnp.maximum(jnp.max(sl, axis=-1, keepdims=True),
                                        jnp.max(sc, axis=-1, keepdims=True)), sink)
            pl_ = jnp.exp(sl - m)
            pc = jnp.exp(sc - m)
            den = (jnp.sum(pl_, axis=-1, keepdims=True) + jnp.sum(pc, axis=-1, keepdims=True)
                   + jnp.exp(sink - m))
            o = (jnp.dot(pl_.astype(BF16), vb[:, ks], preferred_element_type=F32)
                 + jnp.dot(pc.astype(BF16), vc[:, ks], preferred_element_type=F32))
            o_ref[:, h * HEAD_DIM:(h + 1) * HEAD_DIM] = (o / den).astype(BF16)


def _swa(q, k, v, sink, dims):
    B, S, L = dims
    T = q.shape[0]
    n_lat_blocks = S // BLOCK
    n_ctx_blocks = L // BLOCK
    lat_rows = B * S

    def q_idx(b, j):
        return (jnp.where(j < n_lat_blocks, b * n_lat_blocks + j,
                          lat_rows // BLOCK + b * n_ctx_blocks + (j - n_lat_blocks)), 0)

    lat_spec = pl.BlockSpec((S, SWA_KV_WIDTH), lambda b, j: (b, 0))
    ctx_spec = pl.BlockSpec((L, SWA_KV_WIDTH), lambda b, j: (lat_rows // L + b, 0))
    return pl.pallas_call(
        functools.partial(_swa_kernel, n_lat_blocks=n_lat_blocks, seq=S),
        grid=(B, n_lat_blocks + n_ctx_blocks),
        in_specs=[
            pl.BlockSpec((BLOCK, SWA_WIDTH), q_idx),
            lat_spec, lat_spec, ctx_spec, ctx_spec,
            pl.BlockSpec(memory_space=pltpu.SMEM),
        ],
        out_specs=pl.BlockSpec((BLOCK, SWA_WIDTH), q_idx),
        out_shape=jax.ShapeDtypeStruct((T, SWA_WIDTH), BF16),
        compiler_params=_cparams(("parallel", "arbitrary")),
        name="swa",
    )(q, k, v, k, v, sink)


def _diff_kernel(q_ref, kl_ref, vtl_ref, kc_ref, vtc_ref, lam_ref, subln_ref, o_ref, qp_scr, m_scr, l_scr, acc_scr, *,
                 n_lat_tiles, n_lat_chunks, n_ctx_chunks, lam_init):
    j = pl.program_id(1)
    lv = lam_ref[...]
    lam = (jnp.exp(jnp.sum(lv[0:1] * lv[1:2], axis=-1, keepdims=True))
           - jnp.exp(jnp.sum(lv[2:3] * lv[3:4], axis=-1, keepdims=True)) + lam_init)
    lane = lax.broadcasted_iota(jnp.int32, (1, DIFF_V_DIM), 1)
    heads = [slice(h * DIFF_V_DIM, (h + 1) * DIFF_V_DIM) for h in range(DIFF_HEADS)]
    ones = jnp.ones((SUBLANES, SEQ_T), BF16)

    for h, vs in enumerate(heads):
        qh = q_ref[:, vs]
        zero = jnp.zeros_like(qh)
        qp_scr[2 * h] = jnp.where(lane < HEAD_DIM, qh, zero)
        qp_scr[2 * h + 1] = jnp.where(lane >= HEAD_DIM, qh, zero)
    m_scr[...] = jnp.full(m_scr.shape, -jnp.inf, F32)
    l_scr[...] = jnp.zeros(l_scr.shape, F32)
    acc_scr[...] = jnp.zeros(acc_scr.shape, F32)

    def update(blocks_of_head):
        for h, vs in enumerate(heads):
            blocks = blocks_of_head(vs)
            for m in range(2):
                ch = 2 * h + m
                qm = qp_scr[ch]
                m_old = m_scr[ch]
                ss = [_nt_dot(kblk, qm) for kblk, _ in blocks]
                m_blk = jnp.max(ss[0], axis=0, keepdims=True)
                for s in ss[1:]:
                    m_blk = jnp.maximum(m_blk, jnp.max(s, axis=0, keepdims=True))
                m_new = jnp.maximum(m_old, m_blk.astype(BF16).astype(F32))
                m_b = m_new.astype(BF16)
                alpha = jnp.exp(m_old - m_new)
                l_new = alpha * l_scr[ch]
                acc = alpha * acc_scr[ch]
                for s, (_, vtblk) in zip(ss, blocks):
                    p = jnp.exp(s.astype(BF16) - m_b)
                    l_new = l_new + jnp.dot(ones, p, preferred_element_type=F32)[0:1]
                    acc = acc + jnp.dot(vtblk, p, preferred_element_type=F32)
                m_scr[ch] = m_new
                l_scr[ch] = l_new
                acc_scr[ch] = acc

    update(lambda vs: [(kc_ref[c * SEQ_T:(c + 1) * SEQ_T, vs], vtc_ref[c, vs, :]) for c in range(n_ctx_chunks)])

    @pl.when(j < n_lat_tiles)
    def _():
        def body(g, _):
            def blocks_of_head(vs):
                blocks = []
                for i in range(DIFF_GROUP):
                    c = g * DIFF_GROUP + i
                    r0 = pl.multiple_of(c * SEQ_T, SEQ_T)
                    blocks.append((kl_ref[pl.ds(r0, SEQ_T), vs], vtl_ref[c, vs, :]))
                return blocks

            update(blocks_of_head)
            return 0

        lax.fori_loop(0, n_lat_chunks // DIFF_GROUP, body, 0)

    for h, vs in enumerate(heads):
        o = (acc_scr[2 * h] / l_scr[2 * h] - lam * (acc_scr[2 * h + 1] / l_scr[2 * h + 1])).T
        y = o * lax.rsqrt(jnp.mean(o * o, axis=-1, keepdims=True) + EPS) * subln_ref[...]
        o_ref[:, vs] = (y * (1.0 - lam_init)).astype(BF16)


def _diff_attn(dq, dk, dvt, lam_vecs, subln, lam_init, dims):
    B, S, L = dims
    T = dq.shape[0]
    n_lat_tiles = S // SEQ_T
    n_ctx_tiles = L // SEQ_T
    lat_rows = B * S

    def q_idx(b, j):
        return (jnp.where(j < n_lat_tiles, b * n_lat_tiles + j,
                          lat_rows // SEQ_T + b * n_ctx_tiles + (j - n_lat_tiles)), 0)

    return pl.pallas_call(
        functools.partial(_diff_kernel, n_lat_tiles=n_lat_tiles, n_lat_chunks=n_lat_tiles,
                          n_ctx_chunks=n_ctx_tiles, lam_init=lam_init),
        grid=(B, n_lat_tiles + n_ctx_tiles),
        in_specs=[
            pl.BlockSpec((SEQ_T, DIFF_QK_WIDTH), q_idx),
            pl.BlockSpec((S, DIFF_QK_WIDTH), lambda b, j: (b, 0)),
            pl.BlockSpec((n_lat_tiles, DIFF_WIDTH, SEQ_T), lambda b, j: (b, 0, 0)),
            pl.BlockSpec((L, DIFF_QK_WIDTH), lambda b, j: (lat_rows // L + b, 0)),
            pl.BlockSpec((n_ctx_tiles, DIFF_WIDTH, SEQ_T), lambda b, j: (lat_rows // L + b, 0, 0)),
            pl.BlockSpec(lam_vecs.shape, lambda b, j: (0, 0)),
            pl.BlockSpec((1, DIFF_V_DIM), lambda b, j: (0, 0)),
        ],
        out_specs=pl.BlockSpec((SEQ_T, DIFF_WIDTH), q_idx),
        out_shape=jax.ShapeDtypeStruct((T, DIFF_WIDTH), BF16),
        scratch_shapes=[
            pltpu.VMEM((2 * DIFF_HEADS, SEQ_T, DIFF_V_DIM), BF16),
            pltpu.VMEM((2 * DIFF_HEADS, 1, SEQ_T), F32),
            pltpu.VMEM((2 * DIFF_HEADS, 1, SEQ_T), F32),
            pltpu.VMEM((2 * DIFF_HEADS, DIFF_V_DIM, SEQ_T), F32),
        ],
        compiler_params=_cparams(("parallel", "arbitrary")),
        name="diff_attn",
    )(dq, dk, dvt, dk, dvt, lam_vecs, subln)


def _pre_rec_kernel(x_ref, mod_ref, nw_ref, w_ref, xp_ref, xr_ref, g_ref):
    h = _rmsnorm_mod(x_ref[...], nw_ref[...], mod_ref[0:1, :], mod_ref[1:2, :])
    p = jnp.dot(h.astype(BF16), w_ref[...], preferred_element_type=F32)
    xp_ref[...] = p[:, :POOL_WIDTH]
    xr_ref[...] = p[:, POOL_WIDTH:POOL_WIDTH + LRU_WIDTH]
    g_ref[...] = p[:, POOL_WIDTH + LRU_WIDTH:]


def _pre_rec(x, mods_l, norm_w, w_in, dims):
    B, S, L = dims
    T = x.shape[0]
    n_lat = B * S // TM
    s_tiles = S // TM

    def mod_idx(g):
        return (jnp.where(g < n_lat, g // s_tiles, B), 0, 0)

    widths = (POOL_WIDTH, LRU_WIDTH, LRU_WIDTH)
    return pl.pallas_call(
        _pre_rec_kernel,
        grid=(T // TM,),
        in_specs=[
            pl.BlockSpec((TM, D_MODEL), lambda g: (g, 0)),
            pl.BlockSpec((None, N_MOD, D_MODEL), mod_idx),
            pl.BlockSpec((1, D_MODEL), lambda g: (0, 0)),
            pl.BlockSpec(w_in.shape, lambda g: (0, 0)),
        ],
        out_specs=[pl.BlockSpec((TM, w), lambda g: (g, 0)) for w in widths],
        out_shape=[jax.ShapeDtypeStruct((T, w), F32) for w in widths],
        compiler_params=_cparams(("parallel",)),
        name="pre_rec",
    )(x, mods_l, norm_w, w_in)


def _rec_mid_kernel(xp_p, xp_c, xp_n, xr_p, xr_c, xr_n, cw_ref, cb_ref, wa_ref, ba_ref, wx_ref, bx_ref,
                    lam_ref, pw_ref, ps_ref, pool_ref, a_ref, b_ref, *, n_lat_chunks, lat_chunks_per_seq,
                    ctx_chunks_per_seq, lat_len, ctx_len):
    g = pl.program_id(0)
    is_lat = g < n_lat_chunks
    cps = jnp.where(is_lat, lat_chunks_per_seq, ctx_chunks_per_seq)
    within = jnp.where(is_lat, g % lat_chunks_per_seq, (g - n_lat_chunks) % ctx_chunks_per_seq)
    has_prev = within > 0
    has_next = within < cps - 1
    seg_len = jnp.where(is_lat, lat_len, ctx_len)
    ext_rows = SEQ_T + 2 * HALO

    def extended(prev_ref, cur_ref, next_ref):
        prev = jnp.where(has_prev, prev_ref[SEQ_T - HALO:, :], 0.0)
        nxt = jnp.where(has_next, next_ref[:HALO, :], 0.0)
        return jnp.concatenate([prev, cur_ref[...], nxt], axis=0)

    def shifted(ext, off):
        return pltpu.roll(ext, (-off) % ext_rows, 0)[HALO:HALO + SEQ_T, :]

    xp_ext = extended(xp_p, xp_c, xp_n)
    t = within * SEQ_T + lax.broadcasted_iota(jnp.int32, (SEQ_T, 1), 0)
    for gi, w in enumerate(POOL_WINDOWS):
        cols = slice(gi * POOL_GDIM, (gi + 1) * POOL_GDIM)
        eg = xp_ext[:, cols]
        tot = shifted(eg, -(w // 2))
        for off in range(-(w // 2) + 1, w - w // 2):
            tot = tot + shifted(eg, off)
        lo = jnp.clip(t - w // 2, 0, seg_len)
        hi = jnp.clip(t - w // 2 + w, 0, seg_len)
        cnt = (hi - lo).astype(F32)
        d = tot / cnt - xp_c[:, cols]
        y = jnp.dot(d.astype(BF16), pw_ref[gi], preferred_element_type=F32)
        pool_ref[:, cols] = (y * ps_ref[:, cols]).astype(BF16)

    xr_ext = extended(xr_p, xr_c, xr_n)
    u = cb_ref[...] + cw_ref[0:1, :] * shifted(xr_ext, -CONV_LEFT)
    for k in range(1, CONV_W):
        u = u + cw_ref[k:k + 1, :] * shifted(xr_ext, k - CONV_LEFT)
    ub = u.astype(BF16)
    for d in range(2):
        r = jax.nn.sigmoid(jnp.dot(ub, wa_ref[d], preferred_element_type=F32) + ba_ref[d:d + 1, :])
        i = jax.nn.sigmoid(jnp.dot(ub, wx_ref[d], preferred_element_type=F32) + bx_ref[d:d + 1, :])
        nl = -lam_ref[d:d + 1, :]
        softplus = jnp.maximum(nl, 0.0) + jnp.log1p(jnp.exp(-jnp.abs(nl)))
        log_a = -LRU_C * r * softplus
        a_ref[d] = jnp.exp(log_a)
        th = jnp.tanh(log_a)
        b_ref[d] = jnp.sqrt(-2.0 * th / (1.0 - th)) * (i * u)


def _rec_mid(xp, xr, conv_w, conv_b, wa_bd, ba, wx_bd, bx, lam, pool_w, pool_scale, dims):
    B, S, L = dims
    T = xp.shape[0]
    n_chunks = T // SEQ_T

    def cur(g):
        return (g, 0)

    def prv(g):
        return (jnp.maximum(g - 1, 0), 0)

    def nxt(g):
        return (jnp.minimum(g + 1, n_chunks - 1), 0)

    tile = lambda idx: pl.BlockSpec((SEQ_T, LRU_WIDTH), idx)
    full = lambda arr: pl.BlockSpec(arr.shape, lambda g: (0,) * arr.ndim)
    kern = functools.partial(_rec_mid_kernel, n_lat_chunks=B * S // SEQ_T, lat_chunks_per_seq=S // SEQ_T,
                             ctx_chunks_per_seq=L // SEQ_T, lat_len=S, ctx_len=L)
    return pl.pallas_call(
        kern,
        grid=(n_chunks,),
        in_specs=[tile(prv), tile(cur), tile(nxt), tile(prv), tile(cur), tile(nxt),
                  full(conv_w), full(conv_b), full(wa_bd), full(ba), full(wx_bd), full(bx), full(lam),
                  full(pool_w), full(pool_scale)],
        out_specs=[pl.BlockSpec((SEQ_T, POOL_WIDTH), cur),
                   pl.BlockSpec((2, SEQ_T, LRU_WIDTH), lambda g: (0, g, 0)),
                   pl.BlockSpec((2, SEQ_T, LRU_WIDTH), lambda g: (0, g, 0))],
        out_shape=[jax.ShapeDtypeStruct((T, POOL_WIDTH), BF16),
                   jax.ShapeDtypeStruct((2, T, LRU_WIDTH), F32),
                   jax.ShapeDtypeStruct((2, T, LRU_WIDTH), F32)],
        compiler_params=_cparams(("parallel",)),
        name="rec_mid",
    )(xp, xp, xp, xr, xr, xr, conv_w, conv_b, wa_bd, ba, wx_bd, bx, lam, pool_w, pool_scale)


def _scan_kernel(a_ref, b_ref, h_ref, carry_ref):
    d = pl.program_id(1)
    s = pl.program_id(2)

    @pl.when(s == 0)
    def _():
        carry_ref[...] = jnp.zeros_like(carry_ref)

    def run(reverse):
        def body(i, h):
            t = (SEQ_T - 1 - i) if reverse else i
            h = a_ref[pl.ds(t, 1), :] * h + b_ref[pl.ds(t, 1), :]
            h_ref[pl.ds(t, 1), :] = h
            return h

        carry_ref[...] = lax.fori_loop(0, SEQ_T, body, carry_ref[...], unroll=8)

    @pl.when(d == 0)
    def _():
        run(False)

    @pl.when(d == 1)
    def _():
        run(True)


def _scan(a, b, dims):
    B, S, L = dims
    T = a.shape[1]
    lat = S // SEQ_T
    ctx = L // SEQ_T
    lat_base = 0
    ctx_base = B * S // SEQ_T

    def idx(bi, d, s):
        fwd = jnp.where(s < ctx, ctx_base + bi * ctx + s, lat_base + bi * lat + (s - ctx))
        rev = jnp.where(s < ctx, ctx_base + bi * ctx + (ctx - 1 - s), lat_base + bi * lat + (lat - 1 - (s - ctx)))
        return (d, jnp.where(d == 0, fwd, rev), 0)

    spec = pl.BlockSpec((None, SEQ_T, LRU_WIDTH), idx)
    return pl.pallas_call(
        _scan_kernel,
        grid=(B, 2, lat + ctx),
        in_specs=[spec, spec],
        out_specs=spec,
        out_shape=jax.ShapeDtypeStruct((2, T, LRU_WIDTH), F32),
        scratch_shapes=[pltpu.VMEM((1, LRU_WIDTH), F32)],
        compiler_params=_cparams(("parallel", "parallel", "arbitrary")),
        name="lru_scan",
    )(a, b)


def _post_common(y, x_ref, mod_ref, nw_ref, xo_ref, h2_ref):
    x_new = x_ref[...] + mod_ref[2:3, :] * y
    xo_ref[...] = x_new
    h2_ref[...] = _rmsnorm_mod(x_new, nw_ref[...], mod_ref[3:4, :], mod_ref[4:5, :]).astype(BF16)


def _post_attn_kernel(a_ref, b_ref, x_ref, mod_ref, nw_ref, w_ref, xo_ref, h2_ref):
    half = a_ref.shape[1]
    y = (jnp.dot(a_ref[...], w_ref[:half, :], preferred_element_type=F32)
         + jnp.dot(b_ref[...], w_ref[half:, :], preferred_element_type=F32))
    _post_common(y, x_ref, mod_ref, nw_ref, xo_ref, h2_ref)


def _post_rec_kernel(pool_ref, hs_ref, g_ref, x_ref, mod_ref, nw_ref, w_ref, xo_ref, h2_ref):
    half = pool_ref.shape[1]
    rec = ((hs_ref[0] + hs_ref[1]) * _gelu(g_ref[...])).astype(BF16)
    y = (jnp.dot(pool_ref[...], w_ref[:half, :], preferred_element_type=F32)
         + jnp.dot(rec, w_ref[half:, :], preferred_element_type=F32))
    _post_common(y, x_ref, mod_ref, nw_ref, xo_ref, h2_ref)


def _post_mixer(kind, parts, x, mods_l, norm_w, w_out, dims, n_tiles):
    B, S, L = dims
    T = x.shape[0]
    n_lat = B * S // TM
    s_tiles = S // TM

    def mod_idx(g):
        return (jnp.where(g < n_lat, g // s_tiles, B), 0, 0)

    row = lambda w: pl.BlockSpec((TM, w), lambda g: (g, 0))
    if kind == "attn":
        kern = _post_attn_kernel
        part_specs = [row(SWA_WIDTH), row(DIFF_WIDTH)]
    else:
        kern = _post_rec_kernel
        part_specs = [row(POOL_WIDTH), pl.BlockSpec((2, TM, LRU_WIDTH), lambda g: (0, g, 0)), row(LRU_WIDTH)]
    n_in = len(parts)
    return pl.pallas_call(
        kern,
        grid=(n_tiles,),
        in_specs=part_specs + [
            row(D_MODEL),
            pl.BlockSpec((None, N_MOD, D_MODEL), mod_idx),
            pl.BlockSpec((1, D_MODEL), lambda g: (0, 0)),
            pl.BlockSpec(w_out.shape, lambda g: (0, 0)),
        ],
        out_specs=[row(D_MODEL), row(D_MODEL)],
        out_shape=[jax.ShapeDtypeStruct((T, D_MODEL), F32), jax.ShapeDtypeStruct((T, D_MODEL), BF16)],
        input_output_aliases={n_in: 0},
        compiler_params=_cparams(("parallel",)),
        name="post_" + kind,
    )(*parts, x, mods_l, norm_w, w_out)


def _peer_cand_blocks():
    return [(i, PEER_TOPK // (i + 1)) for i in range(1, SUBLANES)]


def _peer_stats_kernel(h2_ref, wq_ref, sk_ref, s1_ref, s2_ref, c_ref, e2_ref, tau_ref,
                       q_scr, cur_scr, top_scr, cand_scr):
    tt = h2_ref.shape[0]
    q_scr[...] = _nt_dot(wq_ref[...], h2_ref[...])
    neg_inf = -jnp.inf
    n_cand = cand_scr.shape[0]

    def extract(src_ref, dst_ref, count):
        def body(k, _):
            cur = src_ref[...]
            m = jnp.max(cur, axis=0, keepdims=True)
            dst_ref[pl.ds(k, 1), :] = m
            src_ref[...] = jnp.where(cur == m, neg_inf, cur)
            return 0

        lax.fori_loop(0, count, body, 0)

    def head_body(h, _):
        for p, s_ref in enumerate((s1_ref, s2_ref)):
            r0 = pl.multiple_of((h * 2 + p) * PEER_HALF, PEER_HALF)
            s = jnp.dot(sk_ref[h * 2 + p], q_scr[pl.ds(r0, PEER_HALF), :], precision=HIGHEST,
                        preferred_element_type=F32)
            s_ref[h] = s
            cur_scr[...] = s
            extract(cur_scr, top_scr.at[p], PEER_TOPK)
        v1 = top_scr[0]
        v2 = top_scr[1]
        cand_scr[0:PEER_TOPK, :] = v1[0:1, :] + v2
        row = lax.broadcasted_iota(jnp.int32, (SUBLANES, 1), 0)
        for i, n_i in _peer_cand_blocks():
            blk = jnp.where(row < n_i, v1[i:i + 1, :] + v2[0:SUBLANES, :], neg_inf)
            cand_scr[PEER_TOPK + (i - 1) * SUBLANES: PEER_TOPK + i * SUBLANES, :] = blk
        cand_scr[n_cand - SUBLANES:, :] = v1[SUBLANES:, :] + v2[0:1, :]
        cand = cand_scr[...]
        extract(cand_scr, top_scr.at[2], PEER_TOPK)
        tau = top_scr[2, PEER_TOPK - 1:PEER_TOPK, :]
        top = v1[0:1, :] + v2[0:1, :]
        z = jnp.sum(jnp.where(cand >= tau, jnp.exp(cand - top), 0.0), axis=0, keepdims=True)
        tau_ref[pl.ds(h, 1), :] = tau
        c_ref[h] = jnp.exp(s1_ref[h] - v1[0:1, :]) / z
        e2_ref[h] = jnp.exp(s2_ref[h] - v2[0:1, :])
        return 0

    lax.fori_loop(0, PEER_HEADS, head_body, 0)


def _peer_stats(h2, wq_t, subkeys, n_tiles):
    T = h2.shape[0]
    tt = PEER_TT
    n_cand = PEER_TOPK + SUBLANES * SUBLANES
    key_spec = pl.BlockSpec((PEER_HEADS, PEER_NKEYS, tt), lambda i: (0, 0, i))
    key_shape = jax.ShapeDtypeStruct((PEER_HEADS, PEER_NKEYS, T), F32)
    return pl.pallas_call(
        _peer_stats_kernel,
        grid=(n_tiles,),
        in_specs=[
            pl.BlockSpec((tt, D_MODEL), lambda i: (i, 0)),
            pl.BlockSpec(wq_t.shape, lambda i: (0, 0)),
            pl.BlockSpec(subkeys.shape, lambda i: (0, 0, 0)),
        ],
        out_specs=[key_spec, key_spec, key_spec, key_spec, pl.BlockSpec((PEER_HEADS, tt), lambda i: (0, i))],
        out_shape=[key_shape, key_shape, key_shape, key_shape, jax.ShapeDtypeStruct((PEER_HEADS, T), F32)],
        scratch_shapes=[
            pltpu.VMEM((PEER_HEADS * PEER_QDIM, tt), F32),
            pltpu.VMEM((PEER_NKEYS, tt), F32),
            pltpu.VMEM((3, PEER_TOPK, tt), F32),
            pltpu.VMEM((n_cand, tt), F32),
        ],
        compiler_params=_cparams(("parallel",)),
        name="peer_stats",
    )(h2, wq_t, subkeys)


def _peer_main_kernel(h2_ref, u_ref, vt_ref, s1_ref, s2_ref, c_ref, e2_ref, tau_ref, x_ref, mod_ref,
                      xo_ref, act_scr, coef_scr, acc_scr):
    j = pl.program_id(1)
    tt = h2_ref.shape[0]

    @pl.when(j == 0)
    def _():
        acc_scr[...] = jnp.zeros_like(acc_scr)

    act_scr[...] = _nt_dot(u_ref[...], h2_ref[...])

    def a_body(al, _):
        a = j * PEER_NA + al
        s1full = [s1_ref[h, pl.ds(a, 1), :] for h in range(PEER_HEADS)]
        cfull = [c_ref[h, pl.ds(a, 1), :] for h in range(PEER_HEADS)]
        for tc in range(tt // LANES):
            ls = slice(tc * LANES, (tc + 1) * LANES)
            taus = [tau_ref[h:h + 1, ls] for h in range(PEER_HEADS)]
            s1rows = [r[:, ls] for r in s1full]
            crows = [r[:, ls] for r in cfull]
            for rb in range(PEER_NKEYS // PEER_RB):
                rs = slice(rb * PEER_RB, (rb + 1) * PEER_RB)
                w = jnp.zeros((PEER_RB, LANES), F32)
                for h in range(PEER_HEADS):
                    ssum = s1rows[h] + s2_ref[h, rs, ls]
                    w = w + jnp.where(ssum >= taus[h], crows[h] * e2_ref[h, rs, ls], 0.0)
                r0 = pl.multiple_of(al * PEER_NKEYS + rb * PEER_RB, PEER_RB)
                act = act_scr[pl.ds(r0, PEER_RB), ls]
                coef_scr[pl.ds(r0, PEER_RB), ls] = (w * _gelu(act)).astype(BF16)
        return 0

    lax.fori_loop(0, PEER_NA, a_body, 0)
    acc_scr[...] += jnp.dot(vt_ref[...], coef_scr[...], preferred_element_type=F32)

    @pl.when(j == pl.num_programs(1) - 1)
    def _():
        xo_ref[...] = x_ref[...] + mod_ref[5:6, :] * acc_scr[...].T


def _peer_main(h2, u, v_t, stats, x, mods_l, dims, n_tiles):
    B, S, L = dims
    T = x.shape[0]
    tt = PEER_TT
    ne = PEER_NA * PEER_NKEYS
    n_lat = B * S // tt
    s_tiles = S // tt
    s1, s2, c, e2, tau = stats

    def mod_idx(i, j):
        return (jnp.where(i < n_lat, i // s_tiles, B), 0, 0)

    key_spec = pl.BlockSpec((PEER_HEADS, PEER_NKEYS, tt), lambda i, j: (0, 0, i))
    return pl.pallas_call(
        _peer_main_kernel,
        grid=(n_tiles, PEER_EXPERTS // ne),
        in_specs=[
            pl.BlockSpec((tt, D_MODEL), lambda i, j: (i, 0)),
            pl.BlockSpec((ne, D_MODEL), lambda i, j: (j, 0)),
            pl.BlockSpec((D_MODEL, ne), lambda i, j: (0, j)),
            key_spec, key_spec, key_spec, key_spec,
            pl.BlockSpec((PEER_HEADS, tt), lambda i, j: (0, i)),
            pl.BlockSpec((tt, D_MODEL), lambda i, j: (i, 0)),
            pl.BlockSpec((None, N_MOD, D_MODEL), mod_idx),
        ],
        out_specs=pl.BlockSpec((tt, D_MODEL), lambda i, j: (i, 0)),
        out_shape=jax.ShapeDtypeStruct((T, D_MODEL), F32),
        scratch_shapes=[
            pltpu.VMEM((ne, tt), F32),
            pltpu.VMEM((ne, tt), BF16),
            pltpu.VMEM((D_MODEL, tt), F32),
        ],
        input_output_aliases={8: 0},
        compiler_params=_cparams(("parallel", "arbitrary")),
        name="peer_main",
    )(h2, u, v_t, s1, s2, c, e2, tau, x, mods_l)


def _final_norm_kernel(x_ref, w_ref, o_ref):
    x = x_ref[...]
    o_ref[...] = x * lax.rsqrt(jnp.mean(x * x, axis=-1, keepdims=True) + EPS) * w_ref[...]


def _final_norm(x, w, n_rows):
    return pl.pallas_call(
        _final_norm_kernel,
        grid=(n_rows // TM,),
        in_specs=[pl.BlockSpec((TM, D_MODEL), lambda g: (g, 0)), pl.BlockSpec((1, D_MODEL), lambda g: (0, 0))],
        out_specs=pl.BlockSpec((TM, D_MODEL), lambda g: (g, 0)),
        out_shape=jax.ShapeDtypeStruct((n_rows, D_MODEL), F32),
        compiler_params=_cparams(("parallel",)),
        name="final_norm",
    )(x, w)


def _lambda_init(layer):
    return 0.8 - 0.6 * math.exp(-0.3 * layer)


def _rope_tables(S):
    rows = S // GRID_W
    row = jnp.repeat(jnp.arange(rows), GRID_W).astype(F32)
    col = jnp.tile(jnp.arange(GRID_W), rows).astype(F32)
    inv = ROPE_THETA ** (-jnp.arange(ROPE_AXIS_FREQS, dtype=F32) / ROPE_AXIS_FREQS)
    ang = jnp.concatenate([row[:, None] * inv, col[:, None] * inv], axis=-1)
    cos, sin = jnp.cos(ang), jnp.sin(ang)
    cos_t = jnp.tile(cos, (1, LANES // ROPE_HALF))
    sin_t = jnp.tile(jnp.concatenate([-sin, sin], axis=-1), (1, LANES // HEAD_DIM))
    cos_t = jnp.concatenate([cos_t, jnp.ones((TM, LANES), F32)], axis=0)
    sin_t = jnp.concatenate([sin_t, jnp.zeros((TM, LANES), F32)], axis=0)
    return cos_t, sin_t


def _block_diag(w):
    nd, nb, bd, _ = w.shape
    eye = jnp.eye(nb, dtype=w.dtype)
    return jnp.einsum("dnij,nm->dnimj", w, eye).reshape(nd, nb * bd, nb * bd)


def kernel(x, c, ctx, c_ctx, w_mod, b_mod, norm_mix, norm_ffn, w_out, attn_w_in, swa_sink, diff_lambda, diff_subln, rec_w_in, pool_w, pool_scale, lru_conv_w, lru_conv_b, lru_wa, lru_ba, lru_wx, lru_bx, lru_lambda, peer_wq, peer_subkeys, peer_u, peer_v, final_norm):
    B, S, D = x.shape
    L = ctx.shape[1]
    depth = w_mod.shape[0]
    dims = (B, S, L)
    assert D == D_MODEL and S % PEER_TT == 0 and (B * L) % PEER_TT == 0 and L % SEQ_T == 0
    assert S % GRID_W == 0 and S >= 3 * BLOCK and B + 1 <= SUBLANES
    lat_rows = B * S
    T = lat_rows + B * L

    xs = jnp.concatenate([x.reshape(lat_rows, D), ctx.reshape(B * L, D)], axis=0)
    cc = jnp.zeros((SUBLANES, D), F32).at[:B].set(c).at[B].set(c_ctx)
    mods = _modulation(cc, w_mod, b_mod)
    cos_t, sin_t = _rope_tables(S)

    for l in range(depth):
        jl = l // 2
        ctx_out = l < depth - 1
        n_rows = T if ctx_out else lat_rows
        mods_l = mods[l]
        if l % 2 == 0:
            q, k, v, dq, dk, dv = _pre_attn(xs, mods_l, norm_mix[l][None], attn_w_in[jl].astype(BF16),
                                            cos_t, sin_t, dims)
            a = _swa(q, k, v, swa_sink[jl], dims)
            bd = _diff_attn(dq, dk, dv, diff_lambda[jl], diff_subln[jl][None], _lambda_init(l), dims)
            xs, h2 = _post_mixer("attn", (a, bd), xs, mods_l, norm_ffn[l][None], w_out[l].astype(BF16),
                                 dims, n_rows // TM)
        else:
            xp, xr, g = _pre_rec(xs, mods_l, norm_mix[l][None], rec_w_in[jl].astype(BF16), dims)
            pool, a_co, b_co = _rec_mid(xp, xr, lru_conv_w[jl], lru_conv_b[jl][None],
                                        _block_diag(lru_wa[jl]).astype(BF16), lru_ba[jl],
                                        _block_diag(lru_wx[jl]).astype(BF16), lru_bx[jl], lru_lambda[jl],
                                        pool_w[jl].astype(BF16), pool_scale[jl][None], dims)
            hs = _scan(a_co, b_co, dims)
            xs, h2 = _post_mixer("rec", (pool, hs, g), xs, mods_l, norm_ffn[l][None], w_out[l].astype(BF16),
                                 dims, n_rows // TM)
        n_peer = n_rows // PEER_TT
        wq_t = peer_wq[l].astype(BF16).T
        sk = peer_subkeys[l].reshape(PEER_HEADS * 2, PEER_NKEYS, PEER_HALF)
        stats = _peer_stats(h2, wq_t, sk, n_peer)
        xs = _peer_main(h2, peer_u[l].astype(BF16), peer_v[l].astype(BF16).T, stats, xs, mods_l, dims, n_peer)

    return _final_norm(xs, final_norm[None], lat_rows).reshape(B, S, D)
```

```python
import functools
import math

import jax
import jax.numpy as jnp
from jax import lax
from jax.experimental import pallas as pl
from jax.experimental.pallas import tpu as pltpu

F32 = jnp.float32
BF16 = jnp.bfloat16
HIGHEST = lax.Precision.HIGHEST

D_MODEL = 1024
HEAD_DIM = 64
EPS = 1e-6
NEG = -1e30
N_MOD = 6
SCALE = HEAD_DIM ** -0.5
ROPE_HALF = HEAD_DIM // 2
ROPE_AXIS_FREQS = ROPE_HALF // 2
ROPE_THETA = 10000.0
GRID_W = 64
SWA_Q_HEADS = 8
SWA_KV_HEADS = 2
SWA_GROUP = SWA_Q_HEADS // SWA_KV_HEADS
WINDOW = 128
BLOCK = 128
SWA_WIDTH = SWA_Q_HEADS * HEAD_DIM
SWA_KV_WIDTH = SWA_KV_HEADS * HEAD_DIM
DIFF_HEADS = 4
DIFF_V_DIM = 2 * HEAD_DIM
DIFF_QK_WIDTH = DIFF_HEADS * 2 * HEAD_DIM
DIFF_WIDTH = DIFF_HEADS * DIFF_V_DIM
POOL_WINDOWS = (2, 4, 8, 16)
POOL_GROUPS = 4
POOL_WIDTH = D_MODEL // 2
POOL_GDIM = POOL_WIDTH // POOL_GROUPS
LRU_WIDTH = D_MODEL // 2
LRU_BLOCKS = 8
CONV_W = 4
CONV_LEFT = CONV_W // 2
LRU_C = 8.0
PEER_HEADS = 8
PEER_NKEYS = 128
PEER_EXPERTS = PEER_NKEYS * PEER_NKEYS
PEER_QDIM = 256
PEER_HALF = PEER_QDIM // 2
PEER_TOPK = 16

LANES = 128
SUBLANES = 8
VMEM_LIMIT = 56 * 1024 * 1024

TM = 256
SEQ_T = 256
DIFF_GROUP = 4
HALO = 8
PEER_TT = 512
PEER_NA = 8
PEER_RB = 64


def _cparams(sem):
    return pltpu.CompilerParams(dimension_semantics=sem, vmem_limit_bytes=VMEM_LIMIT)


def _nt_dot(a, b):
    return lax.dot_general(a, b, (((1,), (1,)), ((), ())), preferred_element_type=F32)


def _rmsnorm_mod(x, w, shift, scale):
    y = x * lax.rsqrt(jnp.mean(x * x, axis=-1, keepdims=True) + EPS) * w
    return y * (1.0 + scale) + shift


def _gelu(x):
    return 0.5 * x * (1.0 + lax.erf(x * (2.0 ** -0.5)))


def _mod_kernel(cc_ref, w_ref, b_ref, o_ref):
    cc = cc_ref[...]
    sc = cc * jax.nn.sigmoid(cc)
    o_ref[...] = jnp.dot(sc, w_ref[...], precision=HIGHEST, preferred_element_type=F32) + b_ref[...]


def _modulation(cc, w_mod, b_mod):
    depth = w_mod.shape[0]
    rows = cc.shape[0]
    out = pl.pallas_call(
        _mod_kernel,
        grid=(depth, N_MOD),
        in_specs=[
            pl.BlockSpec((rows, D_MODEL), lambda l, j: (0, 0)),
            pl.BlockSpec((None, D_MODEL, D_MODEL), lambda l, j: (l, 0, j)),
            pl.BlockSpec((None, 1, D_MODEL), lambda l, j: (l, 0, j)),
        ],
        out_specs=pl.BlockSpec((None, rows, D_MODEL), lambda l, j: (l, 0, j)),
        out_shape=jax.ShapeDtypeStruct((depth, rows, N_MOD * D_MODEL), F32),
        compiler_params=_cparams(("parallel", "parallel")),
        name="modulation",
    )(cc, w_mod, b_mod.reshape(depth, 1, N_MOD * D_MODEL))
    return out.reshape(depth, rows, N_MOD, D_MODEL)


def _rope128(x, cos, sin_signed, first_half):
    partner = jnp.where(first_half, pltpu.roll(x, LANES - ROPE_HALF, 1), pltpu.roll(x, ROPE_HALF, 1))
    return x * cos + partner * sin_signed


def _pre_attn_kernel(x_ref, mod_ref, nw_ref, w_ref, cos_ref, sin_ref,
                     q_ref, k_ref, v_ref, dq_ref, dk_ref, dv_ref):
    h = _rmsnorm_mod(x_ref[...], nw_ref[...], mod_ref[0:1, :], mod_ref[1:2, :])
    p = jnp.dot(h.astype(BF16), w_ref[...], preferred_element_type=F32)
    cos = cos_ref[...]
    sin = sin_ref[...]
    lane = lax.broadcasted_iota(jnp.int32, (1, LANES), 1)
    first_half = (lane % HEAD_DIM) < ROPE_HALF

    def roped(lo, width, scale):
        outs = []
        for c in range(width // LANES):
            xc = p[:, lo + c * LANES: lo + (c + 1) * LANES]
            outs.append((_rope128(xc, cos, sin, first_half) * scale).astype(BF16))
        return outs

    o1 = SWA_WIDTH
    o2 = o1 + SWA_KV_WIDTH
    o3 = o2 + SWA_KV_WIDTH
    o4 = o3 + DIFF_QK_WIDTH
    o5 = o4 + DIFF_QK_WIDTH
    for c, val in enumerate(roped(0, SWA_WIDTH, SCALE)):
        q_ref[:, c * LANES:(c + 1) * LANES] = val
    for c, val in enumerate(roped(o1, SWA_KV_WIDTH, 1.0)):
        k_ref[:, c * LANES:(c + 1) * LANES] = val
    v_ref[...] = p[:, o2:o3].astype(BF16)
    for c, val in enumerate(roped(o3, DIFF_QK_WIDTH, SCALE)):
        dq_ref[:, c * LANES:(c + 1) * LANES] = val
    for c, val in enumerate(roped(o4, DIFF_QK_WIDTH, 1.0)):
        dk_ref[:, c * LANES:(c + 1) * LANES] = val
    dv_ref[...] = p[:, o5:].T.astype(BF16)


def _pre_attn(x, mods_l, norm_w, w_in, cos_t, sin_t, dims):
    B, S, L = dims
    T = x.shape[0]
    n_lat = B * S // TM
    s_tiles = S // TM

    def mod_idx(g):
        return (jnp.where(g < n_lat, g // s_tiles, B), 0, 0)

    def rope_idx(g):
        return (jnp.where(g < n_lat, g % s_tiles, s_tiles), 0)

    widths = (SWA_WIDTH, SWA_KV_WIDTH, SWA_KV_WIDTH, DIFF_QK_WIDTH, DIFF_QK_WIDTH)
    return pl.pallas_call(
        _pre_attn_kernel,
        grid=(T // TM,),
        in_specs=[
            pl.BlockSpec((TM, D_MODEL), lambda g: (g, 0)),
            pl.BlockSpec((None, N_MOD, D_MODEL), mod_idx),
            pl.BlockSpec((1, D_MODEL), lambda g: (0, 0)),
            pl.BlockSpec(w_in.shape, lambda g: (0, 0)),
            pl.BlockSpec((TM, LANES), rope_idx),
            pl.BlockSpec((TM, LANES), rope_idx),
        ],
        out_specs=[pl.BlockSpec((TM, w), lambda g: (g, 0)) for w in widths]
        + [pl.BlockSpec((None, DIFF_WIDTH, TM), lambda g: (g, 0, 0))],
        out_shape=[jax.ShapeDtypeStruct((T, w), BF16) for w in widths]
        + [jax.ShapeDtypeStruct((T // TM, DIFF_WIDTH, TM), BF16)],
        compiler_params=_cparams(("parallel",)),
        name="pre_attn",
    )(x, mods_l, norm_w, w_in, cos_t, sin_t)


def _swa_kernel(q_ref, kl_ref, vl_ref, kc_ref, vc_ref, sink_ref, o_ref, *, n_lat_blocks, seq):
    j = pl.program_id(1)
    is_ctx = j >= n_lat_blocks
    jl = jnp.minimum(j, n_lat_blocks - 1)
    band = 3 * BLOCK
    bs = pl.multiple_of(jnp.clip((jl - 1) * BLOCK, 0, seq - band), BLOCK)
    kb = kl_ref[pl.ds(bs, band), :]
    vb = vl_ref[pl.ds(bs, band), :]
    kc = kc_ref[...]
    vc = vc_ref[...]
    qpos = jl * BLOCK + lax.broadcasted_iota(jnp.int32, (BLOCK, 1), 0)
    kpos = bs + lax.broadcasted_iota(jnp.int32, (1, band), 1)
    valid = jnp.logical_and(jnp.abs(qpos - kpos) <= WINDOW, jnp.logical_not(is_ctx))
    q = q_ref[...]
    for hk in range(SWA_KV_HEADS):
        ks = slice(hk * HEAD_DIM, (hk + 1) * HEAD_DIM)
        for g in range(SWA_GROUP):
            h = hk * SWA_GROUP + g
            qh = q[:, h * HEAD_DIM:(h + 1) * HEAD_DIM]
            sl = jnp.where(valid, _nt_dot(qh, kb[:, ks]), NEG)
            sc = _nt_dot(qh, kc[:, ks])
            sink = sink_ref[h]
            m = jnp.maximum(jnp.maximum(jnp.max(sl, axis=-1, keepdims=True),
                                        jnp.max(sc, axis=-1, keepdims=True)), sink)
            pl_ = jnp.exp(sl - m)
            pc = jnp.exp(sc - m)
            den = (jnp.sum(pl_, axis=-1, keepdims=True) + jnp.sum(pc, axis=-1, keepdims=True)
                   + jnp.exp(sink - m))
            o = (jnp.dot(pl_.astype(BF16), vb[:, ks], preferred_element_type=F32)
                 + jnp.dot(pc.astype(BF16), vc[:, ks], preferred_element_type=F32))
            o_ref[:, h * HEAD_DIM:(h + 1) * HEAD_DIM] = (o / den).astype(BF16)


def _swa(q, k, v, sink, dims):
    B, S, L = dims
    T = q.shape[0]
    n_lat_blocks = S // BLOCK
    n_ctx_blocks = L // BLOCK
    lat_rows = B * S

    def q_idx(b, j):
        return (jnp.where(j < n_lat_blocks, b * n_lat_blocks + j,
                          lat_rows // BLOCK + b * n_ctx_blocks + (j - n_lat_blocks)), 0)

    lat_spec = pl.BlockSpec((S, SWA_KV_WIDTH), lambda b, j: (b, 0))
    ctx_spec = pl.BlockSpec((L, SWA_KV_WIDTH), lambda b, j: (lat_rows // L + b, 0))
    return pl.pallas_call(
        functools.partial(_swa_kernel, n_lat_blocks=n_lat_blocks, seq=S),
        grid=(B, n_lat_blocks + n_ctx_blocks),
        in_specs=[
            pl.BlockSpec((BLOCK, SWA_WIDTH), q_idx),
            lat_spec, lat_spec, ctx_spec, ctx_spec,
            pl.BlockSpec(memory_space=pltpu.SMEM),
        ],
        out_specs=pl.BlockSpec((BLOCK, SWA_WIDTH), q_idx),
        out_shape=jax.ShapeDtypeStruct((T, SWA_WIDTH), BF16),
        compiler_params=_cparams(("parallel", "arbitrary")),
        name="swa",
    )(q, k, v, k, v, sink)


def _diff_kernel(q_ref, kl_ref, vtl_ref, kc_ref, vtc_ref, lam_ref, subln_ref, o_ref, qp_scr, m_scr, l_scr, acc_scr, *,
                 n_lat_tiles, n_lat_chunks, n_ctx_chunks, lam_init):
    j = pl.program_id(1)
    lv = lam_ref[...]
    lam = (jnp.exp(jnp.sum(lv[0:1] * lv[1:2], axis=-1, keepdims=True))
           - jnp.exp(jnp.sum(lv[2:3] * lv[3:4], axis=-1, keepdims=True)) + lam_init)
    lane = lax.broadcasted_iota(jnp.int32, (1, DIFF_V_DIM), 1)
    heads = [slice(h * DIFF_V_DIM, (h + 1) * DIFF_V_DIM) for h in range(DIFF_HEADS)]
    ones = jnp.ones((SUBLANES, SEQ_T), BF16)

    for h, vs in enumerate(heads):
        qh = q_ref[:, vs]
        zero = jnp.zeros_like(qh)
        qp_scr[2 * h] = jnp.where(lane < HEAD_DIM, qh, zero)
        qp_scr[2 * h + 1] = jnp.where(lane >= HEAD_DIM, qh, zero)
    m_scr[...] = jnp.full(m_scr.shape, -jnp.inf, F32)
    l_scr[...] = jnp.zeros(l_scr.shape, F32)
    acc_scr[...] = jnp.zeros(acc_scr.shape, F32)

    def update(blocks_of_head):
        for h, vs in enumerate(heads):
            blocks = blocks_of_head(vs)
            for m in range(2):
                ch = 2 * h + m
                qm = qp_scr[ch]
                m_old = m_scr[ch]
                ss = [_nt_dot(kblk, qm) for kblk, _ in blocks]
                m_blk = jnp.max(ss[0], axis=0, keepdims=True)
                for s in ss[1:]:
                    m_blk = jnp.maximum(m_blk, jnp.max(s, axis=0, keepdims=True))
                m_new = jnp.maximum(m_old, m_blk.astype(BF16).astype(F32))
                m_b = m_new.astype(BF16)
                alpha = jnp.exp(m_old - m_new)
                l_new = alpha * l_scr[ch]
                acc = alpha * acc_scr[ch]
                for s, (_, vtblk) in zip(ss, blocks):
                    p = jnp.exp(s.astype(BF16) - m_b)
                    l_new = l_new + jnp.dot(ones, p, preferred_element_type=F32)[0:1]
                    acc = acc + jnp.dot(vtblk, p, preferred_element_type=F32)
                m_scr[ch] = m_new
                l_scr[ch] = l_new
                acc_scr[ch] = acc

    update(lambda vs: [(kc_ref[c * SEQ_T:(c + 1) * SEQ_T, vs], vtc_ref[c, vs, :]) for c in range(n_ctx_chunks)])

    @pl.when(j < n_lat_tiles)
    def _():
        def body(g, _):
            def blocks_of_head(vs):
                blocks = []
                for i in range(DIFF_GROUP):
                    c = g * DIFF_GROUP + i
                    r0 = pl.multiple_of(c * SEQ_T, SEQ_T)
                    blocks.append((kl_ref[pl.ds(r0, SEQ_T), vs], vtl_ref[c, vs, :]))
                return blocks

            update(blocks_of_head)
            return 0

        lax.fori_loop(0, n_lat_chunks // DIFF_GROUP, body, 0)

    for h, vs in enumerate(heads):
        o = (acc_scr[2 * h] / l_scr[2 * h] - lam * (acc_scr[2 * h + 1] / l_scr[2 * h + 1])).T
        y = o * lax.rsqrt(jnp.mean(o * o, axis=-1, keepdims=True) + EPS) * subln_ref[...]
        o_ref[:, vs] = (y * (1.0 - lam_init)).astype(BF16)


def _diff_attn(dq, dk, dvt, lam_vecs, subln, lam_init, dims):
    B, S, L = dims
    T = dq.shape[0]
    n_lat_tiles = S // SEQ_T
    n_ctx_tiles = L // SEQ_T
    lat_rows = B * S

    def q_idx(b, j):
        return (jnp.where(j < n_lat_tiles, b * n_lat_tiles + j,
                          lat_rows // SEQ_T + b * n_ctx_tiles + (j - n_lat_tiles)), 0)

    return pl.pallas_call(
        functools.partial(_diff_kernel, n_lat_tiles=n_lat_tiles, n_lat_chunks=n_lat_tiles,
                          n_ctx_chunks=n_ctx_tiles, lam_init=lam_init),
        grid=(B, n_lat_tiles + n_ctx_tiles),
        in_specs=[
            pl.BlockSpec((SEQ_T, DIFF_QK_WIDTH), q_idx),
            pl.BlockSpec((S, DIFF_QK_WIDTH), lambda b, j: (b, 0)),
            pl.BlockSpec((n_lat_tiles, DIFF_WIDTH, SEQ_T), lambda b, j: (b, 0, 0)),
            pl.BlockSpec((L, DIFF_QK_WIDTH), lambda b, j: (lat_rows // L + b, 0)),
            pl.BlockSpec((n_ctx_tiles, DIFF_WIDTH, SEQ_T), lambda b, j: (lat_rows // L + b, 0, 0)),
            pl.BlockSpec(lam_vecs.shape, lambda b, j: (0, 0)),
            pl.BlockSpec((1, DIFF_V_DIM), lambda b, j: (0, 0)),
        ],
        out_specs=pl.BlockSpec((SEQ_T, DIFF_WIDTH), q_idx),
        out_shape=jax.ShapeDtypeStruct((T, DIFF_WIDTH), BF16),
        scratch_shapes=[
            pltpu.VMEM((2 * DIFF_HEADS, SEQ_T, DIFF_V_DIM), BF16),
            pltpu.VMEM((2 * DIFF_HEADS, 1, SEQ_T), F32),
            pltpu.VMEM((2 * DIFF_HEADS, 1, SEQ_T), F32),
            pltpu.VMEM((2 * DIFF_HEADS, DIFF_V_DIM, SEQ_T), F32),
        ],
        compiler_params=_cparams(("parallel", "arbitrary")),
        name="diff_attn",
    )(dq, dk, dvt, dk, dvt, lam_vecs, subln)


def _pre_rec_kernel(x_ref, mod_ref, nw_ref, w_ref, xp_ref, xr_ref, g_ref):
    h = _rmsnorm_mod(x_ref[...], nw_ref[...], mod_ref[0:1, :], mod_ref[1:2, :])
    p = jnp.dot(h.astype(BF16), w_ref[...], preferred_element_type=F32)
    xp_ref[...] = p[:, :POOL_WIDTH]
    xr_ref[...] = p[:, POOL_WIDTH:POOL_WIDTH + LRU_WIDTH]
    g_ref[...] = p[:, POOL_WIDTH + LRU_WIDTH:]


def _pre_rec(x, mods_l, norm_w, w_in, dims):
    B, S, L = dims
    T = x.shape[0]
    n_lat = B * S // TM
    s_tiles = S // TM

    def mod_idx(g):
        return (jnp.where(g < n_lat, g // s_tiles, B), 0, 0)

    widths = (POOL_WIDTH, LRU_WIDTH, LRU_WIDTH)
    return pl.pallas_call(
        _pre_rec_kernel,
        grid=(T // TM,),
        in_specs=[
            pl.BlockSpec((TM, D_MODEL), lambda g: (g, 0)),
            pl.BlockSpec((None, N_MOD, D_MODEL), mod_idx),
            pl.BlockSpec((1, D_MODEL), lambda g: (0, 0)),
            pl.BlockSpec(w_in.shape, lambda g: (0, 0)),
        ],
        out_specs=[pl.BlockSpec((TM, w), lambda g: (g, 0)) for w in widths],
        out_shape=[jax.ShapeDtypeStruct((T, w), F32) for w in widths],
        compiler_params=_cparams(("parallel",)),
        name="pre_rec",
    )(x, mods_l, norm_w, w_in)


def _rec_mid_kernel(xp_p, xp_c, xp_n, xr_p, xr_c, xr_n, cw_ref, cb_ref, wa_ref, ba_ref, wx_ref, bx_ref,
                    lam_ref, pw_ref, ps_ref, pool_ref, a_ref, b_ref, *, n_lat_chunks, lat_chunks_per_seq,
                    ctx_chunks_per_seq, lat_len, ctx_len):
    g = pl.program_id(0)
    is_lat = g < n_lat_chunks
    cps = jnp.where(is_lat, lat_chunks_per_seq, ctx_chunks_per_seq)
    within = jnp.where(is_lat, g % lat_chunks_per_seq, (g - n_lat_chunks) % ctx_chunks_per_seq)
    has_prev = within > 0
    has_next = within < cps - 1
    seg_len = jnp.where(is_lat, lat_len, ctx_len)
    ext_rows = SEQ_T + 2 * HALO

    def extended(prev_ref, cur_ref, next_ref):
        prev = jnp.where(has_prev, prev_ref[SEQ_T - HALO:, :], 0.0)
        nxt = jnp.where(has_next, next_ref[:HALO, :], 0.0)
        return jnp.concatenate([prev, cur_ref[...], nxt], axis=0)

    def shifted(ext, off):
        return pltpu.roll(ext, (-off) % ext_rows, 0)[HALO:HALO + SEQ_T, :]

    xp_ext = extended(xp_p, xp_c, xp_n)
    t = within * SEQ_T + lax.broadcasted_iota(jnp.int32, (SEQ_T, 1), 0)
    for gi, w in enumerate(POOL_WINDOWS):
        cols = slice(gi * POOL_GDIM, (gi + 1) * POOL_GDIM)
        eg = xp_ext[:, cols]
        tot = shifted(eg, -(w // 2))
        for off in range(-(w // 2) + 1, w - w // 2):
            tot = tot + shifted(eg, off)
        lo = jnp.clip(t - w // 2, 0, seg_len)
        hi = jnp.clip(t - w // 2 + w, 0, seg_len)
        cnt = (hi - lo).astype(F32)
        d = tot / cnt - xp_c[:, cols]
        y = jnp.dot(d.astype(BF16), pw_ref[gi], preferred_element_type=F32)
        pool_ref[:, cols] = (y * ps_ref[:, cols]).astype(BF16)

    xr_ext = extended(xr_p, xr_c, xr_n)
    u = cb_ref[...] + cw_ref[0:1, :] * shifted(xr_ext, -CONV_LEFT)
    for k in range(1, CONV_W):
        u = u + cw_ref[k:k + 1, :] * shifted(xr_ext, k - CONV_LEFT)
    ub = u.astype(BF16)
    for d in range(2):
        r = jax.nn.sigmoid(jnp.dot(ub, wa_ref[d], preferred_element_type=F32) + ba_ref[d:d + 1, :])
        i = jax.nn.sigmoid(jnp.dot(ub, wx_ref[d], preferred_element_type=F32) + bx_ref[d:d + 1, :])
        nl = -lam_ref[d:d + 1, :]
        softplus = jnp.maximum(nl, 0.0) + jnp.log1p(jnp.exp(-jnp.abs(nl)))
        log_a = -LRU_C * r * softplus
        a_ref[d] = jnp.exp(log_a)
        th = jnp.tanh(log_a)
        b_ref[d] = jnp.sqrt(-2.0 * th / (1.0 - th)) * (i * u)


def _rec_mid(xp, xr, conv_w, conv_b, wa_bd, ba, wx_bd, bx, lam, pool_w, pool_scale, dims):
    B, S, L = dims
    T = xp.shape[0]
    n_chunks = T // SEQ_T

    def cur(g):
        return (g, 0)

    def prv(g):
        return (jnp.maximum(g - 1, 0), 0)

    def nxt(g):
        return (jnp.minimum(g + 1, n_chunks - 1), 0)

    tile = lambda idx: pl.BlockSpec((SEQ_T, LRU_WIDTH), idx)
    full = lambda arr: pl.BlockSpec(arr.shape, lambda g: (0,) * arr.ndim)
    kern = functools.partial(_rec_mid_kernel, n_lat_chunks=B * S // SEQ_T, lat_chunks_per_seq=S // SEQ_T,
                             ctx_chunks_per_seq=L // SEQ_T, lat_len=S, ctx_len=L)
    return pl.pallas_call(
        kern,
        grid=(n_chunks,),
        in_specs=[tile(prv), tile(cur), tile(nxt), tile(prv), tile(cur), tile(nxt),
                  full(conv_w), full(conv_b), full(wa_bd), full(ba), full(wx_bd), full(bx), full(lam),
                  full(pool_w), full(pool_scale)],
        out_specs=[pl.BlockSpec((SEQ_T, POOL_WIDTH), cur),
                   pl.BlockSpec((2, SEQ_T, LRU_WIDTH), lambda g: (0, g, 0)),
                   pl.BlockSpec((2, SEQ_T, LRU_WIDTH), lambda g: (0, g, 0))],
        out_shape=[jax.ShapeDtypeStruct((T, POOL_WIDTH), BF16),
                   jax.ShapeDtypeStruct((2, T, LRU_WIDTH), F32),
                   jax.ShapeDtypeStruct((2, T, LRU_WIDTH), F32)],
        compiler_params=_cparams(("parallel",)),
        name="rec_mid",
    )(xp, xp, xp, xr, xr, xr, conv_w, conv_b, wa_bd, ba, wx_bd, bx, lam, pool_w, pool_scale)


def _scan_kernel(a_ref, b_ref, h_ref, carry_ref):
    d = pl.program_id(1)
    s = pl.program_id(2)

    @pl.when(s == 0)
    def _():
        carry_ref[...] = jnp.zeros_like(carry_ref)

    def run(reverse):
        def body(i, h):
            t = (SEQ_T - 1 - i) if reverse else i
            h = a_ref[pl.ds(t, 1), :] * h + b_ref[pl.ds(t, 1), :]
            h_ref[pl.ds(t, 1), :] = h
            return h

        carry_ref[...] = lax.fori_loop(0, SEQ_T, body, carry_ref[...], unroll=8)

    @pl.when(d == 0)
    def _():
        run(False)

    @pl.when(d == 1)
    def _():
        run(True)


def _scan(a, b, dims):
    B, S, L = dims
    T = a.shape[1]
    lat = S // SEQ_T
    ctx = L // SEQ_T
    lat_base = 0
    ctx_base = B * S // SEQ_T

    def idx(bi, d, s):
        fwd = jnp.where(s < ctx, ctx_base + bi * ctx + s, lat_base + bi * lat + (s - ctx))
        rev = jnp.where(s < ctx, ctx_base + bi * ctx + (ctx - 1 - s), lat_base + bi * lat + (lat - 1 - (s - ctx)))
        return (d, jnp.where(d == 0, fwd, rev), 0)

    spec = pl.BlockSpec((None, SEQ_T, LRU_WIDTH), idx)
    return pl.pallas_call(
        _scan_kernel,
        grid=(B, 2, lat + ctx),
        in_specs=[spec, spec],
        out_specs=spec,
        out_shape=jax.ShapeDtypeStruct((2, T, LRU_WIDTH), F32),
        scratch_shapes=[pltpu.VMEM((1, LRU_WIDTH), F32)],
        compiler_params=_cparams(("parallel", "parallel", "arbitrary")),
        name="lru_scan",
    )(a, b)


def _post_common(y, x_ref, mod_ref, nw_ref, xo_ref, h2_ref):
    x_new = x_ref[...] + mod_ref[2:3, :] * y
    xo_ref[...] = x_new
    h2_ref[...] = _rmsnorm_mod(x_new, nw_ref[...], mod_ref[3:4, :], mod_ref[4:5, :]).astype(BF16)


def _post_attn_kernel(a_ref, b_ref, x_ref, mod_ref, nw_ref, w_ref, xo_ref, h2_ref):
    half = a_ref.shape[1]
    y = (jnp.dot(a_ref[...], w_ref[:half, :], preferred_element_type=F32)
         + jnp.dot(b_ref[...], w_ref[half:, :], preferred_element_type=F32))
    _post_common(y, x_ref, mod_ref, nw_ref, xo_ref, h2_ref)


def _post_rec_kernel(pool_ref, hs_ref, g_ref, x_ref, mod_ref, nw_ref, w_ref, xo_ref, h2_ref):
    half = pool_ref.shape[1]
    rec = ((hs_ref[0] + hs_ref[1]) * _gelu(g_ref[...])).astype(BF16)
    y = (jnp.dot(pool_ref[...], w_ref[:half, :], preferred_element_type=F32)
         + jnp.dot(rec, w_ref[half:, :], preferred_element_type=F32))
    _post_common(y, x_ref, mod_ref, nw_ref, xo_ref, h2_ref)


def _post_mixer(kind, parts, x, mods_l, norm_w, w_out, dims, n_tiles):
    B, S, L = dims
    T = x.shape[0]
    n_lat = B * S // TM
    s_tiles = S // TM

    def mod_idx(g):
        return (jnp.where(g < n_lat, g // s_tiles, B), 0, 0)

    row = lambda w: pl.BlockSpec((TM, w), lambda g: (g, 0))
    if kind == "attn":
        kern = _post_attn_kernel
        part_specs = [row(SWA_WIDTH), row(DIFF_WIDTH)]
    else:
        kern = _post_rec_kernel
        part_specs = [row(POOL_WIDTH), pl.BlockSpec((2, TM, LRU_WIDTH), lambda g: (0, g, 0)), row(LRU_WIDTH)]
    n_in = len(parts)
    return pl.pallas_call(
        kern,
        grid=(n_tiles,),
        in_specs=part_specs + [
            row(D_MODEL),
            pl.BlockSpec((None, N_MOD, D_MODEL), mod_idx),
            pl.BlockSpec((1, D_MODEL), lambda g: (0, 0)),
            pl.BlockSpec(w_out.shape, lambda g: (0, 0)),
        ],
        out_specs=[row(D_MODEL), row(D_MODEL)],
        out_shape=[jax.ShapeDtypeStruct((T, D_MODEL), F32), jax.ShapeDtypeStruct((T, D_MODEL), BF16)],
        input_output_aliases={n_in: 0},
        compiler_params=_cparams(("parallel",)),
        name="post_" + kind,
    )(*parts, x, mods_l, norm_w, w_out)


def _peer_cand_blocks():
    return [(i, PEER_TOPK // (i + 1)) for i in range(1, SUBLANES)]


def _peer_routing(h2_ref, wq_ref, sk_ref, n_ref, c_ref, r2_ref, e2_ref,
                  q_scr, s_scr, cur_scr, rank_scr, top_scr, cand_scr):
    q_scr[...] = _nt_dot(wq_ref[...], h2_ref[...])
    neg_inf = -jnp.inf
    n_cand = cand_scr.shape[0]

    def extract(src_ref, dst_ref, count, rank_ref=None):
        def body(k, _):
            cur = src_ref[...]
            m = jnp.max(cur, axis=0, keepdims=True)
            dst_ref[pl.ds(k, 1), :] = m
            hit = cur == m
            src_ref[...] = jnp.where(hit, neg_inf, cur)
            if rank_ref is not None:
                rank_ref[...] = jnp.where(hit, (k + 1).astype(F32), rank_ref[...])
            return 0

        lax.fori_loop(0, count, body, 0)

    def head_body(h, _):
        for p in range(2):
            r0 = pl.multiple_of((h * 2 + p) * PEER_HALF, PEER_HALF)
            s = jnp.dot(sk_ref[h * 2 + p], q_scr[pl.ds(r0, PEER_HALF), :], precision=HIGHEST,
                        preferred_element_type=F32)
            s_scr[p] = s
            cur_scr[...] = s
            if p == 1:
                rank_scr[...] = jnp.full(rank_scr.shape, PEER_TOPK + 1.0, F32)
            extract(cur_scr, top_scr.at[p], PEER_TOPK, rank_scr if p == 1 else None)
        v1 = top_scr[0]
        v2 = top_scr[1]
        cand_scr[0:PEER_TOPK, :] = v1[0:1, :] + v2
        row = lax.broadcasted_iota(jnp.int32, (SUBLANES, 1), 0)
        for i, n_i in _peer_cand_blocks():
            blk = jnp.where(row < n_i, v1[i:i + 1, :] + v2[0:SUBLANES, :], neg_inf)
            cand_scr[PEER_TOPK + (i - 1) * SUBLANES: PEER_TOPK + i * SUBLANES, :] = blk
        cand_scr[n_cand - SUBLANES:, :] = v1[SUBLANES:, :] + v2[0:1, :]
        cand = cand_scr[...]
        extract(cand_scr, top_scr.at[2], PEER_TOPK)
        tau = top_scr[2, PEER_TOPK - 1:PEER_TOPK, :]
        top = v1[0:1, :] + v2[0:1, :]
        z = jnp.sum(jnp.where(cand >= tau, jnp.exp(cand - top), 0.0), axis=0, keepdims=True)

        s1 = s_scr[0]
        n_ref[h] = jnp.zeros(s1.shape, F32)

        def count_body(jj, _):
            n_ref[h] += jnp.where(s1 + top_scr[1, pl.ds(jj, 1), :] >= tau, 1.0, 0.0)
            return 0

        lax.fori_loop(0, PEER_TOPK, count_body, 0)
        c_ref[h] = jnp.exp(s1 - v1[0:1, :]) / z
        r2_ref[h] = rank_scr[...].astype(BF16)
        e2_ref[h] = jnp.exp(s_scr[1] - v2[0:1, :]).astype(BF16)
        return 0

    lax.fori_loop(0, PEER_HEADS, head_body, 0)


def _peer_kernel(h2_ref, u_ref, vt_ref, wq_ref, sk_ref, x_ref, mod_ref, xo_ref,
                 h2t_scr, coef_scr, acc_scr, n_scr, c_scr, r2_scr, e2_scr,
                 q_scr, s_scr, cur_scr, rank_scr, top_scr, cand_scr):
    j = pl.program_id(1)
    tt = h2_ref.shape[0]
    pair = 2 * PEER_NKEYS
    pack_rows = 2 * SUBLANES

    @pl.when(j == 0)
    def _():
        acc_scr[...] = jnp.zeros_like(acc_scr)
        h2t_scr[...] = h2_ref[...].T
        _peer_routing(h2_ref, wq_ref, sk_ref, n_scr, c_scr, r2_scr, e2_scr,
                      q_scr, s_scr, cur_scr, rank_scr, top_scr, cand_scr)

    def bf16_rows(row):
        packed = jnp.broadcast_to(row, (pack_rows, LANES)).astype(BF16)
        return jnp.concatenate([packed] * (PEER_RB // pack_rows), axis=0)

    for pr in range(PEER_NA // 2):
        act = jnp.dot(u_ref[pr * pair:(pr + 1) * pair, :], h2t_scr[...], preferred_element_type=F32)
        for half in range(2):
            al = 2 * pr + half
            a = j * PEER_NA + al
            n_full = [n_scr[h, pl.ds(a, 1), :] for h in range(PEER_HEADS)]
            c_full = [c_scr[h, pl.ds(a, 1), :] for h in range(PEER_HEADS)]
            for tc in range(tt // LANES):
                ls = slice(tc * LANES, (tc + 1) * LANES)
                n_rows = [bf16_rows(r[:, ls]) for r in n_full]
                c_rows = [bf16_rows(r[:, ls]) for r in c_full]
                for rb in range(PEER_NKEYS // PEER_RB):
                    rs = slice(rb * PEER_RB, (rb + 1) * PEER_RB)
                    w = jnp.zeros((PEER_RB, LANES), BF16)
                    for h in range(PEER_HEADS):
                        hit = r2_scr[h, rs, ls] <= n_rows[h]
                        w = w + jnp.where(hit, c_rows[h] * e2_scr[h, rs, ls], jnp.zeros_like(w))
                    rows = slice(half * PEER_NKEYS + rb * PEER_RB, half * PEER_NKEYS + (rb + 1) * PEER_RB)
                    g = _gelu(act[rows, ls]).astype(BF16)
                    r0 = al * PEER_NKEYS + rb * PEER_RB
                    coef_scr[r0:r0 + PEER_RB, ls] = w * g

    acc_scr[...] += jnp.dot(vt_ref[...], coef_scr[...], preferred_element_type=F32)

    @pl.when(j == pl.num_programs(1) - 1)
    def _():
        xo_ref[...] = x_ref[...] + mod_ref[5:6, :] * acc_scr[...].T


def _peer(h2, u, v_t, wq_t, subkeys, x, mods_l, dims, n_tiles):
    B, S, L = dims
    T = x.shape[0]
    tt = PEER_TT
    ne = PEER_NA * PEER_NKEYS
    n_lat = B * S // tt
    s_tiles = S // tt
    n_cand = PEER_TOPK + SUBLANES * SUBLANES

    def mod_idx(i, j):
        return (jnp.where(i < n_lat, i // s_tiles, B), 0, 0)

    key_f32 = pltpu.VMEM((PEER_HEADS, PEER_NKEYS, tt), F32)
    key_bf16 = pltpu.VMEM((PEER_HEADS, PEER_NKEYS, tt), BF16)
    return pl.pallas_call(
        _peer_kernel,
        grid=(n_tiles, PEER_EXPERTS // ne),
        in_specs=[
            pl.BlockSpec((tt, D_MODEL), lambda i, j: (i, 0)),
            pl.BlockSpec((ne, D_MODEL), lambda i, j: (j, 0)),
            pl.BlockSpec((D_MODEL, ne), lambda i, j: (0, j)),
            pl.BlockSpec(wq_t.shape, lambda i, j: (0, 0)),
            pl.BlockSpec(subkeys.shape, lambda i, j: (0, 0, 0)),
            pl.BlockSpec((tt, D_MODEL), lambda i, j: (i, 0)),
            pl.BlockSpec((None, N_MOD, D_MODEL), mod_idx),
        ],
        out_specs=pl.BlockSpec((tt, D_MODEL), lambda i, j: (i, 0)),
        out_shape=jax.ShapeDtypeStruct((T, D_MODEL), F32),
        scratch_shapes=[
            pltpu.VMEM((D_MODEL, tt), BF16),
            pltpu.VMEM((ne, tt), BF16),
            pltpu.VMEM((D_MODEL, tt), F32),
            key_f32, key_f32, key_bf16, key_bf16,
            pltpu.VMEM((PEER_HEADS * PEER_QDIM, tt), F32),
            pltpu.VMEM((2, PEER_NKEYS, tt), F32),
            pltpu.VMEM((PEER_NKEYS, tt), F32),
            pltpu.VMEM((PEER_NKEYS, tt), F32),
            pltpu.VMEM((3, PEER_TOPK, tt), F32),
            pltpu.VMEM((n_cand, tt), F32),
        ],
        input_output_aliases={5: 0},
        compiler_params=_cparams(("parallel", "arbitrary")),
        name="peer",
    )(h2, u, v_t, wq_t, subkeys, x, mods_l)


def _final_norm_kernel(x_ref, w_ref, o_ref):
    x = x_ref[...]
    o_ref[...] = x * lax.rsqrt(jnp.mean(x * x, axis=-1, keepdims=True) + EPS) * w_ref[...]


def _final_norm(x, w, n_rows):
    return pl.pallas_call(
        _final_norm_kernel,
        grid=(n_rows // TM,),
        in_specs=[pl.BlockSpec((TM, D_MODEL), lambda g: (g, 0)), pl.BlockSpec((1, D_MODEL), lambda g: (0, 0))],
        out_specs=pl.BlockSpec((TM, D_MODEL), lambda g: (g, 0)),
        out_shape=jax.ShapeDtypeStruct((n_rows, D_MODEL), F32),
        compiler_params=_cparams(("parallel",)),
        name="final_norm",
    )(x, w)


def _lambda_init(layer):
    return 0.8 - 0.6 * math.exp(-0.3 * layer)


def _rope_tables(S):
    rows = S // GRID_W
    row = jnp.repeat(jnp.arange(rows), GRID_W).astype(F32)
    col = jnp.tile(jnp.arange(GRID_W), rows).astype(F32)
    inv = ROPE_THETA ** (-jnp.arange(ROPE_AXIS_FREQS, dtype=F32) / ROPE_AXIS_FREQS)
    ang = jnp.concatenate([row[:, None] * inv, col[:, None] * inv], axis=-1)
    cos, sin = jnp.cos(ang), jnp.sin(ang)
    cos_t = jnp.tile(cos, (1, LANES // ROPE_HALF))
    sin_t = jnp.tile(jnp.concatenate([-sin, sin], axis=-1), (1, LANES // HEAD_DIM))
    cos_t = jnp.concatenate([cos_t, jnp.ones((TM, LANES), F32)], axis=0)
    sin_t = jnp.concatenate([sin_t, jnp.zeros((TM, LANES), F32)], axis=0)
    return cos_t, sin_t


def _block_diag(w):
    nd, nb, bd, _ = w.shape
    eye = jnp.eye(nb, dtype=w.dtype)
    return jnp.einsum("dnij,nm->dnimj", w, eye).reshape(nd, nb * bd, nb * bd)


def kernel(x, c, ctx, c_ctx, w_mod, b_mod, norm_mix, norm_ffn, w_out, attn_w_in, swa_sink, diff_lambda, diff_subln, rec_w_in, pool_w, pool_scale, lru_conv_w, lru_conv_b, lru_wa, lru_ba, lru_wx, lru_bx, lru_lambda, peer_wq, peer_subkeys, peer_u, peer_v, final_norm):
    B, S, D = x.shape
    L = ctx.shape[1]
    depth = w_mod.shape[0]
    dims = (B, S, L)
    assert D == D_MODEL and S % PEER_TT == 0 and (B * L) % PEER_TT == 0 and L % SEQ_T == 0
    assert S % GRID_W == 0 and S >= 3 * BLOCK and B + 1 <= SUBLANES and S % (DIFF_GROUP * SEQ_T) == 0
    lat_rows = B * S
    T = lat_rows + B * L

    xs = jnp.concatenate([x.reshape(lat_rows, D), ctx.reshape(B * L, D)], axis=0)
    cc = jnp.zeros((SUBLANES, D), F32).at[:B].set(c).at[B].set(c_ctx)
    mods = _modulation(cc, w_mod, b_mod)
    cos_t, sin_t = _rope_tables(S)

    for l in range(depth):
        jl = l // 2
        ctx_out = l < depth - 1
        n_rows = T if ctx_out else lat_rows
        mods_l = mods[l]
        if l % 2 == 0:
            q, k, v, dq, dk, dv = _pre_attn(xs, mods_l, norm_mix[l][None], attn_w_in[jl].astype(BF16),
                                            cos_t, sin_t, dims)
            a = _swa(q, k, v, swa_sink[jl], dims)
            bd = _diff_attn(dq, dk, dv, diff_lambda[jl], diff_subln[jl][None], _lambda_init(l), dims)
            xs, h2 = _post_mixer("attn", (a, bd), xs, mods_l, norm_ffn[l][None], w_out[l].astype(BF16),
                                 dims, n_rows // TM)
        else:
            xp, xr, g = _pre_rec(xs, mods_l, norm_mix[l][None], rec_w_in[jl].astype(BF16), dims)
            pool, a_co, b_co = _rec_mid(xp, xr, lru_conv_w[jl], lru_conv_b[jl][None],
                                        _block_diag(lru_wa[jl]).astype(BF16), lru_ba[jl],
                                        _block_diag(lru_wx[jl]).astype(BF16), lru_bx[jl], lru_lambda[jl],
                                        pool_w[jl].astype(BF16), pool_scale[jl][None], dims)
            hs = _scan(a_co, b_co, dims)
            xs, h2 = _post_mixer("rec", (pool, hs, g), xs, mods_l, norm_ffn[l][None], w_out[l].astype(BF16),
                                 dims, n_rows // TM)
        n_peer = n_rows // PEER_TT
        wq_t = peer_wq[l].astype(BF16).T
        sk = peer_subkeys[l].reshape(PEER_HEADS * 2, PEER_NKEYS, PEER_HALF)
        xs = _peer(h2, peer_u[l].astype(BF16), peer_v[l].astype(BF16).T, wq_t, sk, xs, mods_l, dims, n_peer)

    return _final_norm(xs, final_norm[None], lat_rows).reshape(B, S, D)
```

```python
import functools
import math

import jax
import jax.numpy as jnp
from jax import lax
from jax.experimental import pallas as pl
from jax.experimental.pallas import tpu as pltpu

F32 = jnp.float32
BF16 = jnp.bfloat16
HIGHEST = lax.Precision.HIGHEST

D_MODEL = 1024
HEAD_DIM = 64
EPS = 1e-6
NEG = -1e30
N_MOD = 6
SCALE = HEAD_DIM ** -0.5
ROPE_HALF = HEAD_DIM // 2
ROPE_AXIS_FREQS = ROPE_HALF // 2
ROPE_THETA = 10000.0
GRID_W = 64
SWA_Q_HEADS = 8
SWA_KV_HEADS = 2
SWA_GROUP = SWA_Q_HEADS // SWA_KV_HEADS
WINDOW = 128
BLOCK = 128
SWA_WIDTH = SWA_Q_HEADS * HEAD_DIM
SWA_KV_WIDTH = SWA_KV_HEADS * HEAD_DIM
DIFF_HEADS = 4
DIFF_V_DIM = 2 * HEAD_DIM
DIFF_QK_WIDTH = DIFF_HEADS * 2 * HEAD_DIM
DIFF_WIDTH = DIFF_HEADS * DIFF_V_DIM
POOL_WINDOWS = (2, 4, 8, 16)
POOL_GROUPS = 4
POOL_WIDTH = D_MODEL // 2
POOL_GDIM = POOL_WIDTH // POOL_GROUPS
LRU_WIDTH = D_MODEL // 2
LRU_BLOCKS = 8
CONV_W = 4
CONV_LEFT = CONV_W // 2
LRU_C = 8.0
PEER_HEADS = 8
PEER_NKEYS = 128
PEER_EXPERTS = PEER_NKEYS * PEER_NKEYS
PEER_QDIM = 256
PEER_HALF = PEER_QDIM // 2
PEER_TOPK = 16

LANES = 128
SUBLANES = 8
VMEM_LIMIT = 56 * 1024 * 1024

TM = 256
SEQ_T = 256
DIFF_GROUP = 4
DIFF_AHEAD = 2
HALO = 8
PEER_TT = 512
PEER_NA = 8
PEER_RB = 64


def _cparams(sem):
    return pltpu.CompilerParams(dimension_semantics=sem, vmem_limit_bytes=VMEM_LIMIT)


def _nt_dot(a, b):
    return lax.dot_general(a, b, (((1,), (1,)), ((), ())), preferred_element_type=F32)


def _rmsnorm_mod(x, w, shift, scale):
    y = x * lax.rsqrt(jnp.mean(x * x, axis=-1, keepdims=True) + EPS) * w
    return y * (1.0 + scale) + shift


def _gelu(x):
    return 0.5 * x * (1.0 + lax.erf(x * (2.0 ** -0.5)))


def _mod_kernel(cc_ref, w_ref, b_ref, o_ref):
    cc = cc_ref[...]
    sc = cc * jax.nn.sigmoid(cc)
    o_ref[...] = jnp.dot(sc, w_ref[...], precision=HIGHEST, preferred_element_type=F32) + b_ref[...]


def _modulation(cc, w_mod, b_mod):
    depth = w_mod.shape[0]
    rows = cc.shape[0]
    out = pl.pallas_call(
        _mod_kernel,
        grid=(depth, N_MOD),
        in_specs=[
            pl.BlockSpec((rows, D_MODEL), lambda l, j: (0, 0)),
            pl.BlockSpec((None, D_MODEL, D_MODEL), lambda l, j: (l, 0, j)),
            pl.BlockSpec((None, 1, D_MODEL), lambda l, j: (l, 0, j)),
        ],
        out_specs=pl.BlockSpec((None, rows, D_MODEL), lambda l, j: (l, 0, j)),
        out_shape=jax.ShapeDtypeStruct((depth, rows, N_MOD * D_MODEL), F32),
        compiler_params=_cparams(("parallel", "parallel")),
        name="modulation",
    )(cc, w_mod, b_mod.reshape(depth, 1, N_MOD * D_MODEL))
    return out.reshape(depth, rows, N_MOD, D_MODEL)


def _rope128(x, cos, sin_signed, first_half):
    partner = jnp.where(first_half, pltpu.roll(x, LANES - ROPE_HALF, 1), pltpu.roll(x, ROPE_HALF, 1))
    return x * cos + partner * sin_signed


def _pre_attn_kernel(x_ref, mod_ref, nw_ref, w_ref, cos_ref, sin_ref,
                     q_ref, k_ref, v_ref, dq_ref, dk_ref, dv_ref):
    h = _rmsnorm_mod(x_ref[...], nw_ref[...], mod_ref[0:1, :], mod_ref[1:2, :])
    p = jnp.dot(h.astype(BF16), w_ref[...], preferred_element_type=F32)
    cos = cos_ref[...]
    sin = sin_ref[...]
    lane = lax.broadcasted_iota(jnp.int32, (1, LANES), 1)
    first_half = (lane % HEAD_DIM) < ROPE_HALF

    def roped(lo, width, scale):
        outs = []
        for c in range(width // LANES):
            xc = p[:, lo + c * LANES: lo + (c + 1) * LANES]
            outs.append((_rope128(xc, cos, sin, first_half) * scale).astype(BF16))
        return outs

    o1 = SWA_WIDTH
    o2 = o1 + SWA_KV_WIDTH
    o3 = o2 + SWA_KV_WIDTH
    o4 = o3 + DIFF_QK_WIDTH
    o5 = o4 + DIFF_QK_WIDTH
    for c, val in enumerate(roped(0, SWA_WIDTH, SCALE)):
        q_ref[:, c * LANES:(c + 1) * LANES] = val
    for c, val in enumerate(roped(o1, SWA_KV_WIDTH, 1.0)):
        k_ref[:, c * LANES:(c + 1) * LANES] = val
    v_ref[...] = p[:, o2:o3].astype(BF16)
    for c, val in enumerate(roped(o3, DIFF_QK_WIDTH, SCALE)):
        dq_ref[:, c * LANES:(c + 1) * LANES] = val
    for c, val in enumerate(roped(o4, DIFF_QK_WIDTH, 1.0)):
        dk_ref[:, c * LANES:(c + 1) * LANES] = val
    dv_ref[...] = p[:, o5:].T.astype(BF16)


def _pre_attn(x, mods_l, norm_w, w_in, cos_t, sin_t, dims):
    B, S, L = dims
    T = x.shape[0]
    n_lat = B * S // TM
    s_tiles = S // TM

    def mod_idx(g):
        return (jnp.where(g < n_lat, g // s_tiles, B), 0, 0)

    def rope_idx(g):
        return (jnp.where(g < n_lat, g % s_tiles, s_tiles), 0)

    widths = (SWA_WIDTH, SWA_KV_WIDTH, SWA_KV_WIDTH, DIFF_QK_WIDTH, DIFF_QK_WIDTH)
    return pl.pallas_call(
        _pre_attn_kernel,
        grid=(T // TM,),
        in_specs=[
            pl.BlockSpec((TM, D_MODEL), lambda g: (g, 0)),
            pl.BlockSpec((None, N_MOD, D_MODEL), mod_idx),
            pl.BlockSpec((1, D_MODEL), lambda g: (0, 0)),
            pl.BlockSpec(w_in.shape, lambda g: (0, 0)),
            pl.BlockSpec((TM, LANES), rope_idx),
            pl.BlockSpec((TM, LANES), rope_idx),
        ],
        out_specs=[pl.BlockSpec((TM, w), lambda g: (g, 0)) for w in widths]
        + [pl.BlockSpec((None, DIFF_WIDTH, TM), lambda g: (g, 0, 0))],
        out_shape=[jax.ShapeDtypeStruct((T, w), BF16) for w in widths]
        + [jax.ShapeDtypeStruct((T // TM, DIFF_WIDTH, TM), BF16)],
        compiler_params=_cparams(("parallel",)),
        name="pre_attn",
    )(x, mods_l, norm_w, w_in, cos_t, sin_t)


def _swa_kernel(q_ref, kl_ref, vl_ref, kc_ref, vc_ref, sink_ref, o_ref, *, n_lat_blocks, seq):
    j = pl.program_id(1)
    is_ctx = j >= n_lat_blocks
    jl = jnp.minimum(j, n_lat_blocks - 1)
    band = 3 * BLOCK
    bs = pl.multiple_of(jnp.clip((jl - 1) * BLOCK, 0, seq - band), BLOCK)
    kb = kl_ref[pl.ds(bs, band), :]
    vb = vl_ref[pl.ds(bs, band), :]
    kc = kc_ref[...]
    vc = vc_ref[...]
    qpos = jl * BLOCK + lax.broadcasted_iota(jnp.int32, (BLOCK, 1), 0)
    kpos = bs + lax.broadcasted_iota(jnp.int32, (1, band), 1)
    valid = jnp.logical_and(jnp.abs(qpos - kpos) <= WINDOW, jnp.logical_not(is_ctx))
    q = q_ref[...]
    for hk in range(SWA_KV_HEADS):
        ks = slice(hk * HEAD_DIM, (hk + 1) * HEAD_DIM)
        for g in range(SWA_GROUP):
            h = hk * SWA_GROUP + g
            qh = q[:, h * HEAD_DIM:(h + 1) * HEAD_DIM]
            sl = jnp.where(valid, _nt_dot(qh, kb[:, ks]), NEG)
            sc = _nt_dot(qh, kc[:, ks])
            sink = sink_ref[h]
            m = jnp.maximum(jnp.maximum(jnp.max(sl, axis=-1, keepdims=True),
                                        jnp.max(sc, axis=-1, keepdims=True)), sink)
            pl_ = jnp.exp(sl - m)
            pc = jnp.exp(sc - m)
            den = (jnp.sum(pl_, axis=-1, keepdims=True) + jnp.sum(pc, axis=-1, keepdims=True)
                   + jnp.exp(sink - m))
            o = (jnp.dot(pl_.astype(BF16), vb[:, ks], preferred_element_type=F32)
                 + jnp.dot(pc.astype(BF16), vc[:, ks], preferred_element_type=F32))
            o_ref[:, h * HEAD_DIM:(h + 1) * HEAD_DIM] = (o / den).astype(BF16)


def _swa(q, k, v, sink, dims):
    B, S, L = dims
    T = q.shape[0]
    n_lat_blocks = S // BLOCK
    n_ctx_blocks = L // BLOCK
    lat_rows = B * S

    def q_idx(b, j):
        return (jnp.where(j < n_lat_blocks, b * n_lat_blocks + j,
                          lat_rows // BLOCK + b * n_ctx_blocks + (j - n_lat_blocks)), 0)

    lat_spec = pl.BlockSpec((S, SWA_KV_WIDTH), lambda b, j: (b, 0))
    ctx_spec = pl.BlockSpec((L, SWA_KV_WIDTH), lambda b, j: (lat_rows // L + b, 0))
    return pl.pallas_call(
        functools.partial(_swa_kernel, n_lat_blocks=n_lat_blocks, seq=S),
        grid=(B, n_lat_blocks + n_ctx_blocks),
        in_specs=[
            pl.BlockSpec((BLOCK, SWA_WIDTH), q_idx),
            lat_spec, lat_spec, ctx_spec, ctx_spec,
            pl.BlockSpec(memory_space=pltpu.SMEM),
        ],
        out_specs=pl.BlockSpec((BLOCK, SWA_WIDTH), q_idx),
        out_shape=jax.ShapeDtypeStruct((T, SWA_WIDTH), BF16),
        compiler_params=_cparams(("parallel", "arbitrary")),
        name="swa",
    )(q, k, v, k, v, sink)


def _diff_kernel(q_ref, kl_ref, vtl_ref, kc_ref, vtc_ref, lam_ref, subln_ref, o_ref, qp_scr, m_scr, l_scr, acc_scr, *,
                 n_lat_tiles, n_lat_chunks, n_ctx_chunks, lam_init):
    j = pl.program_id(1)
    lv = lam_ref[...]
    lam = (jnp.exp(jnp.sum(lv[0:1] * lv[1:2], axis=-1, keepdims=True))
           - jnp.exp(jnp.sum(lv[2:3] * lv[3:4], axis=-1, keepdims=True)) + lam_init)
    lane = lax.broadcasted_iota(jnp.int32, (1, DIFF_V_DIM), 1)
    heads = [slice(h * DIFF_V_DIM, (h + 1) * DIFF_V_DIM) for h in range(DIFF_HEADS)]
    ones = jnp.ones((SUBLANES, SEQ_T), BF16)

    for h, vs in enumerate(heads):
        qh = q_ref[:, vs]
        zero = jnp.zeros_like(qh)
        qp_scr[2 * h] = jnp.where(lane < HEAD_DIM, qh, zero)
        qp_scr[2 * h + 1] = jnp.where(lane >= HEAD_DIM, qh, zero)
    m_scr[...] = jnp.full(m_scr.shape, -jnp.inf, F32)
    l_scr[...] = jnp.zeros(l_scr.shape, F32)
    acc_scr[...] = jnp.zeros(acc_scr.shape, F32)

    def update(blocks_of_head):
        all_blocks = [blocks_of_head(vs) for vs in heads]

        def scores(ch):
            return [_nt_dot(kblk, qp_scr[ch]) for kblk, _ in all_blocks[ch // 2]]

        n_chains = 2 * DIFF_HEADS
        queue = [scores(ch) for ch in range(DIFF_AHEAD)]
        for h, vs in enumerate(heads):
            blocks = all_blocks[h]
            for m in range(2):
                ch = 2 * h + m
                ss = queue.pop(0)
                if ch + DIFF_AHEAD < n_chains:
                    queue.append(scores(ch + DIFF_AHEAD))
                m_old = m_scr[ch]
                m_blk = jnp.max(ss[0], axis=0, keepdims=True)
                for s in ss[1:]:
                    m_blk = jnp.maximum(m_blk, jnp.max(s, axis=0, keepdims=True))
                m_new = jnp.maximum(m_old, m_blk.astype(BF16).astype(F32))
                m_b = m_new.astype(BF16)
                alpha = jnp.exp(m_old - m_new)
                l_new = alpha * l_scr[ch]
                acc = alpha * acc_scr[ch]
                for s, (_, vtblk) in zip(ss, blocks):
                    p = jnp.exp(s.astype(BF16) - m_b)
                    l_new = l_new + jnp.dot(ones, p, preferred_element_type=F32)[0:1]
                    acc = acc + jnp.dot(vtblk, p, preferred_element_type=F32)
                m_scr[ch] = m_new
                l_scr[ch] = l_new
                acc_scr[ch] = acc

    update(lambda vs: [(kc_ref[c * SEQ_T:(c + 1) * SEQ_T, vs], vtc_ref[c, vs, :]) for c in range(n_ctx_chunks)])

    @pl.when(j < n_lat_tiles)
    def _():
        def body(g, _):
            def blocks_of_head(vs):
                blocks = []
                for i in range(DIFF_GROUP):
                    c = g * DIFF_GROUP + i
                    r0 = pl.multiple_of(c * SEQ_T, SEQ_T)
                    blocks.append((kl_ref[pl.ds(r0, SEQ_T), vs], vtl_ref[c, vs, :]))
                return blocks

            update(blocks_of_head)
            return 0

        lax.fori_loop(0, n_lat_chunks // DIFF_GROUP, body, 0)

    for h, vs in enumerate(heads):
        o = (acc_scr[2 * h] / l_scr[2 * h] - lam * (acc_scr[2 * h + 1] / l_scr[2 * h + 1])).T
        y = o * lax.rsqrt(jnp.mean(o * o, axis=-1, keepdims=True) + EPS) * subln_ref[...]
        o_ref[:, vs] = (y * (1.0 - lam_init)).astype(BF16)


def _diff_attn(dq, dk, dvt, lam_vecs, subln, lam_init, dims):
    B, S, L = dims
    T = dq.shape[0]
    n_lat_tiles = S // SEQ_T
    n_ctx_tiles = L // SEQ_T
    lat_rows = B * S

    def q_idx(b, j):
        return (jnp.where(j < n_lat_tiles, b * n_lat_tiles + j,
                          lat_rows // SEQ_T + b * n_ctx_tiles + (j - n_lat_tiles)), 0)

    return pl.pallas_call(
        functools.partial(_diff_kernel, n_lat_tiles=n_lat_tiles, n_lat_chunks=n_lat_tiles,
                          n_ctx_chunks=n_ctx_tiles, lam_init=lam_init),
        grid=(B, n_lat_tiles + n_ctx_tiles),
        in_specs=[
            pl.BlockSpec((SEQ_T, DIFF_QK_WIDTH), q_idx),
            pl.BlockSpec((S, DIFF_QK_WIDTH), lambda b, j: (b, 0)),
            pl.BlockSpec((n_lat_tiles, DIFF_WIDTH, SEQ_T), lambda b, j: (b, 0, 0)),
            pl.BlockSpec((L, DIFF_QK_WIDTH), lambda b, j: (lat_rows // L + b, 0)),
            pl.BlockSpec((n_ctx_tiles, DIFF_WIDTH, SEQ_T), lambda b, j: (lat_rows // L + b, 0, 0)),
            pl.BlockSpec(lam_vecs.shape, lambda b, j: (0, 0)),
            pl.BlockSpec((1, DIFF_V_DIM), lambda b, j: (0, 0)),
        ],
        out_specs=pl.BlockSpec((SEQ_T, DIFF_WIDTH), q_idx),
        out_shape=jax.ShapeDtypeStruct((T, DIFF_WIDTH), BF16),
        scratch_shapes=[
            pltpu.VMEM((2 * DIFF_HEADS, SEQ_T, DIFF_V_DIM), BF16),
            pltpu.VMEM((2 * DIFF_HEADS, 1, SEQ_T), F32),
            pltpu.VMEM((2 * DIFF_HEADS, 1, SEQ_T), F32),
            pltpu.VMEM((2 * DIFF_HEADS, DIFF_V_DIM, SEQ_T), F32),
        ],
        compiler_params=_cparams(("parallel", "arbitrary")),
        name="diff_attn",
    )(dq, dk, dvt, dk, dvt, lam_vecs, subln)


def _pre_rec_kernel(x_ref, mod_ref, nw_ref, w_ref, xp_ref, xr_ref, g_ref):
    h = _rmsnorm_mod(x_ref[...], nw_ref[...], mod_ref[0:1, :], mod_ref[1:2, :])
    p = jnp.dot(h.astype(BF16), w_ref[...], preferred_element_type=F32)
    xp_ref[...] = p[:, :POOL_WIDTH]
    xr_ref[...] = p[:, POOL_WIDTH:POOL_WIDTH + LRU_WIDTH]
    g_ref[...] = p[:, POOL_WIDTH + LRU_WIDTH:]


def _pre_rec(x, mods_l, norm_w, w_in, dims):
    B, S, L = dims
    T = x.shape[0]
    n_lat = B * S // TM
    s_tiles = S // TM

    def mod_idx(g):
        return (jnp.where(g < n_lat, g // s_tiles, B), 0, 0)

    widths = (POOL_WIDTH, LRU_WIDTH, LRU_WIDTH)
    return pl.pallas_call(
        _pre_rec_kernel,
        grid=(T // TM,),
        in_specs=[
            pl.BlockSpec((TM, D_MODEL), lambda g: (g, 0)),
            pl.BlockSpec((None, N_MOD, D_MODEL), mod_idx),
            pl.BlockSpec((1, D_MODEL), lambda g: (0, 0)),
            pl.BlockSpec(w_in.shape, lambda g: (0, 0)),
        ],
        out_specs=[pl.BlockSpec((TM, w), lambda g: (g, 0)) for w in widths],
        out_shape=[jax.ShapeDtypeStruct((T, w), F32) for w in widths],
        compiler_params=_cparams(("parallel",)),
        name="pre_rec",
    )(x, mods_l, norm_w, w_in)


def _rec_mid_kernel(xp_p, xp_c, xp_n, xr_p, xr_c, xr_n, cw_ref, cb_ref, wa_ref, ba_ref, wx_ref, bx_ref,
                    lam_ref, pw_ref, ps_ref, pool_ref, a_ref, b_ref, *, n_lat_chunks, lat_chunks_per_seq,
                    ctx_chunks_per_seq, lat_len, ctx_len):
    g = pl.program_id(0)
    is_lat = g < n_lat_chunks
    cps = jnp.where(is_lat, lat_chunks_per_seq, ctx_chunks_per_seq)
    within = jnp.where(is_lat, g % lat_chunks_per_seq, (g - n_lat_chunks) % ctx_chunks_per_seq)
    has_prev = within > 0
    has_next = within < cps - 1
    seg_len = jnp.where(is_lat, lat_len, ctx_len)
    ext_rows = SEQ_T + 2 * HALO

    def extended(prev_ref, cur_ref, next_ref):
        prev = jnp.where(has_prev, prev_ref[SEQ_T - HALO:, :], 0.0)
        nxt = jnp.where(has_next, next_ref[:HALO, :], 0.0)
        return jnp.concatenate([prev, cur_ref[...], nxt], axis=0)

    def shifted(ext, off):
        return pltpu.roll(ext, (-off) % ext_rows, 0)[HALO:HALO + SEQ_T, :]

    xp_ext = extended(xp_p, xp_c, xp_n)
    t = within * SEQ_T + lax.broadcasted_iota(jnp.int32, (SEQ_T, 1), 0)
    for gi, w in enumerate(POOL_WINDOWS):
        cols = slice(gi * POOL_GDIM, (gi + 1) * POOL_GDIM)
        eg = xp_ext[:, cols]
        tot = shifted(eg, -(w // 2))
        for off in range(-(w // 2) + 1, w - w // 2):
            tot = tot + shifted(eg, off)
        lo = jnp.clip(t - w // 2, 0, seg_len)
        hi = jnp.clip(t - w // 2 + w, 0, seg_len)
        cnt = (hi - lo).astype(F32)
        d = tot / cnt - xp_c[:, cols]
        y = jnp.dot(d.astype(BF16), pw_ref[gi], preferred_element_type=F32)
        pool_ref[:, cols] = (y * ps_ref[:, cols]).astype(BF16)

    xr_ext = extended(xr_p, xr_c, xr_n)
    u = cb_ref[...] + cw_ref[0:1, :] * shifted(xr_ext, -CONV_LEFT)
    for k in range(1, CONV_W):
        u = u + cw_ref[k:k + 1, :] * shifted(xr_ext, k - CONV_LEFT)
    ub = u.astype(BF16)
    for d in range(2):
        r = jax.nn.sigmoid(jnp.dot(ub, wa_ref[d], preferred_element_type=F32) + ba_ref[d:d + 1, :])
        i = jax.nn.sigmoid(jnp.dot(ub, wx_ref[d], preferred_element_type=F32) + bx_ref[d:d + 1, :])
        nl = -lam_ref[d:d + 1, :]
        softplus = jnp.maximum(nl, 0.0) + jnp.log1p(jnp.exp(-jnp.abs(nl)))
        log_a = -LRU_C * r * softplus
        a_ref[d] = jnp.exp(log_a)
        th = jnp.tanh(log_a)
        b_ref[d] = jnp.sqrt(-2.0 * th / (1.0 - th)) * (i * u)


def _rec_mid(xp, xr, conv_w, conv_b, wa_bd, ba, wx_bd, bx, lam, pool_w, pool_scale, dims):
    B, S, L = dims
    T = xp.shape[0]
    n_chunks = T // SEQ_T

    def cur(g):
        return (g, 0)

    def prv(g):
        return (jnp.maximum(g - 1, 0), 0)

    def nxt(g):
        return (jnp.minimum(g + 1, n_chunks - 1), 0)

    tile = lambda idx: pl.BlockSpec((SEQ_T, LRU_WIDTH), idx)
    full = lambda arr: pl.BlockSpec(arr.shape, lambda g: (0,) * arr.ndim)
    kern = functools.partial(_rec_mid_kernel, n_lat_chunks=B * S // SEQ_T, lat_chunks_per_seq=S // SEQ_T,
                             ctx_chunks_per_seq=L // SEQ_T, lat_len=S, ctx_len=L)
    return pl.pallas_call(
        kern,
        grid=(n_chunks,),
        in_specs=[tile(prv), tile(cur), tile(nxt), tile(prv), tile(cur), tile(nxt),
                  full(conv_w), full(conv_b), full(wa_bd), full(ba), full(wx_bd), full(bx), full(lam),
                  full(pool_w), full(pool_scale)],
        out_specs=[pl.BlockSpec((SEQ_T, POOL_WIDTH), cur),
                   pl.BlockSpec((2, SEQ_T, LRU_WIDTH), lambda g: (0, g, 0)),
                   pl.BlockSpec((2, SEQ_T, LRU_WIDTH), lambda g: (0, g, 0))],
        out_shape=[jax.ShapeDtypeStruct((T, POOL_WIDTH), BF16),
                   jax.ShapeDtypeStruct((2, T, LRU_WIDTH), F32),
                   jax.ShapeDtypeStruct((2, T, LRU_WIDTH), F32)],
        compiler_params=_cparams(("parallel",)),
        name="rec_mid",
    )(xp, xp, xp, xr, xr, xr, conv_w, conv_b, wa_bd, ba, wx_bd, bx, lam, pool_w, pool_scale)


def _scan_kernel(a_ref, b_ref, h_ref, carry_ref):
    d = pl.program_id(1)
    s = pl.program_id(2)

    @pl.when(s == 0)
    def _():
        carry_ref[...] = jnp.zeros_like(carry_ref)

    def run(reverse):
        def body(i, h):
            t = (SEQ_T - 1 - i) if reverse else i
            h = a_ref[pl.ds(t, 1), :] * h + b_ref[pl.ds(t, 1), :]
            h_ref[pl.ds(t, 1), :] = h
            return h

        carry_ref[...] = lax.fori_loop(0, SEQ_T, body, carry_ref[...], unroll=8)

    @pl.when(d == 0)
    def _():
        run(False)

    @pl.when(d == 1)
    def _():
        run(True)


def _scan(a, b, dims):
    B, S, L = dims
    T = a.shape[1]
    lat = S // SEQ_T
    ctx = L // SEQ_T
    lat_base = 0
    ctx_base = B * S // SEQ_T

    def idx(bi, d, s):
        fwd = jnp.where(s < ctx, ctx_base + bi * ctx + s, lat_base + bi * lat + (s - ctx))
        rev = jnp.where(s < ctx, ctx_base + bi * ctx + (ctx - 1 - s), lat_base + bi * lat + (lat - 1 - (s - ctx)))
        return (d, jnp.where(d == 0, fwd, rev), 0)

    spec = pl.BlockSpec((None, SEQ_T, LRU_WIDTH), idx)
    return pl.pallas_call(
        _scan_kernel,
        grid=(B, 2, lat + ctx),
        in_specs=[spec, spec],
        out_specs=spec,
        out_shape=jax.ShapeDtypeStruct((2, T, LRU_WIDTH), F32),
        scratch_shapes=[pltpu.VMEM((1, LRU_WIDTH), F32)],
        compiler_params=_cparams(("parallel", "parallel", "arbitrary")),
        name="lru_scan",
    )(a, b)


def _post_common(y, x_ref, mod_ref, nw_ref, xo_ref, h2_ref):
    x_new = x_ref[...] + mod_ref[2:3, :] * y
    xo_ref[...] = x_new
    h2_ref[...] = _rmsnorm_mod(x_new, nw_ref[...], mod_ref[3:4, :], mod_ref[4:5, :]).astype(BF16)


def _post_attn_kernel(a_ref, b_ref, x_ref, mod_ref, nw_ref, w_ref, xo_ref, h2_ref):
    half = a_ref.shape[1]
    y = (jnp.dot(a_ref[...], w_ref[:half, :], preferred_element_type=F32)
         + jnp.dot(b_ref[...], w_ref[half:, :], preferred_element_type=F32))
    _post_common(y, x_ref, mod_ref, nw_ref, xo_ref, h2_ref)


def _post_rec_kernel(pool_ref, hs_ref, g_ref, x_ref, mod_ref, nw_ref, w_ref, xo_ref, h2_ref):
    half = pool_ref.shape[1]
    rec = ((hs_ref[0] + hs_ref[1]) * _gelu(g_ref[...])).astype(BF16)
    y = (jnp.dot(pool_ref[...], w_ref[:half, :], preferred_element_type=F32)
         + jnp.dot(rec, w_ref[half:, :], preferred_element_type=F32))
    _post_common(y, x_ref, mod_ref, nw_ref, xo_ref, h2_ref)


def _post_mixer(kind, parts, x, mods_l, norm_w, w_out, dims, n_tiles):
    B, S, L = dims
    T = x.shape[0]
    n_lat = B * S // TM
    s_tiles = S // TM

    def mod_idx(g):
        return (jnp.where(g < n_lat, g // s_tiles, B), 0, 0)

    row = lambda w: pl.BlockSpec((TM, w), lambda g: (g, 0))
    if kind == "attn":
        kern = _post_attn_kernel
        part_specs = [row(SWA_WIDTH), row(DIFF_WIDTH)]
    else:
        kern = _post_rec_kernel
        part_specs = [row(POOL_WIDTH), pl.BlockSpec((2, TM, LRU_WIDTH), lambda g: (0, g, 0)), row(LRU_WIDTH)]
    n_in = len(parts)
    return pl.pallas_call(
        kern,
        grid=(n_tiles,),
        in_specs=part_specs + [
            row(D_MODEL),
            pl.BlockSpec((None, N_MOD, D_MODEL), mod_idx),
            pl.BlockSpec((1, D_MODEL), lambda g: (0, 0)),
            pl.BlockSpec(w_out.shape, lambda g: (0, 0)),
        ],
        out_specs=[row(D_MODEL), row(D_MODEL)],
        out_shape=[jax.ShapeDtypeStruct((T, D_MODEL), F32), jax.ShapeDtypeStruct((T, D_MODEL), BF16)],
        input_output_aliases={n_in: 0},
        compiler_params=_cparams(("parallel",)),
        name="post_" + kind,
    )(*parts, x, mods_l, norm_w, w_out)


def _peer_cand_blocks():
    return [(i, PEER_TOPK // (i + 1)) for i in range(1, SUBLANES)]


def _peer_routing(h2_ref, wq_ref, sk_ref, n_ref, c_ref, r2_ref, e2_ref,
                  q_scr, s_scr, cur_scr, rank_scr, top_scr, cand_scr):
    neg_inf = -jnp.inf
    n_cand = cand_scr.shape[0]

    def extract(src_ref, dst_ref, count, rank_ref=None):
        def body(k, _):
            cur = src_ref[...]
            m = jnp.max(cur, axis=0, keepdims=True)
            dst_ref[pl.ds(k, 1), :] = m
            hit = cur == m
            src_ref[...] = jnp.where(hit, neg_inf, cur)
            if rank_ref is not None:
                rank_ref[...] = jnp.where(hit, (k + 1).astype(F32), rank_ref[...])
            return 0

        lax.fori_loop(0, count, body, 0)

    def head_body(h, _):
        q0 = pl.multiple_of(h * PEER_QDIM, PEER_QDIM)
        q_scr[...] = _nt_dot(wq_ref[pl.ds(q0, PEER_QDIM), :], h2_ref[...])
        for p in range(2):
            s = jnp.dot(sk_ref[h * 2 + p], q_scr[p * PEER_HALF:(p + 1) * PEER_HALF, :], precision=HIGHEST,
                        preferred_element_type=F32)
            s_scr[p] = s
            cur_scr[...] = s
            if p == 1:
                rank_scr[...] = jnp.full(rank_scr.shape, PEER_TOPK + 1.0, F32)
            extract(cur_scr, top_scr.at[p], PEER_TOPK, rank_scr if p == 1 else None)
        v1 = top_scr[0]
        v2 = top_scr[1]
        cand_scr[0:PEER_TOPK, :] = v1[0:1, :] + v2
        row = lax.broadcasted_iota(jnp.int32, (SUBLANES, 1), 0)
        for i, n_i in _peer_cand_blocks():
            blk = jnp.where(row < n_i, v1[i:i + 1, :] + v2[0:SUBLANES, :], neg_inf)
            cand_scr[PEER_TOPK + (i - 1) * SUBLANES: PEER_TOPK + i * SUBLANES, :] = blk
        cand_scr[n_cand - SUBLANES:, :] = v1[SUBLANES:, :] + v2[0:1, :]
        cand = cand_scr[...]
        extract(cand_scr, top_scr.at[2], PEER_TOPK)
        tau = top_scr[2, PEER_TOPK - 1:PEER_TOPK, :]
        top = v1[0:1, :] + v2[0:1, :]
        z = jnp.sum(jnp.where(cand >= tau, jnp.exp(cand - top), 0.0), axis=0, keepdims=True)

        s1 = s_scr[0]
        n_ref[h] = jnp.zeros(s1.shape, F32)

        def count_body(jj, _):
            n_ref[h] += jnp.where(s1 + top_scr[1, pl.ds(jj, 1), :] >= tau, 1.0, 0.0)
            return 0

        lax.fori_loop(0, PEER_TOPK, count_body, 0)
        c_ref[h] = jnp.exp(s1 - v1[0:1, :]) / z
        r2_ref[h] = rank_scr[...]
        e2_ref[h] = jnp.exp(s_scr[1] - v2[0:1, :])
        return 0

    lax.fori_loop(0, PEER_HEADS, head_body, 0)


def _peer_kernel(h2_ref, u_ref, vt_ref, wq_ref, sk_ref, x_ref, mod_ref, xo_ref,
                 h2t_scr, coef_scr, acc_scr, n_scr, c_scr, r2_scr, e2_scr,
                 q_scr, s_scr, cur_scr, rank_scr, top_scr, cand_scr):
    j = pl.program_id(1)
    tt = h2_ref.shape[0]
    pair = 2 * PEER_NKEYS

    @pl.when(j == 0)
    def _():
        acc_scr[...] = jnp.zeros_like(acc_scr)
        h2t_scr[...] = h2_ref[...].T
        _peer_routing(h2_ref, wq_ref, sk_ref, n_scr, c_scr, r2_scr, e2_scr,
                      q_scr, s_scr, cur_scr, rank_scr, top_scr, cand_scr)

    for pr in range(PEER_NA // 2):
        act = jnp.dot(u_ref[pr * pair:(pr + 1) * pair, :], h2t_scr[...], preferred_element_type=F32)
        for half in range(2):
            al = 2 * pr + half
            a = j * PEER_NA + al
            n_full = [n_scr[h, pl.ds(a, 1), :] for h in range(PEER_HEADS)]
            c_full = [c_scr[h, pl.ds(a, 1), :] for h in range(PEER_HEADS)]
            for tc in range(tt // LANES):
                ls = slice(tc * LANES, (tc + 1) * LANES)
                n_rows = [r[:, ls] for r in n_full]
                c_rows = [r[:, ls] for r in c_full]
                for rb in range(PEER_NKEYS // PEER_RB):
                    rs = slice(rb * PEER_RB, (rb + 1) * PEER_RB)
                    w = jnp.zeros((PEER_RB, LANES), F32)
                    for h in range(PEER_HEADS):
                        hit = r2_scr[h, rs, ls] <= n_rows[h]
                        w = w + jnp.where(hit, c_rows[h] * e2_scr[h, rs, ls], 0.0)
                    rows = slice(half * PEER_NKEYS + rb * PEER_RB, half * PEER_NKEYS + (rb + 1) * PEER_RB)
                    r0 = al * PEER_NKEYS + rb * PEER_RB
                    coef_scr[r0:r0 + PEER_RB, ls] = (w * _gelu(act[rows, ls])).astype(BF16)

    acc_scr[...] += jnp.dot(vt_ref[...], coef_scr[...], preferred_element_type=F32)

    @pl.when(j == pl.num_programs(1) - 1)
    def _():
        xo_ref[...] = x_ref[...] + mod_ref[5:6, :] * acc_scr[...].T


def _peer(h2, u, v_t, wq_t, subkeys, x, mods_l, dims, n_tiles):
    B, S, L = dims
    T = x.shape[0]
    tt = PEER_TT
    ne = PEER_NA * PEER_NKEYS
    n_lat = B * S // tt
    s_tiles = S // tt
    n_cand = PEER_TOPK + SUBLANES * SUBLANES

    def mod_idx(i, j):
        return (jnp.where(i < n_lat, i // s_tiles, B), 0, 0)

    key_f32 = pltpu.VMEM((PEER_HEADS, PEER_NKEYS, tt), F32)
    const = pl.Buffered(1)
    return pl.pallas_call(
        _peer_kernel,
        grid=(n_tiles, PEER_EXPERTS // ne),
        in_specs=[
            pl.BlockSpec((tt, D_MODEL), lambda i, j: (i, 0)),
            pl.BlockSpec((ne, D_MODEL), lambda i, j: (j, 0)),
            pl.BlockSpec((None, D_MODEL, ne), lambda i, j: (j, 0, 0)),
            pl.BlockSpec(wq_t.shape, lambda i, j: (0, 0), pipeline_mode=const),
            pl.BlockSpec(subkeys.shape, lambda i, j: (0, 0, 0), pipeline_mode=const),
            pl.BlockSpec((tt, D_MODEL), lambda i, j: (i, 0)),
            pl.BlockSpec((None, N_MOD, D_MODEL), mod_idx),
        ],
        out_specs=pl.BlockSpec((tt, D_MODEL), lambda i, j: (i, 0)),
        out_shape=jax.ShapeDtypeStruct((T, D_MODEL), F32),
        scratch_shapes=[
            pltpu.VMEM((D_MODEL, tt), BF16),
            pltpu.VMEM((ne, tt), BF16),
            pltpu.VMEM((D_MODEL, tt), F32),
            key_f32, key_f32, key_f32, key_f32,
            pltpu.VMEM((PEER_QDIM, tt), F32),
            pltpu.VMEM((2, PEER_NKEYS, tt), F32),
            pltpu.VMEM((PEER_NKEYS, tt), F32),
            pltpu.VMEM((PEER_NKEYS, tt), F32),
            pltpu.VMEM((3, PEER_TOPK, tt), F32),
            pltpu.VMEM((n_cand, tt), F32),
        ],
        input_output_aliases={5: 0},
        compiler_params=_cparams(("parallel", "arbitrary")),
        name="peer",
    )(h2, u, v_t, wq_t, subkeys, x, mods_l)


def _final_norm_kernel(x_ref, w_ref, o_ref):
    x = x_ref[...]
    o_ref[...] = x * lax.rsqrt(jnp.mean(x * x, axis=-1, keepdims=True) + EPS) * w_ref[...]


def _final_norm(x, w, n_rows):
    return pl.pallas_call(
        _final_norm_kernel,
        grid=(n_rows // TM,),
        in_specs=[pl.BlockSpec((TM, D_MODEL), lambda g: (g, 0)), pl.BlockSpec((1, D_MODEL), lambda g: (0, 0))],
        out_specs=pl.BlockSpec((TM, D_MODEL), lambda g: (g, 0)),
        out_shape=jax.ShapeDtypeStruct((n_rows, D_MODEL), F32),
        compiler_params=_cparams(("parallel",)),
        name="final_norm",
    )(x, w)


def _lambda_init(layer):
    return 0.8 - 0.6 * math.exp(-0.3 * layer)


def _rope_tables(S):
    rows = S // GRID_W
    row = jnp.repeat(jnp.arange(rows), GRID_W).astype(F32)
    col = jnp.tile(jnp.arange(GRID_W), rows).astype(F32)
    inv = ROPE_THETA ** (-jnp.arange(ROPE_AXIS_FREQS, dtype=F32) / ROPE_AXIS_FREQS)
    ang = jnp.concatenate([row[:, None] * inv, col[:, None] * inv], axis=-1)
    cos, sin = jnp.cos(ang), jnp.sin(ang)
    cos_t = jnp.tile(cos, (1, LANES // ROPE_HALF))
    sin_t = jnp.tile(jnp.concatenate([-sin, sin], axis=-1), (1, LANES // HEAD_DIM))
    cos_t = jnp.concatenate([cos_t, jnp.ones((TM, LANES), F32)], axis=0)
    sin_t = jnp.concatenate([sin_t, jnp.zeros((TM, LANES), F32)], axis=0)
    return cos_t, sin_t


def _block_diag(w):
    nd, nb, bd, _ = w.shape
    eye = jnp.eye(nb, dtype=w.dtype)
    return jnp.einsum("dnij,nm->dnimj", w, eye).reshape(nd, nb * bd, nb * bd)


def kernel(x, c, ctx, c_ctx, w_mod, b_mod, norm_mix, norm_ffn, w_out, attn_w_in, swa_sink, diff_lambda, diff_subln, rec_w_in, pool_w, pool_scale, lru_conv_w, lru_conv_b, lru_wa, lru_ba, lru_wx, lru_bx, lru_lambda, peer_wq, peer_subkeys, peer_u, peer_v, final_norm):
    B, S, D = x.shape
    L = ctx.shape[1]
    depth = w_mod.shape[0]
    dims = (B, S, L)
    assert D == D_MODEL and S % PEER_TT == 0 and (B * L) % PEER_TT == 0 and L % SEQ_T == 0
    assert S % GRID_W == 0 and S >= 3 * BLOCK and B + 1 <= SUBLANES and S % (DIFF_GROUP * SEQ_T) == 0
    lat_rows = B * S
    T = lat_rows + B * L

    xs = jnp.concatenate([x.reshape(lat_rows, D), ctx.reshape(B * L, D)], axis=0)
    cc = jnp.zeros((SUBLANES, D), F32).at[:B].set(c).at[B].set(c_ctx)
    mods = _modulation(cc, w_mod, b_mod)
    cos_t, sin_t = _rope_tables(S)

    for l in range(depth):
        jl = l // 2
        ctx_out = l < depth - 1
        n_rows = T if ctx_out else lat_rows
        mods_l = mods[l]
        if l % 2 == 0:
            q, k, v, dq, dk, dv = _pre_attn(xs, mods_l, norm_mix[l][None], attn_w_in[jl].astype(BF16),
                                            cos_t, sin_t, dims)
            a = _swa(q, k, v, swa_sink[jl], dims)
            bd = _diff_attn(dq, dk, dv, diff_lambda[jl], diff_subln[jl][None], _lambda_init(l), dims)
            xs, h2 = _post_mixer("attn", (a, bd), xs, mods_l, norm_ffn[l][None], w_out[l].astype(BF16),
                                 dims, n_rows // TM)
        else:
            xp, xr, g = _pre_rec(xs, mods_l, norm_mix[l][None], rec_w_in[jl].astype(BF16), dims)
            pool, a_co, b_co = _rec_mid(xp, xr, lru_conv_w[jl], lru_conv_b[jl][None],
                                        _block_diag(lru_wa[jl]).astype(BF16), lru_ba[jl],
                                        _block_diag(lru_wx[jl]).astype(BF16), lru_bx[jl], lru_lambda[jl],
                                        pool_w[jl].astype(BF16), pool_scale[jl][None], dims)
            hs = _scan(a_co, b_co, dims)
            xs, h2 = _post_mixer("rec", (pool, hs, g), xs, mods_l, norm_ffn[l][None], w_out[l].astype(BF16),
                                 dims, n_rows // TM)
        n_peer = n_rows // PEER_TT
        wq_t = peer_wq[l].astype(BF16).T
        sk = peer_subkeys[l].reshape(PEER_HEADS * 2, PEER_NKEYS, PEER_HALF)
        ne = PEER_NA * PEER_NKEYS
        v_t = peer_v[l].astype(BF16).reshape(PEER_EXPERTS // ne, ne, D).transpose(0, 2, 1)
        xs = _peer(h2, peer_u[l].astype(BF16), v_t, wq_t, sk, xs, mods_l, dims, n_peer)

    return _final_norm(xs, final_norm[None], lat_rows).reshape(B, S, D)
```

```python
import functools
import math

import jax
import jax.numpy as jnp
from jax import lax
from jax.experimental import pallas as pl
from jax.experimental.pallas import tpu as pltpu

F32 = jnp.float32
BF16 = jnp.bfloat16
HIGHEST = lax.Precision.HIGHEST

D_MODEL = 1024
HEAD_DIM = 64
EPS = 1e-6
NEG = -1e30
N_MOD = 6
SCALE = HEAD_DIM ** -0.5
ROPE_HALF = HEAD_DIM // 2
ROPE_AXIS_FREQS = ROPE_HALF // 2
ROPE_THETA = 10000.0
GRID_W = 64
SWA_Q_HEADS = 8
SWA_KV_HEADS = 2
SWA_GROUP = SWA_Q_HEADS // SWA_KV_HEADS
WINDOW = 128
BLOCK = 128
SWA_WIDTH = SWA_Q_HEADS * HEAD_DIM
SWA_KV_WIDTH = SWA_KV_HEADS * HEAD_DIM
DIFF_HEADS = 4
DIFF_V_DIM = 2 * HEAD_DIM
DIFF_QK_WIDTH = DIFF_HEADS * 2 * HEAD_DIM
DIFF_WIDTH = DIFF_HEADS * DIFF_V_DIM
POOL_WINDOWS = (2, 4, 8, 16)
POOL_GROUPS = 4
POOL_WIDTH = D_MODEL // 2
POOL_GDIM = POOL_WIDTH // POOL_GROUPS
LRU_WIDTH = D_MODEL // 2
LRU_BLOCKS = 8
CONV_W = 4
CONV_LEFT = CONV_W // 2
LRU_C = 8.0
PEER_HEADS = 8
PEER_NKEYS = 128
PEER_EXPERTS = PEER_NKEYS * PEER_NKEYS
PEER_QDIM = 256
PEER_HALF = PEER_QDIM // 2
PEER_TOPK = 16

LANES = 128
SUBLANES = 8
VMEM_LIMIT = 56 * 1024 * 1024

TM = 256
SEQ_T = 256
DIFF_GROUP = 4
DIFF_AHEAD = 2
HALO = 8
PEER_TT = 512
PEER_NA = 8
PEER_RB = 64


def _cparams(sem):
    return pltpu.CompilerParams(dimension_semantics=sem, vmem_limit_bytes=VMEM_LIMIT)


def _nt_dot(a, b):
    return lax.dot_general(a, b, (((1,), (1,)), ((), ())), preferred_element_type=F32)


def _rmsnorm_mod(x, w, shift, scale):
    y = x * lax.rsqrt(jnp.mean(x * x, axis=-1, keepdims=True) + EPS) * w
    return y * (1.0 + scale) + shift


def _gelu(x):
    return 0.5 * x * (1.0 + lax.erf(x * (2.0 ** -0.5)))


def _mod_kernel(cc_ref, w_ref, b_ref, o_ref):
    cc = cc_ref[...]
    sc = cc * jax.nn.sigmoid(cc)
    o_ref[...] = jnp.dot(sc, w_ref[...], precision=HIGHEST, preferred_element_type=F32) + b_ref[...]


def _modulation(cc, w_mod, b_mod):
    depth = w_mod.shape[0]
    rows = cc.shape[0]
    out = pl.pallas_call(
        _mod_kernel,
        grid=(depth, N_MOD),
        in_specs=[
            pl.BlockSpec((rows, D_MODEL), lambda l, j: (0, 0)),
            pl.BlockSpec((None, D_MODEL, D_MODEL), lambda l, j: (l, 0, j)),
            pl.BlockSpec((None, 1, D_MODEL), lambda l, j: (l, 0, j)),
        ],
        out_specs=pl.BlockSpec((None, rows, D_MODEL), lambda l, j: (l, 0, j)),
        out_shape=jax.ShapeDtypeStruct((depth, rows, N_MOD * D_MODEL), F32),
        compiler_params=_cparams(("parallel", "parallel")),
        name="modulation",
    )(cc, w_mod, b_mod.reshape(depth, 1, N_MOD * D_MODEL))
    return out.reshape(depth, rows, N_MOD, D_MODEL)


def _rope128(x, cos, sin_signed, first_half):
    partner = jnp.where(first_half, pltpu.roll(x, LANES - ROPE_HALF, 1), pltpu.roll(x, ROPE_HALF, 1))
    return x * cos + partner * sin_signed


def _pre_attn_kernel(x_ref, mod_ref, nw_ref, w_ref, cos_ref, sin_ref,
                     q_ref, k_ref, v_ref, dq_ref, dk_ref, dv_ref):
    h = _rmsnorm_mod(x_ref[...], nw_ref[...], mod_ref[0:1, :], mod_ref[1:2, :])
    p = jnp.dot(h.astype(BF16), w_ref[...], preferred_element_type=F32)
    cos = cos_ref[...]
    sin = sin_ref[...]
    lane = lax.broadcasted_iota(jnp.int32, (1, LANES), 1)
    first_half = (lane % HEAD_DIM) < ROPE_HALF

    def roped(lo, width, scale):
        outs = []
        for c in range(width // LANES):
            xc = p[:, lo + c * LANES: lo + (c + 1) * LANES]
            outs.append((_rope128(xc, cos, sin, first_half) * scale).astype(BF16))
        return outs

    o1 = SWA_WIDTH
    o2 = o1 + SWA_KV_WIDTH
    o3 = o2 + SWA_KV_WIDTH
    o4 = o3 + DIFF_QK_WIDTH
    o5 = o4 + DIFF_QK_WIDTH
    for c, val in enumerate(roped(0, SWA_WIDTH, SCALE)):
        q_ref[:, c * LANES:(c + 1) * LANES] = val
    for c, val in enumerate(roped(o1, SWA_KV_WIDTH, 1.0)):
        k_ref[:, c * LANES:(c + 1) * LANES] = val
    v_ref[...] = p[:, o2:o3].astype(BF16)
    for c, val in enumerate(roped(o3, DIFF_QK_WIDTH, SCALE)):
        dq_ref[:, c * LANES:(c + 1) * LANES] = val
    for c, val in enumerate(roped(o4, DIFF_QK_WIDTH, 1.0)):
        dk_ref[:, c * LANES:(c + 1) * LANES] = val
    dv_ref[...] = p[:, o5:].T.astype(BF16)


def _pre_attn(x, mods_l, norm_w, w_in, cos_t, sin_t, dims):
    B, S, L = dims
    T = x.shape[0]
    n_lat = B * S // TM
    s_tiles = S // TM

    def mod_idx(g):
        return (jnp.where(g < n_lat, g // s_tiles, B), 0, 0)

    def rope_idx(g):
        return (jnp.where(g < n_lat, g % s_tiles, s_tiles), 0)

    widths = (SWA_WIDTH, SWA_KV_WIDTH, SWA_KV_WIDTH, DIFF_QK_WIDTH, DIFF_QK_WIDTH)
    return pl.pallas_call(
        _pre_attn_kernel,
        grid=(T // TM,),
        in_specs=[
            pl.BlockSpec((TM, D_MODEL), lambda g: (g, 0)),
            pl.BlockSpec((None, N_MOD, D_MODEL), mod_idx),
            pl.BlockSpec((1, D_MODEL), lambda g: (0, 0)),
            pl.BlockSpec(w_in.shape, lambda g: (0, 0)),
            pl.BlockSpec((TM, LANES), rope_idx),
            pl.BlockSpec((TM, LANES), rope_idx),
        ],
        out_specs=[pl.BlockSpec((TM, w), lambda g: (g, 0)) for w in widths]
        + [pl.BlockSpec((None, DIFF_WIDTH, TM), lambda g: (g, 0, 0))],
        out_shape=[jax.ShapeDtypeStruct((T, w), BF16) for w in widths]
        + [jax.ShapeDtypeStruct((T // TM, DIFF_WIDTH, TM), BF16)],
        compiler_params=_cparams(("parallel",)),
        name="pre_attn",
    )(x, mods_l, norm_w, w_in, cos_t, sin_t)


def _swa_kernel(q_ref, kl_ref, vl_ref, kc_ref, vc_ref, sink_ref, o_ref, *, n_lat_blocks, seq):
    j = pl.program_id(1)
    is_ctx = j >= n_lat_blocks
    jl = jnp.minimum(j, n_lat_blocks - 1)
    band = 3 * BLOCK
    bs = pl.multiple_of(jnp.clip((jl - 1) * BLOCK, 0, seq - band), BLOCK)
    kb = kl_ref[pl.ds(bs, band), :]
    vb = vl_ref[pl.ds(bs, band), :]
    kc = kc_ref[...]
    vc = vc_ref[...]
    qpos = jl * BLOCK + lax.broadcasted_iota(jnp.int32, (BLOCK, 1), 0)
    kpos = bs + lax.broadcasted_iota(jnp.int32, (1, band), 1)
    valid = jnp.logical_and(jnp.abs(qpos - kpos) <= WINDOW, jnp.logical_not(is_ctx))
    q = q_ref[...]
    for hk in range(SWA_KV_HEADS):
        ks = slice(hk * HEAD_DIM, (hk + 1) * HEAD_DIM)
        for g in range(SWA_GROUP):
            h = hk * SWA_GROUP + g
            qh = q[:, h * HEAD_DIM:(h + 1) * HEAD_DIM]
            sl = jnp.where(valid, _nt_dot(qh, kb[:, ks]), NEG)
            sc = _nt_dot(qh, kc[:, ks])
            sink = sink_ref[h]
            m = jnp.maximum(jnp.maximum(jnp.max(sl, axis=-1, keepdims=True),
                                        jnp.max(sc, axis=-1, keepdims=True)), sink)
            pl_ = jnp.exp(sl - m)
            pc = jnp.exp(sc - m)
            den = (jnp.sum(pl_, axis=-1, keepdims=True) + jnp.sum(pc, axis=-1, keepdims=True)
                   + jnp.exp(sink - m))
            o = (jnp.dot(pl_.astype(BF16), vb[:, ks], preferred_element_type=F32)
                 + jnp.dot(pc.astype(BF16), vc[:, ks], preferred_element_type=F32))
            o_ref[:, h * HEAD_DIM:(h + 1) * HEAD_DIM] = (o / den).astype(BF16)


def _swa(q, k, v, sink, dims):
    B, S, L = dims
    T = q.shape[0]
    n_lat_blocks = S // BLOCK
    n_ctx_blocks = L // BLOCK
    lat_rows = B * S

    def q_idx(b, j):
        return (jnp.where(j < n_lat_blocks, b * n_lat_blocks + j,
                          lat_rows // BLOCK + b * n_ctx_blocks + (j - n_lat_blocks)), 0)

    lat_spec = pl.BlockSpec((S, SWA_KV_WIDTH), lambda b, j: (b, 0))
    ctx_spec = pl.BlockSpec((L, SWA_KV_WIDTH), lambda b, j: (lat_rows // L + b, 0))
    return pl.pallas_call(
        functools.partial(_swa_kernel, n_lat_blocks=n_lat_blocks, seq=S),
        grid=(B, n_lat_blocks + n_ctx_blocks),
        in_specs=[
            pl.BlockSpec((BLOCK, SWA_WIDTH), q_idx),
            lat_spec, lat_spec, ctx_spec, ctx_spec,
            pl.BlockSpec(memory_space=pltpu.SMEM),
        ],
        out_specs=pl.BlockSpec((BLOCK, SWA_WIDTH), q_idx),
        out_shape=jax.ShapeDtypeStruct((T, SWA_WIDTH), BF16),
        compiler_params=_cparams(("parallel", "arbitrary")),
        name="swa",
    )(q, k, v, k, v, sink)


def _diff_kernel(q_ref, kl_ref, vtl_ref, kc_ref, vtc_ref, lam_ref, subln_ref, o_ref, qp_scr, m_scr, l_scr, acc_scr, *,
                 n_lat_tiles, n_lat_chunks, n_ctx_chunks, lam_init):
    j = pl.program_id(1)
    lv = lam_ref[...]
    lam = (jnp.exp(jnp.sum(lv[0:1] * lv[1:2], axis=-1, keepdims=True))
           - jnp.exp(jnp.sum(lv[2:3] * lv[3:4], axis=-1, keepdims=True)) + lam_init)
    lane = lax.broadcasted_iota(jnp.int32, (1, DIFF_V_DIM), 1)
    heads = [slice(h * DIFF_V_DIM, (h + 1) * DIFF_V_DIM) for h in range(DIFF_HEADS)]
    ones = jnp.ones((SUBLANES, SEQ_T), BF16)

    for h, vs in enumerate(heads):
        qh = q_ref[:, vs]
        zero = jnp.zeros_like(qh)
        qp_scr[2 * h] = jnp.where(lane < HEAD_DIM, qh, zero)
        qp_scr[2 * h + 1] = jnp.where(lane >= HEAD_DIM, qh, zero)
    m_scr[...] = jnp.full(m_scr.shape, -jnp.inf, F32)
    l_scr[...] = jnp.zeros(l_scr.shape, F32)
    acc_scr[...] = jnp.zeros(acc_scr.shape, F32)

    def update(blocks_of_head):
        all_blocks = [blocks_of_head(vs) for vs in heads]

        def scores(ch):
            return [_nt_dot(kblk, qp_scr[ch]) for kblk, _ in all_blocks[ch // 2]]

        n_chains = 2 * DIFF_HEADS
        queue = [scores(ch) for ch in range(DIFF_AHEAD)]
        for h, vs in enumerate(heads):
            blocks = all_blocks[h]
            for m in range(2):
                ch = 2 * h + m
                ss = queue.pop(0)
                if ch + DIFF_AHEAD < n_chains:
                    queue.append(scores(ch + DIFF_AHEAD))
                m_old = m_scr[ch]
                m_blk = jnp.max(ss[0], axis=0, keepdims=True)
                for s in ss[1:]:
                    m_blk = jnp.maximum(m_blk, jnp.max(s, axis=0, keepdims=True))
                m_new = jnp.maximum(m_old, m_blk.astype(BF16).astype(F32))
                m_b = m_new.astype(BF16)
                alpha = jnp.exp(m_old - m_new)
                l_new = alpha * l_scr[ch]
                acc = alpha * acc_scr[ch]
                for s, (_, vtblk) in zip(ss, blocks):
                    p = jnp.exp(s.astype(BF16) - m_b)
                    l_new = l_new + jnp.dot(ones, p, preferred_element_type=F32)[0:1]
                    acc = acc + jnp.dot(vtblk, p, preferred_element_type=F32)
                m_scr[ch] = m_new
                l_scr[ch] = l_new
                acc_scr[ch] = acc

    update(lambda vs: [(kc_ref[c * SEQ_T:(c + 1) * SEQ_T, vs], vtc_ref[c, vs, :]) for c in range(n_ctx_chunks)])

    @pl.when(j < n_lat_tiles)
    def _():
        def body(g, _):
            def blocks_of_head(vs):
                blocks = []
                for i in range(DIFF_GROUP):
                    c = g * DIFF_GROUP + i
                    r0 = pl.multiple_of(c * SEQ_T, SEQ_T)
                    blocks.append((kl_ref[pl.ds(r0, SEQ_T), vs], vtl_ref[c, vs, :]))
                return blocks

            update(blocks_of_head)
            return 0

        lax.fori_loop(0, n_lat_chunks // DIFF_GROUP, body, 0)

    for h, vs in enumerate(heads):
        o = (acc_scr[2 * h] / l_scr[2 * h] - lam * (acc_scr[2 * h + 1] / l_scr[2 * h + 1])).T
        y = o * lax.rsqrt(jnp.mean(o * o, axis=-1, keepdims=True) + EPS) * subln_ref[...]
        o_ref[:, vs] = (y * (1.0 - lam_init)).astype(BF16)


def _diff_attn(dq, dk, dvt, lam_vecs, subln, lam_init, dims):
    B, S, L = dims
    T = dq.shape[0]
    n_lat_tiles = S // SEQ_T
    n_ctx_tiles = L // SEQ_T
    lat_rows = B * S

    def q_idx(b, j):
        return (jnp.where(j < n_lat_tiles, b * n_lat_tiles + j,
                          lat_rows // SEQ_T + b * n_ctx_tiles + (j - n_lat_tiles)), 0)

    return pl.pallas_call(
        functools.partial(_diff_kernel, n_lat_tiles=n_lat_tiles, n_lat_chunks=n_lat_tiles,
                          n_ctx_chunks=n_ctx_tiles, lam_init=lam_init),
        grid=(B, n_lat_tiles + n_ctx_tiles),
        in_specs=[
            pl.BlockSpec((SEQ_T, DIFF_QK_WIDTH), q_idx),
            pl.BlockSpec((S, DIFF_QK_WIDTH), lambda b, j: (b, 0)),
            pl.BlockSpec((n_lat_tiles, DIFF_WIDTH, SEQ_T), lambda b, j: (b, 0, 0)),
            pl.BlockSpec((L, DIFF_QK_WIDTH), lambda b, j: (lat_rows // L + b, 0)),
            pl.BlockSpec((n_ctx_tiles, DIFF_WIDTH, SEQ_T), lambda b, j: (lat_rows // L + b, 0, 0)),
            pl.BlockSpec(lam_vecs.shape, lambda b, j: (0, 0)),
            pl.BlockSpec((1, DIFF_V_DIM), lambda b, j: (0, 0)),
        ],
        out_specs=pl.BlockSpec((SEQ_T, DIFF_WIDTH), q_idx),
        out_shape=jax.ShapeDtypeStruct((T, DIFF_WIDTH), BF16),
        scratch_shapes=[
            pltpu.VMEM((2 * DIFF_HEADS, SEQ_T, DIFF_V_DIM), BF16),
            pltpu.VMEM((2 * DIFF_HEADS, 1, SEQ_T), F32),
            pltpu.VMEM((2 * DIFF_HEADS, 1, SEQ_T), F32),
            pltpu.VMEM((2 * DIFF_HEADS, DIFF_V_DIM, SEQ_T), F32),
        ],
        compiler_params=_cparams(("parallel", "arbitrary")),
        name="diff_attn",
    )(dq, dk, dvt, dk, dvt, lam_vecs, subln)


def _pre_rec_kernel(x_ref, mod_ref, nw_ref, w_ref, xp_ref, xr_ref, g_ref):
    h = _rmsnorm_mod(x_ref[...], nw_ref[...], mod_ref[0:1, :], mod_ref[1:2, :])
    p = jnp.dot(h.astype(BF16), w_ref[...], preferred_element_type=F32)
    xp_ref[...] = p[:, :POOL_WIDTH]
    xr_ref[...] = p[:, POOL_WIDTH:POOL_WIDTH + LRU_WIDTH]
    g_ref[...] = p[:, POOL_WIDTH + LRU_WIDTH:]


def _pre_rec(x, mods_l, norm_w, w_in, dims):
    B, S, L = dims
    T = x.shape[0]
    n_lat = B * S // TM
    s_tiles = S // TM

    def mod_idx(g):
        return (jnp.where(g < n_lat, g // s_tiles, B), 0, 0)

    widths = (POOL_WIDTH, LRU_WIDTH, LRU_WIDTH)
    return pl.pallas_call(
        _pre_rec_kernel,
        grid=(T // TM,),
        in_specs=[
            pl.BlockSpec((TM, D_MODEL), lambda g: (g, 0)),
            pl.BlockSpec((None, N_MOD, D_MODEL), mod_idx),
            pl.BlockSpec((1, D_MODEL), lambda g: (0, 0)),
            pl.BlockSpec(w_in.shape, lambda g: (0, 0)),
        ],
        out_specs=[pl.BlockSpec((TM, w), lambda g: (g, 0)) for w in widths],
        out_shape=[jax.ShapeDtypeStruct((T, w), F32) for w in widths],
        compiler_params=_cparams(("parallel",)),
        name="pre_rec",
    )(x, mods_l, norm_w, w_in)


def _rec_mid_kernel(xp_p, xp_c, xp_n, xr_p, xr_c, xr_n, cw_ref, cb_ref, wa_ref, ba_ref, wx_ref, bx_ref,
                    lam_ref, pw_ref, ps_ref, pool_ref, a_ref, b_ref, *, n_lat_chunks, lat_chunks_per_seq,
                    ctx_chunks_per_seq, lat_len, ctx_len):
    g = pl.program_id(0)
    is_lat = g < n_lat_chunks
    cps = jnp.where(is_lat, lat_chunks_per_seq, ctx_chunks_per_seq)
    within = jnp.where(is_lat, g % lat_chunks_per_seq, (g - n_lat_chunks) % ctx_chunks_per_seq)
    has_prev = within > 0
    has_next = within < cps - 1
    seg_len = jnp.where(is_lat, lat_len, ctx_len)
    ext_rows = SEQ_T + 2 * HALO

    def extended(prev_ref, cur_ref, next_ref):
        prev = jnp.where(has_prev, prev_ref[SEQ_T - HALO:, :], 0.0)
        nxt = jnp.where(has_next, next_ref[:HALO, :], 0.0)
        return jnp.concatenate([prev, cur_ref[...], nxt], axis=0)

    def shifted(ext, off):
        return pltpu.roll(ext, (-off) % ext_rows, 0)[HALO:HALO + SEQ_T, :]

    xp_ext = extended(xp_p, xp_c, xp_n)
    t = within * SEQ_T + lax.broadcasted_iota(jnp.int32, (SEQ_T, 1), 0)
    for gi, w in enumerate(POOL_WINDOWS):
        cols = slice(gi * POOL_GDIM, (gi + 1) * POOL_GDIM)
        eg = xp_ext[:, cols]
        tot = shifted(eg, -(w // 2))
        for off in range(-(w // 2) + 1, w - w // 2):
            tot = tot + shifted(eg, off)
        lo = jnp.clip(t - w // 2, 0, seg_len)
        hi = jnp.clip(t - w // 2 + w, 0, seg_len)
        cnt = (hi - lo).astype(F32)
        d = tot / cnt - xp_c[:, cols]
        y = jnp.dot(d.astype(BF16), pw_ref[gi], preferred_element_type=F32)
        pool_ref[:, cols] = (y * ps_ref[:, cols]).astype(BF16)

    xr_ext = extended(xr_p, xr_c, xr_n)
    u = cb_ref[...] + cw_ref[0:1, :] * shifted(xr_ext, -CONV_LEFT)
    for k in range(1, CONV_W):
        u = u + cw_ref[k:k + 1, :] * shifted(xr_ext, k - CONV_LEFT)
    ub = u.astype(BF16)
    for d in range(2):
        r = jax.nn.sigmoid(jnp.dot(ub, wa_ref[d], preferred_element_type=F32) + ba_ref[d:d + 1, :])
        i = jax.nn.sigmoid(jnp.dot(ub, wx_ref[d], preferred_element_type=F32) + bx_ref[d:d + 1, :])
        nl = -lam_ref[d:d + 1, :]
        softplus = jnp.maximum(nl, 0.0) + jnp.log1p(jnp.exp(-jnp.abs(nl)))
        log_a = -LRU_C * r * softplus
        a_ref[d] = jnp.exp(log_a)
        th = jnp.tanh(log_a)
        b_ref[d] = jnp.sqrt(-2.0 * th / (1.0 - th)) * (i * u)


def _rec_mid(xp, xr, conv_w, conv_b, wa_bd, ba, wx_bd, bx, lam, pool_w, pool_scale, dims):
    B, S, L = dims
    T = xp.shape[0]
    n_chunks = T // SEQ_T

    def cur(g):
        return (g, 0)

    def prv(g):
        return (jnp.maximum(g - 1, 0), 0)

    def nxt(g):
        return (jnp.minimum(g + 1, n_chunks - 1), 0)

    tile = lambda idx: pl.BlockSpec((SEQ_T, LRU_WIDTH), idx)
    full = lambda arr: pl.BlockSpec(arr.shape, lambda g: (0,) * arr.ndim)
    kern = functools.partial(_rec_mid_kernel, n_lat_chunks=B * S // SEQ_T, lat_chunks_per_seq=S // SEQ_T,
                             ctx_chunks_per_seq=L // SEQ_T, lat_len=S, ctx_len=L)
    return pl.pallas_call(
        kern,
        grid=(n_chunks,),
        in_specs=[tile(prv), tile(cur), tile(nxt), tile(prv), tile(cur), tile(nxt),
                  full(conv_w), full(conv_b), full(wa_bd), full(ba), full(wx_bd), full(bx), full(lam),
                  full(pool_w), full(pool_scale)],
        out_specs=[pl.BlockSpec((SEQ_T, POOL_WIDTH), cur),
                   pl.BlockSpec((2, SEQ_T, LRU_WIDTH), lambda g: (0, g, 0)),
                   pl.BlockSpec((2, SEQ_T, LRU_WIDTH), lambda g: (0, g, 0))],
        out_shape=[jax.ShapeDtypeStruct((T, POOL_WIDTH), BF16),
                   jax.ShapeDtypeStruct((2, T, LRU_WIDTH), F32),
                   jax.ShapeDtypeStruct((2, T, LRU_WIDTH), F32)],
        compiler_params=_cparams(("parallel",)),
        name="rec_mid",
    )(xp, xp, xp, xr, xr, xr, conv_w, conv_b, wa_bd, ba, wx_bd, bx, lam, pool_w, pool_scale)


def _scan_kernel(a_ref, b_ref, h_ref, carry_ref):
    d = pl.program_id(1)
    s = pl.program_id(2)

    @pl.when(s == 0)
    def _():
        carry_ref[...] = jnp.zeros_like(carry_ref)

    def run(reverse):
        def body(i, h):
            t = (SEQ_T - 1 - i) if reverse else i
            h = a_ref[pl.ds(t, 1), :] * h + b_ref[pl.ds(t, 1), :]
            h_ref[pl.ds(t, 1), :] = h
            return h

        carry_ref[...] = lax.fori_loop(0, SEQ_T, body, carry_ref[...], unroll=8)

    @pl.when(d == 0)
    def _():
        run(False)

    @pl.when(d == 1)
    def _():
        run(True)


def _scan(a, b, dims):
    B, S, L = dims
    T = a.shape[1]
    lat = S // SEQ_T
    ctx = L // SEQ_T
    lat_base = 0
    ctx_base = B * S // SEQ_T

    def idx(bi, d, s):
        fwd = jnp.where(s < ctx, ctx_base + bi * ctx + s, lat_base + bi * lat + (s - ctx))
        rev = jnp.where(s < ctx, ctx_base + bi * ctx + (ctx - 1 - s), lat_base + bi * lat + (lat - 1 - (s - ctx)))
        return (d, jnp.where(d == 0, fwd, rev), 0)

    spec = pl.BlockSpec((None, SEQ_T, LRU_WIDTH), idx)
    return pl.pallas_call(
        _scan_kernel,
        grid=(B, 2, lat + ctx),
        in_specs=[spec, spec],
        out_specs=spec,
        out_shape=jax.ShapeDtypeStruct((2, T, LRU_WIDTH), F32),
        scratch_shapes=[pltpu.VMEM((1, LRU_WIDTH), F32)],
        compiler_params=_cparams(("parallel", "parallel", "arbitrary")),
        name="lru_scan",
    )(a, b)


def _post_common(y, x_ref, mod_ref, nw_ref, xo_ref, h2_ref):
    x_new = x_ref[...] + mod_ref[2:3, :] * y
    xo_ref[...] = x_new
    h2_ref[...] = _rmsnorm_mod(x_new, nw_ref[...], mod_ref[3:4, :], mod_ref[4:5, :]).astype(BF16)


def _post_attn_kernel(a_ref, b_ref, x_ref, mod_ref, nw_ref, w_ref, xo_ref, h2_ref):
    half = a_ref.shape[1]
    y = (jnp.dot(a_ref[...], w_ref[:half, :], preferred_element_type=F32)
         + jnp.dot(b_ref[...], w_ref[half:, :], preferred_element_type=F32))
    _post_common(y, x_ref, mod_ref, nw_ref, xo_ref, h2_ref)


def _post_rec_kernel(pool_ref, hs_ref, g_ref, x_ref, mod_ref, nw_ref, w_ref, xo_ref, h2_ref):
    half = pool_ref.shape[1]
    rec = ((hs_ref[0] + hs_ref[1]) * _gelu(g_ref[...])).astype(BF16)
    y = (jnp.dot(pool_ref[...], w_ref[:half, :], preferred_element_type=F32)
         + jnp.dot(rec, w_ref[half:, :], preferred_element_type=F32))
    _post_common(y, x_ref, mod_ref, nw_ref, xo_ref, h2_ref)


def _post_mixer(kind, parts, x, mods_l, norm_w, w_out, dims, n_tiles):
    B, S, L = dims
    T = x.shape[0]
    n_lat = B * S // TM
    s_tiles = S // TM

    def mod_idx(g):
        return (jnp.where(g < n_lat, g // s_tiles, B), 0, 0)

    row = lambda w: pl.BlockSpec((TM, w), lambda g: (g, 0))
    if kind == "attn":
        kern = _post_attn_kernel
        part_specs = [row(SWA_WIDTH), row(DIFF_WIDTH)]
    else:
        kern = _post_rec_kernel
        part_specs = [row(POOL_WIDTH), pl.BlockSpec((2, TM, LRU_WIDTH), lambda g: (0, g, 0)), row(LRU_WIDTH)]
    n_in = len(parts)
    return pl.pallas_call(
        kern,
        grid=(n_tiles,),
        in_specs=part_specs + [
            row(D_MODEL),
            pl.BlockSpec((None, N_MOD, D_MODEL), mod_idx),
            pl.BlockSpec((1, D_MODEL), lambda g: (0, 0)),
            pl.BlockSpec(w_out.shape, lambda g: (0, 0)),
        ],
        out_specs=[row(D_MODEL), row(D_MODEL)],
        out_shape=[jax.ShapeDtypeStruct((T, D_MODEL), F32), jax.ShapeDtypeStruct((T, D_MODEL), BF16)],
        input_output_aliases={n_in: 0},
        compiler_params=_cparams(("parallel",)),
        name="post_" + kind,
    )(*parts, x, mods_l, norm_w, w_out)


def _peer_cand_blocks():
    return [(i, PEER_TOPK // (i + 1)) for i in range(1, SUBLANES)]


PEER_TAG_BITS = 31


def _peer_tagged(e, rank):
    bits = pltpu.bitcast(e, jnp.int32)
    tag = PEER_TAG_BITS - rank.astype(jnp.int32)
    return pltpu.bitcast((bits & ~PEER_TAG_BITS) | tag, F32)


def _peer_routing(h2_ref, wq_ref, sk_ref, thr_ref, c_ref, e2_ref,
                  q_scr, s_scr, cur_scr, rank_scr, top_scr, cand_scr):
    neg_inf = -jnp.inf
    n_cand = cand_scr.shape[0]

    def extract(src_ref, dst_ref, count, rank_ref=None):
        def body(k, _):
            cur = src_ref[...]
            m = jnp.max(cur, axis=0, keepdims=True)
            dst_ref[pl.ds(k, 1), :] = m
            hit = cur == m
            src_ref[...] = jnp.where(hit, neg_inf, cur)
            if rank_ref is not None:
                rank_ref[...] = jnp.where(hit, lax.convert_element_type(k + 1, F32), rank_ref[...])
            return 0

        lax.fori_loop(0, count, body, 0)

    def head_body(h, _):
        q0 = pl.multiple_of(h * PEER_QDIM, PEER_QDIM)
        q_scr[...] = _nt_dot(wq_ref[pl.ds(q0, PEER_QDIM), :], h2_ref[...])
        for p in range(2):
            s = jnp.dot(sk_ref[h * 2 + p], q_scr[p * PEER_HALF:(p + 1) * PEER_HALF, :], precision=HIGHEST,
                        preferred_element_type=F32)
            s_scr[p] = s
            cur_scr[...] = s
            if p == 1:
                rank_scr[...] = jnp.full(rank_scr.shape, PEER_TOPK + 1.0, F32)
            extract(cur_scr, top_scr.at[p], PEER_TOPK, rank_scr if p == 1 else None)
        v1 = top_scr[0]
        v2 = top_scr[1]
        cand_scr[0:PEER_TOPK, :] = v1[0:1, :] + v2
        row = lax.broadcasted_iota(jnp.int32, (SUBLANES, 1), 0)
        for i, n_i in _peer_cand_blocks():
            blk = jnp.where(row < n_i, v1[i:i + 1, :] + v2[0:SUBLANES, :], neg_inf)
            cand_scr[PEER_TOPK + (i - 1) * SUBLANES: PEER_TOPK + i * SUBLANES, :] = blk
        cand_scr[n_cand - SUBLANES:, :] = v1[SUBLANES:, :] + v2[0:1, :]
        cand = cand_scr[...]
        extract(cand_scr, top_scr.at[2], PEER_TOPK)
        tau = top_scr[2, PEER_TOPK - 1:PEER_TOPK, :]
        top = v1[0:1, :] + v2[0:1, :]
        z = jnp.sum(jnp.where(cand >= tau, jnp.exp(cand - top), 0.0), axis=0, keepdims=True)

        s1 = s_scr[0]
        ranks = (lax.broadcasted_iota(jnp.int32, (PEER_TOPK, 1), 0) + 1).astype(F32)
        top_scr[2] = _peer_tagged(jnp.exp(v2 - v2[0:1, :]), ranks)
        thr_ref[h] = jnp.full(s1.shape, jnp.inf, F32)

        def thr_body(jj, _):
            hit = s1 + top_scr[1, pl.ds(jj, 1), :] >= tau
            thr_ref[h] = jnp.where(hit, top_scr[2, pl.ds(jj, 1), :], thr_ref[h])
            return 0

        lax.fori_loop(0, PEER_TOPK, thr_body, 0)
        c_ref[h] = jnp.exp(s1 - v1[0:1, :]) / z
        e2_ref[h] = _peer_tagged(jnp.exp(s_scr[1] - v2[0:1, :]), rank_scr[...])
        return 0

    lax.fori_loop(0, PEER_HEADS, head_body, 0)


def _peer_kernel(h2_ref, u_ref, vt_ref, wq_ref, sk_ref, x_ref, mod_ref, xo_ref,
                 h2t_scr, coef_scr, acc_scr, thr_scr, c_scr, e2_scr,
                 q_scr, s_scr, cur_scr, rank_scr, top_scr, cand_scr):
    j = pl.program_id(1)
    tt = h2_ref.shape[0]
    pair = 2 * PEER_NKEYS

    @pl.when(j == 0)
    def _():
        acc_scr[...] = jnp.zeros_like(acc_scr)
        h2t_scr[...] = h2_ref[...].T
        _peer_routing(h2_ref, wq_ref, sk_ref, thr_scr, c_scr, e2_scr,
                      q_scr, s_scr, cur_scr, rank_scr, top_scr, cand_scr)

    for pr in range(PEER_NA // 2):
        act = jnp.dot(u_ref[pr * pair:(pr + 1) * pair, :], h2t_scr[...], preferred_element_type=F32)
        for half in range(2):
            al = 2 * pr + half
            a = j * PEER_NA + al
            thr_full = [thr_scr[h, pl.ds(a, 1), :] for h in range(PEER_HEADS)]
            c_full = [c_scr[h, pl.ds(a, 1), :] for h in range(PEER_HEADS)]
            for tc in range(tt // LANES):
                ls = slice(tc * LANES, (tc + 1) * LANES)
                thr_rows = [r[:, ls] for r in thr_full]
                c_rows = [r[:, ls] for r in c_full]
                for rb in range(PEER_NKEYS // PEER_RB):
                    rs = slice(rb * PEER_RB, (rb + 1) * PEER_RB)
                    w = jnp.zeros((PEER_RB, LANES), F32)
                    for h in range(PEER_HEADS):
                        e2_t = e2_scr[h, rs, ls]
                        w = w + jnp.where(e2_t >= thr_rows[h], c_rows[h] * e2_t, 0.0)
                    rows = slice(half * PEER_NKEYS + rb * PEER_RB, half * PEER_NKEYS + (rb + 1) * PEER_RB)
                    r0 = al * PEER_NKEYS + rb * PEER_RB
                    coef_scr[r0:r0 + PEER_RB, ls] = (w * _gelu(act[rows, ls])).astype(BF16)

    acc_scr[...] += jnp.dot(vt_ref[...], coef_scr[...], preferred_element_type=F32)

    @pl.when(j == pl.num_programs(1) - 1)
    def _():
        xo_ref[...] = x_ref[...] + mod_ref[5:6, :] * acc_scr[...].T


def _peer(h2, u, v_t, wq_t, subkeys, x, mods_l, dims, n_tiles):
    B, S, L = dims
    T = x.shape[0]
    tt = PEER_TT
    ne = PEER_NA * PEER_NKEYS
    n_lat = B * S // tt
    s_tiles = S // tt
    n_cand = PEER_TOPK + SUBLANES * SUBLANES

    def mod_idx(i, j):
        return (jnp.where(i < n_lat, i // s_tiles, B), 0, 0)

    key_f32 = pltpu.VMEM((PEER_HEADS, PEER_NKEYS, tt), F32)
    const = pl.Buffered(1)
    return pl.pallas_call(
        _peer_kernel,
        grid=(n_tiles, PEER_EXPERTS // ne),
        in_specs=[
            pl.BlockSpec((tt, D_MODEL), lambda i, j: (i, 0)),
            pl.BlockSpec((ne, D_MODEL), lambda i, j: (j, 0)),
            pl.BlockSpec((None, D_MODEL, ne), lambda i, j: (j, 0, 0)),
            pl.BlockSpec(wq_t.shape, lambda i, j: (0, 0), pipeline_mode=const),
            pl.BlockSpec(subkeys.shape, lambda i, j: (0, 0, 0), pipeline_mode=const),
            pl.BlockSpec((tt, D_MODEL), lambda i, j: (i, 0)),
            pl.BlockSpec((None, N_MOD, D_MODEL), mod_idx),
        ],
        out_specs=pl.BlockSpec((tt, D_MODEL), lambda i, j: (i, 0)),
        out_shape=jax.ShapeDtypeStruct((T, D_MODEL), F32),
        scratch_shapes=[
            pltpu.VMEM((D_MODEL, tt), BF16),
            pltpu.VMEM((ne, tt), BF16),
            pltpu.VMEM((D_MODEL, tt), F32),
            key_f32, key_f32, key_f32,
            pltpu.VMEM((PEER_QDIM, tt), F32),
            pltpu.VMEM((2, PEER_NKEYS, tt), F32),
            pltpu.VMEM((PEER_NKEYS, tt), F32),
            pltpu.VMEM((PEER_NKEYS, tt), F32),
            pltpu.VMEM((3, PEER_TOPK, tt), F32),
            pltpu.VMEM((n_cand, tt), F32),
        ],
        input_output_aliases={5: 0},
        compiler_params=_cparams(("parallel", "arbitrary")),
        name="peer",
    )(h2, u, v_t, wq_t, subkeys, x, mods_l)


def _final_norm_kernel(x_ref, w_ref, o_ref):
    x = x_ref[...]
    o_ref[...] = x * lax.rsqrt(jnp.mean(x * x, axis=-1, keepdims=True) + EPS) * w_ref[...]


def _final_norm(x, w, n_rows):
    return pl.pallas_call(
        _final_norm_kernel,
        grid=(n_rows // TM,),
        in_specs=[pl.BlockSpec((TM, D_MODEL), lambda g: (g, 0)), pl.BlockSpec((1, D_MODEL), lambda g: (0, 0))],
        out_specs=pl.BlockSpec((TM, D_MODEL), lambda g: (g, 0)),
        out_shape=jax.ShapeDtypeStruct((n_rows, D_MODEL), F32),
        compiler_params=_cparams(("parallel",)),
        name="final_norm",
    )(x, w)


def _lambda_init(layer):
    return 0.8 - 0.6 * math.exp(-0.3 * layer)


def _rope_tables(S):
    rows = S // GRID_W
    row = jnp.repeat(jnp.arange(rows), GRID_W).astype(F32)
    col = jnp.tile(jnp.arange(GRID_W), rows).astype(F32)
    inv = ROPE_THETA ** (-jnp.arange(ROPE_AXIS_FREQS, dtype=F32) / ROPE_AXIS_FREQS)
    ang = jnp.concatenate([row[:, None] * inv, col[:, None] * inv], axis=-1)
    cos, sin = jnp.cos(ang), jnp.sin(ang)
    cos_t = jnp.tile(cos, (1, LANES // ROPE_HALF))
    sin_t = jnp.tile(jnp.concatenate([-sin, sin], axis=-1), (1, LANES // HEAD_DIM))
    cos_t = jnp.concatenate([cos_t, jnp.ones((TM, LANES), F32)], axis=0)
    sin_t = jnp.concatenate([sin_t, jnp.zeros((TM, LANES), F32)], axis=0)
    return cos_t, sin_t


def _block_diag(w):
    nd, nb, bd, _ = w.shape
    eye = jnp.eye(nb, dtype=w.dtype)
    return jnp.einsum("dnij,nm->dnimj", w, eye).reshape(nd, nb * bd, nb * bd)


def kernel(x, c, ctx, c_ctx, w_mod, b_mod, norm_mix, norm_ffn, w_out, attn_w_in, swa_sink, diff_lambda, diff_subln, rec_w_in, pool_w, pool_scale, lru_conv_w, lru_conv_b, lru_wa, lru_ba, lru_wx, lru_bx, lru_lambda, peer_wq, peer_subkeys, peer_u, peer_v, final_norm):
    B, S, D = x.shape
    L = ctx.shape[1]
    depth = w_mod.shape[0]
    dims = (B, S, L)
    assert D == D_MODEL and S % PEER_TT == 0 and (B * L) % PEER_TT == 0 and L % SEQ_T == 0
    assert S % GRID_W == 0 and S >= 3 * BLOCK and B + 1 <= SUBLANES and S % (DIFF_GROUP * SEQ_T) == 0
    lat_rows = B * S
    T = lat_rows + B * L

    xs = jnp.concatenate([x.reshape(lat_rows, D), ctx.reshape(B * L, D)], axis=0)
    cc = jnp.zeros((SUBLANES, D), F32).at[:B].set(c).at[B].set(c_ctx)
    mods = _modulation(cc, w_mod, b_mod)
    cos_t, sin_t = _rope_tables(S)

    for l in range(depth):
        jl = l // 2
        ctx_out = l < depth - 1
        n_rows = T if ctx_out else lat_rows
        mods_l = mods[l]
        if l % 2 == 0:
            q, k, v, dq, dk, dv = _pre_attn(xs, mods_l, norm_mix[l][None], attn_w_in[jl].astype(BF16),
                                            cos_t, sin_t, dims)
            a = _swa(q, k, v, swa_sink[jl], dims)
            bd = _diff_attn(dq, dk, dv, diff_lambda[jl], diff_subln[jl][None], _lambda_init(l), dims)
            xs, h2 = _post_mixer("attn", (a, bd), xs, mods_l, norm_ffn[l][None], w_out[l].astype(BF16),
                                 dims, n_rows // TM)
        else:
            xp, xr, g = _pre_rec(xs, mods_l, norm_mix[l][None], rec_w_in[jl].astype(BF16), dims)
            pool, a_co, b_co = _rec_mid(xp, xr, lru_conv_w[jl], lru_conv_b[jl][None],
                                        _block_diag(lru_wa[jl]).astype(BF16), lru_ba[jl],
                                        _block_diag(lru_wx[jl]).astype(BF16), lru_bx[jl], lru_lambda[jl],
                                        pool_w[jl].astype(BF16), pool_scale[jl][None], dims)
            hs = _scan(a_co, b_co, dims)
            xs, h2 = _post_mixer("rec", (pool, hs, g), xs, mods_l, norm_ffn[l][None], w_out[l].astype(BF16),
                                 dims, n_rows // TM)
        n_peer = n_rows // PEER_TT
        wq_t = peer_wq[l].astype(BF16).T
        sk = peer_subkeys[l].reshape(PEER_HEADS * 2, PEER_NKEYS, PEER_HALF)
        ne = PEER_NA * PEER_NKEYS
        v_t = peer_v[l].astype(BF16).reshape(PEER_EXPERTS // ne, ne, D).transpose(0, 2, 1)
        xs = _peer(h2, peer_u[l].astype(BF16), v_t, wq_t, sk, xs, mods_l, dims, n_peer)

    return _final_norm(xs, final_norm[None], lat_rows).reshape(B, S, D)
```

```python
import functools
import math

import jax
import jax.numpy as jnp
from jax import lax
from jax.experimental import pallas as pl
from jax.experimental.pallas import tpu as pltpu

F32 = jnp.float32
BF16 = jnp.bfloat16
HIGHEST = lax.Precision.HIGHEST

D_MODEL = 1024
HEAD_DIM = 64
EPS = 1e-6
NEG = -1e30
N_MOD = 6
SCALE = HEAD_DIM ** -0.5
ROPE_HALF = HEAD_DIM // 2
ROPE_AXIS_FREQS = ROPE_HALF // 2
ROPE_THETA = 10000.0
GRID_W = 64
SWA_Q_HEADS = 8
SWA_KV_HEADS = 2
SWA_GROUP = SWA_Q_HEADS // SWA_KV_HEADS
WINDOW = 128
BLOCK = 128
SWA_WIDTH = SWA_Q_HEADS * HEAD_DIM
SWA_KV_WIDTH = SWA_KV_HEADS * HEAD_DIM
DIFF_HEADS = 4
DIFF_V_DIM = 2 * HEAD_DIM
DIFF_QK_WIDTH = DIFF_HEADS * 2 * HEAD_DIM
DIFF_WIDTH = DIFF_HEADS * DIFF_V_DIM
POOL_WINDOWS = (2, 4, 8, 16)
POOL_GROUPS = 4
POOL_WIDTH = D_MODEL // 2
POOL_GDIM = POOL_WIDTH // POOL_GROUPS
LRU_WIDTH = D_MODEL // 2
LRU_BLOCKS = 8
CONV_W = 4
CONV_LEFT = CONV_W // 2
LRU_C = 8.0
PEER_HEADS = 8
PEER_NKEYS = 128
PEER_EXPERTS = PEER_NKEYS * PEER_NKEYS
PEER_QDIM = 256
PEER_HALF = PEER_QDIM // 2
PEER_TOPK = 16

LANES = 128
SUBLANES = 8
VMEM_LIMIT = 56 * 1024 * 1024

TM = 256
SEQ_T = 256
DIFF_GROUP = 4
DIFF_AHEAD = 2
DIFF_SUM_ROWS = 16
HALO = 8
PEER_TT = 512
PEER_NA = 8
PEER_RB = 64


def _cparams(sem):
    return pltpu.CompilerParams(dimension_semantics=sem, vmem_limit_bytes=VMEM_LIMIT)


def _nt_dot(a, b):
    return lax.dot_general(a, b, (((1,), (1,)), ((), ())), preferred_element_type=F32)


def _rmsnorm_mod(x, w, shift, scale):
    y = x * lax.rsqrt(jnp.mean(x * x, axis=-1, keepdims=True) + EPS) * w
    return y * (1.0 + scale) + shift


def _gelu(x):
    return 0.5 * x * (1.0 + lax.erf(x * (2.0 ** -0.5)))


def _mod_kernel(cc_ref, w_ref, b_ref, o_ref):
    cc = cc_ref[...]
    sc = cc * jax.nn.sigmoid(cc)
    o_ref[...] = jnp.dot(sc, w_ref[...], precision=HIGHEST, preferred_element_type=F32) + b_ref[...]


def _modulation(cc, w_mod, b_mod):
    depth = w_mod.shape[0]
    rows = cc.shape[0]
    out = pl.pallas_call(
        _mod_kernel,
        grid=(depth, N_MOD),
        in_specs=[
            pl.BlockSpec((rows, D_MODEL), lambda l, j: (0, 0)),
            pl.BlockSpec((None, D_MODEL, D_MODEL), lambda l, j: (l, 0, j)),
            pl.BlockSpec((None, 1, D_MODEL), lambda l, j: (l, 0, j)),
        ],
        out_specs=pl.BlockSpec((None, rows, D_MODEL), lambda l, j: (l, 0, j)),
        out_shape=jax.ShapeDtypeStruct((depth, rows, N_MOD * D_MODEL), F32),
        compiler_params=_cparams(("parallel", "parallel")),
        name="modulation",
    )(cc, w_mod, b_mod.reshape(depth, 1, N_MOD * D_MODEL))
    return out.reshape(depth, rows, N_MOD, D_MODEL)


def _rope128(x, cos, sin_signed, first_half):
    partner = jnp.where(first_half, pltpu.roll(x, LANES - ROPE_HALF, 1), pltpu.roll(x, ROPE_HALF, 1))
    return x * cos + partner * sin_signed


def _pre_attn_kernel(x_ref, mod_ref, nw_ref, w_ref, cos_ref, sin_ref,
                     q_ref, k_ref, dq_ref, dk_ref, v_ref, dv_ref):
    h = _rmsnorm_mod(x_ref[...], nw_ref[...], mod_ref[0:1, :], mod_ref[1:2, :])
    p = jnp.dot(h.astype(BF16), w_ref[...], preferred_element_type=F32)
    cos = cos_ref[...]
    sin = sin_ref[...]
    lane = lax.broadcasted_iota(jnp.int32, (1, LANES), 1)
    first_half = (lane % HEAD_DIM) < ROPE_HALF

    def roped(lo, width, scale):
        outs = []
        for c in range(width // LANES):
            xc = p[:, lo + c * LANES: lo + (c + 1) * LANES]
            outs.append((_rope128(xc, cos, sin, first_half) * scale).astype(BF16))
        return outs

    o1 = SWA_WIDTH
    o2 = o1 + SWA_KV_WIDTH
    o3 = o2 + SWA_KV_WIDTH
    o4 = o3 + DIFF_QK_WIDTH
    o5 = o4 + DIFF_QK_WIDTH
    for c, val in enumerate(roped(0, SWA_WIDTH, SCALE)):
        q_ref[:, c * LANES:(c + 1) * LANES] = val
    for c, val in enumerate(roped(o1, SWA_KV_WIDTH, 1.0)):
        k_ref[:, c * LANES:(c + 1) * LANES] = val
    v_ref[...] = p[:, o2:o3].T.astype(BF16)
    for c, val in enumerate(roped(o3, DIFF_QK_WIDTH, SCALE)):
        dq_ref[:, c * LANES:(c + 1) * LANES] = val
    for c, val in enumerate(roped(o4, DIFF_QK_WIDTH, 1.0)):
        dk_ref[:, c * LANES:(c + 1) * LANES] = val
    dv_ref[...] = p[:, o5:].T.astype(BF16)


def _pre_attn(x, mods_l, norm_w, w_in, cos_t, sin_t, dims):
    B, S, L = dims
    T = x.shape[0]
    n_lat = B * S // TM
    s_tiles = S // TM

    def mod_idx(g):
        return (jnp.where(g < n_lat, g // s_tiles, B), 0, 0)

    def rope_idx(g):
        return (jnp.where(g < n_lat, g % s_tiles, s_tiles), 0)

    widths = (SWA_WIDTH, SWA_KV_WIDTH, DIFF_QK_WIDTH, DIFF_QK_WIDTH)
    t_widths = (SWA_KV_WIDTH, DIFF_WIDTH)
    return pl.pallas_call(
        _pre_attn_kernel,
        grid=(T // TM,),
        in_specs=[
            pl.BlockSpec((TM, D_MODEL), lambda g: (g, 0)),
            pl.BlockSpec((None, N_MOD, D_MODEL), mod_idx),
            pl.BlockSpec((1, D_MODEL), lambda g: (0, 0)),
            pl.BlockSpec(w_in.shape, lambda g: (0, 0)),
            pl.BlockSpec((TM, LANES), rope_idx),
            pl.BlockSpec((TM, LANES), rope_idx),
        ],
        out_specs=[pl.BlockSpec((TM, w), lambda g: (g, 0)) for w in widths]
        + [pl.BlockSpec((None, w, TM), lambda g: (g, 0, 0)) for w in t_widths],
        out_shape=[jax.ShapeDtypeStruct((T, w), BF16) for w in widths]
        + [jax.ShapeDtypeStruct((T // TM, w, TM), BF16) for w in t_widths],
        compiler_params=_cparams(("parallel",)),
        name="pre_attn",
    )(x, mods_l, norm_w, w_in, cos_t, sin_t)


def _swa_kernel(q_ref, kl_ref, vtl_ref, kc_ref, vtc_ref, sink_ref, o_ref, *, n_lat_blocks, n_lat_chunks,
                n_ctx_chunks):
    j = pl.program_id(1)
    is_ctx = j >= n_lat_blocks
    jl = jnp.minimum(j, n_lat_blocks - 1)
    band_chunks = 2
    per_chunk = SEQ_T // BLOCK
    c0 = jnp.clip((jl - 1) // per_chunk, 0, n_lat_chunks - band_chunks)
    r0 = pl.multiple_of(c0 * SEQ_T, SEQ_T)
    kb = kl_ref[pl.ds(r0, band_chunks * SEQ_T), :]
    kc = kc_ref[...]
    glanes = SWA_GROUP * BLOCK
    kpos = r0 + lax.broadcasted_iota(jnp.int32, (band_chunks * SEQ_T, 1), 0)
    qpos = jl * BLOCK + lax.broadcasted_iota(jnp.int32, (1, glanes), 1) % BLOCK
    valid = jnp.logical_and(jnp.abs(qpos - kpos) <= WINDOW, jnp.logical_not(is_ctx))
    q = q_ref[...]
    for hk in range(SWA_KV_HEADS):
        ks = slice(hk * HEAD_DIM, (hk + 1) * HEAD_DIM)
        hs = [hk * SWA_GROUP + g for g in range(SWA_GROUP)]
        qg = jnp.concatenate([q[:, h * HEAD_DIM:(h + 1) * HEAD_DIM] for h in hs], axis=0)
        sink = jnp.concatenate([jnp.full((1, BLOCK), sink_ref[h], F32) for h in hs], axis=1)
        sl = jnp.where(valid, _nt_dot(kb[:, ks], qg), NEG)
        sc = _nt_dot(kc[:, ks], qg)
        m = jnp.maximum(jnp.maximum(jnp.max(sl, axis=0, keepdims=True), jnp.max(sc, axis=0, keepdims=True)), sink)
        pl_ = jnp.exp(sl - m)
        pc = jnp.exp(sc - m)
        den = jnp.sum(pl_, axis=0, keepdims=True) + jnp.sum(pc, axis=0, keepdims=True) + jnp.exp(sink - m)
        pl_b = pl_.astype(BF16)
        pc_b = pc.astype(BF16)
        o = jnp.zeros((HEAD_DIM, glanes), F32)
        for i in range(band_chunks):
            o = o + jnp.dot(vtl_ref[c0 + i, ks, :], pl_b[i * SEQ_T:(i + 1) * SEQ_T], preferred_element_type=F32)
        for i in range(n_ctx_chunks):
            o = o + jnp.dot(vtc_ref[i, ks, :], pc_b[i * SEQ_T:(i + 1) * SEQ_T], preferred_element_type=F32)
        o = o / den
        for g, h in enumerate(hs):
            o_ref[:, h * HEAD_DIM:(h + 1) * HEAD_DIM] = o[:, g * BLOCK:(g + 1) * BLOCK].T.astype(BF16)


def _swa(q, k, vt, sink, dims):
    B, S, L = dims
    T = q.shape[0]
    n_lat_blocks = S // BLOCK
    n_ctx_blocks = L // BLOCK
    n_lat_chunks = S // SEQ_T
    n_ctx_chunks = L // SEQ_T
    lat_rows = B * S

    def q_idx(b, j):
        return (jnp.where(j < n_lat_blocks, b * n_lat_blocks + j,
                          lat_rows // BLOCK + b * n_ctx_blocks + (j - n_lat_blocks)), 0)

    return pl.pallas_call(
        functools.partial(_swa_kernel, n_lat_blocks=n_lat_blocks, n_lat_chunks=n_lat_chunks,
                          n_ctx_chunks=n_ctx_chunks),
        grid=(B, n_lat_blocks + n_ctx_blocks),
        in_specs=[
            pl.BlockSpec((BLOCK, SWA_WIDTH), q_idx),
            pl.BlockSpec((S, SWA_KV_WIDTH), lambda b, j: (b, 0)),
            pl.BlockSpec((n_lat_chunks, SWA_KV_WIDTH, SEQ_T), lambda b, j: (b, 0, 0)),
            pl.BlockSpec((L, SWA_KV_WIDTH), lambda b, j: (lat_rows // L + b, 0)),
            pl.BlockSpec((n_ctx_chunks, SWA_KV_WIDTH, SEQ_T), lambda b, j: (lat_rows // L + b, 0, 0)),
            pl.BlockSpec(memory_space=pltpu.SMEM),
        ],
        out_specs=pl.BlockSpec((BLOCK, SWA_WIDTH), q_idx),
        out_shape=jax.ShapeDtypeStruct((T, SWA_WIDTH), BF16),
        compiler_params=_cparams(("parallel", "arbitrary")),
        name="swa",
    )(q, k, vt, k, vt, sink)


def _diff_kernel(q_ref, kl_ref, vtl_ref, kc_ref, vtc_ref, lam_ref, subln_ref, o_ref, qp_scr, m_scr, acc_scr, *,
                 n_lat_tiles, n_lat_chunks, n_ctx_chunks, lam_init):
    j = pl.program_id(1)
    lv = lam_ref[...]
    lam = (jnp.exp(jnp.sum(lv[0:1] * lv[1:2], axis=-1, keepdims=True))
           - jnp.exp(jnp.sum(lv[2:3] * lv[3:4], axis=-1, keepdims=True)) + lam_init)
    lane = lax.broadcasted_iota(jnp.int32, (1, DIFF_V_DIM), 1)
    heads = [slice(h * DIFF_V_DIM, (h + 1) * DIFF_V_DIM) for h in range(DIFF_HEADS)]
    ones = jnp.ones((DIFF_SUM_ROWS, SEQ_T), BF16)

    for h, vs in enumerate(heads):
        qh = q_ref[:, vs]
        zero = jnp.zeros_like(qh)
        qp_scr[2 * h] = jnp.where(lane < HEAD_DIM, qh, zero)
        qp_scr[2 * h + 1] = jnp.where(lane >= HEAD_DIM, qh, zero)
    m_scr[...] = jnp.full(m_scr.shape, -jnp.inf, F32)
    acc_scr[...] = jnp.zeros(acc_scr.shape, F32)

    def update(blocks_of_head):
        all_blocks = [blocks_of_head(vs) for vs in heads]

        def scores(ch):
            return [_nt_dot(kblk, qp_scr[ch]) for kblk, _ in all_blocks[ch // 2]]

        n_chains = 2 * DIFF_HEADS
        queue = [scores(ch) for ch in range(DIFF_AHEAD)]
        for h, vs in enumerate(heads):
            blocks = all_blocks[h]
            for m in range(2):
                ch = 2 * h + m
                ss = queue.pop(0)
                if ch + DIFF_AHEAD < n_chains:
                    queue.append(scores(ch + DIFF_AHEAD))
                m_old = m_scr[ch]
                m_blk = jnp.max(ss[0], axis=0, keepdims=True)
                for s in ss[1:]:
                    m_blk = jnp.maximum(m_blk, jnp.max(s, axis=0, keepdims=True))
                m_new = jnp.maximum(m_old, m_blk.astype(BF16).astype(F32))
                m_b = m_new.astype(BF16)
                alpha = jnp.exp(m_old - m_new)
                acc = alpha * acc_scr[ch]
                for s, (_, vtblk) in zip(ss, blocks):
                    p = jnp.exp(s.astype(BF16) - m_b)
                    acc = acc + jnp.dot(jnp.concatenate([vtblk, ones], axis=0), p, preferred_element_type=F32)
                m_scr[ch] = m_new
                acc_scr[ch] = acc

    update(lambda vs: [(kc_ref[c * SEQ_T:(c + 1) * SEQ_T, vs], vtc_ref[c, vs, :]) for c in range(n_ctx_chunks)])

    @pl.when(j < n_lat_tiles)
    def _():
        def body(g, _):
            def blocks_of_head(vs):
                blocks = []
                for i in range(DIFF_GROUP):
                    c = g * DIFF_GROUP + i
                    r0 = pl.multiple_of(c * SEQ_T, SEQ_T)
                    blocks.append((kl_ref[pl.ds(r0, SEQ_T), vs], vtl_ref[c, vs, :]))
                return blocks

            update(blocks_of_head)
            return 0

        lax.fori_loop(0, n_lat_chunks // DIFF_GROUP, body, 0)

    def normalised(ch):
        acc = acc_scr[ch]
        return acc[:DIFF_V_DIM] / acc[DIFF_V_DIM:DIFF_V_DIM + 1]

    for h, vs in enumerate(heads):
        o = (normalised(2 * h) - lam * normalised(2 * h + 1)).T
        y = o * lax.rsqrt(jnp.mean(o * o, axis=-1, keepdims=True) + EPS) * subln_ref[...]
        o_ref[:, vs] = (y * (1.0 - lam_init)).astype(BF16)


def _diff_attn(dq, dk, dvt, lam_vecs, subln, lam_init, dims):
    B, S, L = dims
    T = dq.shape[0]
    n_lat_tiles = S // SEQ_T
    n_ctx_tiles = L // SEQ_T
    lat_rows = B * S

    def q_idx(b, j):
        return (jnp.where(j < n_lat_tiles, b * n_lat_tiles + j,
                          lat_rows // SEQ_T + b * n_ctx_tiles + (j - n_lat_tiles)), 0)

    return pl.pallas_call(
        functools.partial(_diff_kernel, n_lat_tiles=n_lat_tiles, n_lat_chunks=n_lat_tiles,
                          n_ctx_chunks=n_ctx_tiles, lam_init=lam_init),
        grid=(B, n_lat_tiles + n_ctx_tiles),
        in_specs=[
            pl.BlockSpec((SEQ_T, DIFF_QK_WIDTH), q_idx),
            pl.BlockSpec((S, DIFF_QK_WIDTH), lambda b, j: (b, 0)),
            pl.BlockSpec((n_lat_tiles, DIFF_WIDTH, SEQ_T), lambda b, j: (b, 0, 0)),
            pl.BlockSpec((L, DIFF_QK_WIDTH), lambda b, j: (lat_rows // L + b, 0)),
            pl.BlockSpec((n_ctx_tiles, DIFF_WIDTH, SEQ_T), lambda b, j: (lat_rows // L + b, 0, 0)),
            pl.BlockSpec(lam_vecs.shape, lambda b, j: (0, 0)),
            pl.BlockSpec((1, DIFF_V_DIM), lambda b, j: (0, 0)),
        ],
        out_specs=pl.BlockSpec((SEQ_T, DIFF_WIDTH), q_idx),
        out_shape=jax.ShapeDtypeStruct((T, DIFF_WIDTH), BF16),
        scratch_shapes=[
            pltpu.VMEM((2 * DIFF_HEADS, SEQ_T, DIFF_V_DIM), BF16),
            pltpu.VMEM((2 * DIFF_HEADS, 1, SEQ_T), F32),
            pltpu.VMEM((2 * DIFF_HEADS, DIFF_V_DIM + DIFF_SUM_ROWS, SEQ_T), F32),
        ],
        compiler_params=_cparams(("parallel", "arbitrary")),
        name="diff_attn",
    )(dq, dk, dvt, dk, dvt, lam_vecs, subln)


def _pre_rec_kernel(x_ref, mod_ref, nw_ref, w_ref, xp_ref, xr_ref, g_ref):
    h = _rmsnorm_mod(x_ref[...], nw_ref[...], mod_ref[0:1, :], mod_ref[1:2, :])
    p = jnp.dot(h.astype(BF16), w_ref[...], preferred_element_type=F32)
    xp_ref[...] = p[:, :POOL_WIDTH]
    xr_ref[...] = p[:, POOL_WIDTH:POOL_WIDTH + LRU_WIDTH]
    g_ref[...] = p[:, POOL_WIDTH + LRU_WIDTH:]


def _pre_rec(x, mods_l, norm_w, w_in, dims):
    B, S, L = dims
    T = x.shape[0]
    n_lat = B * S // TM
    s_tiles = S // TM

    def mod_idx(g):
        return (jnp.where(g < n_lat, g // s_tiles, B), 0, 0)

    widths = (POOL_WIDTH, LRU_WIDTH, LRU_WIDTH)
    return pl.pallas_call(
        _pre_rec_kernel,
        grid=(T // TM,),
        in_specs=[
            pl.BlockSpec((TM, D_MODEL), lambda g: (g, 0)),
            pl.BlockSpec((None, N_MOD, D_MODEL), mod_idx),
            pl.BlockSpec((1, D_MODEL), lambda g: (0, 0)),
            pl.BlockSpec(w_in.shape, lambda g: (0, 0)),
        ],
        out_specs=[pl.BlockSpec((TM, w), lambda g: (g, 0)) for w in widths],
        out_shape=[jax.ShapeDtypeStruct((T, w), F32) for w in widths],
        compiler_params=_cparams(("parallel",)),
        name="pre_rec",
    )(x, mods_l, norm_w, w_in)


def _rec_mid_kernel(xp_p, xp_c, xp_n, xr_p, xr_c, xr_n, cw_ref, cb_ref, wa_ref, ba_ref, wx_ref, bx_ref,
                    lam_ref, pw_ref, ps_ref, pool_ref, a_ref, b_ref, *, n_lat_chunks, lat_chunks_per_seq,
                    ctx_chunks_per_seq, lat_len, ctx_len):
    g = pl.program_id(0)
    is_lat = g < n_lat_chunks
    cps = jnp.where(is_lat, lat_chunks_per_seq, ctx_chunks_per_seq)
    within = jnp.where(is_lat, g % lat_chunks_per_seq, (g - n_lat_chunks) % ctx_chunks_per_seq)
    has_prev = within > 0
    has_next = within < cps - 1
    seg_len = jnp.where(is_lat, lat_len, ctx_len)
    ext_rows = SEQ_T + 2 * HALO

    def extended(prev_ref, cur_ref, next_ref):
        prev = jnp.where(has_prev, prev_ref[SEQ_T - HALO:, :], 0.0)
        nxt = jnp.where(has_next, next_ref[:HALO, :], 0.0)
        return jnp.concatenate([prev, cur_ref[...], nxt], axis=0)

    def shifted(ext, off):
        return pltpu.roll(ext, (-off) % ext_rows, 0)[HALO:HALO + SEQ_T, :]

    xp_ext = extended(xp_p, xp_c, xp_n)
    t = within * SEQ_T + lax.broadcasted_iota(jnp.int32, (SEQ_T, 1), 0)
    for gi, w in enumerate(POOL_WINDOWS):
        cols = slice(gi * POOL_GDIM, (gi + 1) * POOL_GDIM)
        eg = xp_ext[:, cols]
        tot = shifted(eg, -(w // 2))
        for off in range(-(w // 2) + 1, w - w // 2):
            tot = tot + shifted(eg, off)
        lo = jnp.clip(t - w // 2, 0, seg_len)
        hi = jnp.clip(t - w // 2 + w, 0, seg_len)
        cnt = (hi - lo).astype(F32)
        d = tot / cnt - xp_c[:, cols]
        y = jnp.dot(d.astype(BF16), pw_ref[gi], preferred_element_type=F32)
        pool_ref[:, cols] = (y * ps_ref[:, cols]).astype(BF16)

    xr_ext = extended(xr_p, xr_c, xr_n)
    u = cb_ref[...] + cw_ref[0:1, :] * shifted(xr_ext, -CONV_LEFT)
    for k in range(1, CONV_W):
        u = u + cw_ref[k:k + 1, :] * shifted(xr_ext, k - CONV_LEFT)
    ub = u.astype(BF16)
    for d in range(2):
        r = jax.nn.sigmoid(jnp.dot(ub, wa_ref[d], preferred_element_type=F32) + ba_ref[d:d + 1, :])
        i = jax.nn.sigmoid(jnp.dot(ub, wx_ref[d], preferred_element_type=F32) + bx_ref[d:d + 1, :])
        nl = -lam_ref[d:d + 1, :]
        softplus = jnp.maximum(nl, 0.0) + jnp.log1p(jnp.exp(-jnp.abs(nl)))
        log_a = -LRU_C * r * softplus
        a_ref[d] = jnp.exp(log_a)
        th = jnp.tanh(log_a)
        b_ref[d] = jnp.sqrt(-2.0 * th / (1.0 - th)) * (i * u)


def _rec_mid(xp, xr, conv_w, conv_b, wa_bd, ba, wx_bd, bx, lam, pool_w, pool_scale, dims):
    B, S, L = dims
    T = xp.shape[0]
    n_chunks = T // SEQ_T

    def cur(g):
        return (g, 0)

    def prv(g):
        return (jnp.maximum(g - 1, 0), 0)

    def nxt(g):
        return (jnp.minimum(g + 1, n_chunks - 1), 0)

    tile = lambda idx: pl.BlockSpec((SEQ_T, LRU_WIDTH), idx)
    full = lambda arr: pl.BlockSpec(arr.shape, lambda g: (0,) * arr.ndim)
    kern = functools.partial(_rec_mid_kernel, n_lat_chunks=B * S // SEQ_T, lat_chunks_per_seq=S // SEQ_T,
                             ctx_chunks_per_seq=L // SEQ_T, lat_len=S, ctx_len=L)
    return pl.pallas_call(
        kern,
        grid=(n_chunks,),
        in_specs=[tile(prv), tile(cur), tile(nxt), tile(prv), tile(cur), tile(nxt),
                  full(conv_w), full(conv_b), full(wa_bd), full(ba), full(wx_bd), full(bx), full(lam),
                  full(pool_w), full(pool_scale)],
        out_specs=[pl.BlockSpec((SEQ_T, POOL_WIDTH), cur),
                   pl.BlockSpec((2, SEQ_T, LRU_WIDTH), lambda g: (0, g, 0)),
                   pl.BlockSpec((2, SEQ_T, LRU_WIDTH), lambda g: (0, g, 0))],
        out_shape=[jax.ShapeDtypeStruct((T, POOL_WIDTH), BF16),
                   jax.ShapeDtypeStruct((2, T, LRU_WIDTH), F32),
                   jax.ShapeDtypeStruct((2, T, LRU_WIDTH), F32)],
        compiler_params=_cparams(("parallel",)),
        name="rec_mid",
    )(xp, xp, xp, xr, xr, xr, conv_w, conv_b, wa_bd, ba, wx_bd, bx, lam, pool_w, pool_scale)


def _scan_kernel(a_ref, b_ref, h_ref, carry_ref):
    d = pl.program_id(1)
    s = pl.program_id(2)

    @pl.when(s == 0)
    def _():
        carry_ref[...] = jnp.zeros_like(carry_ref)

    def run(reverse):
        def body(i, h):
            t = (SEQ_T - 1 - i) if reverse else i
            h = a_ref[pl.ds(t, 1), :] * h + b_ref[pl.ds(t, 1), :]
            h_ref[pl.ds(t, 1), :] = h
            return h

        carry_ref[...] = lax.fori_loop(0, SEQ_T, body, carry_ref[...], unroll=8)

    @pl.when(d == 0)
    def _():
        run(False)

    @pl.when(d == 1)
    def _():
        run(True)


def _scan(a, b, dims):
    B, S, L = dims
    T = a.shape[1]
    lat = S // SEQ_T
    ctx = L // SEQ_T
    lat_base = 0
    ctx_base = B * S // SEQ_T

    def idx(bi, d, s):
        fwd = jnp.where(s < ctx, ctx_base + bi * ctx + s, lat_base + bi * lat + (s - ctx))
        rev = jnp.where(s < ctx, ctx_base + bi * ctx + (ctx - 1 - s), lat_base + bi * lat + (lat - 1 - (s - ctx)))
        return (d, jnp.where(d == 0, fwd, rev), 0)

    spec = pl.BlockSpec((None, SEQ_T, LRU_WIDTH), idx)
    return pl.pallas_call(
        _scan_kernel,
        grid=(B, 2, lat + ctx),
        in_specs=[spec, spec],
        out_specs=spec,
        out_shape=jax.ShapeDtypeStruct((2, T, LRU_WIDTH), F32),
        scratch_shapes=[pltpu.VMEM((1, LRU_WIDTH), F32)],
        compiler_params=_cparams(("parallel", "parallel", "arbitrary")),
        name="lru_scan",
    )(a, b)


def _post_common(y, x_ref, mod_ref, nw_ref, xo_ref, h2_ref):
    x_new = x_ref[...] + mod_ref[2:3, :] * y
    xo_ref[...] = x_new
    h2_ref[...] = _rmsnorm_mod(x_new, nw_ref[...], mod_ref[3:4, :], mod_ref[4:5, :]).astype(BF16)


def _post_attn_kernel(a_ref, b_ref, x_ref, mod_ref, nw_ref, w_ref, xo_ref, h2_ref):
    half = a_ref.shape[1]
    y = (jnp.dot(a_ref[...], w_ref[:half, :], preferred_element_type=F32)
         + jnp.dot(b_ref[...], w_ref[half:, :], preferred_element_type=F32))
    _post_common(y, x_ref, mod_ref, nw_ref, xo_ref, h2_ref)


def _post_rec_kernel(pool_ref, hs_ref, g_ref, x_ref, mod_ref, nw_ref, w_ref, xo_ref, h2_ref):
    half = pool_ref.shape[1]
    rec = ((hs_ref[0] + hs_ref[1]) * _gelu(g_ref[...])).astype(BF16)
    y = (jnp.dot(pool_ref[...], w_ref[:half, :], preferred_element_type=F32)
         + jnp.dot(rec, w_ref[half:, :], preferred_element_type=F32))
    _post_common(y, x_ref, mod_ref, nw_ref, xo_ref, h2_ref)


def _post_mixer(kind, parts, x, mods_l, norm_w, w_out, dims, n_tiles):
    B, S, L = dims
    T = x.shape[0]
    n_lat = B * S // TM
    s_tiles = S // TM

    def mod_idx(g):
        return (jnp.where(g < n_lat, g // s_tiles, B), 0, 0)

    row = lambda w: pl.BlockSpec((TM, w), lambda g: (g, 0))
    if kind == "attn":
        kern = _post_attn_kernel
        part_specs = [row(SWA_WIDTH), row(DIFF_WIDTH)]
    else:
        kern = _post_rec_kernel
        part_specs = [row(POOL_WIDTH), pl.BlockSpec((2, TM, LRU_WIDTH), lambda g: (0, g, 0)), row(LRU_WIDTH)]
    n_in = len(parts)
    return pl.pallas_call(
        kern,
        grid=(n_tiles,),
        in_specs=part_specs + [
            row(D_MODEL),
            pl.BlockSpec((None, N_MOD, D_MODEL), mod_idx),
            pl.BlockSpec((1, D_MODEL), lambda g: (0, 0)),
            pl.BlockSpec(w_out.shape, lambda g: (0, 0)),
        ],
        out_specs=[row(D_MODEL), row(D_MODEL)],
        out_shape=[jax.ShapeDtypeStruct((T, D_MODEL), F32), jax.ShapeDtypeStruct((T, D_MODEL), BF16)],
        input_output_aliases={n_in: 0},
        compiler_params=_cparams(("parallel",)),
        name="post_" + kind,
    )(*parts, x, mods_l, norm_w, w_out)


def _peer_cand_blocks():
    return [(i, PEER_TOPK // (i + 1)) for i in range(1, SUBLANES)]


PEER_TAG_BITS = 31


def _peer_tagged(e, rank):
    bits = pltpu.bitcast(e, jnp.int32)
    tag = PEER_TAG_BITS - rank.astype(jnp.int32)
    return pltpu.bitcast((bits & ~PEER_TAG_BITS) | tag, F32)


def _peer_routing(h2_ref, wq_ref, sk_ref, thr_ref, c_ref, e2_ref,
                  q_scr, s_scr, cur_scr, rank_scr, top_scr, cand_scr):
    neg_inf = -jnp.inf
    n_cand = cand_scr.shape[0]

    def extract_step(k, src_ref, dst_ref, rank_ref=None):
        cur = src_ref[...]
        m = jnp.max(cur, axis=0, keepdims=True)
        dst_ref[pl.ds(k, 1), :] = m
        hit = cur == m
        src_ref[...] = jnp.where(hit, neg_inf, cur)
        if rank_ref is not None:
            rank_ref[...] = jnp.where(hit, lax.convert_element_type(k + 1, F32), rank_ref[...])

    def head_body(h, _):
        q0 = pl.multiple_of(h * PEER_QDIM, PEER_QDIM)
        q_scr[...] = _nt_dot(wq_ref[pl.ds(q0, PEER_QDIM), :], h2_ref[...])
        for p in range(2):
            s = jnp.dot(sk_ref[h * 2 + p], q_scr[p * PEER_HALF:(p + 1) * PEER_HALF, :], precision=HIGHEST,
                        preferred_element_type=F32)
            s_scr[p] = s
            cur_scr[p] = s
        rank_scr[...] = jnp.full(rank_scr.shape, PEER_TOPK + 1.0, F32)

        def top_body(k, _):
            extract_step(k, cur_scr.at[0], top_scr.at[0])
            extract_step(k, cur_scr.at[1], top_scr.at[1], rank_scr)
            return 0

        lax.fori_loop(0, PEER_TOPK, top_body, 0)
        v1 = top_scr[0]
        v2 = top_scr[1]
        cand_scr[0:PEER_TOPK, :] = v1[0:1, :] + v2
        row = lax.broadcasted_iota(jnp.int32, (SUBLANES, 1), 0)
        for i, n_i in _peer_cand_blocks():
            blk = jnp.where(row < n_i, v1[i:i + 1, :] + v2[0:SUBLANES, :], neg_inf)
            cand_scr[PEER_TOPK + (i - 1) * SUBLANES: PEER_TOPK + i * SUBLANES, :] = blk
        cand_scr[n_cand - SUBLANES:, :] = v1[SUBLANES:, :] + v2[0:1, :]
        cand = cand_scr[...]

        def cand_body(k, _):
            extract_step(k, cand_scr, top_scr.at[2])
            return 0

        lax.fori_loop(0, PEER_TOPK, cand_body, 0)
        tau = top_scr[2, PEER_TOPK - 1:PEER_TOPK, :]
        top = v1[0:1, :] + v2[0:1, :]
        z = jnp.sum(jnp.where(cand >= tau, jnp.exp(cand - top), 0.0), axis=0, keepdims=True)

        s1 = s_scr[0]
        ranks = (lax.broadcasted_iota(jnp.int32, (PEER_TOPK, 1), 0) + 1).astype(F32)
        top_scr[2] = _peer_tagged(jnp.exp(v2 - v2[0:1, :]), ranks)
        thr_ref[h] = jnp.full(s1.shape, jnp.inf, F32)

        def thr_body(jj, _):
            hit = s1 + top_scr[1, pl.ds(jj, 1), :] >= tau
            thr_ref[h] = jnp.where(hit, top_scr[2, pl.ds(jj, 1), :], thr_ref[h])
            return 0

        lax.fori_loop(0, PEER_TOPK, thr_body, 0)
        c_ref[h] = jnp.exp(s1 - v1[0:1, :]) / z
        e2_ref[h] = _peer_tagged(jnp.exp(s_scr[1] - v2[0:1, :]), rank_scr[...])
        return 0

    lax.fori_loop(0, PEER_HEADS, head_body, 0)


def _peer_kernel(h2_ref, u_ref, vt_ref, wq_ref, sk_ref, x_ref, mod_ref, xo_ref,
                 h2t_scr, coef_scr, acc_scr, thr_scr, c_scr, e2_scr,
                 q_scr, s_scr, cur_scr, rank_scr, top_scr, cand_scr):
    j = pl.program_id(1)
    tt = h2_ref.shape[0]
    pair = 2 * PEER_NKEYS

    @pl.when(j == 0)
    def _():
        acc_scr[...] = jnp.zeros_like(acc_scr)
        h2t_scr[...] = h2_ref[...].T
        _peer_routing(h2_ref, wq_ref, sk_ref, thr_scr, c_scr, e2_scr,
                      q_scr, s_scr, cur_scr, rank_scr, top_scr, cand_scr)

    for pr in range(PEER_NA // 2):
        act = jnp.dot(u_ref[pr * pair:(pr + 1) * pair, :], h2t_scr[...], preferred_element_type=F32)
        for half in range(2):
            al = 2 * pr + half
            a = j * PEER_NA + al
            thr_full = [thr_scr[h, pl.ds(a, 1), :] for h in range(PEER_HEADS)]
            c_full = [c_scr[h, pl.ds(a, 1), :] for h in range(PEER_HEADS)]
            for tc in range(tt // LANES):
                ls = slice(tc * LANES, (tc + 1) * LANES)
                thr_rows = [r[:, ls] for r in thr_full]
                c_rows = [r[:, ls] for r in c_full]
                for rb in range(PEER_NKEYS // PEER_RB):
                    rs = slice(rb * PEER_RB, (rb + 1) * PEER_RB)
                    w = jnp.zeros((PEER_RB, LANES), F32)
                    for h in range(PEER_HEADS):
                        e2_t = e2_scr[h, rs, ls]
                        w = w + jnp.where(e2_t >= thr_rows[h], c_rows[h] * e2_t, 0.0)
                    rows = slice(half * PEER_NKEYS + rb * PEER_RB, half * PEER_NKEYS + (rb + 1) * PEER_RB)
                    r0 = al * PEER_NKEYS + rb * PEER_RB
                    coef_scr[r0:r0 + PEER_RB, ls] = (w * _gelu(act[rows, ls])).astype(BF16)

    acc_scr[...] += jnp.dot(vt_ref[...], coef_scr[...], preferred_element_type=F32)

    @pl.when(j == pl.num_programs(1) - 1)
    def _():
        xo_ref[...] = x_ref[...] + mod_ref[5:6, :] * acc_scr[...].T


def _peer(h2, u, v_t, wq_t, subkeys, x, mods_l, dims, n_tiles):
    B, S, L = dims
    T = x.shape[0]
    tt = PEER_TT
    ne = PEER_NA * PEER_NKEYS
    n_lat = B * S // tt
    s_tiles = S // tt
    n_cand = PEER_TOPK + SUBLANES * SUBLANES

    def mod_idx(i, j):
        return (jnp.where(i < n_lat, i // s_tiles, B), 0, 0)

    key_f32 = pltpu.VMEM((PEER_HEADS, PEER_NKEYS, tt), F32)
    const = pl.Buffered(1)
    return pl.pallas_call(
        _peer_kernel,
        grid=(n_tiles, PEER_EXPERTS // ne),
        in_specs=[
            pl.BlockSpec((tt, D_MODEL), lambda i, j: (i, 0)),
            pl.BlockSpec((ne, D_MODEL), lambda i, j: (j, 0)),
            pl.BlockSpec((None, D_MODEL, ne), lambda i, j: (j, 0, 0)),
            pl.BlockSpec(wq_t.shape, lambda i, j: (0, 0), pipeline_mode=const),
            pl.BlockSpec(subkeys.shape, lambda i, j: (0, 0, 0), pipeline_mode=const),
            pl.BlockSpec((tt, D_MODEL), lambda i, j: (i, 0)),
            pl.BlockSpec((None, N_MOD, D_MODEL), mod_idx),
        ],
        out_specs=pl.BlockSpec((tt, D_MODEL), lambda i, j: (i, 0)),
        out_shape=jax.ShapeDtypeStruct((T, D_MODEL), F32),
        scratch_shapes=[
            pltpu.VMEM((D_MODEL, tt), BF16),
            pltpu.VMEM((ne, tt), BF16),
            pltpu.VMEM((D_MODEL, tt), F32),
            key_f32, key_f32, key_f32,
            pltpu.VMEM((PEER_QDIM, tt), F32),
            pltpu.VMEM((2, PEER_NKEYS, tt), F32),
            pltpu.VMEM((2, PEER_NKEYS, tt), F32),
            pltpu.VMEM((PEER_NKEYS, tt), F32),
            pltpu.VMEM((3, PEER_TOPK, tt), F32),
            pltpu.VMEM((n_cand, tt), F32),
        ],
        input_output_aliases={5: 0},
        compiler_params=_cparams(("parallel", "arbitrary")),
        name="peer",
    )(h2, u, v_t, wq_t, subkeys, x, mods_l)


def _final_norm_kernel(x_ref, w_ref, o_ref):
    x = x_ref[...]
    o_ref[...] = x * lax.rsqrt(jnp.mean(x * x, axis=-1, keepdims=True) + EPS) * w_ref[...]


def _final_norm(x, w, n_rows):
    return pl.pallas_call(
        _final_norm_kernel,
        grid=(n_rows // TM,),
        in_specs=[pl.BlockSpec((TM, D_MODEL), lambda g: (g, 0)), pl.BlockSpec((1, D_MODEL), lambda g: (0, 0))],
        out_specs=pl.BlockSpec((TM, D_MODEL), lambda g: (g, 0)),
        out_shape=jax.ShapeDtypeStruct((n_rows, D_MODEL), F32),
        compiler_params=_cparams(("parallel",)),
        name="final_norm",
    )(x, w)


def _lambda_init(layer):
    return 0.8 - 0.6 * math.exp(-0.3 * layer)


def _rope_tables(S):
    rows = S // GRID_W
    row = jnp.repeat(jnp.arange(rows), GRID_W).astype(F32)
    col = jnp.tile(jnp.arange(GRID_W), rows).astype(F32)
    inv = ROPE_THETA ** (-jnp.arange(ROPE_AXIS_FREQS, dtype=F32) / ROPE_AXIS_FREQS)
    ang = jnp.concatenate([row[:, None] * inv, col[:, None] * inv], axis=-1)
    cos, sin = jnp.cos(ang), jnp.sin(ang)
    cos_t = jnp.tile(cos, (1, LANES // ROPE_HALF))
    sin_t = jnp.tile(jnp.concatenate([-sin, sin], axis=-1), (1, LANES // HEAD_DIM))
    cos_t = jnp.concatenate([cos_t, jnp.ones((TM, LANES), F32)], axis=0)
    sin_t = jnp.concatenate([sin_t, jnp.zeros((TM, LANES), F32)], axis=0)
    return cos_t, sin_t


def _block_diag(w):
    nd, nb, bd, _ = w.shape
    eye = jnp.eye(nb, dtype=w.dtype)
    return jnp.einsum("dnij,nm->dnimj", w, eye).reshape(nd, nb * bd, nb * bd)


def kernel(x, c, ctx, c_ctx, w_mod, b_mod, norm_mix, norm_ffn, w_out, attn_w_in, swa_sink, diff_lambda, diff_subln, rec_w_in, pool_w, pool_scale, lru_conv_w, lru_conv_b, lru_wa, lru_ba, lru_wx, lru_bx, lru_lambda, peer_wq, peer_subkeys, peer_u, peer_v, final_norm):
    B, S, D = x.shape
    L = ctx.shape[1]
    depth = w_mod.shape[0]
    dims = (B, S, L)
    assert D == D_MODEL and S % PEER_TT == 0 and (B * L) % PEER_TT == 0 and L % SEQ_T == 0
    assert S % GRID_W == 0 and S >= 3 * BLOCK and B + 1 <= SUBLANES and S % (DIFF_GROUP * SEQ_T) == 0
    lat_rows = B * S
    T = lat_rows + B * L

    xs = jnp.concatenate([x.reshape(lat_rows, D), ctx.reshape(B * L, D)], axis=0)
    cc = jnp.zeros((SUBLANES, D), F32).at[:B].set(c).at[B].set(c_ctx)
    mods = _modulation(cc, w_mod, b_mod)
    cos_t, sin_t = _rope_tables(S)

    for l in range(depth):
        jl = l // 2
        ctx_out = l < depth - 1
        n_rows = T if ctx_out else lat_rows
        mods_l = mods[l]
        if l % 2 == 0:
            q, k, dq, dk, vt, dvt = _pre_attn(xs, mods_l, norm_mix[l][None], attn_w_in[jl].astype(BF16),
                                              cos_t, sin_t, dims)
            a = _swa(q, k, vt, swa_sink[jl], dims)
            bd = _diff_attn(dq, dk, dvt, diff_lambda[jl], diff_subln[jl][None], _lambda_init(l), dims)
            xs, h2 = _post_mixer("attn", (a, bd), xs, mods_l, norm_ffn[l][None], w_out[l].astype(BF16),
                                 dims, n_rows // TM)
        else:
            xp, xr, g = _pre_rec(xs, mods_l, norm_mix[l][None], rec_w_in[jl].astype(BF16), dims)
            pool, a_co, b_co = _rec_mid(xp, xr, lru_conv_w[jl], lru_conv_b[jl][None],
                                        _block_diag(lru_wa[jl]).astype(BF16), lru_ba[jl],
                                        _block_diag(lru_wx[jl]).astype(BF16), lru_bx[jl], lru_lambda[jl],
                                        pool_w[jl].astype(BF16), pool_scale[jl][None], dims)
            hs = _scan(a_co, b_co, dims)
            xs, h2 = _post_mixer("rec", (pool, hs, g), xs, mods_l, norm_ffn[l][None], w_out[l].astype(BF16),
                                 dims, n_rows // TM)
        n_peer = n_rows // PEER_TT
        wq_t = peer_wq[l].astype(BF16).T
        sk = peer_subkeys[l].reshape(PEER_HEADS * 2, PEER_NKEYS, PEER_HALF)
        ne = PEER_NA * PEER_NKEYS
        v_t = peer_v[l].astype(BF16).reshape(PEER_EXPERTS // ne, ne, D).transpose(0, 2, 1)
        xs = _peer(h2, peer_u[l].astype(BF16), v_t, wq_t, sk, xs, mods_l, dims, n_peer)

    return _final_norm(xs, final_norm[None], lat_rows).reshape(B, S, D)
```

```python
import functools
import math

import jax
import jax.numpy as jnp
from jax import lax
from jax.experimental import pallas as pl
from jax.experimental.pallas import tpu as pltpu

F32 = jnp.float32
BF16 = jnp.bfloat16
HIGHEST = lax.Precision.HIGHEST

D_MODEL = 1024
HEAD_DIM = 64
EPS = 1e-6
NEG = -1e30
N_MOD = 6
SCALE = HEAD_DIM ** -0.5
ROPE_HALF = HEAD_DIM // 2
ROPE_AXIS_FREQS = ROPE_HALF // 2
ROPE_THETA = 10000.0
GRID_W = 64
SWA_Q_HEADS = 8
SWA_KV_HEADS = 2
SWA_GROUP = SWA_Q_HEADS // SWA_KV_HEADS
WINDOW = 128
BLOCK = 128
SWA_WIDTH = SWA_Q_HEADS * HEAD_DIM
SWA_KV_WIDTH = SWA_KV_HEADS * HEAD_DIM
DIFF_HEADS = 4
DIFF_V_DIM = 2 * HEAD_DIM
DIFF_QK_WIDTH = DIFF_HEADS * 2 * HEAD_DIM
DIFF_WIDTH = DIFF_HEADS * DIFF_V_DIM
POOL_WINDOWS = (2, 4, 8, 16)
POOL_GROUPS = 4
POOL_WIDTH = D_MODEL // 2
POOL_GDIM = POOL_WIDTH // POOL_GROUPS
LRU_WIDTH = D_MODEL // 2
LRU_BLOCKS = 8
CONV_W = 4
CONV_LEFT = CONV_W // 2
LRU_C = 8.0
PEER_HEADS = 8
PEER_NKEYS = 128
PEER_EXPERTS = PEER_NKEYS * PEER_NKEYS
PEER_QDIM = 256
PEER_HALF = PEER_QDIM // 2
PEER_TOPK = 16

LANES = 128
SUBLANES = 8
VMEM_LIMIT = 56 * 1024 * 1024

TM = 256
SEQ_T = 256
DIFF_GROUP = 16
DIFF_AHEAD = 2
DIFF_SUM_ROWS = 16
HALO = 8
PEER_TT = 512
PEER_NA = 16
PEER_UP_GROUP = 2
PEER_RB = 64


def _cparams(sem):
    return pltpu.CompilerParams(dimension_semantics=sem, vmem_limit_bytes=VMEM_LIMIT)


def _nt_dot(a, b):
    return lax.dot_general(a, b, (((1,), (1,)), ((), ())), preferred_element_type=F32)


def _rmsnorm_mod(x, w, shift, scale):
    y = x * lax.rsqrt(jnp.mean(x * x, axis=-1, keepdims=True) + EPS) * w
    return y * (1.0 + scale) + shift


def _gelu(x):
    return 0.5 * x * (1.0 + lax.erf(x * (2.0 ** -0.5)))


def _mod_kernel(cc_ref, w_ref, b_ref, o_ref):
    cc = cc_ref[...]
    sc = cc * jax.nn.sigmoid(cc)
    o_ref[...] = jnp.dot(sc, w_ref[...], precision=HIGHEST, preferred_element_type=F32) + b_ref[...]


def _modulation(cc, w_mod, b_mod):
    depth = w_mod.shape[0]
    rows = cc.shape[0]
    out = pl.pallas_call(
        _mod_kernel,
        grid=(depth, N_MOD),
        in_specs=[
            pl.BlockSpec((rows, D_MODEL), lambda l, j: (0, 0)),
            pl.BlockSpec((None, D_MODEL, D_MODEL), lambda l, j: (l, 0, j)),
            pl.BlockSpec((None, 1, D_MODEL), lambda l, j: (l, 0, j)),
        ],
        out_specs=pl.BlockSpec((None, rows, D_MODEL), lambda l, j: (l, 0, j)),
        out_shape=jax.ShapeDtypeStruct((depth, rows, N_MOD * D_MODEL), F32),
        compiler_params=_cparams(("parallel", "parallel")),
        name="modulation",
    )(cc, w_mod, b_mod.reshape(depth, 1, N_MOD * D_MODEL))
    return out.reshape(depth, rows, N_MOD, D_MODEL)


def _rope128(x, cos, sin_signed, first_half):
    partner = jnp.where(first_half, pltpu.roll(x, LANES - ROPE_HALF, 1), pltpu.roll(x, ROPE_HALF, 1))
    return x * cos + partner * sin_signed


def _pre_attn_kernel(x_ref, mod_ref, nw_ref, w_ref, cos_ref, sin_ref,
                     q_ref, k_ref, dq_ref, dk_ref, v_ref, dv_ref):
    h = _rmsnorm_mod(x_ref[...], nw_ref[...], mod_ref[0:1, :], mod_ref[1:2, :])
    p = jnp.dot(h.astype(BF16), w_ref[...], preferred_element_type=F32)
    cos = cos_ref[...]
    sin = sin_ref[...]
    lane = lax.broadcasted_iota(jnp.int32, (1, LANES), 1)
    first_half = (lane % HEAD_DIM) < ROPE_HALF

    def roped(lo, width, scale):
        outs = []
        for c in range(width // LANES):
            xc = p[:, lo + c * LANES: lo + (c + 1) * LANES]
            outs.append((_rope128(xc, cos, sin, first_half) * scale).astype(BF16))
        return outs

    o1 = SWA_WIDTH
    o2 = o1 + SWA_KV_WIDTH
    o3 = o2 + SWA_KV_WIDTH
    o4 = o3 + DIFF_QK_WIDTH
    o5 = o4 + DIFF_QK_WIDTH
    for c, val in enumerate(roped(0, SWA_WIDTH, SCALE)):
        q_ref[:, c * LANES:(c + 1) * LANES] = val
    for c, val in enumerate(roped(o1, SWA_KV_WIDTH, 1.0)):
        k_ref[:, c * LANES:(c + 1) * LANES] = val
    v_ref[...] = p[:, o2:o3].T.astype(BF16)
    for c, val in enumerate(roped(o3, DIFF_QK_WIDTH, SCALE)):
        dq_ref[:, c * LANES:(c + 1) * LANES] = val
    for c, val in enumerate(roped(o4, DIFF_QK_WIDTH, 1.0)):
        dk_ref[:, c * LANES:(c + 1) * LANES] = val
    dv_ref[...] = p[:, o5:].T.astype(BF16)


def _pre_attn(x, mods_l, norm_w, w_in, cos_t, sin_t, dims):
    B, S, L = dims
    T = x.shape[0]
    n_lat = B * S // TM
    s_tiles = S // TM

    def mod_idx(g):
        return (jnp.where(g < n_lat, g // s_tiles, B), 0, 0)

    def rope_idx(g):
        return (jnp.where(g < n_lat, g % s_tiles, s_tiles), 0)

    widths = (SWA_WIDTH, SWA_KV_WIDTH, DIFF_QK_WIDTH, DIFF_QK_WIDTH)
    t_widths = (SWA_KV_WIDTH, DIFF_WIDTH)
    return pl.pallas_call(
        _pre_attn_kernel,
        grid=(T // TM,),
        in_specs=[
            pl.BlockSpec((TM, D_MODEL), lambda g: (g, 0)),
            pl.BlockSpec((None, N_MOD, D_MODEL), mod_idx),
            pl.BlockSpec((1, D_MODEL), lambda g: (0, 0)),
            pl.BlockSpec(w_in.shape, lambda g: (0, 0)),
            pl.BlockSpec((TM, LANES), rope_idx),
            pl.BlockSpec((TM, LANES), rope_idx),
        ],
        out_specs=[pl.BlockSpec((TM, w), lambda g: (g, 0)) for w in widths]
        + [pl.BlockSpec((None, w, TM), lambda g: (g, 0, 0)) for w in t_widths],
        out_shape=[jax.ShapeDtypeStruct((T, w), BF16) for w in widths]
        + [jax.ShapeDtypeStruct((T // TM, w, TM), BF16) for w in t_widths],
        compiler_params=_cparams(("parallel",)),
        name="pre_attn",
    )(x, mods_l, norm_w, w_in, cos_t, sin_t)


def _swa_kernel(q_ref, kl_ref, vtl_ref, kc_ref, vtc_ref, sink_ref, o_ref, *, n_lat_blocks, n_lat_chunks,
                n_ctx_chunks):
    j = pl.program_id(1)
    is_ctx = j >= n_lat_blocks
    jl = jnp.minimum(j, n_lat_blocks - 1)
    band_chunks = 2
    per_chunk = SEQ_T // BLOCK
    c0 = jnp.clip((jl - 1) // per_chunk, 0, n_lat_chunks - band_chunks)
    r0 = pl.multiple_of(c0 * SEQ_T, SEQ_T)
    kb = kl_ref[pl.ds(r0, band_chunks * SEQ_T), :]
    kc = kc_ref[...]
    glanes = SWA_GROUP * BLOCK
    kpos = r0 + lax.broadcasted_iota(jnp.int32, (band_chunks * SEQ_T, 1), 0)
    qpos = jl * BLOCK + lax.broadcasted_iota(jnp.int32, (1, glanes), 1) % BLOCK
    valid = jnp.logical_and(jnp.abs(qpos - kpos) <= WINDOW, jnp.logical_not(is_ctx))
    q = q_ref[...]
    for hk in range(SWA_KV_HEADS):
        ks = slice(hk * HEAD_DIM, (hk + 1) * HEAD_DIM)
        hs = [hk * SWA_GROUP + g for g in range(SWA_GROUP)]
        qg = jnp.concatenate([q[:, h * HEAD_DIM:(h + 1) * HEAD_DIM] for h in hs], axis=0)
        sink = jnp.concatenate([jnp.full((1, BLOCK), sink_ref[h], F32) for h in hs], axis=1)
        sl = jnp.where(valid, _nt_dot(kb[:, ks], qg), NEG)
        sc = _nt_dot(kc[:, ks], qg)
        m = jnp.maximum(jnp.maximum(jnp.max(sl, axis=0, keepdims=True), jnp.max(sc, axis=0, keepdims=True)), sink)
        pl_ = jnp.exp(sl - m)
        pc = jnp.exp(sc - m)
        den = jnp.sum(pl_, axis=0, keepdims=True) + jnp.sum(pc, axis=0, keepdims=True) + jnp.exp(sink - m)
        pl_b = pl_.astype(BF16)
        pc_b = pc.astype(BF16)
        o = jnp.zeros((HEAD_DIM, glanes), F32)
        for i in range(band_chunks):
            o = o + jnp.dot(vtl_ref[c0 + i, ks, :], pl_b[i * SEQ_T:(i + 1) * SEQ_T], preferred_element_type=F32)
        for i in range(n_ctx_chunks):
            o = o + jnp.dot(vtc_ref[i, ks, :], pc_b[i * SEQ_T:(i + 1) * SEQ_T], preferred_element_type=F32)
        o = o / den
        for g, h in enumerate(hs):
            o_ref[:, h * HEAD_DIM:(h + 1) * HEAD_DIM] = o[:, g * BLOCK:(g + 1) * BLOCK].T.astype(BF16)


def _swa(q, k, vt, sink, dims):
    B, S, L = dims
    T = q.shape[0]
    n_lat_blocks = S // BLOCK
    n_ctx_blocks = L // BLOCK
    n_lat_chunks = S // SEQ_T
    n_ctx_chunks = L // SEQ_T
    lat_rows = B * S

    def q_idx(b, j):
        return (jnp.where(j < n_lat_blocks, b * n_lat_blocks + j,
                          lat_rows // BLOCK + b * n_ctx_blocks + (j - n_lat_blocks)), 0)

    return pl.pallas_call(
        functools.partial(_swa_kernel, n_lat_blocks=n_lat_blocks, n_lat_chunks=n_lat_chunks,
                          n_ctx_chunks=n_ctx_chunks),
        grid=(B, n_lat_blocks + n_ctx_blocks),
        in_specs=[
            pl.BlockSpec((BLOCK, SWA_WIDTH), q_idx),
            pl.BlockSpec((S, SWA_KV_WIDTH), lambda b, j: (b, 0)),
            pl.BlockSpec((n_lat_chunks, SWA_KV_WIDTH, SEQ_T), lambda b, j: (b, 0, 0)),
            pl.BlockSpec((L, SWA_KV_WIDTH), lambda b, j: (lat_rows // L + b, 0)),
            pl.BlockSpec((n_ctx_chunks, SWA_KV_WIDTH, SEQ_T), lambda b, j: (lat_rows // L + b, 0, 0)),
            pl.BlockSpec(memory_space=pltpu.SMEM),
        ],
        out_specs=pl.BlockSpec((BLOCK, SWA_WIDTH), q_idx),
        out_shape=jax.ShapeDtypeStruct((T, SWA_WIDTH), BF16),
        compiler_params=_cparams(("parallel", "arbitrary")),
        name="swa",
    )(q, k, vt, k, vt, sink)


def _diff_kernel(q_ref, kl_ref, vtl_ref, kc_ref, vtc_ref, lam_ref, subln_ref, o_ref, qp_scr, m_scr, acc_scr, *,
                 n_lat_tiles, n_lat_chunks, n_ctx_chunks, lam_init):
    j = pl.program_id(1)
    lv = lam_ref[...]
    lam = (jnp.exp(jnp.sum(lv[0:1] * lv[1:2], axis=-1, keepdims=True))
           - jnp.exp(jnp.sum(lv[2:3] * lv[3:4], axis=-1, keepdims=True)) + lam_init)
    lane = lax.broadcasted_iota(jnp.int32, (1, DIFF_V_DIM), 1)
    heads = [slice(h * DIFF_V_DIM, (h + 1) * DIFF_V_DIM) for h in range(DIFF_HEADS)]
    ones = jnp.ones((DIFF_SUM_ROWS, SEQ_T), BF16)

    for h, vs in enumerate(heads):
        qh = q_ref[:, vs]
        zero = jnp.zeros_like(qh)
        qp_scr[2 * h] = jnp.where(lane < HEAD_DIM, qh, zero)
        qp_scr[2 * h + 1] = jnp.where(lane >= HEAD_DIM, qh, zero)
    m_scr[...] = jnp.full(m_scr.shape, -jnp.inf, F32)
    acc_scr[...] = jnp.zeros(acc_scr.shape, F32)

    def update(blocks_of_head):
        all_blocks = [blocks_of_head(vs) for vs in heads]

        def scores(ch):
            return [_nt_dot(kblk, qp_scr[ch]) for kblk, _ in all_blocks[ch // 2]]

        n_chains = 2 * DIFF_HEADS
        queue = [scores(ch) for ch in range(DIFF_AHEAD)]
        for h, vs in enumerate(heads):
            blocks = all_blocks[h]
            for m in range(2):
                ch = 2 * h + m
                ss = queue.pop(0)
                if ch + DIFF_AHEAD < n_chains:
                    queue.append(scores(ch + DIFF_AHEAD))
                m_old = m_scr[ch]
                m_blk = jnp.max(ss[0], axis=0, keepdims=True)
                for s in ss[1:]:
                    m_blk = jnp.maximum(m_blk, jnp.max(s, axis=0, keepdims=True))
                m_new = jnp.maximum(m_old, m_blk.astype(BF16).astype(F32))
                m_b = m_new.astype(BF16)
                alpha = jnp.exp(m_old - m_new)
                acc = alpha * acc_scr[ch]
                for s, (_, vtblk) in zip(ss, blocks):
                    p = jnp.exp(s.astype(BF16) - m_b)
                    acc = acc + jnp.dot(jnp.concatenate([vtblk, ones], axis=0), p, preferred_element_type=F32)
                m_scr[ch] = m_new
                acc_scr[ch] = acc

    update(lambda vs: [(kc_ref[c * SEQ_T:(c + 1) * SEQ_T, vs], vtc_ref[c, vs, :]) for c in range(n_ctx_chunks)])

    @pl.when(j < n_lat_tiles)
    def _():
        def body(g, _):
            def blocks_of_head(vs):
                blocks = []
                for i in range(DIFF_GROUP):
                    c = g * DIFF_GROUP + i
                    r0 = pl.multiple_of(c * SEQ_T, SEQ_T)
                    blocks.append((kl_ref[pl.ds(r0, SEQ_T), vs], vtl_ref[c, vs, :]))
                return blocks

            update(blocks_of_head)
            return 0

        lax.fori_loop(0, n_lat_chunks // DIFF_GROUP, body, 0)

    def normalised(ch):
        acc = acc_scr[ch]
        return acc[:DIFF_V_DIM] / acc[DIFF_V_DIM:DIFF_V_DIM + 1]

    for h, vs in enumerate(heads):
        o = (normalised(2 * h) - lam * normalised(2 * h + 1)).T
        y = o * lax.rsqrt(jnp.mean(o * o, axis=-1, keepdims=True) + EPS) * subln_ref[...]
        o_ref[:, vs] = (y * (1.0 - lam_init)).astype(BF16)


def _diff_attn(dq, dk, dvt, lam_vecs, subln, lam_init, dims):
    B, S, L = dims
    T = dq.shape[0]
    n_lat_tiles = S // SEQ_T
    n_ctx_tiles = L // SEQ_T
    lat_rows = B * S

    def q_idx(b, j):
        return (jnp.where(j < n_lat_tiles, b * n_lat_tiles + j,
                          lat_rows // SEQ_T + b * n_ctx_tiles + (j - n_lat_tiles)), 0)

    return pl.pallas_call(
        functools.partial(_diff_kernel, n_lat_tiles=n_lat_tiles, n_lat_chunks=n_lat_tiles,
                          n_ctx_chunks=n_ctx_tiles, lam_init=lam_init),
        grid=(B, n_lat_tiles + n_ctx_tiles),
        in_specs=[
            pl.BlockSpec((SEQ_T, DIFF_QK_WIDTH), q_idx),
            pl.BlockSpec((S, DIFF_QK_WIDTH), lambda b, j: (b, 0)),
            pl.BlockSpec((n_lat_tiles, DIFF_WIDTH, SEQ_T), lambda b, j: (b, 0, 0)),
            pl.BlockSpec((L, DIFF_QK_WIDTH), lambda b, j: (lat_rows // L + b, 0)),
            pl.BlockSpec((n_ctx_tiles, DIFF_WIDTH, SEQ_T), lambda b, j: (lat_rows // L + b, 0, 0)),
            pl.BlockSpec(lam_vecs.shape, lambda b, j: (0, 0)),
            pl.BlockSpec((1, DIFF_V_DIM), lambda b, j: (0, 0)),
        ],
        out_specs=pl.BlockSpec((SEQ_T, DIFF_WIDTH), q_idx),
        out_shape=jax.ShapeDtypeStruct((T, DIFF_WIDTH), BF16),
        scratch_shapes=[
            pltpu.VMEM((2 * DIFF_HEADS, SEQ_T, DIFF_V_DIM), BF16),
            pltpu.VMEM((2 * DIFF_HEADS, 1, SEQ_T), F32),
            pltpu.VMEM((2 * DIFF_HEADS, DIFF_V_DIM + DIFF_SUM_ROWS, SEQ_T), F32),
        ],
        compiler_params=_cparams(("parallel", "arbitrary")),
        name="diff_attn",
    )(dq, dk, dvt, dk, dvt, lam_vecs, subln)


def _pre_rec_kernel(x_ref, mod_ref, nw_ref, w_ref, xp_ref, xr_ref, g_ref):
    h = _rmsnorm_mod(x_ref[...], nw_ref[...], mod_ref[0:1, :], mod_ref[1:2, :])
    p = jnp.dot(h.astype(BF16), w_ref[...], preferred_element_type=F32)
    xp_ref[...] = p[:, :POOL_WIDTH]
    xr_ref[...] = p[:, POOL_WIDTH:POOL_WIDTH + LRU_WIDTH]
    g_ref[...] = p[:, POOL_WIDTH + LRU_WIDTH:]


def _pre_rec(x, mods_l, norm_w, w_in, dims):
    B, S, L = dims
    T = x.shape[0]
    n_lat = B * S // TM
    s_tiles = S // TM

    def mod_idx(g):
        return (jnp.where(g < n_lat, g // s_tiles, B), 0, 0)

    widths = (POOL_WIDTH, LRU_WIDTH, LRU_WIDTH)
    return pl.pallas_call(
        _pre_rec_kernel,
        grid=(T // TM,),
        in_specs=[
            pl.BlockSpec((TM, D_MODEL), lambda g: (g, 0)),
            pl.BlockSpec((None, N_MOD, D_MODEL), mod_idx),
            pl.BlockSpec((1, D_MODEL), lambda g: (0, 0)),
            pl.BlockSpec(w_in.shape, lambda g: (0, 0)),
        ],
        out_specs=[pl.BlockSpec((TM, w), lambda g: (g, 0)) for w in widths],
        out_shape=[jax.ShapeDtypeStruct((T, w), F32) for w in widths],
        compiler_params=_cparams(("parallel",)),
        name="pre_rec",
    )(x, mods_l, norm_w, w_in)


def _rec_mid_kernel(xp_p, xp_c, xp_n, xr_p, xr_c, xr_n, cw_ref, cb_ref, wa_ref, ba_ref, wx_ref, bx_ref,
                    lam_ref, pw_ref, ps_ref, pool_ref, a_ref, b_ref, *, n_lat_chunks, lat_chunks_per_seq,
                    ctx_chunks_per_seq, lat_len, ctx_len):
    g = pl.program_id(0)
    is_lat = g < n_lat_chunks
    cps = jnp.where(is_lat, lat_chunks_per_seq, ctx_chunks_per_seq)
    within = jnp.where(is_lat, g % lat_chunks_per_seq, (g - n_lat_chunks) % ctx_chunks_per_seq)
    has_prev = within > 0
    has_next = within < cps - 1
    seg_len = jnp.where(is_lat, lat_len, ctx_len)
    ext_rows = SEQ_T + 2 * HALO

    def extended(prev_ref, cur_ref, next_ref):
        prev = jnp.where(has_prev, prev_ref[SEQ_T - HALO:, :], 0.0)
        nxt = jnp.where(has_next, next_ref[:HALO, :], 0.0)
        return jnp.concatenate([prev, cur_ref[...], nxt], axis=0)

    def shifted(ext, off):
        return pltpu.roll(ext, (-off) % ext_rows, 0)[HALO:HALO + SEQ_T, :]

    xp_ext = extended(xp_p, xp_c, xp_n)
    t = within * SEQ_T + lax.broadcasted_iota(jnp.int32, (SEQ_T, 1), 0)
    for gi, w in enumerate(POOL_WINDOWS):
        cols = slice(gi * POOL_GDIM, (gi + 1) * POOL_GDIM)
        eg = xp_ext[:, cols]
        tot = shifted(eg, -(w // 2))
        for off in range(-(w // 2) + 1, w - w // 2):
            tot = tot + shifted(eg, off)
        lo = jnp.clip(t - w // 2, 0, seg_len)
        hi = jnp.clip(t - w // 2 + w, 0, seg_len)
        cnt = (hi - lo).astype(F32)
        d = tot / cnt - xp_c[:, cols]
        y = jnp.dot(d.astype(BF16), pw_ref[gi], preferred_element_type=F32)
        pool_ref[:, cols] = (y * ps_ref[:, cols]).astype(BF16)

    xr_ext = extended(xr_p, xr_c, xr_n)
    u = cb_ref[...] + cw_ref[0:1, :] * shifted(xr_ext, -CONV_LEFT)
    for k in range(1, CONV_W):
        u = u + cw_ref[k:k + 1, :] * shifted(xr_ext, k - CONV_LEFT)
    ub = u.astype(BF16)
    for d in range(2):
        r = jax.nn.sigmoid(jnp.dot(ub, wa_ref[d], preferred_element_type=F32) + ba_ref[d:d + 1, :])
        i = jax.nn.sigmoid(jnp.dot(ub, wx_ref[d], preferred_element_type=F32) + bx_ref[d:d + 1, :])
        nl = -lam_ref[d:d + 1, :]
        softplus = jnp.maximum(nl, 0.0) + jnp.log1p(jnp.exp(-jnp.abs(nl)))
        log_a = -LRU_C * r * softplus
        a_ref[d] = jnp.exp(log_a)
        th = jnp.tanh(log_a)
        b_ref[d] = jnp.sqrt(-2.0 * th / (1.0 - th)) * (i * u)


def _rec_mid(xp, xr, conv_w, conv_b, wa_bd, ba, wx_bd, bx, lam, pool_w, pool_scale, dims):
    B, S, L = dims
    T = xp.shape[0]
    n_chunks = T // SEQ_T

    def cur(g):
        return (g, 0)

    def prv(g):
        return (jnp.maximum(g - 1, 0), 0)

    def nxt(g):
        return (jnp.minimum(g + 1, n_chunks - 1), 0)

    tile = lambda idx: pl.BlockSpec((SEQ_T, LRU_WIDTH), idx)
    full = lambda arr: pl.BlockSpec(arr.shape, lambda g: (0,) * arr.ndim)
    kern = functools.partial(_rec_mid_kernel, n_lat_chunks=B * S // SEQ_T, lat_chunks_per_seq=S // SEQ_T,
                             ctx_chunks_per_seq=L // SEQ_T, lat_len=S, ctx_len=L)
    return pl.pallas_call(
        kern,
        grid=(n_chunks,),
        in_specs=[tile(prv), tile(cur), tile(nxt), tile(prv), tile(cur), tile(nxt),
                  full(conv_w), full(conv_b), full(wa_bd), full(ba), full(wx_bd), full(bx), full(lam),
                  full(pool_w), full(pool_scale)],
        out_specs=[pl.BlockSpec((SEQ_T, POOL_WIDTH), cur),
                   pl.BlockSpec((2, SEQ_T, LRU_WIDTH), lambda g: (0, g, 0)),
                   pl.BlockSpec((2, SEQ_T, LRU_WIDTH), lambda g: (0, g, 0))],
        out_shape=[jax.ShapeDtypeStruct((T, POOL_WIDTH), BF16),
                   jax.ShapeDtypeStruct((2, T, LRU_WIDTH), F32),
                   jax.ShapeDtypeStruct((2, T, LRU_WIDTH), F32)],
        compiler_params=_cparams(("parallel",)),
        name="rec_mid",
    )(xp, xp, xp, xr, xr, xr, conv_w, conv_b, wa_bd, ba, wx_bd, bx, lam, pool_w, pool_scale)


def _scan_kernel(a_ref, b_ref, h_ref, carry_ref):
    d = pl.program_id(1)
    s = pl.program_id(2)

    @pl.when(s == 0)
    def _():
        carry_ref[...] = jnp.zeros_like(carry_ref)

    def run(reverse):
        def body(i, h):
            t = (SEQ_T - 1 - i) if reverse else i
            h = a_ref[pl.ds(t, 1), :] * h + b_ref[pl.ds(t, 1), :]
            h_ref[pl.ds(t, 1), :] = h
            return h

        carry_ref[...] = lax.fori_loop(0, SEQ_T, body, carry_ref[...], unroll=8)

    @pl.when(d == 0)
    def _():
        run(False)

    @pl.when(d == 1)
    def _():
        run(True)


def _scan(a, b, dims):
    B, S, L = dims
    T = a.shape[1]
    lat = S // SEQ_T
    ctx = L // SEQ_T
    lat_base = 0
    ctx_base = B * S // SEQ_T

    def idx(bi, d, s):
        fwd = jnp.where(s < ctx, ctx_base + bi * ctx + s, lat_base + bi * lat + (s - ctx))
        rev = jnp.where(s < ctx, ctx_base + bi * ctx + (ctx - 1 - s), lat_base + bi * lat + (lat - 1 - (s - ctx)))
        return (d, jnp.where(d == 0, fwd, rev), 0)

    spec = pl.BlockSpec((None, SEQ_T, LRU_WIDTH), idx)
    return pl.pallas_call(
        _scan_kernel,
        grid=(B, 2, lat + ctx),
        in_specs=[spec, spec],
        out_specs=spec,
        out_shape=jax.ShapeDtypeStruct((2, T, LRU_WIDTH), F32),
        scratch_shapes=[pltpu.VMEM((1, LRU_WIDTH), F32)],
        compiler_params=_cparams(("parallel", "parallel", "arbitrary")),
        name="lru_scan",
    )(a, b)


def _post_common(y, x_ref, mod_ref, nw_ref, xo_ref, h2_ref):
    x_new = x_ref[...] + mod_ref[2:3, :] * y
    xo_ref[...] = x_new
    h2_ref[...] = _rmsnorm_mod(x_new, nw_ref[...], mod_ref[3:4, :], mod_ref[4:5, :]).astype(BF16)


def _post_attn_kernel(a_ref, b_ref, x_ref, mod_ref, nw_ref, w_ref, xo_ref, h2_ref):
    half = a_ref.shape[1]
    y = (jnp.dot(a_ref[...], w_ref[:half, :], preferred_element_type=F32)
         + jnp.dot(b_ref[...], w_ref[half:, :], preferred_element_type=F32))
    _post_common(y, x_ref, mod_ref, nw_ref, xo_ref, h2_ref)


def _post_rec_kernel(pool_ref, hs_ref, g_ref, x_ref, mod_ref, nw_ref, w_ref, xo_ref, h2_ref):
    half = pool_ref.shape[1]
    rec = ((hs_ref[0] + hs_ref[1]) * _gelu(g_ref[...])).astype(BF16)
    y = (jnp.dot(pool_ref[...], w_ref[:half, :], preferred_element_type=F32)
         + jnp.dot(rec, w_ref[half:, :], preferred_element_type=F32))
    _post_common(y, x_ref, mod_ref, nw_ref, xo_ref, h2_ref)


def _post_mixer(kind, parts, x, mods_l, norm_w, w_out, dims, n_tiles):
    B, S, L = dims
    T = x.shape[0]
    n_lat = B * S // TM
    s_tiles = S // TM

    def mod_idx(g):
        return (jnp.where(g < n_lat, g // s_tiles, B), 0, 0)

    row = lambda w: pl.BlockSpec((TM, w), lambda g: (g, 0))
    if kind == "attn":
        kern = _post_attn_kernel
        part_specs = [row(SWA_WIDTH), row(DIFF_WIDTH)]
    else:
        kern = _post_rec_kernel
        part_specs = [row(POOL_WIDTH), pl.BlockSpec((2, TM, LRU_WIDTH), lambda g: (0, g, 0)), row(LRU_WIDTH)]
    n_in = len(parts)
    return pl.pallas_call(
        kern,
        grid=(n_tiles,),
        in_specs=part_specs + [
            row(D_MODEL),
            pl.BlockSpec((None, N_MOD, D_MODEL), mod_idx),
            pl.BlockSpec((1, D_MODEL), lambda g: (0, 0)),
            pl.BlockSpec(w_out.shape, lambda g: (0, 0)),
        ],
        out_specs=[row(D_MODEL), row(D_MODEL)],
        out_shape=[jax.ShapeDtypeStruct((T, D_MODEL), F32), jax.ShapeDtypeStruct((T, D_MODEL), BF16)],
        input_output_aliases={n_in: 0},
        compiler_params=_cparams(("parallel",)),
        name="post_" + kind,
    )(*parts, x, mods_l, norm_w, w_out)


def _peer_cand_blocks():
    return [(i, PEER_TOPK // (i + 1)) for i in range(1, SUBLANES)]


PEER_TAG_BITS = 31


def _peer_tagged(e, rank):
    bits = pltpu.bitcast(e, jnp.int32)
    tag = PEER_TAG_BITS - rank.astype(jnp.int32)
    return pltpu.bitcast((bits & ~PEER_TAG_BITS) | tag, F32)


def _peer_routing(h2_ref, wq_ref, sk_ref, thr_ref, c_ref, e2_ref,
                  q_scr, s_scr, cur_scr, rank_scr, top_scr, cand_scr):
    neg_inf = -jnp.inf
    n_cand = cand_scr.shape[0]

    def extract_step(k, src_ref, dst_ref, rank_ref=None):
        cur = src_ref[...]
        m = jnp.max(cur, axis=0, keepdims=True)
        dst_ref[pl.ds(k, 1), :] = m
        hit = cur == m
        src_ref[...] = jnp.where(hit, neg_inf, cur)
        if rank_ref is not None:
            rank_ref[...] = jnp.where(hit, lax.convert_element_type(k + 1, F32), rank_ref[...])

    def head_body(h, _):
        q0 = pl.multiple_of(h * PEER_QDIM, PEER_QDIM)
        q_scr[...] = _nt_dot(wq_ref[pl.ds(q0, PEER_QDIM), :], h2_ref[...])
        for p in range(2):
            s = jnp.dot(sk_ref[h * 2 + p], q_scr[p * PEER_HALF:(p + 1) * PEER_HALF, :], precision=HIGHEST,
                        preferred_element_type=F32)
            s_scr[p] = s
            cur_scr[p] = s
        rank_scr[...] = jnp.full(rank_scr.shape, PEER_TOPK + 1.0, F32)

        def top_body(k, _):
            extract_step(k, cur_scr.at[0], top_scr.at[0])
            extract_step(k, cur_scr.at[1], top_scr.at[1], rank_scr)
            return 0

        lax.fori_loop(0, PEER_TOPK, top_body, 0)
        v1 = top_scr[0]
        v2 = top_scr[1]
        cand_scr[0:PEER_TOPK, :] = v1[0:1, :] + v2
        row = lax.broadcasted_iota(jnp.int32, (SUBLANES, 1), 0)
        for i, n_i in _peer_cand_blocks():
            blk = jnp.where(row < n_i, v1[i:i + 1, :] + v2[0:SUBLANES, :], neg_inf)
            cand_scr[PEER_TOPK + (i - 1) * SUBLANES: PEER_TOPK + i * SUBLANES, :] = blk
        cand_scr[n_cand - SUBLANES:, :] = v1[SUBLANES:, :] + v2[0:1, :]
        cand = cand_scr[...]

        def cand_body(k, _):
            extract_step(k, cand_scr, top_scr.at[2])
            return 0

        lax.fori_loop(0, PEER_TOPK, cand_body, 0)
        tau = top_scr[2, PEER_TOPK - 1:PEER_TOPK, :]
        top = v1[0:1, :] + v2[0:1, :]
        z = jnp.sum(jnp.where(cand >= tau, jnp.exp(cand - top), 0.0), axis=0, keepdims=True)

        s1 = s_scr[0]
        ranks = (lax.broadcasted_iota(jnp.int32, (PEER_TOPK, 1), 0) + 1).astype(F32)
        top_scr[2] = _peer_tagged(jnp.exp(v2 - v2[0:1, :]), ranks)
        thr_ref[h] = jnp.full(s1.shape, jnp.inf, F32)

        def thr_body(jj, _):
            hit = s1 + top_scr[1, pl.ds(jj, 1), :] >= tau
            thr_ref[h] = jnp.where(hit, top_scr[2, pl.ds(jj, 1), :], thr_ref[h])
            return 0

        lax.fori_loop(0, PEER_TOPK, thr_body, 0)
        c_ref[h] = jnp.exp(s1 - v1[0:1, :]) / z
        e2_ref[h] = _peer_tagged(jnp.exp(s_scr[1] - v2[0:1, :]), rank_scr[...])
        return 0

    lax.fori_loop(0, PEER_HEADS, head_body, 0)


def _peer_kernel(h2_ref, u_ref, vt_ref, wq_ref, sk_ref, x_ref, mod_ref, xo_ref,
                 h2t_scr, coef_scr, acc_scr, thr_scr, c_scr, e2_scr,
                 q_scr, s_scr, cur_scr, rank_scr, top_scr, cand_scr):
    j = pl.program_id(1)
    tt = h2_ref.shape[0]
    pair = PEER_UP_GROUP * PEER_NKEYS

    @pl.when(j == 0)
    def _():
        acc_scr[...] = jnp.zeros_like(acc_scr)
        h2t_scr[...] = h2_ref[...].T
        _peer_routing(h2_ref, wq_ref, sk_ref, thr_scr, c_scr, e2_scr,
                      q_scr, s_scr, cur_scr, rank_scr, top_scr, cand_scr)

    for pr in range(PEER_NA // PEER_UP_GROUP):
        act = jnp.dot(u_ref[pr * pair:(pr + 1) * pair, :], h2t_scr[...], preferred_element_type=F32)
        for half in range(PEER_UP_GROUP):
            al = PEER_UP_GROUP * pr + half
            a = j * PEER_NA + al
            thr_full = [thr_scr[h, pl.ds(a, 1), :] for h in range(PEER_HEADS)]
            c_full = [c_scr[h, pl.ds(a, 1), :] for h in range(PEER_HEADS)]
            for tc in range(tt // LANES):
                ls = slice(tc * LANES, (tc + 1) * LANES)
                thr_rows = [r[:, ls] for r in thr_full]
                c_rows = [r[:, ls] for r in c_full]
                for rb in range(PEER_NKEYS // PEER_RB):
                    rs = slice(rb * PEER_RB, (rb + 1) * PEER_RB)
                    w = jnp.zeros((PEER_RB, LANES), F32)
                    for h in range(PEER_HEADS):
                        e2_t = e2_scr[h, rs, ls]
                        w = w + jnp.where(e2_t >= thr_rows[h], c_rows[h] * e2_t, 0.0)
                    rows = slice(half * PEER_NKEYS + rb * PEER_RB, half * PEER_NKEYS + (rb + 1) * PEER_RB)
                    r0 = al * PEER_NKEYS + rb * PEER_RB
                    coef_scr[r0:r0 + PEER_RB, ls] = (w * _gelu(act[rows, ls])).astype(BF16)

    acc_scr[...] += jnp.dot(vt_ref[...], coef_scr[...], preferred_element_type=F32)

    @pl.when(j == pl.num_programs(1) - 1)
    def _():
        xo_ref[...] = x_ref[...] + mod_ref[5:6, :] * acc_scr[...].T


def _peer(h2, u, v_t, wq_t, subkeys, x, mods_l, dims, n_tiles):
    B, S, L = dims
    T = x.shape[0]
    tt = PEER_TT
    ne = PEER_NA * PEER_NKEYS
    n_lat = B * S // tt
    s_tiles = S // tt
    n_cand = PEER_TOPK + SUBLANES * SUBLANES

    def mod_idx(i, j):
        return (jnp.where(i < n_lat, i // s_tiles, B), 0, 0)

    key_f32 = pltpu.VMEM((PEER_HEADS, PEER_NKEYS, tt), F32)
    const = pl.Buffered(1)
    return pl.pallas_call(
        _peer_kernel,
        grid=(n_tiles, PEER_EXPERTS // ne),
        in_specs=[
            pl.BlockSpec((tt, D_MODEL), lambda i, j: (i, 0)),
            pl.BlockSpec((ne, D_MODEL), lambda i, j: (j, 0)),
            pl.BlockSpec((None, D_MODEL, ne), lambda i, j: (j, 0, 0)),
            pl.BlockSpec(wq_t.shape, lambda i, j: (0, 0), pipeline_mode=const),
            pl.BlockSpec(subkeys.shape, lambda i, j: (0, 0, 0), pipeline_mode=const),
            pl.BlockSpec((tt, D_MODEL), lambda i, j: (i, 0)),
            pl.BlockSpec((None, N_MOD, D_MODEL), mod_idx),
        ],
        out_specs=pl.BlockSpec((tt, D_MODEL), lambda i, j: (i, 0)),
        out_shape=jax.ShapeDtypeStruct((T, D_MODEL), F32),
        scratch_shapes=[
            pltpu.VMEM((D_MODEL, tt), BF16),
            pltpu.VMEM((ne, tt), BF16),
            pltpu.VMEM((D_MODEL, tt), F32),
            key_f32, key_f32, key_f32,
            pltpu.VMEM((PEER_QDIM, tt), F32),
            pltpu.VMEM((2, PEER_NKEYS, tt), F32),
            pltpu.VMEM((2, PEER_NKEYS, tt), F32),
            pltpu.VMEM((PEER_NKEYS, tt), F32),
            pltpu.VMEM((3, PEER_TOPK, tt), F32),
            pltpu.VMEM((n_cand, tt), F32),
        ],
        input_output_aliases={5: 0},
        compiler_params=_cparams(("parallel", "arbitrary")),
        name="peer",
    )(h2, u, v_t, wq_t, subkeys, x, mods_l)


def _final_norm_kernel(x_ref, w_ref, o_ref):
    x = x_ref[...]
    o_ref[...] = x * lax.rsqrt(jnp.mean(x * x, axis=-1, keepdims=True) + EPS) * w_ref[...]


def _final_norm(x, w, n_rows):
    return pl.pallas_call(
        _final_norm_kernel,
        grid=(n_rows // TM,),
        in_specs=[pl.BlockSpec((TM, D_MODEL), lambda g: (g, 0)), pl.BlockSpec((1, D_MODEL), lambda g: (0, 0))],
        out_specs=pl.BlockSpec((TM, D_MODEL), lambda g: (g, 0)),
        out_shape=jax.ShapeDtypeStruct((n_rows, D_MODEL), F32),
        compiler_params=_cparams(("parallel",)),
        name="final_norm",
    )(x, w)


def _lambda_init(layer):
    return 0.8 - 0.6 * math.exp(-0.3 * layer)


def _rope_tables(S):
    rows = S // GRID_W
    row = jnp.repeat(jnp.arange(rows), GRID_W).astype(F32)
    col = jnp.tile(jnp.arange(GRID_W), rows).astype(F32)
    inv = ROPE_THETA ** (-jnp.arange(ROPE_AXIS_FREQS, dtype=F32) / ROPE_AXIS_FREQS)
    ang = jnp.concatenate([row[:, None] * inv, col[:, None] * inv], axis=-1)
    cos, sin = jnp.cos(ang), jnp.sin(ang)
    cos_t = jnp.tile(cos, (1, LANES // ROPE_HALF))
    sin_t = jnp.tile(jnp.concatenate([-sin, sin], axis=-1), (1, LANES // HEAD_DIM))
    cos_t = jnp.concatenate([cos_t, jnp.ones((TM, LANES), F32)], axis=0)
    sin_t = jnp.concatenate([sin_t, jnp.zeros((TM, LANES), F32)], axis=0)
    return cos_t, sin_t


def _block_diag(w):
    nd, nb, bd, _ = w.shape
    eye = jnp.eye(nb, dtype=w.dtype)
    return jnp.einsum("dnij,nm->dnimj", w, eye).reshape(nd, nb * bd, nb * bd)


def kernel(x, c, ctx, c_ctx, w_mod, b_mod, norm_mix, norm_ffn, w_out, attn_w_in, swa_sink, diff_lambda, diff_subln, rec_w_in, pool_w, pool_scale, lru_conv_w, lru_conv_b, lru_wa, lru_ba, lru_wx, lru_bx, lru_lambda, peer_wq, peer_subkeys, peer_u, peer_v, final_norm):
    B, S, D = x.shape
    L = ctx.shape[1]
    depth = w_mod.shape[0]
    dims = (B, S, L)
    assert D == D_MODEL and S % PEER_TT == 0 and (B * L) % PEER_TT == 0 and L % SEQ_T == 0
    assert S % GRID_W == 0 and S >= 3 * BLOCK and B + 1 <= SUBLANES and S % (DIFF_GROUP * SEQ_T) == 0
    lat_rows = B * S
    T = lat_rows + B * L

    xs = jnp.concatenate([x.reshape(lat_rows, D), ctx.reshape(B * L, D)], axis=0)
    cc = jnp.zeros((SUBLANES, D), F32).at[:B].set(c).at[B].set(c_ctx)
    mods = _modulation(cc, w_mod, b_mod)
    cos_t, sin_t = _rope_tables(S)

    for l in range(depth):
        jl = l // 2
        ctx_out = l < depth - 1
        n_rows = T if ctx_out else lat_rows
        mods_l = mods[l]
        if l % 2 == 0:
            q, k, dq, dk, vt, dvt = _pre_attn(xs, mods_l, norm_mix[l][None], attn_w_in[jl].astype(BF16),
                                              cos_t, sin_t, dims)
            a = _swa(q, k, vt, swa_sink[jl], dims)
            bd = _diff_attn(dq, dk, dvt, diff_lambda[jl], diff_subln[jl][None], _lambda_init(l), dims)
            xs, h2 = _post_mixer("attn", (a, bd), xs, mods_l, norm_ffn[l][None], w_out[l].astype(BF16),
                                 dims, n_rows // TM)
        else:
            xp, xr, g = _pre_rec(xs, mods_l, norm_mix[l][None], rec_w_in[jl].astype(BF16), dims)
            pool, a_co, b_co = _rec_mid(xp, xr, lru_conv_w[jl], lru_conv_b[jl][None],
                                        _block_diag(lru_wa[jl]).astype(BF16), lru_ba[jl],
                                        _block_diag(lru_wx[jl]).astype(BF16), lru_bx[jl], lru_lambda[jl],
                                        pool_w[jl].astype(BF16), pool_scale[jl][None], dims)
            hs = _scan(a_co, b_co, dims)
            xs, h2 = _post_mixer("rec", (pool, hs, g), xs, mods_l, norm_ffn[l][None], w_out[l].astype(BF16),
                                 dims, n_rows // TM)
        n_peer = n_rows // PEER_TT
        wq_t = peer_wq[l].astype(BF16).T
        sk = peer_subkeys[l].reshape(PEER_HEADS * 2, PEER_NKEYS, PEER_HALF)
        ne = PEER_NA * PEER_NKEYS
        v_t = peer_v[l].astype(BF16).reshape(PEER_EXPERTS // ne, ne, D).transpose(0, 2, 1)
        xs = _peer(h2, peer_u[l].astype(BF16), v_t, wq_t, sk, xs, mods_l, dims, n_peer)

    return _final_norm(xs, final_norm[None], lat_rows).reshape(B, S, D)
```

```python
import functools
import math

import jax
import jax.numpy as jnp
from jax import lax
from jax.experimental import pallas as pl
from jax.experimental.pallas import tpu as pltpu

F32 = jnp.float32
BF16 = jnp.bfloat16
HIGHEST = lax.Precision.HIGHEST

D_MODEL = 1024
HEAD_DIM = 64
EPS = 1e-6
NEG = -1e30
N_MOD = 6
SCALE = HEAD_DIM ** -0.5
ROPE_HALF = HEAD_DIM // 2
ROPE_AXIS_FREQS = ROPE_HALF // 2
ROPE_THETA = 10000.0
GRID_W = 64
SWA_Q_HEADS = 8
SWA_KV_HEADS = 2
SWA_GROUP = SWA_Q_HEADS // SWA_KV_HEADS
WINDOW = 128
BLOCK = 128
SWA_WIDTH = SWA_Q_HEADS * HEAD_DIM
SWA_KV_WIDTH = SWA_KV_HEADS * HEAD_DIM
DIFF_HEADS = 4
DIFF_V_DIM = 2 * HEAD_DIM
DIFF_QK_WIDTH = DIFF_HEADS * 2 * HEAD_DIM
DIFF_WIDTH = DIFF_HEADS * DIFF_V_DIM
POOL_WINDOWS = (2, 4, 8, 16)
POOL_GROUPS = 4
POOL_WIDTH = D_MODEL // 2
POOL_GDIM = POOL_WIDTH // POOL_GROUPS
LRU_WIDTH = D_MODEL // 2
LRU_BLOCKS = 8
CONV_W = 4
CONV_LEFT = CONV_W // 2
LRU_C = 8.0
PEER_HEADS = 8
PEER_NKEYS = 128
PEER_EXPERTS = PEER_NKEYS * PEER_NKEYS
PEER_QDIM = 256
PEER_HALF = PEER_QDIM // 2
PEER_TOPK = 16

LANES = 128
SUBLANES = 8
VMEM_LIMIT = 56 * 1024 * 1024

TM = 256
SEQ_T = 256
DIFF_GROUP = 16
DIFF_AHEAD = 2
DIFF_SUM_ROWS = 16
HALO = 8
PEER_TT = 512
PEER_NA = 16
PEER_UP_GROUP = 2
PEER_RB = 64


def _cparams(sem):
    return pltpu.CompilerParams(dimension_semantics=sem, vmem_limit_bytes=VMEM_LIMIT)


def _nt_dot(a, b):
    return lax.dot_general(a, b, (((1,), (1,)), ((), ())), preferred_element_type=F32)


def _rmsnorm_mod(x, w, shift, scale):
    y = x * lax.rsqrt(jnp.mean(x * x, axis=-1, keepdims=True) + EPS) * w
    return y * (1.0 + scale) + shift


def _gelu(x):
    return 0.5 * x * (1.0 + lax.erf(x * (2.0 ** -0.5)))


def _mod_kernel(cc_ref, w_ref, b_ref, o_ref):
    cc = cc_ref[...]
    sc = cc * jax.nn.sigmoid(cc)
    o_ref[...] = jnp.dot(sc, w_ref[...], precision=HIGHEST, preferred_element_type=F32) + b_ref[...]


def _modulation(cc, w_mod, b_mod):
    depth = w_mod.shape[0]
    rows = cc.shape[0]
    out = pl.pallas_call(
        _mod_kernel,
        grid=(depth, N_MOD),
        in_specs=[
            pl.BlockSpec((rows, D_MODEL), lambda l, j: (0, 0)),
            pl.BlockSpec((None, D_MODEL, D_MODEL), lambda l, j: (l, 0, j)),
            pl.BlockSpec((None, 1, D_MODEL), lambda l, j: (l, 0, j)),
        ],
        out_specs=pl.BlockSpec((None, rows, D_MODEL), lambda l, j: (l, 0, j)),
        out_shape=jax.ShapeDtypeStruct((depth, rows, N_MOD * D_MODEL), F32),
        compiler_params=_cparams(("parallel", "parallel")),
        name="modulation",
    )(cc, w_mod, b_mod.reshape(depth, 1, N_MOD * D_MODEL))
    return out.reshape(depth, rows, N_MOD, D_MODEL)


def _rope128(x, cos, sin_signed, first_half):
    partner = jnp.where(first_half, pltpu.roll(x, LANES - ROPE_HALF, 1), pltpu.roll(x, ROPE_HALF, 1))
    return x * cos + partner * sin_signed


def _pre_attn_kernel(x_ref, mod_ref, nw_ref, w_ref, cos_ref, sin_ref,
                     q_ref, k_ref, dq_ref, dk_ref, v_ref, dv_ref):
    h = _rmsnorm_mod(x_ref[...], nw_ref[...], mod_ref[0:1, :], mod_ref[1:2, :])
    p = jnp.dot(h.astype(BF16), w_ref[...], preferred_element_type=F32)
    cos = cos_ref[...]
    sin = sin_ref[...]
    lane = lax.broadcasted_iota(jnp.int32, (1, LANES), 1)
    first_half = (lane % HEAD_DIM) < ROPE_HALF

    def roped(lo, width, scale):
        outs = []
        for c in range(width // LANES):
            xc = p[:, lo + c * LANES: lo + (c + 1) * LANES]
            outs.append((_rope128(xc, cos, sin, first_half) * scale).astype(BF16))
        return outs

    o1 = SWA_WIDTH
    o2 = o1 + SWA_KV_WIDTH
    o3 = o2 + SWA_KV_WIDTH
    o4 = o3 + DIFF_QK_WIDTH
    o5 = o4 + DIFF_QK_WIDTH
    for c, val in enumerate(roped(0, SWA_WIDTH, SCALE)):
        q_ref[:, c * LANES:(c + 1) * LANES] = val
    for c, val in enumerate(roped(o1, SWA_KV_WIDTH, 1.0)):
        k_ref[:, c * LANES:(c + 1) * LANES] = val
    v_ref[...] = p[:, o2:o3].T.astype(BF16)
    for c, val in enumerate(roped(o3, DIFF_QK_WIDTH, SCALE)):
        dq_ref[:, c * LANES:(c + 1) * LANES] = val
    for c, val in enumerate(roped(o4, DIFF_QK_WIDTH, 1.0)):
        dk_ref[:, c * LANES:(c + 1) * LANES] = val
    dv_ref[...] = p[:, o5:].T.astype(BF16)


def _pre_attn(x, mods_l, norm_w, w_in, cos_t, sin_t, dims):
    B, S, L = dims
    T = x.shape[0]
    n_lat = B * S // TM
    s_tiles = S // TM

    def mod_idx(g):
        return (jnp.where(g < n_lat, g // s_tiles, B), 0, 0)

    def rope_idx(g):
        return (jnp.where(g < n_lat, g % s_tiles, s_tiles), 0)

    widths = (SWA_WIDTH, SWA_KV_WIDTH, DIFF_QK_WIDTH, DIFF_QK_WIDTH)
    t_widths = (SWA_KV_WIDTH, DIFF_WIDTH)
    return pl.pallas_call(
        _pre_attn_kernel,
        grid=(T // TM,),
        in_specs=[
            pl.BlockSpec((TM, D_MODEL), lambda g: (g, 0)),
            pl.BlockSpec((None, N_MOD, D_MODEL), mod_idx),
            pl.BlockSpec((1, D_MODEL), lambda g: (0, 0)),
            pl.BlockSpec(w_in.shape, lambda g: (0, 0)),
            pl.BlockSpec((TM, LANES), rope_idx),
            pl.BlockSpec((TM, LANES), rope_idx),
        ],
        out_specs=[pl.BlockSpec((TM, w), lambda g: (g, 0)) for w in widths]
        + [pl.BlockSpec((None, w, TM), lambda g: (g, 0, 0)) for w in t_widths],
        out_shape=[jax.ShapeDtypeStruct((T, w), BF16) for w in widths]
        + [jax.ShapeDtypeStruct((T // TM, w, TM), BF16) for w in t_widths],
        compiler_params=_cparams(("parallel",)),
        name="pre_attn",
    )(x, mods_l, norm_w, w_in, cos_t, sin_t)


def _swa_kernel(q_ref, kl_ref, vtl_ref, kc_ref, vtc_ref, sink_ref, o_ref, *, n_lat_blocks, n_lat_chunks,
                n_ctx_chunks):
    j = pl.program_id(1)
    is_ctx = j >= n_lat_blocks
    jl = jnp.minimum(j, n_lat_blocks - 1)
    band_chunks = 2
    per_chunk = SEQ_T // BLOCK
    c0 = jnp.clip((jl - 1) // per_chunk, 0, n_lat_chunks - band_chunks)
    r0 = pl.multiple_of(c0 * SEQ_T, SEQ_T)
    kb = kl_ref[pl.ds(r0, band_chunks * SEQ_T), :]
    kc = kc_ref[...]
    glanes = SWA_GROUP * BLOCK
    kpos = r0 + lax.broadcasted_iota(jnp.int32, (band_chunks * SEQ_T, 1), 0)
    qpos = jl * BLOCK + lax.broadcasted_iota(jnp.int32, (1, glanes), 1) % BLOCK
    valid = jnp.logical_and(jnp.abs(qpos - kpos) <= WINDOW, jnp.logical_not(is_ctx))
    q = q_ref[...]
    for hk in range(SWA_KV_HEADS):
        ks = slice(hk * HEAD_DIM, (hk + 1) * HEAD_DIM)
        hs = [hk * SWA_GROUP + g for g in range(SWA_GROUP)]
        qg = jnp.concatenate([q[:, h * HEAD_DIM:(h + 1) * HEAD_DIM] for h in hs], axis=0)
        sink = jnp.concatenate([jnp.full((1, BLOCK), sink_ref[h], F32) for h in hs], axis=1)
        sl = jnp.where(valid, _nt_dot(kb[:, ks], qg), NEG)
        sc = _nt_dot(kc[:, ks], qg)
        m = jnp.maximum(jnp.maximum(jnp.max(sl, axis=0, keepdims=True), jnp.max(sc, axis=0, keepdims=True)), sink)
        pl_ = jnp.exp(sl - m)
        pc = jnp.exp(sc - m)
        den = jnp.sum(pl_, axis=0, keepdims=True) + jnp.sum(pc, axis=0, keepdims=True) + jnp.exp(sink - m)
        pl_b = pl_.astype(BF16)
        pc_b = pc.astype(BF16)
        o = jnp.zeros((HEAD_DIM, glanes), F32)
        for i in range(band_chunks):
            o = o + jnp.dot(vtl_ref[c0 + i, ks, :], pl_b[i * SEQ_T:(i + 1) * SEQ_T], preferred_element_type=F32)
        for i in range(n_ctx_chunks):
            o = o + jnp.dot(vtc_ref[i, ks, :], pc_b[i * SEQ_T:(i + 1) * SEQ_T], preferred_element_type=F32)
        o = o / den
        for g, h in enumerate(hs):
            o_ref[:, h * HEAD_DIM:(h + 1) * HEAD_DIM] = o[:, g * BLOCK:(g + 1) * BLOCK].T.astype(BF16)


def _swa(q, k, vt, sink, dims):
    B, S, L = dims
    T = q.shape[0]
    n_lat_blocks = S // BLOCK
    n_ctx_blocks = L // BLOCK
    n_lat_chunks = S // SEQ_T
    n_ctx_chunks = L // SEQ_T
    lat_rows = B * S

    def q_idx(b, j):
        return (jnp.where(j < n_lat_blocks, b * n_lat_blocks + j,
                          lat_rows // BLOCK + b * n_ctx_blocks + (j - n_lat_blocks)), 0)

    return pl.pallas_call(
        functools.partial(_swa_kernel, n_lat_blocks=n_lat_blocks, n_lat_chunks=n_lat_chunks,
                          n_ctx_chunks=n_ctx_chunks),
        grid=(B, n_lat_blocks + n_ctx_blocks),
        in_specs=[
            pl.BlockSpec((BLOCK, SWA_WIDTH), q_idx),
            pl.BlockSpec((S, SWA_KV_WIDTH), lambda b, j: (b, 0)),
            pl.BlockSpec((n_lat_chunks, SWA_KV_WIDTH, SEQ_T), lambda b, j: (b, 0, 0)),
            pl.BlockSpec((L, SWA_KV_WIDTH), lambda b, j: (lat_rows // L + b, 0)),
            pl.BlockSpec((n_ctx_chunks, SWA_KV_WIDTH, SEQ_T), lambda b, j: (lat_rows // L + b, 0, 0)),
            pl.BlockSpec(memory_space=pltpu.SMEM),
        ],
        out_specs=pl.BlockSpec((BLOCK, SWA_WIDTH), q_idx),
        out_shape=jax.ShapeDtypeStruct((T, SWA_WIDTH), BF16),
        compiler_params=_cparams(("parallel", "arbitrary")),
        name="swa",
    )(q, k, vt, k, vt, sink)


def _diff_kernel(q_ref, kl_ref, vtl_ref, kc_ref, vtc_ref, lam_ref, subln_ref, o_ref, qp_scr, m_scr, acc_scr, *,
                 n_lat_tiles, n_lat_chunks, n_ctx_chunks, lam_init):
    j = pl.program_id(1)
    lv = lam_ref[...]
    lam = (jnp.exp(jnp.sum(lv[0:1] * lv[1:2], axis=-1, keepdims=True))
           - jnp.exp(jnp.sum(lv[2:3] * lv[3:4], axis=-1, keepdims=True)) + lam_init)
    lane = lax.broadcasted_iota(jnp.int32, (1, DIFF_V_DIM), 1)
    heads = [slice(h * DIFF_V_DIM, (h + 1) * DIFF_V_DIM) for h in range(DIFF_HEADS)]
    ones = jnp.ones((DIFF_SUM_ROWS, SEQ_T), BF16)

    for h, vs in enumerate(heads):
        qh = q_ref[:, vs]
        zero = jnp.zeros_like(qh)
        qp_scr[2 * h] = jnp.where(lane < HEAD_DIM, qh, zero)
        qp_scr[2 * h + 1] = jnp.where(lane >= HEAD_DIM, qh, zero)
    m_scr[...] = jnp.full(m_scr.shape, -jnp.inf, F32)
    acc_scr[...] = jnp.zeros(acc_scr.shape, F32)

    def update(blocks_of_head):
        all_blocks = [blocks_of_head(vs) for vs in heads]

        def scores(ch):
            return [_nt_dot(kblk, qp_scr[ch]) for kblk, _ in all_blocks[ch // 2]]

        n_chains = 2 * DIFF_HEADS
        queue = [scores(ch) for ch in range(DIFF_AHEAD)]
        for h, vs in enumerate(heads):
            blocks = all_blocks[h]
            for m in range(2):
                ch = 2 * h + m
                ss = queue.pop(0)
                if ch + DIFF_AHEAD < n_chains:
                    queue.append(scores(ch + DIFF_AHEAD))
                m_old = m_scr[ch]
                m_blk = jnp.max(ss[0], axis=0, keepdims=True)
                for s in ss[1:]:
                    m_blk = jnp.maximum(m_blk, jnp.max(s, axis=0, keepdims=True))
                m_new = jnp.maximum(m_old, m_blk.astype(BF16).astype(F32))
                m_b = m_new.astype(BF16)
                alpha = jnp.exp(m_old - m_new)
                acc = alpha * acc_scr[ch]
                for s, (_, vtblk) in zip(ss, blocks):
                    p = jnp.exp(s.astype(BF16) - m_b)
                    acc = acc + jnp.dot(jnp.concatenate([vtblk, ones], axis=0), p, preferred_element_type=F32)
                m_scr[ch] = m_new
                acc_scr[ch] = acc

    update(lambda vs: [(kc_ref[c * SEQ_T:(c + 1) * SEQ_T, vs], vtc_ref[c, vs, :]) for c in range(n_ctx_chunks)])

    @pl.when(j < n_lat_tiles)
    def _():
        def body(g, _):
            def blocks_of_head(vs):
                blocks = []
                for i in range(DIFF_GROUP):
                    c = g * DIFF_GROUP + i
                    r0 = pl.multiple_of(c * SEQ_T, SEQ_T)
                    blocks.append((kl_ref[pl.ds(r0, SEQ_T), vs], vtl_ref[c, vs, :]))
                return blocks

            update(blocks_of_head)
            return 0

        lax.fori_loop(0, n_lat_chunks // DIFF_GROUP, body, 0)

    def normalised(ch):
        acc = acc_scr[ch]
        return acc[:DIFF_V_DIM] / acc[DIFF_V_DIM:DIFF_V_DIM + 1]

    for h, vs in enumerate(heads):
        o = (normalised(2 * h) - lam * normalised(2 * h + 1)).T
        y = o * lax.rsqrt(jnp.mean(o * o, axis=-1, keepdims=True) + EPS) * subln_ref[...]
        o_ref[:, vs] = (y * (1.0 - lam_init)).astype(BF16)


def _diff_attn(dq, dk, dvt, lam_vecs, subln, lam_init, dims):
    B, S, L = dims
    T = dq.shape[0]
    n_lat_tiles = S // SEQ_T
    n_ctx_tiles = L // SEQ_T
    lat_rows = B * S

    def q_idx(b, j):
        return (jnp.where(j < n_lat_tiles, b * n_lat_tiles + j,
                          lat_rows // SEQ_T + b * n_ctx_tiles + (j - n_lat_tiles)), 0)

    return pl.pallas_call(
        functools.partial(_diff_kernel, n_lat_tiles=n_lat_tiles, n_lat_chunks=n_lat_tiles,
                          n_ctx_chunks=n_ctx_tiles, lam_init=lam_init),
        grid=(B, n_lat_tiles + n_ctx_tiles),
        in_specs=[
            pl.BlockSpec((SEQ_T, DIFF_QK_WIDTH), q_idx),
            pl.BlockSpec((S, DIFF_QK_WIDTH), lambda b, j: (b, 0)),
            pl.BlockSpec((n_lat_tiles, DIFF_WIDTH, SEQ_T), lambda b, j: (b, 0, 0)),
            pl.BlockSpec((L, DIFF_QK_WIDTH), lambda b, j: (lat_rows // L + b, 0)),
            pl.BlockSpec((n_ctx_tiles, DIFF_WIDTH, SEQ_T), lambda b, j: (lat_rows // L + b, 0, 0)),
            pl.BlockSpec(lam_vecs.shape, lambda b, j: (0, 0)),
            pl.BlockSpec((1, DIFF_V_DIM), lambda b, j: (0, 0)),
        ],
        out_specs=pl.BlockSpec((SEQ_T, DIFF_WIDTH), q_idx),
        out_shape=jax.ShapeDtypeStruct((T, DIFF_WIDTH), BF16),
        scratch_shapes=[
            pltpu.VMEM((2 * DIFF_HEADS, SEQ_T, DIFF_V_DIM), BF16),
            pltpu.VMEM((2 * DIFF_HEADS, 1, SEQ_T), F32),
            pltpu.VMEM((2 * DIFF_HEADS, DIFF_V_DIM + DIFF_SUM_ROWS, SEQ_T), F32),
        ],
        compiler_params=_cparams(("parallel", "arbitrary")),
        name="diff_attn",
    )(dq, dk, dvt, dk, dvt, lam_vecs, subln)


def _pre_rec_kernel(x_ref, mod_ref, nw_ref, w_ref, xp_ref, xr_ref, g_ref):
    h = _rmsnorm_mod(x_ref[...], nw_ref[...], mod_ref[0:1, :], mod_ref[1:2, :])
    p = jnp.dot(h.astype(BF16), w_ref[...], preferred_element_type=F32)
    xp_ref[...] = p[:, :POOL_WIDTH]
    xr_ref[...] = p[:, POOL_WIDTH:POOL_WIDTH + LRU_WIDTH]
    g_ref[...] = p[:, POOL_WIDTH + LRU_WIDTH:]


def _pre_rec(x, mods_l, norm_w, w_in, dims):
    B, S, L = dims
    T = x.shape[0]
    n_lat = B * S // TM
    s_tiles = S // TM

    def mod_idx(g):
        return (jnp.where(g < n_lat, g // s_tiles, B), 0, 0)

    widths = (POOL_WIDTH, LRU_WIDTH, LRU_WIDTH)
    return pl.pallas_call(
        _pre_rec_kernel,
        grid=(T // TM,),
        in_specs=[
            pl.BlockSpec((TM, D_MODEL), lambda g: (g, 0)),
            pl.BlockSpec((None, N_MOD, D_MODEL), mod_idx),
            pl.BlockSpec((1, D_MODEL), lambda g: (0, 0)),
            pl.BlockSpec(w_in.shape, lambda g: (0, 0)),
        ],
        out_specs=[pl.BlockSpec((TM, w), lambda g: (g, 0)) for w in widths],
        out_shape=[jax.ShapeDtypeStruct((T, w), F32) for w in widths],
        compiler_params=_cparams(("parallel",)),
        name="pre_rec",
    )(x, mods_l, norm_w, w_in)


def _rec_mid_kernel(xp_p, xp_c, xp_n, xr_p, xr_c, xr_n, cw_ref, cb_ref, wa_ref, ba_ref, wx_ref, bx_ref,
                    lam_ref, pw_ref, ps_ref, pool_ref, a_ref, b_ref, *, n_lat_chunks, lat_chunks_per_seq,
                    ctx_chunks_per_seq, lat_len, ctx_len):
    g = pl.program_id(0)
    is_lat = g < n_lat_chunks
    cps = jnp.where(is_lat, lat_chunks_per_seq, ctx_chunks_per_seq)
    within = jnp.where(is_lat, g % lat_chunks_per_seq, (g - n_lat_chunks) % ctx_chunks_per_seq)
    has_prev = within > 0
    has_next = within < cps - 1
    seg_len = jnp.where(is_lat, lat_len, ctx_len)
    ext_rows = SEQ_T + 2 * HALO

    def extended(prev_ref, cur_ref, next_ref):
        prev = jnp.where(has_prev, prev_ref[SEQ_T - HALO:, :], 0.0)
        nxt = jnp.where(has_next, next_ref[:HALO, :], 0.0)
        return jnp.concatenate([prev, cur_ref[...], nxt], axis=0)

    def shifted(ext, off):
        return pltpu.roll(ext, (-off) % ext_rows, 0)[HALO:HALO + SEQ_T, :]

    xp_ext = extended(xp_p, xp_c, xp_n)
    t = within * SEQ_T + lax.broadcasted_iota(jnp.int32, (SEQ_T, 1), 0)
    for gi, w in enumerate(POOL_WINDOWS):
        cols = slice(gi * POOL_GDIM, (gi + 1) * POOL_GDIM)
        eg = xp_ext[:, cols]
        tot = shifted(eg, -(w // 2))
        for off in range(-(w // 2) + 1, w - w // 2):
            tot = tot + shifted(eg, off)
        lo = jnp.clip(t - w // 2, 0, seg_len)
        hi = jnp.clip(t - w // 2 + w, 0, seg_len)
        cnt = (hi - lo).astype(F32)
        d = tot / cnt - xp_c[:, cols]
        y = jnp.dot(d.astype(BF16), pw_ref[gi], preferred_element_type=F32)
        pool_ref[:, cols] = (y * ps_ref[:, cols]).astype(BF16)

    xr_ext = extended(xr_p, xr_c, xr_n)
    u = cb_ref[...] + cw_ref[0:1, :] * shifted(xr_ext, -CONV_LEFT)
    for k in range(1, CONV_W):
        u = u + cw_ref[k:k + 1, :] * shifted(xr_ext, k - CONV_LEFT)
    ub = u.astype(BF16)
    for d in range(2):
        r = jax.nn.sigmoid(jnp.dot(ub, wa_ref[d], preferred_element_type=F32) + ba_ref[d:d + 1, :])
        i = jax.nn.sigmoid(jnp.dot(ub, wx_ref[d], preferred_element_type=F32) + bx_ref[d:d + 1, :])
        nl = -lam_ref[d:d + 1, :]
        softplus = jnp.maximum(nl, 0.0) + jnp.log1p(jnp.exp(-jnp.abs(nl)))
        log_a = -LRU_C * r * softplus
        a_ref[d] = jnp.exp(log_a)
        th = jnp.tanh(log_a)
        b_ref[d] = jnp.sqrt(-2.0 * th / (1.0 - th)) * (i * u)


def _rec_mid(xp, xr, conv_w, conv_b, wa_bd, ba, wx_bd, bx, lam, pool_w, pool_scale, dims):
    B, S, L = dims
    T = xp.shape[0]
    n_chunks = T // SEQ_T

    def cur(g):
        return (g, 0)

    def prv(g):
        return (jnp.maximum(g - 1, 0), 0)

    def nxt(g):
        return (jnp.minimum(g + 1, n_chunks - 1), 0)

    tile = lambda idx: pl.BlockSpec((SEQ_T, LRU_WIDTH), idx)
    full = lambda arr: pl.BlockSpec(arr.shape, lambda g: (0,) * arr.ndim)
    kern = functools.partial(_rec_mid_kernel, n_lat_chunks=B * S // SEQ_T, lat_chunks_per_seq=S // SEQ_T,
                             ctx_chunks_per_seq=L // SEQ_T, lat_len=S, ctx_len=L)
    return pl.pallas_call(
        kern,
        grid=(n_chunks,),
        in_specs=[tile(prv), tile(cur), tile(nxt), tile(prv), tile(cur), tile(nxt),
                  full(conv_w), full(conv_b), full(wa_bd), full(ba), full(wx_bd), full(bx), full(lam),
                  full(pool_w), full(pool_scale)],
        out_specs=[pl.BlockSpec((SEQ_T, POOL_WIDTH), cur),
                   pl.BlockSpec((2, SEQ_T, LRU_WIDTH), lambda g: (0, g, 0)),
                   pl.BlockSpec((2, SEQ_T, LRU_WIDTH), lambda g: (0, g, 0))],
        out_shape=[jax.ShapeDtypeStruct((T, POOL_WIDTH), BF16),
                   jax.ShapeDtypeStruct((2, T, LRU_WIDTH), F32),
                   jax.ShapeDtypeStruct((2, T, LRU_WIDTH), F32)],
        compiler_params=_cparams(("parallel",)),
        name="rec_mid",
    )(xp, xp, xp, xr, xr, xr, conv_w, conv_b, wa_bd, ba, wx_bd, bx, lam, pool_w, pool_scale)


def _scan_kernel(a_ref, b_ref, h_ref, carry_ref):
    d = pl.program_id(1)
    s = pl.program_id(2)

    @pl.when(s == 0)
    def _():
        carry_ref[...] = jnp.zeros_like(carry_ref)

    def run(reverse):
        def body(i, h):
            t = (SEQ_T - 1 - i) if reverse else i
            h = a_ref[pl.ds(t, 1), :] * h + b_ref[pl.ds(t, 1), :]
            h_ref[pl.ds(t, 1), :] = h
            return h

        carry_ref[...] = lax.fori_loop(0, SEQ_T, body, carry_ref[...], unroll=8)

    @pl.when(d == 0)
    def _():
        run(False)

    @pl.when(d == 1)
    def _():
        run(True)


def _scan(a, b, dims):
    B, S, L = dims
    T = a.shape[1]
    lat = S // SEQ_T
    ctx = L // SEQ_T
    lat_base = 0
    ctx_base = B * S // SEQ_T

    def idx(bi, d, s):
        fwd = jnp.where(s < ctx, ctx_base + bi * ctx + s, lat_base + bi * lat + (s - ctx))
        rev = jnp.where(s < ctx, ctx_base + bi * ctx + (ctx - 1 - s), lat_base + bi * lat + (lat - 1 - (s - ctx)))
        return (d, jnp.where(d == 0, fwd, rev), 0)

    spec = pl.BlockSpec((None, SEQ_T, LRU_WIDTH), idx)
    return pl.pallas_call(
        _scan_kernel,
        grid=(B, 2, lat + ctx),
        in_specs=[spec, spec],
        out_specs=spec,
        out_shape=jax.ShapeDtypeStruct((2, T, LRU_WIDTH), F32),
        scratch_shapes=[pltpu.VMEM((1, LRU_WIDTH), F32)],
        compiler_params=_cparams(("parallel", "parallel", "arbitrary")),
        name="lru_scan",
    )(a, b)


def _post_common(y, x_ref, mod_ref, nw_ref, xo_ref, h2_ref):
    x_new = x_ref[...] + mod_ref[2:3, :] * y
    xo_ref[...] = x_new
    h2_ref[...] = _rmsnorm_mod(x_new, nw_ref[...], mod_ref[3:4, :], mod_ref[4:5, :]).astype(BF16)


def _post_attn_kernel(a_ref, b_ref, x_ref, mod_ref, nw_ref, w_ref, xo_ref, h2_ref):
    half = a_ref.shape[1]
    y = (jnp.dot(a_ref[...], w_ref[:half, :], preferred_element_type=F32)
         + jnp.dot(b_ref[...], w_ref[half:, :], preferred_element_type=F32))
    _post_common(y, x_ref, mod_ref, nw_ref, xo_ref, h2_ref)


def _post_rec_kernel(pool_ref, hs_ref, g_ref, x_ref, mod_ref, nw_ref, w_ref, xo_ref, h2_ref):
    half = pool_ref.shape[1]
    rec = ((hs_ref[0] + hs_ref[1]) * _gelu(g_ref[...])).astype(BF16)
    y = (jnp.dot(pool_ref[...], w_ref[:half, :], preferred_element_type=F32)
         + jnp.dot(rec, w_ref[half:, :], preferred_element_type=F32))
    _post_common(y, x_ref, mod_ref, nw_ref, xo_ref, h2_ref)


def _post_mixer(kind, parts, x, mods_l, norm_w, w_out, dims, n_tiles):
    B, S, L = dims
    T = x.shape[0]
    n_lat = B * S // TM
    s_tiles = S // TM

    def mod_idx(g):
        return (jnp.where(g < n_lat, g // s_tiles, B), 0, 0)

    row = lambda w: pl.BlockSpec((TM, w), lambda g: (g, 0))
    if kind == "attn":
        kern = _post_attn_kernel
        part_specs = [row(SWA_WIDTH), row(DIFF_WIDTH)]
    else:
        kern = _post_rec_kernel
        part_specs = [row(POOL_WIDTH), pl.BlockSpec((2, TM, LRU_WIDTH), lambda g: (0, g, 0)), row(LRU_WIDTH)]
    n_in = len(parts)
    return pl.pallas_call(
        kern,
        grid=(n_tiles,),
        in_specs=part_specs + [
            row(D_MODEL),
            pl.BlockSpec((None, N_MOD, D_MODEL), mod_idx),
            pl.BlockSpec((1, D_MODEL), lambda g: (0, 0)),
            pl.BlockSpec(w_out.shape, lambda g: (0, 0)),
        ],
        out_specs=[row(D_MODEL), row(D_MODEL)],
        out_shape=[jax.ShapeDtypeStruct((T, D_MODEL), F32), jax.ShapeDtypeStruct((T, D_MODEL), BF16)],
        input_output_aliases={n_in: 0},
        compiler_params=_cparams(("parallel",)),
        name="post_" + kind,
    )(*parts, x, mods_l, norm_w, w_out)


def _peer_cand_blocks():
    return [(i, PEER_TOPK // (i + 1)) for i in range(1, SUBLANES)]


PEER_TAG_BITS = 31


def _peer_tagged(e, rank):
    bits = pltpu.bitcast(e, jnp.int32)
    tag = PEER_TAG_BITS - rank.astype(jnp.int32)
    return pltpu.bitcast((bits & ~PEER_TAG_BITS) | tag, F32)


def _peer_routing(h2_ref, wq_ref, sk_ref, thr_ref, c_ref, e2_ref,
                  q_scr, s_scr, cur_scr, rank_scr, top_scr, cand_scr):
    neg_inf = -jnp.inf
    n_cand = cand_scr.shape[0]

    def extract_step(k, src_ref, dst_ref, rank_ref=None):
        cur = src_ref[...]
        m = jnp.max(cur, axis=0, keepdims=True)
        dst_ref[pl.ds(k, 1), :] = m
        hit = cur == m
        src_ref[...] = jnp.where(hit, neg_inf, cur)
        if rank_ref is not None:
            rank_ref[...] = jnp.where(hit, lax.convert_element_type(k + 1, F32), rank_ref[...])

    def head_body(h, _):
        q0 = pl.multiple_of(h * PEER_QDIM, PEER_QDIM)
        q_scr[...] = _nt_dot(wq_ref[pl.ds(q0, PEER_QDIM), :], h2_ref[...])
        for p in range(2):
            s = jnp.dot(sk_ref[h * 2 + p].astype(BF16), q_scr[p * PEER_HALF:(p + 1) * PEER_HALF, :].astype(BF16),
                        preferred_element_type=F32)
            s_scr[p] = s
            cur_scr[p] = s
        rank_scr[...] = jnp.full(rank_scr.shape, PEER_TOPK + 1.0, F32)

        def top_body(k, _):
            extract_step(k, cur_scr.at[0], top_scr.at[0])
            extract_step(k, cur_scr.at[1], top_scr.at[1], rank_scr)
            return 0

        lax.fori_loop(0, PEER_TOPK, top_body, 0)
        v1 = top_scr[0]
        v2 = top_scr[1]
        cand_scr[0:PEER_TOPK, :] = v1[0:1, :] + v2
        row = lax.broadcasted_iota(jnp.int32, (SUBLANES, 1), 0)
        for i, n_i in _peer_cand_blocks():
            blk = jnp.where(row < n_i, v1[i:i + 1, :] + v2[0:SUBLANES, :], neg_inf)
            cand_scr[PEER_TOPK + (i - 1) * SUBLANES: PEER_TOPK + i * SUBLANES, :] = blk
        cand_scr[n_cand - SUBLANES:, :] = v1[SUBLANES:, :] + v2[0:1, :]
        cand = cand_scr[...]

        def cand_body(k, _):
            extract_step(k, cand_scr, top_scr.at[2])
            return 0

        lax.fori_loop(0, PEER_TOPK, cand_body, 0)
        tau = top_scr[2, PEER_TOPK - 1:PEER_TOPK, :]
        top = v1[0:1, :] + v2[0:1, :]
        z = jnp.sum(jnp.where(cand >= tau, jnp.exp(cand - top), 0.0), axis=0, keepdims=True)

        s1 = s_scr[0]
        ranks = (lax.broadcasted_iota(jnp.int32, (PEER_TOPK, 1), 0) + 1).astype(F32)
        top_scr[2] = _peer_tagged(jnp.exp(v2 - v2[0:1, :]), ranks)
        thr_ref[h] = jnp.full(s1.shape, jnp.inf, F32)

        def thr_body(jj, _):
            hit = s1 + top_scr[1, pl.ds(jj, 1), :] >= tau
            thr_ref[h] = jnp.where(hit, top_scr[2, pl.ds(jj, 1), :], thr_ref[h])
            return 0

        lax.fori_loop(0, PEER_TOPK, thr_body, 0)
        c_ref[h] = jnp.exp(s1 - v1[0:1, :]) / z
        e2_ref[h] = _peer_tagged(jnp.exp(s_scr[1] - v2[0:1, :]), rank_scr[...])
        return 0

    lax.fori_loop(0, PEER_HEADS, head_body, 0)


def _peer_kernel(h2_ref, u_ref, vt_ref, wq_ref, sk_ref, x_ref, mod_ref, fw_ref, xo_ref,
                 h2t_scr, coef_scr, acc_scr, thr_scr, c_scr, e2_scr,
                 q_scr, s_scr, cur_scr, rank_scr, top_scr, cand_scr, *, final):
    j = pl.program_id(1)
    tt = h2_ref.shape[0]
    pair = PEER_UP_GROUP * PEER_NKEYS

    @pl.when(j == 0)
    def _():
        acc_scr[...] = jnp.zeros_like(acc_scr)
        h2t_scr[...] = h2_ref[...].T
        _peer_routing(h2_ref, wq_ref, sk_ref, thr_scr, c_scr, e2_scr,
                      q_scr, s_scr, cur_scr, rank_scr, top_scr, cand_scr)

    for pr in range(PEER_NA // PEER_UP_GROUP):
        act = jnp.dot(u_ref[pr * pair:(pr + 1) * pair, :], h2t_scr[...], preferred_element_type=F32)
        for half in range(PEER_UP_GROUP):
            al = PEER_UP_GROUP * pr + half
            a = j * PEER_NA + al
            thr_full = [thr_scr[h, pl.ds(a, 1), :] for h in range(PEER_HEADS)]
            c_full = [c_scr[h, pl.ds(a, 1), :] for h in range(PEER_HEADS)]
            for tc in range(tt // LANES):
                ls = slice(tc * LANES, (tc + 1) * LANES)
                thr_rows = [r[:, ls] for r in thr_full]
                c_rows = [r[:, ls] for r in c_full]
                for rb in range(PEER_NKEYS // PEER_RB):
                    rs = slice(rb * PEER_RB, (rb + 1) * PEER_RB)
                    w = jnp.zeros((PEER_RB, LANES), F32)
                    for h in range(PEER_HEADS):
                        e2_t = e2_scr[h, rs, ls]
                        w = w + jnp.where(e2_t >= thr_rows[h], c_rows[h] * e2_t, 0.0)
                    rows = slice(half * PEER_NKEYS + rb * PEER_RB, half * PEER_NKEYS + (rb + 1) * PEER_RB)
                    r0 = al * PEER_NKEYS + rb * PEER_RB
                    coef_scr[r0:r0 + PEER_RB, ls] = (w * _gelu(act[rows, ls])).astype(BF16)

    acc_scr[...] += jnp.dot(vt_ref[...], coef_scr[...], preferred_element_type=F32)

    @pl.when(j == pl.num_programs(1) - 1)
    def _():
        x_new = x_ref[...] + mod_ref[5:6, :] * acc_scr[...].T
        if final:
            x_new = x_new * lax.rsqrt(jnp.mean(x_new * x_new, axis=-1, keepdims=True) + EPS) * fw_ref[...]
        xo_ref[...] = x_new


def _peer(h2, u, v_t, wq_t, subkeys, x, mods_l, final_w, dims, n_tiles, final):
    B, S, L = dims
    T = x.shape[0]
    tt = PEER_TT
    ne = PEER_NA * PEER_NKEYS
    n_lat = B * S // tt
    s_tiles = S // tt
    n_cand = PEER_TOPK + SUBLANES * SUBLANES

    def mod_idx(i, j):
        return (jnp.where(i < n_lat, i // s_tiles, B), 0, 0)

    key_f32 = pltpu.VMEM((PEER_HEADS, PEER_NKEYS, tt), F32)
    const = pl.Buffered(1)
    return pl.pallas_call(
        functools.partial(_peer_kernel, final=final),
        grid=(n_tiles, PEER_EXPERTS // ne),
        in_specs=[
            pl.BlockSpec((tt, D_MODEL), lambda i, j: (i, 0)),
            pl.BlockSpec((ne, D_MODEL), lambda i, j: (j, 0)),
            pl.BlockSpec((None, D_MODEL, ne), lambda i, j: (j, 0, 0)),
            pl.BlockSpec(wq_t.shape, lambda i, j: (0, 0), pipeline_mode=const),
            pl.BlockSpec(subkeys.shape, lambda i, j: (0, 0, 0), pipeline_mode=const),
            pl.BlockSpec((tt, D_MODEL), lambda i, j: (i, 0)),
            pl.BlockSpec((None, N_MOD, D_MODEL), mod_idx),
            pl.BlockSpec((1, D_MODEL), lambda i, j: (0, 0)),
        ],
        out_specs=pl.BlockSpec((tt, D_MODEL), lambda i, j: (i, 0)),
        out_shape=jax.ShapeDtypeStruct((n_tiles * tt if final else T, D_MODEL), F32),
        scratch_shapes=[
            pltpu.VMEM((D_MODEL, tt), BF16),
            pltpu.VMEM((ne, tt), BF16),
            pltpu.VMEM((D_MODEL, tt), F32),
            key_f32, key_f32, key_f32,
            pltpu.VMEM((PEER_QDIM, tt), F32),
            pltpu.VMEM((2, PEER_NKEYS, tt), F32),
            pltpu.VMEM((2, PEER_NKEYS, tt), F32),
            pltpu.VMEM((PEER_NKEYS, tt), F32),
            pltpu.VMEM((3, PEER_TOPK, tt), F32),
            pltpu.VMEM((n_cand, tt), F32),
        ],
        input_output_aliases={} if final else {5: 0},
        compiler_params=_cparams(("parallel", "arbitrary")),
        name="peer",
    )(h2, u, v_t, wq_t, subkeys, x, mods_l, final_w)


def _lambda_init(layer):
    return 0.8 - 0.6 * math.exp(-0.3 * layer)


def _rope_tables(S):
    rows = S // GRID_W
    row = jnp.repeat(jnp.arange(rows), GRID_W).astype(F32)
    col = jnp.tile(jnp.arange(GRID_W), rows).astype(F32)
    inv = ROPE_THETA ** (-jnp.arange(ROPE_AXIS_FREQS, dtype=F32) / ROPE_AXIS_FREQS)
    ang = jnp.concatenate([row[:, None] * inv, col[:, None] * inv], axis=-1)
    cos, sin = jnp.cos(ang), jnp.sin(ang)
    cos_t = jnp.tile(cos, (1, LANES // ROPE_HALF))
    sin_t = jnp.tile(jnp.concatenate([-sin, sin], axis=-1), (1, LANES // HEAD_DIM))
    cos_t = jnp.concatenate([cos_t, jnp.ones((TM, LANES), F32)], axis=0)
    sin_t = jnp.concatenate([sin_t, jnp.zeros((TM, LANES), F32)], axis=0)
    return cos_t, sin_t


def _block_diag(w):
    nd, nb, bd, _ = w.shape
    eye = jnp.eye(nb, dtype=w.dtype)
    return jnp.einsum("dnij,nm->dnimj", w, eye).reshape(nd, nb * bd, nb * bd)


def kernel(x, c, ctx, c_ctx, w_mod, b_mod, norm_mix, norm_ffn, w_out, attn_w_in, swa_sink, diff_lambda, diff_subln, rec_w_in, pool_w, pool_scale, lru_conv_w, lru_conv_b, lru_wa, lru_ba, lru_wx, lru_bx, lru_lambda, peer_wq, peer_subkeys, peer_u, peer_v, final_norm):
    B, S, D = x.shape
    L = ctx.shape[1]
    depth = w_mod.shape[0]
    dims = (B, S, L)
    assert D == D_MODEL and S % PEER_TT == 0 and (B * L) % PEER_TT == 0 and L % SEQ_T == 0
    assert S % GRID_W == 0 and S >= 3 * BLOCK and B + 1 <= SUBLANES and S % (DIFF_GROUP * SEQ_T) == 0
    lat_rows = B * S
    T = lat_rows + B * L

    xs = jnp.concatenate([x.reshape(lat_rows, D), ctx.reshape(B * L, D)], axis=0)
    cc = jnp.zeros((SUBLANES, D), F32).at[:B].set(c).at[B].set(c_ctx)
    mods = _modulation(cc, w_mod, b_mod)
    cos_t, sin_t = _rope_tables(S)

    for l in range(depth):
        jl = l // 2
        ctx_out = l < depth - 1
        n_rows = T if ctx_out else lat_rows
        mods_l = mods[l]
        if l % 2 == 0:
            q, k, dq, dk, vt, dvt = _pre_attn(xs, mods_l, norm_mix[l][None], attn_w_in[jl].astype(BF16),
                                              cos_t, sin_t, dims)
            a = _swa(q, k, vt, swa_sink[jl], dims)
            bd = _diff_attn(dq, dk, dvt, diff_lambda[jl], diff_subln[jl][None], _lambda_init(l), dims)
            xs, h2 = _post_mixer("attn", (a, bd), xs, mods_l, norm_ffn[l][None], w_out[l].astype(BF16),
                                 dims, n_rows // TM)
        else:
            xp, xr, g = _pre_rec(xs, mods_l, norm_mix[l][None], rec_w_in[jl].astype(BF16), dims)
            pool, a_co, b_co = _rec_mid(xp, xr, lru_conv_w[jl], lru_conv_b[jl][None],
                                        _block_diag(lru_wa[jl]).astype(BF16), lru_ba[jl],
                                        _block_diag(lru_wx[jl]).astype(BF16), lru_bx[jl], lru_lambda[jl],
                                        pool_w[jl].astype(BF16), pool_scale[jl][None], dims)
            hs = _scan(a_co, b_co, dims)
            xs, h2 = _post_mixer("rec", (pool, hs, g), xs, mods_l, norm_ffn[l][None], w_out[l].astype(BF16),
                                 dims, n_rows // TM)
        n_peer = n_rows // PEER_TT
        wq_t = peer_wq[l].astype(BF16).T
        sk = peer_subkeys[l].reshape(PEER_HEADS * 2, PEER_NKEYS, PEER_HALF)
        ne = PEER_NA * PEER_NKEYS
        v_t = peer_v[l].astype(BF16).reshape(PEER_EXPERTS // ne, ne, D).transpose(0, 2, 1)
        xs = _peer(h2, peer_u[l].astype(BF16), v_t, wq_t, sk, xs, mods_l, final_norm[None], dims, n_peer,
                   final=not ctx_out)

    return xs.reshape(B, S, D)
```

```python
import functools
import math

import jax
import jax.numpy as jnp
from jax import lax
from jax.experimental import pallas as pl
from jax.experimental.pallas import tpu as pltpu

F32 = jnp.float32
BF16 = jnp.bfloat16
HIGHEST = lax.Precision.HIGHEST

D_MODEL = 1024
HEAD_DIM = 64
EPS = 1e-6
NEG = -1e30
N_MOD = 6
SCALE = HEAD_DIM ** -0.5
ROPE_HALF = HEAD_DIM // 2
ROPE_AXIS_FREQS = ROPE_HALF // 2
ROPE_THETA = 10000.0
GRID_W = 64
SWA_Q_HEADS = 8
SWA_KV_HEADS = 2
SWA_GROUP = SWA_Q_HEADS // SWA_KV_HEADS
WINDOW = 128
BLOCK = 128
SWA_WIDTH = SWA_Q_HEADS * HEAD_DIM
SWA_KV_WIDTH = SWA_KV_HEADS * HEAD_DIM
DIFF_HEADS = 4
DIFF_V_DIM = 2 * HEAD_DIM
DIFF_QK_WIDTH = DIFF_HEADS * 2 * HEAD_DIM
DIFF_WIDTH = DIFF_HEADS * DIFF_V_DIM
POOL_WINDOWS = (2, 4, 8, 16)
POOL_GROUPS = 4
POOL_WIDTH = D_MODEL // 2
POOL_GDIM = POOL_WIDTH // POOL_GROUPS
LRU_WIDTH = D_MODEL // 2
LRU_BLOCKS = 8
CONV_W = 4
CONV_LEFT = CONV_W // 2
LRU_C = 8.0
PEER_HEADS = 8
PEER_NKEYS = 128
PEER_EXPERTS = PEER_NKEYS * PEER_NKEYS
PEER_QDIM = 256
PEER_HALF = PEER_QDIM // 2
PEER_TOPK = 16

LANES = 128
SUBLANES = 8
VMEM_LIMIT = 56 * 1024 * 1024

TM = 512
SEQ_T = 256
DIFF_GROUP = 16
DIFF_AHEAD = 2
DIFF_SUM_ROWS = 16
HALO = 8
PEER_TT = 512
PEER_NA = 16
PEER_UP_GROUP = 2
PEER_RB = 64


def _cparams(sem):
    return pltpu.CompilerParams(dimension_semantics=sem, vmem_limit_bytes=VMEM_LIMIT)


def _nt_dot(a, b):
    return lax.dot_general(a, b, (((1,), (1,)), ((), ())), preferred_element_type=F32)


def _rmsnorm_mod(x, w, shift, scale):
    y = x * lax.rsqrt(jnp.mean(x * x, axis=-1, keepdims=True) + EPS) * w
    return y * (1.0 + scale) + shift


def _gelu(x):
    return 0.5 * x * (1.0 + lax.erf(x * (2.0 ** -0.5)))


def _mod_kernel(cc_ref, w_ref, b_ref, o_ref):
    cc = cc_ref[...]
    sc = cc * jax.nn.sigmoid(cc)
    o_ref[...] = jnp.dot(sc, w_ref[...], precision=HIGHEST, preferred_element_type=F32) + b_ref[...]


def _modulation(cc, w_mod, b_mod):
    depth = w_mod.shape[0]
    rows = cc.shape[0]
    out = pl.pallas_call(
        _mod_kernel,
        grid=(depth, N_MOD),
        in_specs=[
            pl.BlockSpec((rows, D_MODEL), lambda l, j: (0, 0)),
            pl.BlockSpec((None, D_MODEL, D_MODEL), lambda l, j: (l, 0, j)),
            pl.BlockSpec((None, 1, D_MODEL), lambda l, j: (l, 0, j)),
        ],
        out_specs=pl.BlockSpec((None, rows, D_MODEL), lambda l, j: (l, 0, j)),
        out_shape=jax.ShapeDtypeStruct((depth, rows, N_MOD * D_MODEL), F32),
        compiler_params=_cparams(("parallel", "parallel")),
        name="modulation",
    )(cc, w_mod, b_mod.reshape(depth, 1, N_MOD * D_MODEL))
    return out.reshape(depth, rows, N_MOD, D_MODEL)


def _rope128(x, cos, sin_signed, first_half):
    partner = jnp.where(first_half, pltpu.roll(x, LANES - ROPE_HALF, 1), pltpu.roll(x, ROPE_HALF, 1))
    return x * cos + partner * sin_signed


def _pre_attn_kernel(x_ref, mod_ref, nw_ref, w_ref, cos_ref, sin_ref,
                     q_ref, k_ref, dq_ref, dk_ref, v_ref, dv_ref):
    h = _rmsnorm_mod(x_ref[...], nw_ref[...], mod_ref[0:1, :], mod_ref[1:2, :])
    p = jnp.dot(h.astype(BF16), w_ref[...], preferred_element_type=F32)
    cos = cos_ref[...]
    sin = sin_ref[...]
    lane = lax.broadcasted_iota(jnp.int32, (1, LANES), 1)
    first_half = (lane % HEAD_DIM) < ROPE_HALF

    def roped(lo, width, scale):
        outs = []
        for c in range(width // LANES):
            xc = p[:, lo + c * LANES: lo + (c + 1) * LANES]
            outs.append((_rope128(xc, cos, sin, first_half) * scale).astype(BF16))
        return outs

    o1 = SWA_WIDTH
    o2 = o1 + SWA_KV_WIDTH
    o3 = o2 + SWA_KV_WIDTH
    o4 = o3 + DIFF_QK_WIDTH
    o5 = o4 + DIFF_QK_WIDTH
    for c, val in enumerate(roped(0, SWA_WIDTH, SCALE)):
        q_ref[:, c * LANES:(c + 1) * LANES] = val
    for c, val in enumerate(roped(o1, SWA_KV_WIDTH, 1.0)):
        k_ref[:, c * LANES:(c + 1) * LANES] = val
    for c in range(TM // SEQ_T):
        v_ref[c] = p[c * SEQ_T:(c + 1) * SEQ_T, o2:o3].T.astype(BF16)
        dv_ref[c] = p[c * SEQ_T:(c + 1) * SEQ_T, o5:].T.astype(BF16)
    for c, val in enumerate(roped(o3, DIFF_QK_WIDTH, SCALE)):
        dq_ref[:, c * LANES:(c + 1) * LANES] = val
    for c, val in enumerate(roped(o4, DIFF_QK_WIDTH, 1.0)):
        dk_ref[:, c * LANES:(c + 1) * LANES] = val


def _pre_attn(x, mods_l, norm_w, w_in, cos_t, sin_t, dims):
    B, S, L = dims
    T = x.shape[0]
    n_lat = B * S // TM
    s_tiles = S // TM

    def mod_idx(g):
        return (jnp.where(g < n_lat, g // s_tiles, B), 0, 0)

    def rope_idx(g):
        return (jnp.where(g < n_lat, g % s_tiles, s_tiles), 0)

    widths = (SWA_WIDTH, SWA_KV_WIDTH, DIFF_QK_WIDTH, DIFF_QK_WIDTH)
    t_widths = (SWA_KV_WIDTH, DIFF_WIDTH)
    return pl.pallas_call(
        _pre_attn_kernel,
        grid=(T // TM,),
        in_specs=[
            pl.BlockSpec((TM, D_MODEL), lambda g: (g, 0)),
            pl.BlockSpec((None, N_MOD, D_MODEL), mod_idx),
            pl.BlockSpec((1, D_MODEL), lambda g: (0, 0)),
            pl.BlockSpec(w_in.shape, lambda g: (0, 0)),
            pl.BlockSpec((TM, LANES), rope_idx),
            pl.BlockSpec((TM, LANES), rope_idx),
        ],
        out_specs=[pl.BlockSpec((TM, w), lambda g: (g, 0)) for w in widths]
        + [pl.BlockSpec((TM // SEQ_T, w, SEQ_T), lambda g: (g, 0, 0)) for w in t_widths],
        out_shape=[jax.ShapeDtypeStruct((T, w), BF16) for w in widths]
        + [jax.ShapeDtypeStruct((T // SEQ_T, w, SEQ_T), BF16) for w in t_widths],
        compiler_params=_cparams(("parallel",)),
        name="pre_attn",
    )(x, mods_l, norm_w, w_in, cos_t, sin_t)


def _swa_kernel(q_ref, kl_ref, vtl_ref, kc_ref, vtc_ref, sink_ref, o_ref, *, n_lat_blocks, n_lat_chunks,
                n_ctx_chunks):
    j = pl.program_id(1)
    is_ctx = j >= n_lat_blocks
    jl = jnp.minimum(j, n_lat_blocks - 1)
    band_chunks = 2
    per_chunk = SEQ_T // BLOCK
    c0 = jnp.clip((jl - 1) // per_chunk, 0, n_lat_chunks - band_chunks)
    r0 = pl.multiple_of(c0 * SEQ_T, SEQ_T)
    kb = kl_ref[pl.ds(r0, band_chunks * SEQ_T), :]
    kc = kc_ref[...]
    glanes = SWA_GROUP * BLOCK
    kpos = r0 + lax.broadcasted_iota(jnp.int32, (band_chunks * SEQ_T, 1), 0)
    qpos = jl * BLOCK + lax.broadcasted_iota(jnp.int32, (1, glanes), 1) % BLOCK
    valid = jnp.logical_and(jnp.abs(qpos - kpos) <= WINDOW, jnp.logical_not(is_ctx))
    q = q_ref[...]
    for hk in range(SWA_KV_HEADS):
        ks = slice(hk * HEAD_DIM, (hk + 1) * HEAD_DIM)
        hs = [hk * SWA_GROUP + g for g in range(SWA_GROUP)]
        qg = jnp.concatenate([q[:, h * HEAD_DIM:(h + 1) * HEAD_DIM] for h in hs], axis=0)
        sink = jnp.concatenate([jnp.full((1, BLOCK), sink_ref[h], F32) for h in hs], axis=1)
        sl = jnp.where(valid, _nt_dot(kb[:, ks], qg), NEG)
        sc = _nt_dot(kc[:, ks], qg)
        m = jnp.maximum(jnp.maximum(jnp.max(sl, axis=0, keepdims=True), jnp.max(sc, axis=0, keepdims=True)), sink)
        pl_ = jnp.exp(sl - m)
        pc = jnp.exp(sc - m)
        den = jnp.sum(pl_, axis=0, keepdims=True) + jnp.sum(pc, axis=0, keepdims=True) + jnp.exp(sink - m)
        pl_b = pl_.astype(BF16)
        pc_b = pc.astype(BF16)
        o = jnp.zeros((HEAD_DIM, glanes), F32)
        for i in range(band_chunks):
            o = o + jnp.dot(vtl_ref[c0 + i, ks, :], pl_b[i * SEQ_T:(i + 1) * SEQ_T], preferred_element_type=F32)
        for i in range(n_ctx_chunks):
            o = o + jnp.dot(vtc_ref[i, ks, :], pc_b[i * SEQ_T:(i + 1) * SEQ_T], preferred_element_type=F32)
        o = o / den
        for g, h in enumerate(hs):
            o_ref[:, h * HEAD_DIM:(h + 1) * HEAD_DIM] = o[:, g * BLOCK:(g + 1) * BLOCK].T.astype(BF16)


def _swa(q, k, vt, sink, dims):
    B, S, L = dims
    T = q.shape[0]
    n_lat_blocks = S // BLOCK
    n_ctx_blocks = L // BLOCK
    n_lat_chunks = S // SEQ_T
    n_ctx_chunks = L // SEQ_T
    lat_rows = B * S

    def q_idx(b, j):
        return (jnp.where(j < n_lat_blocks, b * n_lat_blocks + j,
                          lat_rows // BLOCK + b * n_ctx_blocks + (j - n_lat_blocks)), 0)

    return pl.pallas_call(
        functools.partial(_swa_kernel, n_lat_blocks=n_lat_blocks, n_lat_chunks=n_lat_chunks,
                          n_ctx_chunks=n_ctx_chunks),
        grid=(B, n_lat_blocks + n_ctx_blocks),
        in_specs=[
            pl.BlockSpec((BLOCK, SWA_WIDTH), q_idx),
            pl.BlockSpec((S, SWA_KV_WIDTH), lambda b, j: (b, 0)),
            pl.BlockSpec((n_lat_chunks, SWA_KV_WIDTH, SEQ_T), lambda b, j: (b, 0, 0)),
            pl.BlockSpec((L, SWA_KV_WIDTH), lambda b, j: (lat_rows // L + b, 0)),
            pl.BlockSpec((n_ctx_chunks, SWA_KV_WIDTH, SEQ_T), lambda b, j: (lat_rows // L + b, 0, 0)),
            pl.BlockSpec(memory_space=pltpu.SMEM),
        ],
        out_specs=pl.BlockSpec((BLOCK, SWA_WIDTH), q_idx),
        out_shape=jax.ShapeDtypeStruct((T, SWA_WIDTH), BF16),
        compiler_params=_cparams(("parallel", "arbitrary")),
        name="swa",
    )(q, k, vt, k, vt, sink)


def _diff_kernel(q_ref, kl_ref, vtl_ref, kc_ref, vtc_ref, lam_ref, subln_ref, o_ref, qp_scr, m_scr, acc_scr, *,
                 n_lat_tiles, n_lat_chunks, n_ctx_chunks, lam_init):
    j = pl.program_id(1)
    lv = lam_ref[...]
    lam = (jnp.exp(jnp.sum(lv[0:1] * lv[1:2], axis=-1, keepdims=True))
           - jnp.exp(jnp.sum(lv[2:3] * lv[3:4], axis=-1, keepdims=True)) + lam_init)
    lane = lax.broadcasted_iota(jnp.int32, (1, DIFF_V_DIM), 1)
    heads = [slice(h * DIFF_V_DIM, (h + 1) * DIFF_V_DIM) for h in range(DIFF_HEADS)]
    ones = jnp.ones((DIFF_SUM_ROWS, SEQ_T), BF16)

    for h, vs in enumerate(heads):
        qh = q_ref[:, vs]
        zero = jnp.zeros_like(qh)
        qp_scr[2 * h] = jnp.where(lane < HEAD_DIM, qh, zero)
        qp_scr[2 * h + 1] = jnp.where(lane >= HEAD_DIM, qh, zero)
    m_scr[...] = jnp.full(m_scr.shape, -jnp.inf, F32)
    acc_scr[...] = jnp.zeros(acc_scr.shape, F32)

    def update(blocks_of_head):
        all_blocks = [blocks_of_head(vs) for vs in heads]

        def scores(ch):
            return [_nt_dot(kblk, qp_scr[ch]) for kblk, _ in all_blocks[ch // 2]]

        n_chains = 2 * DIFF_HEADS
        queue = [scores(ch) for ch in range(DIFF_AHEAD)]
        for h, vs in enumerate(heads):
            blocks = all_blocks[h]
            for m in range(2):
                ch = 2 * h + m
                ss = queue.pop(0)
                if ch + DIFF_AHEAD < n_chains:
                    queue.append(scores(ch + DIFF_AHEAD))
                m_old = m_scr[ch]
                m_blk = jnp.max(ss[0], axis=0, keepdims=True)
                for s in ss[1:]:
                    m_blk = jnp.maximum(m_blk, jnp.max(s, axis=0, keepdims=True))
                m_new = jnp.maximum(m_old, m_blk.astype(BF16).astype(F32))
                m_b = m_new.astype(BF16)
                alpha = jnp.exp(m_old - m_new)
                acc = alpha * acc_scr[ch]
                for s, (_, vtblk) in zip(ss, blocks):
                    p = jnp.exp(s.astype(BF16) - m_b)
                    acc = acc + jnp.dot(jnp.concatenate([vtblk, ones], axis=0), p, preferred_element_type=F32)
                m_scr[ch] = m_new
                acc_scr[ch] = acc

    update(lambda vs: [(kc_ref[c * SEQ_T:(c + 1) * SEQ_T, vs], vtc_ref[c, vs, :]) for c in range(n_ctx_chunks)])

    @pl.when(j < n_lat_tiles)
    def _():
        def body(g, _):
            def blocks_of_head(vs):
                blocks = []
                for i in range(DIFF_GROUP):
                    c = g * DIFF_GROUP + i
                    r0 = pl.multiple_of(c * SEQ_T, SEQ_T)
                    blocks.append((kl_ref[pl.ds(r0, SEQ_T), vs], vtl_ref[c, vs, :]))
                return blocks

            update(blocks_of_head)
            return 0

        lax.fori_loop(0, n_lat_chunks // DIFF_GROUP, body, 0)

    def normalised(ch):
        acc = acc_scr[ch]
        return acc[:DIFF_V_DIM] / acc[DIFF_V_DIM:DIFF_V_DIM + 1]

    for h, vs in enumerate(heads):
        o = (normalised(2 * h) - lam * normalised(2 * h + 1)).T
        y = o * lax.rsqrt(jnp.mean(o * o, axis=-1, keepdims=True) + EPS) * subln_ref[...]
        o_ref[:, vs] = (y * (1.0 - lam_init)).astype(BF16)


def _diff_attn(dq, dk, dvt, lam_vecs, subln, lam_init, dims):
    B, S, L = dims
    T = dq.shape[0]
    n_lat_tiles = S // SEQ_T
    n_ctx_tiles = L // SEQ_T
    lat_rows = B * S

    def q_idx(b, j):
        return (jnp.where(j < n_lat_tiles, b * n_lat_tiles + j,
                          lat_rows // SEQ_T + b * n_ctx_tiles + (j - n_lat_tiles)), 0)

    return pl.pallas_call(
        functools.partial(_diff_kernel, n_lat_tiles=n_lat_tiles, n_lat_chunks=n_lat_tiles,
                          n_ctx_chunks=n_ctx_tiles, lam_init=lam_init),
        grid=(B, n_lat_tiles + n_ctx_tiles),
        in_specs=[
            pl.BlockSpec((SEQ_T, DIFF_QK_WIDTH), q_idx),
            pl.BlockSpec((S, DIFF_QK_WIDTH), lambda b, j: (b, 0)),
            pl.BlockSpec((n_lat_tiles, DIFF_WIDTH, SEQ_T), lambda b, j: (b, 0, 0)),
            pl.BlockSpec((L, DIFF_QK_WIDTH), lambda b, j: (lat_rows // L + b, 0)),
            pl.BlockSpec((n_ctx_tiles, DIFF_WIDTH, SEQ_T), lambda b, j: (lat_rows // L + b, 0, 0)),
            pl.BlockSpec(lam_vecs.shape, lambda b, j: (0, 0)),
            pl.BlockSpec((1, DIFF_V_DIM), lambda b, j: (0, 0)),
        ],
        out_specs=pl.BlockSpec((SEQ_T, DIFF_WIDTH), q_idx),
        out_shape=jax.ShapeDtypeStruct((T, DIFF_WIDTH), BF16),
        scratch_shapes=[
            pltpu.VMEM((2 * DIFF_HEADS, SEQ_T, DIFF_V_DIM), BF16),
            pltpu.VMEM((2 * DIFF_HEADS, 1, SEQ_T), F32),
            pltpu.VMEM((2 * DIFF_HEADS, DIFF_V_DIM + DIFF_SUM_ROWS, SEQ_T), F32),
        ],
        compiler_params=_cparams(("parallel", "arbitrary")),
        name="diff_attn",
    )(dq, dk, dvt, dk, dvt, lam_vecs, subln)


def _pre_rec_kernel(x_ref, mod_ref, nw_ref, w_ref, xp_ref, xr_ref, g_ref):
    h = _rmsnorm_mod(x_ref[...], nw_ref[...], mod_ref[0:1, :], mod_ref[1:2, :])
    p = jnp.dot(h.astype(BF16), w_ref[...], preferred_element_type=F32)
    xp_ref[...] = p[:, :POOL_WIDTH]
    xr_ref[...] = p[:, POOL_WIDTH:POOL_WIDTH + LRU_WIDTH]
    g_ref[...] = p[:, POOL_WIDTH + LRU_WIDTH:]


def _pre_rec(x, mods_l, norm_w, w_in, dims):
    B, S, L = dims
    T = x.shape[0]
    n_lat = B * S // TM
    s_tiles = S // TM

    def mod_idx(g):
        return (jnp.where(g < n_lat, g // s_tiles, B), 0, 0)

    widths = (POOL_WIDTH, LRU_WIDTH, LRU_WIDTH)
    return pl.pallas_call(
        _pre_rec_kernel,
        grid=(T // TM,),
        in_specs=[
            pl.BlockSpec((TM, D_MODEL), lambda g: (g, 0)),
            pl.BlockSpec((None, N_MOD, D_MODEL), mod_idx),
            pl.BlockSpec((1, D_MODEL), lambda g: (0, 0)),
            pl.BlockSpec(w_in.shape, lambda g: (0, 0)),
        ],
        out_specs=[pl.BlockSpec((TM, w), lambda g: (g, 0)) for w in widths],
        out_shape=[jax.ShapeDtypeStruct((T, w), F32) for w in widths],
        compiler_params=_cparams(("parallel",)),
        name="pre_rec",
    )(x, mods_l, norm_w, w_in)


def _rec_mid_kernel(xp_p, xp_c, xp_n, xr_p, xr_c, xr_n, cw_ref, cb_ref, wa_ref, ba_ref, wx_ref, bx_ref,
                    lam_ref, pw_ref, ps_ref, pool_ref, a_ref, b_ref, *, n_lat_chunks, lat_chunks_per_seq,
                    ctx_chunks_per_seq, lat_len, ctx_len):
    g = pl.program_id(0)
    is_lat = g < n_lat_chunks
    cps = jnp.where(is_lat, lat_chunks_per_seq, ctx_chunks_per_seq)
    within = jnp.where(is_lat, g % lat_chunks_per_seq, (g - n_lat_chunks) % ctx_chunks_per_seq)
    has_prev = within > 0
    has_next = within < cps - 1
    seg_len = jnp.where(is_lat, lat_len, ctx_len)
    ext_rows = SEQ_T + 2 * HALO

    def extended(prev_ref, cur_ref, next_ref):
        prev = jnp.where(has_prev, prev_ref[SEQ_T - HALO:, :], 0.0)
        nxt = jnp.where(has_next, next_ref[:HALO, :], 0.0)
        return jnp.concatenate([prev, cur_ref[...], nxt], axis=0)

    def shifted(ext, off):
        return pltpu.roll(ext, (-off) % ext_rows, 0)[HALO:HALO + SEQ_T, :]

    xp_ext = extended(xp_p, xp_c, xp_n)
    t = within * SEQ_T + lax.broadcasted_iota(jnp.int32, (SEQ_T, 1), 0)
    for gi, w in enumerate(POOL_WINDOWS):
        cols = slice(gi * POOL_GDIM, (gi + 1) * POOL_GDIM)
        eg = xp_ext[:, cols]
        tot = shifted(eg, -(w // 2))
        for off in range(-(w // 2) + 1, w - w // 2):
            tot = tot + shifted(eg, off)
        lo = jnp.clip(t - w // 2, 0, seg_len)
        hi = jnp.clip(t - w // 2 + w, 0, seg_len)
        cnt = (hi - lo).astype(F32)
        d = tot / cnt - xp_c[:, cols]
        y = jnp.dot(d.astype(BF16), pw_ref[gi], preferred_element_type=F32)
        pool_ref[:, cols] = (y * ps_ref[:, cols]).astype(BF16)

    xr_ext = extended(xr_p, xr_c, xr_n)
    u = cb_ref[...] + cw_ref[0:1, :] * shifted(xr_ext, -CONV_LEFT)
    for k in range(1, CONV_W):
        u = u + cw_ref[k:k + 1, :] * shifted(xr_ext, k - CONV_LEFT)
    ub = u.astype(BF16)
    for d in range(2):
        r = jax.nn.sigmoid(jnp.dot(ub, wa_ref[d], preferred_element_type=F32) + ba_ref[d:d + 1, :])
        i = jax.nn.sigmoid(jnp.dot(ub, wx_ref[d], preferred_element_type=F32) + bx_ref[d:d + 1, :])
        nl = -lam_ref[d:d + 1, :]
        softplus = jnp.maximum(nl, 0.0) + jnp.log1p(jnp.exp(-jnp.abs(nl)))
        log_a = -LRU_C * r * softplus
        a_ref[d] = jnp.exp(log_a)
        th = jnp.tanh(log_a)
        b_ref[d] = jnp.sqrt(-2.0 * th / (1.0 - th)) * (i * u)


def _rec_mid(xp, xr, conv_w, conv_b, wa_bd, ba, wx_bd, bx, lam, pool_w, pool_scale, dims):
    B, S, L = dims
    T = xp.shape[0]
    n_chunks = T // SEQ_T

    def cur(g):
        return (g, 0)

    def prv(g):
        return (jnp.maximum(g - 1, 0), 0)

    def nxt(g):
        return (jnp.minimum(g + 1, n_chunks - 1), 0)

    tile = lambda idx: pl.BlockSpec((SEQ_T, LRU_WIDTH), idx)
    full = lambda arr: pl.BlockSpec(arr.shape, lambda g: (0,) * arr.ndim)
    kern = functools.partial(_rec_mid_kernel, n_lat_chunks=B * S // SEQ_T, lat_chunks_per_seq=S // SEQ_T,
                             ctx_chunks_per_seq=L // SEQ_T, lat_len=S, ctx_len=L)
    return pl.pallas_call(
        kern,
        grid=(n_chunks,),
        in_specs=[tile(prv), tile(cur), tile(nxt), tile(prv), tile(cur), tile(nxt),
                  full(conv_w), full(conv_b), full(wa_bd), full(ba), full(wx_bd), full(bx), full(lam),
                  full(pool_w), full(pool_scale)],
        out_specs=[pl.BlockSpec((SEQ_T, POOL_WIDTH), cur),
                   pl.BlockSpec((2, SEQ_T, LRU_WIDTH), lambda g: (0, g, 0)),
                   pl.BlockSpec((2, SEQ_T, LRU_WIDTH), lambda g: (0, g, 0))],
        out_shape=[jax.ShapeDtypeStruct((T, POOL_WIDTH), BF16),
                   jax.ShapeDtypeStruct((2, T, LRU_WIDTH), F32),
                   jax.ShapeDtypeStruct((2, T, LRU_WIDTH), F32)],
        compiler_params=_cparams(("parallel",)),
        name="rec_mid",
    )(xp, xp, xp, xr, xr, xr, conv_w, conv_b, wa_bd, ba, wx_bd, bx, lam, pool_w, pool_scale)


def _scan_kernel(af_ref, bf_ref, ar_ref, br_ref, hf_ref, hr_ref, carry_ref):
    s = pl.program_id(1)

    @pl.when(s == 0)
    def _():
        carry_ref[...] = jnp.zeros_like(carry_ref)

    def body(i, carry):
        hf, hr = carry
        t = SEQ_T - 1 - i
        hf = af_ref[pl.ds(i, 1), :] * hf + bf_ref[pl.ds(i, 1), :]
        hr = ar_ref[pl.ds(t, 1), :] * hr + br_ref[pl.ds(t, 1), :]
        hf_ref[pl.ds(i, 1), :] = hf
        hr_ref[pl.ds(t, 1), :] = hr
        return hf, hr

    hf, hr = lax.fori_loop(0, SEQ_T, body, (carry_ref[0], carry_ref[1]), unroll=8)
    carry_ref[0] = hf
    carry_ref[1] = hr


def _scan(a, b, dims):
    B, S, L = dims
    T = a.shape[1]
    lat = S // SEQ_T
    ctx = L // SEQ_T
    lat_base = 0
    ctx_base = B * S // SEQ_T

    def fwd(bi, s):
        return jnp.where(s < ctx, ctx_base + bi * ctx + s, lat_base + bi * lat + (s - ctx))

    def rev(bi, s):
        return jnp.where(s < ctx, ctx_base + bi * ctx + (ctx - 1 - s), lat_base + bi * lat + (lat - 1 - (s - ctx)))

    in_f = pl.BlockSpec((None, SEQ_T, LRU_WIDTH), lambda bi, s: (0, fwd(bi, s), 0))
    in_r = pl.BlockSpec((None, SEQ_T, LRU_WIDTH), lambda bi, s: (1, rev(bi, s), 0))
    out_f = pl.BlockSpec((SEQ_T, LRU_WIDTH), lambda bi, s: (fwd(bi, s), 0))
    out_r = pl.BlockSpec((SEQ_T, LRU_WIDTH), lambda bi, s: (rev(bi, s), 0))
    out = jax.ShapeDtypeStruct((T, LRU_WIDTH), F32)
    return pl.pallas_call(
        _scan_kernel,
        grid=(B, lat + ctx),
        in_specs=[in_f, in_f, in_r, in_r],
        out_specs=[out_f, out_r],
        out_shape=[out, out],
        scratch_shapes=[pltpu.VMEM((2, 1, LRU_WIDTH), F32)],
        compiler_params=_cparams(("parallel", "arbitrary")),
        name="lru_scan",
    )(a, b, a, b)


def _post_common(y, x_ref, mod_ref, nw_ref, xo_ref, h2_ref):
    x_new = x_ref[...] + mod_ref[2:3, :] * y
    xo_ref[...] = x_new
    h2_ref[...] = _rmsnorm_mod(x_new, nw_ref[...], mod_ref[3:4, :], mod_ref[4:5, :]).astype(BF16)


def _post_attn_kernel(a_ref, b_ref, x_ref, mod_ref, nw_ref, w_ref, xo_ref, h2_ref):
    half = a_ref.shape[1]
    y = (jnp.dot(a_ref[...], w_ref[:half, :], preferred_element_type=F32)
         + jnp.dot(b_ref[...], w_ref[half:, :], preferred_element_type=F32))
    _post_common(y, x_ref, mod_ref, nw_ref, xo_ref, h2_ref)


def _post_rec_kernel(pool_ref, hf_ref, hr_ref, g_ref, x_ref, mod_ref, nw_ref, w_ref, xo_ref, h2_ref):
    half = pool_ref.shape[1]
    rec = ((hf_ref[...] + hr_ref[...]) * _gelu(g_ref[...])).astype(BF16)
    y = (jnp.dot(pool_ref[...], w_ref[:half, :], preferred_element_type=F32)
         + jnp.dot(rec, w_ref[half:, :], preferred_element_type=F32))
    _post_common(y, x_ref, mod_ref, nw_ref, xo_ref, h2_ref)


def _post_mixer(kind, parts, x, mods_l, norm_w, w_out, dims, n_tiles):
    B, S, L = dims
    T = x.shape[0]
    n_lat = B * S // TM
    s_tiles = S // TM

    def mod_idx(g):
        return (jnp.where(g < n_lat, g // s_tiles, B), 0, 0)

    row = lambda w: pl.BlockSpec((TM, w), lambda g: (g, 0))
    if kind == "attn":
        kern = _post_attn_kernel
        part_specs = [row(SWA_WIDTH), row(DIFF_WIDTH)]
    else:
        kern = _post_rec_kernel
        part_specs = [row(POOL_WIDTH), row(LRU_WIDTH), row(LRU_WIDTH), row(LRU_WIDTH)]
    n_in = len(parts)
    return pl.pallas_call(
        kern,
        grid=(n_tiles,),
        in_specs=part_specs + [
            row(D_MODEL),
            pl.BlockSpec((None, N_MOD, D_MODEL), mod_idx),
            pl.BlockSpec((1, D_MODEL), lambda g: (0, 0)),
            pl.BlockSpec(w_out.shape, lambda g: (0, 0)),
        ],
        out_specs=[row(D_MODEL), row(D_MODEL)],
        out_shape=[jax.ShapeDtypeStruct((T, D_MODEL), F32), jax.ShapeDtypeStruct((T, D_MODEL), BF16)],
        input_output_aliases={n_in: 0},
        compiler_params=_cparams(("parallel",)),
        name="post_" + kind,
    )(*parts, x, mods_l, norm_w, w_out)


def _peer_cand_blocks():
    return [(i, PEER_TOPK // (i + 1)) for i in range(1, SUBLANES)]


PEER_TAG_BITS = 31


def _peer_tagged(e, rank):
    bits = pltpu.bitcast(e, jnp.int32)
    tag = PEER_TAG_BITS - rank.astype(jnp.int32)
    return pltpu.bitcast((bits & ~PEER_TAG_BITS) | tag, F32)


def _peer_routing(h2_ref, wq_ref, sk_ref, thr_ref, c_ref, e2_ref,
                  q_scr, s_scr, cur_scr, rank_scr, top_scr, cand_scr):
    neg_inf = -jnp.inf
    n_cand = cand_scr.shape[0]

    def extract_step(k, src_ref, dst_ref, rank_ref=None):
        cur = src_ref[...]
        m = jnp.max(cur, axis=0, keepdims=True)
        dst_ref[pl.ds(k, 1), :] = m
        hit = cur == m
        src_ref[...] = jnp.where(hit, neg_inf, cur)
        if rank_ref is not None:
            rank_ref[...] = jnp.where(hit, lax.convert_element_type(k + 1, F32), rank_ref[...])

    def head_body(h, _):
        q0 = pl.multiple_of(h * PEER_QDIM, PEER_QDIM)
        q_scr[...] = _nt_dot(wq_ref[pl.ds(q0, PEER_QDIM), :], h2_ref[...])
        for p in range(2):
            s = jnp.dot(sk_ref[h * 2 + p].astype(BF16), q_scr[p * PEER_HALF:(p + 1) * PEER_HALF, :].astype(BF16),
                        preferred_element_type=F32)
            s_scr[p] = s
            cur_scr[p] = s
        rank_scr[...] = jnp.full(rank_scr.shape, PEER_TOPK + 1.0, F32)

        def top_body(k, _):
            extract_step(k, cur_scr.at[0], top_scr.at[0])
            extract_step(k, cur_scr.at[1], top_scr.at[1], rank_scr)
            return 0

        lax.fori_loop(0, PEER_TOPK, top_body, 0)
        v1 = top_scr[0]
        v2 = top_scr[1]
        cand_scr[0:PEER_TOPK, :] = v1[0:1, :] + v2
        row = lax.broadcasted_iota(jnp.int32, (SUBLANES, 1), 0)
        for i, n_i in _peer_cand_blocks():
            blk = jnp.where(row < n_i, v1[i:i + 1, :] + v2[0:SUBLANES, :], neg_inf)
            cand_scr[PEER_TOPK + (i - 1) * SUBLANES: PEER_TOPK + i * SUBLANES, :] = blk
        cand_scr[n_cand - SUBLANES:, :] = v1[SUBLANES:, :] + v2[0:1, :]
        cand = cand_scr[...]

        def cand_body(k, _):
            extract_step(k, cand_scr, top_scr.at[2])
            return 0

        lax.fori_loop(0, PEER_TOPK, cand_body, 0)
        tau = top_scr[2, PEER_TOPK - 1:PEER_TOPK, :]
        top = v1[0:1, :] + v2[0:1, :]
        z = jnp.sum(jnp.where(cand >= tau, jnp.exp(cand - top), 0.0), axis=0, keepdims=True)

        s1 = s_scr[0]
        ranks = (lax.broadcasted_iota(jnp.int32, (PEER_TOPK, 1), 0) + 1).astype(F32)
        top_scr[2] = _peer_tagged(jnp.exp(v2 - v2[0:1, :]), ranks)
        thr_ref[h] = jnp.full(s1.shape, jnp.inf, F32)

        def thr_body(jj, _):
            hit = s1 + top_scr[1, pl.ds(jj, 1), :] >= tau
            thr_ref[h] = jnp.where(hit, top_scr[2, pl.ds(jj, 1), :], thr_ref[h])
            return 0

        lax.fori_loop(0, PEER_TOPK, thr_body, 0)
        c_ref[h] = jnp.exp(s1 - v1[0:1, :]) / z
        e2_ref[h] = _peer_tagged(jnp.exp(s_scr[1] - v2[0:1, :]), rank_scr[...])
        return 0

    lax.fori_loop(0, PEER_HEADS, head_body, 0)


def _peer_kernel(h2_ref, u_ref, vt_ref, wq_ref, sk_ref, x_ref, mod_ref, fw_ref, xo_ref,
                 h2t_scr, coef_scr, acc_scr, thr_scr, c_scr, e2_scr,
                 q_scr, s_scr, cur_scr, rank_scr, top_scr, cand_scr, *, final):
    j = pl.program_id(1)
    tt = h2_ref.shape[0]
    pair = PEER_UP_GROUP * PEER_NKEYS

    @pl.when(j == 0)
    def _():
        acc_scr[...] = jnp.zeros_like(acc_scr)
        h2t_scr[...] = h2_ref[...].T
        _peer_routing(h2_ref, wq_ref, sk_ref, thr_scr, c_scr, e2_scr,
                      q_scr, s_scr, cur_scr, rank_scr, top_scr, cand_scr)

    for pr in range(PEER_NA // PEER_UP_GROUP):
        act = jnp.dot(u_ref[pr * pair:(pr + 1) * pair, :], h2t_scr[...], preferred_element_type=F32)
        for half in range(PEER_UP_GROUP):
            al = PEER_UP_GROUP * pr + half
            a = j * PEER_NA + al
            thr_full = [thr_scr[h, pl.ds(a, 1), :] for h in range(PEER_HEADS)]
            c_full = [c_scr[h, pl.ds(a, 1), :] for h in range(PEER_HEADS)]
            for tc in range(tt // LANES):
                ls = slice(tc * LANES, (tc + 1) * LANES)
                thr_rows = [r[:, ls] for r in thr_full]
                c_rows = [r[:, ls] for r in c_full]
                for rb in range(PEER_NKEYS // PEER_RB):
                    rs = slice(rb * PEER_RB, (rb + 1) * PEER_RB)
                    w = jnp.zeros((PEER_RB, LANES), F32)
                    for h in range(PEER_HEADS):
                        e2_t = e2_scr[h, rs, ls]
                        w = w + jnp.where(e2_t >= thr_rows[h], c_rows[h] * e2_t, 0.0)
                    rows = slice(half * PEER_NKEYS + rb * PEER_RB, half * PEER_NKEYS + (rb + 1) * PEER_RB)
                    r0 = al * PEER_NKEYS + rb * PEER_RB
                    coef_scr[r0:r0 + PEER_RB, ls] = (w * _gelu(act[rows, ls])).astype(BF16)

    acc_scr[...] += jnp.dot(vt_ref[...], coef_scr[...], preferred_element_type=F32)

    @pl.when(j == pl.num_programs(1) - 1)
    def _():
        x_new = x_ref[...] + mod_ref[5:6, :] * acc_scr[...].T
        if final:
            x_new = x_new * lax.rsqrt(jnp.mean(x_new * x_new, axis=-1, keepdims=True) + EPS) * fw_ref[...]
        xo_ref[...] = x_new


def _peer(h2, u, v_t, wq_t, subkeys, x, mods_l, final_w, dims, n_tiles, final):
    B, S, L = dims
    T = x.shape[0]
    tt = PEER_TT
    ne = PEER_NA * PEER_NKEYS
    n_lat = B * S // tt
    s_tiles = S // tt
    n_cand = PEER_TOPK + SUBLANES * SUBLANES

    def mod_idx(i, j):
        return (jnp.where(i < n_lat, i // s_tiles, B), 0, 0)

    key_f32 = pltpu.VMEM((PEER_HEADS, PEER_NKEYS, tt), F32)
    const = pl.Buffered(1)
    return pl.pallas_call(
        functools.partial(_peer_kernel, final=final),
        grid=(n_tiles, PEER_EXPERTS // ne),
        in_specs=[
            pl.BlockSpec((tt, D_MODEL), lambda i, j: (i, 0)),
            pl.BlockSpec((ne, D_MODEL), lambda i, j: (j, 0)),
            pl.BlockSpec((None, D_MODEL, ne), lambda i, j: (j, 0, 0)),
            pl.BlockSpec(wq_t.shape, lambda i, j: (0, 0), pipeline_mode=const),
            pl.BlockSpec(subkeys.shape, lambda i, j: (0, 0, 0), pipeline_mode=const),
            pl.BlockSpec((tt, D_MODEL), lambda i, j: (i, 0)),
            pl.BlockSpec((None, N_MOD, D_MODEL), mod_idx),
            pl.BlockSpec((1, D_MODEL), lambda i, j: (0, 0)),
        ],
        out_specs=pl.BlockSpec((tt, D_MODEL), lambda i, j: (i, 0)),
        out_shape=jax.ShapeDtypeStruct((n_tiles * tt if final else T, D_MODEL), F32),
        scratch_shapes=[
            pltpu.VMEM((D_MODEL, tt), BF16),
            pltpu.VMEM((ne, tt), BF16),
            pltpu.VMEM((D_MODEL, tt), F32),
            key_f32, key_f32, key_f32,
            pltpu.VMEM((PEER_QDIM, tt), F32),
            pltpu.VMEM((2, PEER_NKEYS, tt), F32),
            pltpu.VMEM((2, PEER_NKEYS, tt), F32),
            pltpu.VMEM((PEER_NKEYS, tt), F32),
            pltpu.VMEM((3, PEER_TOPK, tt), F32),
            pltpu.VMEM((n_cand, tt), F32),
        ],
        input_output_aliases={} if final else {5: 0},
        compiler_params=_cparams(("parallel", "arbitrary")),
        name="peer",
    )(h2, u, v_t, wq_t, subkeys, x, mods_l, final_w)


def _lambda_init(layer):
    return 0.8 - 0.6 * math.exp(-0.3 * layer)


def _rope_tables(S):
    rows = S // GRID_W
    row = jnp.repeat(jnp.arange(rows), GRID_W).astype(F32)
    col = jnp.tile(jnp.arange(GRID_W), rows).astype(F32)
    inv = ROPE_THETA ** (-jnp.arange(ROPE_AXIS_FREQS, dtype=F32) / ROPE_AXIS_FREQS)
    ang = jnp.concatenate([row[:, None] * inv, col[:, None] * inv], axis=-1)
    cos, sin = jnp.cos(ang), jnp.sin(ang)
    cos_t = jnp.tile(cos, (1, LANES // ROPE_HALF))
    sin_t = jnp.tile(jnp.concatenate([-sin, sin], axis=-1), (1, LANES // HEAD_DIM))
    cos_t = jnp.concatenate([cos_t, jnp.ones((TM, LANES), F32)], axis=0)
    sin_t = jnp.concatenate([sin_t, jnp.zeros((TM, LANES), F32)], axis=0)
    return cos_t, sin_t


def _block_diag(w):
    nd, nb, bd, _ = w.shape
    eye = jnp.eye(nb, dtype=w.dtype)
    return jnp.einsum("dnij,nm->dnimj", w, eye).reshape(nd, nb * bd, nb * bd)


def kernel(x, c, ctx, c_ctx, w_mod, b_mod, norm_mix, norm_ffn, w_out, attn_w_in, swa_sink, diff_lambda, diff_subln, rec_w_in, pool_w, pool_scale, lru_conv_w, lru_conv_b, lru_wa, lru_ba, lru_wx, lru_bx, lru_lambda, peer_wq, peer_subkeys, peer_u, peer_v, final_norm):
    B, S, D = x.shape
    L = ctx.shape[1]
    depth = w_mod.shape[0]
    dims = (B, S, L)
    assert D == D_MODEL and S % PEER_TT == 0 and (B * L) % PEER_TT == 0 and L % SEQ_T == 0
    assert TM % SEQ_T == 0 and S % TM == 0 and (B * L) % TM == 0
    assert S % GRID_W == 0 and S >= 3 * BLOCK and B + 1 <= SUBLANES and S % (DIFF_GROUP * SEQ_T) == 0
    lat_rows = B * S
    T = lat_rows + B * L

    xs = jnp.concatenate([x.reshape(lat_rows, D), ctx.reshape(B * L, D)], axis=0)
    cc = jnp.zeros((SUBLANES, D), F32).at[:B].set(c).at[B].set(c_ctx)
    mods = _modulation(cc, w_mod, b_mod)
    cos_t, sin_t = _rope_tables(S)

    for l in range(depth):
        jl = l // 2
        ctx_out = l < depth - 1
        n_rows = T if ctx_out else lat_rows
        mods_l = mods[l]
        if l % 2 == 0:
            q, k, dq, dk, vt, dvt = _pre_attn(xs, mods_l, norm_mix[l][None], attn_w_in[jl].astype(BF16),
                                              cos_t, sin_t, dims)
            a = _swa(q, k, vt, swa_sink[jl], dims)
            bd = _diff_attn(dq, dk, dvt, diff_lambda[jl], diff_subln[jl][None], _lambda_init(l), dims)
            xs, h2 = _post_mixer("attn", (a, bd), xs, mods_l, norm_ffn[l][None], w_out[l].astype(BF16),
                                 dims, n_rows // TM)
        else:
            xp, xr, g = _pre_rec(xs, mods_l, norm_mix[l][None], rec_w_in[jl].astype(BF16), dims)
            pool, a_co, b_co = _rec_mid(xp, xr, lru_conv_w[jl], lru_conv_b[jl][None],
                                        _block_diag(lru_wa[jl]).astype(BF16), lru_ba[jl],
                                        _block_diag(lru_wx[jl]).astype(BF16), lru_bx[jl], lru_lambda[jl],
                                        pool_w[jl].astype(BF16), pool_scale[jl][None], dims)
            h_fwd, h_rev = _scan(a_co, b_co, dims)
            xs, h2 = _post_mixer("rec", (pool, h_fwd, h_rev, g), xs, mods_l, norm_ffn[l][None], w_out[l].astype(BF16),
                                 dims, n_rows // TM)
        n_peer = n_rows // PEER_TT
        wq_t = peer_wq[l].astype(BF16).T
        sk = peer_subkeys[l].reshape(PEER_HEADS * 2, PEER_NKEYS, PEER_HALF)
        ne = PEER_NA * PEER_NKEYS
        v_t = peer_v[l].astype(BF16).reshape(PEER_EXPERTS // ne, ne, D).transpose(0, 2, 1)
        xs = _peer(h2, peer_u[l].astype(BF16), v_t, wq_t, sk, xs, mods_l, final_norm[None], dims, n_peer,
                   final=not ctx_out)

    return xs.reshape(B, S, D)
```

```python
import functools
import math

import jax
import jax.numpy as jnp
from jax import lax
from jax.experimental import pallas as pl
from jax.experimental.pallas import tpu as pltpu

F32 = jnp.float32
BF16 = jnp.bfloat16
HIGHEST = lax.Precision.HIGHEST

D_MODEL = 1024
HEAD_DIM = 64
EPS = 1e-6
NEG = -1e30
N_MOD = 6
SCALE = HEAD_DIM ** -0.5
ROPE_HALF = HEAD_DIM // 2
ROPE_AXIS_FREQS = ROPE_HALF // 2
ROPE_THETA = 10000.0
GRID_W = 64
SWA_Q_HEADS = 8
SWA_KV_HEADS = 2
SWA_GROUP = SWA_Q_HEADS // SWA_KV_HEADS
WINDOW = 128
BLOCK = 128
SWA_WIDTH = SWA_Q_HEADS * HEAD_DIM
SWA_KV_WIDTH = SWA_KV_HEADS * HEAD_DIM
DIFF_HEADS = 4
DIFF_V_DIM = 2 * HEAD_DIM
DIFF_QK_WIDTH = DIFF_HEADS * 2 * HEAD_DIM
DIFF_WIDTH = DIFF_HEADS * DIFF_V_DIM
POOL_WINDOWS = (2, 4, 8, 16)
POOL_GROUPS = 4
POOL_WIDTH = D_MODEL // 2
POOL_GDIM = POOL_WIDTH // POOL_GROUPS
LRU_WIDTH = D_MODEL // 2
CONV_W = 4
CONV_LEFT = CONV_W // 2
LRU_C = 8.0
PEER_HEADS = 8
PEER_NKEYS = 128
PEER_EXPERTS = PEER_NKEYS * PEER_NKEYS
PEER_QDIM = 256
PEER_HALF = PEER_QDIM // 2
PEER_TOPK = 16

LANES = 128
SUBLANES = 8
VMEM_LIMIT = 56 * 1024 * 1024

TM = 1024
SEQ_T = 256
DIFF_GROUP = 16
DIFF_AHEAD = 2
DIFF_SUM_ROWS = 16
HALO = 8
PEER_TT = 512
PEER_NA = 16
PEER_UP_GROUP = 2
PEER_RB = 64


def _cparams(sem):
    return pltpu.CompilerParams(dimension_semantics=sem, vmem_limit_bytes=VMEM_LIMIT)


def _nt_dot(a, b):
    return lax.dot_general(a, b, (((1,), (1,)), ((), ())), preferred_element_type=F32)


def _rmsnorm_mod(x, w, shift, scale):
    y = x * lax.rsqrt(jnp.mean(x * x, axis=-1, keepdims=True) + EPS) * w
    return y * (1.0 + scale) + shift


def _gelu(x):
    return 0.5 * x * (1.0 + lax.erf(x * (2.0 ** -0.5)))


def _mod_kernel(cc_ref, w_ref, b_ref, o_ref):
    cc = cc_ref[...]
    sc = cc * jax.nn.sigmoid(cc)
    o_ref[...] = jnp.dot(sc, w_ref[...], precision=HIGHEST, preferred_element_type=F32) + b_ref[...]


def _modulation(cc, w_mod, b_mod):
    depth = w_mod.shape[0]
    rows = cc.shape[0]
    out = pl.pallas_call(
        _mod_kernel,
        grid=(depth, N_MOD),
        in_specs=[
            pl.BlockSpec((rows, D_MODEL), lambda l, j: (0, 0)),
            pl.BlockSpec((None, D_MODEL, D_MODEL), lambda l, j: (l, 0, j)),
            pl.BlockSpec((None, 1, D_MODEL), lambda l, j: (l, 0, j)),
        ],
        out_specs=pl.BlockSpec((None, rows, D_MODEL), lambda l, j: (l, 0, j)),
        out_shape=jax.ShapeDtypeStruct((depth, rows, N_MOD * D_MODEL), F32),
        compiler_params=_cparams(("parallel", "parallel")),
        name="modulation",
    )(cc, w_mod, b_mod.reshape(depth, 1, N_MOD * D_MODEL))
    return out.reshape(depth, rows, N_MOD, D_MODEL)


def _rope128(x, cos, sin_signed, first_half):
    partner = jnp.where(first_half, pltpu.roll(x, LANES - ROPE_HALF, 1), pltpu.roll(x, ROPE_HALF, 1))
    return x * cos + partner * sin_signed


def _pre_attn_kernel(x_ref, mod_ref, nw_ref, w_ref, cos_ref, sin_ref,
                     q_ref, k_ref, dq_ref, dk_ref, v_ref, dv_ref):
    h = _rmsnorm_mod(x_ref[...], nw_ref[...], mod_ref[0:1, :], mod_ref[1:2, :])
    p = jnp.dot(h.astype(BF16), w_ref[...], preferred_element_type=F32)
    cos = cos_ref[...]
    sin = sin_ref[...]
    lane = lax.broadcasted_iota(jnp.int32, (1, LANES), 1)
    first_half = (lane % HEAD_DIM) < ROPE_HALF

    def roped(lo, width, scale):
        outs = []
        for c in range(width // LANES):
            xc = p[:, lo + c * LANES: lo + (c + 1) * LANES]
            outs.append((_rope128(xc, cos, sin, first_half) * scale).astype(BF16))
        return outs

    o1 = SWA_WIDTH
    o2 = o1 + SWA_KV_WIDTH
    o3 = o2 + SWA_KV_WIDTH
    o4 = o3 + DIFF_QK_WIDTH
    o5 = o4 + DIFF_QK_WIDTH
    for c, val in enumerate(roped(0, SWA_WIDTH, SCALE)):
        q_ref[:, c * LANES:(c + 1) * LANES] = val
    for c, val in enumerate(roped(o1, SWA_KV_WIDTH, 1.0)):
        k_ref[:, c * LANES:(c + 1) * LANES] = val
    for c in range(TM // SEQ_T):
        v_ref[c] = p[c * SEQ_T:(c + 1) * SEQ_T, o2:o3].T.astype(BF16)
        dv_ref[c] = p[c * SEQ_T:(c + 1) * SEQ_T, o5:].T.astype(BF16)
    for c, val in enumerate(roped(o3, DIFF_QK_WIDTH, SCALE)):
        dq_ref[:, c * LANES:(c + 1) * LANES] = val
    for c, val in enumerate(roped(o4, DIFF_QK_WIDTH, 1.0)):
        dk_ref[:, c * LANES:(c + 1) * LANES] = val


def _pre_attn(x, mods_l, norm_w, w_in, cos_t, sin_t, dims):
    B, S, L = dims
    T = x.shape[0]
    n_lat = B * S // TM
    s_tiles = S // TM

    def mod_idx(g):
        return (jnp.where(g < n_lat, g // s_tiles, B), 0, 0)

    def rope_idx(g):
        return (jnp.where(g < n_lat, g % s_tiles, s_tiles), 0)

    widths = (SWA_WIDTH, SWA_KV_WIDTH, DIFF_QK_WIDTH, DIFF_QK_WIDTH)
    t_widths = (SWA_KV_WIDTH, DIFF_WIDTH)
    return pl.pallas_call(
        _pre_attn_kernel,
        grid=(T // TM,),
        in_specs=[
            pl.BlockSpec((TM, D_MODEL), lambda g: (g, 0)),
            pl.BlockSpec((None, N_MOD, D_MODEL), mod_idx),
            pl.BlockSpec((1, D_MODEL), lambda g: (0, 0)),
            pl.BlockSpec(w_in.shape, lambda g: (0, 0)),
            pl.BlockSpec((TM, LANES), rope_idx),
            pl.BlockSpec((TM, LANES), rope_idx),
        ],
        out_specs=[pl.BlockSpec((TM, w), lambda g: (g, 0)) for w in widths]
        + [pl.BlockSpec((TM // SEQ_T, w, SEQ_T), lambda g: (g, 0, 0)) for w in t_widths],
        out_shape=[jax.ShapeDtypeStruct((T, w), BF16) for w in widths]
        + [jax.ShapeDtypeStruct((T // SEQ_T, w, SEQ_T), BF16) for w in t_widths],
        compiler_params=_cparams(("parallel",)),
        name="pre_attn",
    )(x, mods_l, norm_w, w_in, cos_t, sin_t)


def _swa_kernel(q_ref, kl_ref, vtl_ref, kc_ref, vtc_ref, sink_ref, o_ref, *, n_lat_blocks, n_lat_chunks,
                n_ctx_chunks):
    j = pl.program_id(1)
    is_ctx = j >= n_lat_blocks
    jl = jnp.minimum(j, n_lat_blocks - 1)
    band_chunks = 2
    per_chunk = SEQ_T // BLOCK
    c0 = jnp.clip((jl - 1) // per_chunk, 0, n_lat_chunks - band_chunks)
    r0 = pl.multiple_of(c0 * SEQ_T, SEQ_T)
    kb = kl_ref[pl.ds(r0, band_chunks * SEQ_T), :]
    kc = kc_ref[...]
    glanes = SWA_GROUP * BLOCK
    kpos = r0 + lax.broadcasted_iota(jnp.int32, (band_chunks * SEQ_T, 1), 0)
    qpos = jl * BLOCK + lax.broadcasted_iota(jnp.int32, (1, glanes), 1) % BLOCK
    valid = jnp.logical_and(jnp.abs(qpos - kpos) <= WINDOW, jnp.logical_not(is_ctx))
    q = q_ref[...]
    for hk in range(SWA_KV_HEADS):
        ks = slice(hk * HEAD_DIM, (hk + 1) * HEAD_DIM)
        hs = [hk * SWA_GROUP + g for g in range(SWA_GROUP)]
        qg = jnp.concatenate([q[:, h * HEAD_DIM:(h + 1) * HEAD_DIM] for h in hs], axis=0)
        sink = jnp.concatenate([jnp.full((1, BLOCK), sink_ref[h], F32) for h in hs], axis=1)
        sl = jnp.where(valid, _nt_dot(kb[:, ks], qg), NEG)
        sc = _nt_dot(kc[:, ks], qg)
        m = jnp.maximum(jnp.maximum(jnp.max(sl, axis=0, keepdims=True), jnp.max(sc, axis=0, keepdims=True)), sink)
        pl_ = jnp.exp(sl - m)
        pc = jnp.exp(sc - m)
        den = jnp.sum(pl_, axis=0, keepdims=True) + jnp.sum(pc, axis=0, keepdims=True) + jnp.exp(sink - m)
        pl_b = pl_.astype(BF16)
        pc_b = pc.astype(BF16)
        o = jnp.zeros((HEAD_DIM, glanes), F32)
        for i in range(band_chunks):
            o = o + jnp.dot(vtl_ref[c0 + i, ks, :], pl_b[i * SEQ_T:(i + 1) * SEQ_T], preferred_element_type=F32)
        for i in range(n_ctx_chunks):
            o = o + jnp.dot(vtc_ref[i, ks, :], pc_b[i * SEQ_T:(i + 1) * SEQ_T], preferred_element_type=F32)
        o = o / den
        for g, h in enumerate(hs):
            o_ref[:, h * HEAD_DIM:(h + 1) * HEAD_DIM] = o[:, g * BLOCK:(g + 1) * BLOCK].T.astype(BF16)


def _swa(q, k, vt, sink, dims):
    B, S, L = dims
    T = q.shape[0]
    n_lat_blocks = S // BLOCK
    n_ctx_blocks = L // BLOCK
    n_lat_chunks = S // SEQ_T
    n_ctx_chunks = L // SEQ_T
    lat_rows = B * S

    def q_idx(b, j):
        return (jnp.where(j < n_lat_blocks, b * n_lat_blocks + j,
                          lat_rows // BLOCK + b * n_ctx_blocks + (j - n_lat_blocks)), 0)

    return pl.pallas_call(
        functools.partial(_swa_kernel, n_lat_blocks=n_lat_blocks, n_lat_chunks=n_lat_chunks,
                          n_ctx_chunks=n_ctx_chunks),
        grid=(B, n_lat_blocks + n_ctx_blocks),
        in_specs=[
            pl.BlockSpec((BLOCK, SWA_WIDTH), q_idx),
            pl.BlockSpec((S, SWA_KV_WIDTH), lambda b, j: (b, 0)),
            pl.BlockSpec((n_lat_chunks, SWA_KV_WIDTH, SEQ_T), lambda b, j: (b, 0, 0)),
            pl.BlockSpec((L, SWA_KV_WIDTH), lambda b, j: (lat_rows // L + b, 0)),
            pl.BlockSpec((n_ctx_chunks, SWA_KV_WIDTH, SEQ_T), lambda b, j: (lat_rows // L + b, 0, 0)),
            pl.BlockSpec(memory_space=pltpu.SMEM),
        ],
        out_specs=pl.BlockSpec((BLOCK, SWA_WIDTH), q_idx),
        out_shape=jax.ShapeDtypeStruct((T, SWA_WIDTH), BF16),
        compiler_params=_cparams(("parallel", "arbitrary")),
        name="swa",
    )(q, k, vt, k, vt, sink)


def _diff_kernel(q_ref, kl_ref, vtl_ref, kc_ref, vtc_ref, lam_ref, subln_ref, o_ref, qp_scr, m_scr, acc_scr, *,
                 n_lat_tiles, n_lat_chunks, n_ctx_chunks, lam_init):
    j = pl.program_id(1)
    lv = lam_ref[...]
    lam = (jnp.exp(jnp.sum(lv[0:1] * lv[1:2], axis=-1, keepdims=True))
           - jnp.exp(jnp.sum(lv[2:3] * lv[3:4], axis=-1, keepdims=True)) + lam_init)
    lane = lax.broadcasted_iota(jnp.int32, (1, DIFF_V_DIM), 1)
    heads = [slice(h * DIFF_V_DIM, (h + 1) * DIFF_V_DIM) for h in range(DIFF_HEADS)]
    ones = jnp.ones((DIFF_SUM_ROWS, SEQ_T), BF16)

    for h, vs in enumerate(heads):
        qh = q_ref[:, vs]
        zero = jnp.zeros_like(qh)
        qp_scr[2 * h] = jnp.where(lane < HEAD_DIM, qh, zero)
        qp_scr[2 * h + 1] = jnp.where(lane >= HEAD_DIM, qh, zero)
    m_scr[...] = jnp.full(m_scr.shape, -jnp.inf, F32)
    acc_scr[...] = jnp.zeros(acc_scr.shape, F32)

    def update(blocks_of_head):
        all_blocks = [blocks_of_head(vs) for vs in heads]

        def scores(ch):
            return [_nt_dot(kblk, qp_scr[ch]) for kblk, _ in all_blocks[ch // 2]]

        n_chains = 2 * DIFF_HEADS
        queue = [scores(ch) for ch in range(DIFF_AHEAD)]
        for h, vs in enumerate(heads):
            blocks = all_blocks[h]
            for m in range(2):
                ch = 2 * h + m
                ss = queue.pop(0)
                if ch + DIFF_AHEAD < n_chains:
                    queue.append(scores(ch + DIFF_AHEAD))
                m_old = m_scr[ch]
                m_blk = jnp.max(ss[0], axis=0, keepdims=True)
                for s in ss[1:]:
                    m_blk = jnp.maximum(m_blk, jnp.max(s, axis=0, keepdims=True))
                m_new = jnp.maximum(m_old, m_blk.astype(BF16).astype(F32))
                m_b = m_new.astype(BF16)
                alpha = jnp.exp(m_old - m_new)
                acc = alpha * acc_scr[ch]
                for s, (_, vtblk) in zip(ss, blocks):
                    p = jnp.exp(s.astype(BF16) - m_b)
                    acc = acc + jnp.dot(jnp.concatenate([vtblk, ones], axis=0), p, preferred_element_type=F32)
                m_scr[ch] = m_new
                acc_scr[ch] = acc

    update(lambda vs: [(kc_ref[c * SEQ_T:(c + 1) * SEQ_T, vs], vtc_ref[c, vs, :]) for c in range(n_ctx_chunks)])

    @pl.when(j < n_lat_tiles)
    def _():
        def body(g, _):
            def blocks_of_head(vs):
                blocks = []
                for i in range(DIFF_GROUP):
                    c = g * DIFF_GROUP + i
                    r0 = pl.multiple_of(c * SEQ_T, SEQ_T)
                    blocks.append((kl_ref[pl.ds(r0, SEQ_T), vs], vtl_ref[c, vs, :]))
                return blocks

            update(blocks_of_head)
            return 0

        lax.fori_loop(0, n_lat_chunks // DIFF_GROUP, body, 0)

    def normalised(ch):
        acc = acc_scr[ch]
        return acc[:DIFF_V_DIM] / acc[DIFF_V_DIM:DIFF_V_DIM + 1]

    for h, vs in enumerate(heads):
        o = (normalised(2 * h) - lam * normalised(2 * h + 1)).T
        y = o * lax.rsqrt(jnp.mean(o * o, axis=-1, keepdims=True) + EPS) * subln_ref[...]
        o_ref[:, vs] = (y * (1.0 - lam_init)).astype(BF16)


def _diff_attn(dq, dk, dvt, lam_vecs, subln, lam_init, dims):
    B, S, L = dims
    T = dq.shape[0]
    n_lat_tiles = S // SEQ_T
    n_ctx_tiles = L // SEQ_T
    lat_rows = B * S

    def q_idx(b, j):
        return (jnp.where(j < n_lat_tiles, b * n_lat_tiles + j,
                          lat_rows // SEQ_T + b * n_ctx_tiles + (j - n_lat_tiles)), 0)

    return pl.pallas_call(
        functools.partial(_diff_kernel, n_lat_tiles=n_lat_tiles, n_lat_chunks=n_lat_tiles,
                          n_ctx_chunks=n_ctx_tiles, lam_init=lam_init),
        grid=(B, n_lat_tiles + n_ctx_tiles),
        in_specs=[
            pl.BlockSpec((SEQ_T, DIFF_QK_WIDTH), q_idx),
            pl.BlockSpec((S, DIFF_QK_WIDTH), lambda b, j: (b, 0)),
            pl.BlockSpec((n_lat_tiles, DIFF_WIDTH, SEQ_T), lambda b, j: (b, 0, 0)),
            pl.BlockSpec((L, DIFF_QK_WIDTH), lambda b, j: (lat_rows // L + b, 0)),
            pl.BlockSpec((n_ctx_tiles, DIFF_WIDTH, SEQ_T), lambda b, j: (lat_rows // L + b, 0, 0)),
            pl.BlockSpec(lam_vecs.shape, lambda b, j: (0, 0)),
            pl.BlockSpec((1, DIFF_V_DIM), lambda b, j: (0, 0)),
        ],
        out_specs=pl.BlockSpec((SEQ_T, DIFF_WIDTH), q_idx),
        out_shape=jax.ShapeDtypeStruct((T, DIFF_WIDTH), BF16),
        scratch_shapes=[
            pltpu.VMEM((2 * DIFF_HEADS, SEQ_T, DIFF_V_DIM), BF16),
            pltpu.VMEM((2 * DIFF_HEADS, 1, SEQ_T), F32),
            pltpu.VMEM((2 * DIFF_HEADS, DIFF_V_DIM + DIFF_SUM_ROWS, SEQ_T), F32),
        ],
        compiler_params=_cparams(("parallel", "arbitrary")),
        name="diff_attn",
    )(dq, dk, dvt, dk, dvt, lam_vecs, subln)


def _pre_rec_kernel(x_ref, mod_ref, nw_ref, w_ref, xp_ref, xr_ref, g_ref):
    h = _rmsnorm_mod(x_ref[...], nw_ref[...], mod_ref[0:1, :], mod_ref[1:2, :])
    p = jnp.dot(h.astype(BF16), w_ref[...], preferred_element_type=F32)
    xp_ref[...] = p[:, :POOL_WIDTH]
    xr_ref[...] = p[:, POOL_WIDTH:POOL_WIDTH + LRU_WIDTH]
    g_ref[...] = p[:, POOL_WIDTH + LRU_WIDTH:]


def _pre_rec(x, mods_l, norm_w, w_in, dims):
    B, S, L = dims
    T = x.shape[0]
    n_lat = B * S // TM
    s_tiles = S // TM

    def mod_idx(g):
        return (jnp.where(g < n_lat, g // s_tiles, B), 0, 0)

    widths = (POOL_WIDTH, LRU_WIDTH, LRU_WIDTH)
    return pl.pallas_call(
        _pre_rec_kernel,
        grid=(T // TM,),
        in_specs=[
            pl.BlockSpec((TM, D_MODEL), lambda g: (g, 0)),
            pl.BlockSpec((None, N_MOD, D_MODEL), mod_idx),
            pl.BlockSpec((1, D_MODEL), lambda g: (0, 0)),
            pl.BlockSpec(w_in.shape, lambda g: (0, 0)),
        ],
        out_specs=[pl.BlockSpec((TM, w), lambda g: (g, 0)) for w in widths],
        out_shape=[jax.ShapeDtypeStruct((T, w), F32) for w in widths],
        compiler_params=_cparams(("parallel",)),
        name="pre_rec",
    )(x, mods_l, norm_w, w_in)


def _rec_mid_kernel(xp_p, xp_c, xp_n, xr_p, xr_c, xr_n, cw_ref, cb_ref, wa_ref, ba_ref, wx_ref, bx_ref,
                    lam_ref, pw_ref, ps_ref, pool_ref, a_ref, b_ref, *, n_lat_chunks, lat_chunks_per_seq,
                    ctx_chunks_per_seq, lat_len, ctx_len):
    g = pl.program_id(0)
    is_lat = g < n_lat_chunks
    cps = jnp.where(is_lat, lat_chunks_per_seq, ctx_chunks_per_seq)
    within = jnp.where(is_lat, g % lat_chunks_per_seq, (g - n_lat_chunks) % ctx_chunks_per_seq)
    has_prev = within > 0
    has_next = within < cps - 1
    seg_len = jnp.where(is_lat, lat_len, ctx_len)
    ext_rows = SEQ_T + 2 * HALO

    def extended(prev_ref, cur_ref, next_ref):
        prev = jnp.where(has_prev, prev_ref[SEQ_T - HALO:, :], 0.0)
        nxt = jnp.where(has_next, next_ref[:HALO, :], 0.0)
        return jnp.concatenate([prev, cur_ref[...], nxt], axis=0)

    def shifted(ext, off):
        return pltpu.roll(ext, (-off) % ext_rows, 0)[HALO:HALO + SEQ_T, :]

    xp_ext = extended(xp_p, xp_c, xp_n)
    t = within * SEQ_T + lax.broadcasted_iota(jnp.int32, (SEQ_T, 1), 0)
    for gi, w in enumerate(POOL_WINDOWS):
        cols = slice(gi * POOL_GDIM, (gi + 1) * POOL_GDIM)
        eg = xp_ext[:, cols]
        tot = shifted(eg, -(w // 2))
        for off in range(-(w // 2) + 1, w - w // 2):
            tot = tot + shifted(eg, off)
        lo = jnp.clip(t - w // 2, 0, seg_len)
        hi = jnp.clip(t - w // 2 + w, 0, seg_len)
        cnt = (hi - lo).astype(F32)
        d = tot / cnt - xp_c[:, cols]
        y = jnp.dot(d.astype(BF16), pw_ref[gi], preferred_element_type=F32)
        pool_ref[:, cols] = (y * ps_ref[:, cols]).astype(BF16)

    xr_ext = extended(xr_p, xr_c, xr_n)
    u = cb_ref[...] + cw_ref[0:1, :] * shifted(xr_ext, -CONV_LEFT)
    for k in range(1, CONV_W):
        u = u + cw_ref[k:k + 1, :] * shifted(xr_ext, k - CONV_LEFT)
    ub = u.astype(BF16)
    for d in range(2):
        r = jax.nn.sigmoid(jnp.dot(ub, wa_ref[d], preferred_element_type=F32) + ba_ref[d:d + 1, :])
        i = jax.nn.sigmoid(jnp.dot(ub, wx_ref[d], preferred_element_type=F32) + bx_ref[d:d + 1, :])
        nl = -lam_ref[d:d + 1, :]
        softplus = jnp.maximum(nl, 0.0) + jnp.log1p(jnp.exp(-jnp.abs(nl)))
        log_a = -LRU_C * r * softplus
        a_ref[d] = jnp.exp(log_a)
        th = jnp.tanh(log_a)
        b_ref[d] = jnp.sqrt(-2.0 * th / (1.0 - th)) * (i * u)


def _rec_mid(xp, xr, conv_w, conv_b, wa_bd, ba, wx_bd, bx, lam, pool_w, pool_scale, dims):
    B, S, L = dims
    T = xp.shape[0]
    n_chunks = T // SEQ_T

    def cur(g):
        return (g, 0)

    def prv(g):
        return (jnp.maximum(g - 1, 0), 0)

    def nxt(g):
        return (jnp.minimum(g + 1, n_chunks - 1), 0)

    tile = lambda idx: pl.BlockSpec((SEQ_T, LRU_WIDTH), idx)
    full = lambda arr: pl.BlockSpec(arr.shape, lambda g: (0,) * arr.ndim)
    kern = functools.partial(_rec_mid_kernel, n_lat_chunks=B * S // SEQ_T, lat_chunks_per_seq=S // SEQ_T,
                             ctx_chunks_per_seq=L // SEQ_T, lat_len=S, ctx_len=L)
    return pl.pallas_call(
        kern,
        grid=(n_chunks,),
        in_specs=[tile(prv), tile(cur), tile(nxt), tile(prv), tile(cur), tile(nxt),
                  full(conv_w), full(conv_b), full(wa_bd), full(ba), full(wx_bd), full(bx), full(lam),
                  full(pool_w), full(pool_scale)],
        out_specs=[pl.BlockSpec((SEQ_T, POOL_WIDTH), cur),
                   pl.BlockSpec((2, SEQ_T, LRU_WIDTH), lambda g: (0, g, 0)),
                   pl.BlockSpec((2, SEQ_T, LRU_WIDTH), lambda g: (0, g, 0))],
        out_shape=[jax.ShapeDtypeStruct((T, POOL_WIDTH), BF16),
                   jax.ShapeDtypeStruct((2, T, LRU_WIDTH), F32),
                   jax.ShapeDtypeStruct((2, T, LRU_WIDTH), F32)],
        compiler_params=_cparams(("parallel",)),
        name="rec_mid",
    )(xp, xp, xp, xr, xr, xr, conv_w, conv_b, wa_bd, ba, wx_bd, bx, lam, pool_w, pool_scale)


def _scan_kernel(af_ref, bf_ref, ar_ref, br_ref, hf_ref, hr_ref, carry_ref):
    s = pl.program_id(1)

    @pl.when(s == 0)
    def _():
        carry_ref[...] = jnp.zeros_like(carry_ref)

    def body(i, carry):
        hf, hr = carry
        t = SEQ_T - 1 - i
        hf = af_ref[pl.ds(i, 1), :] * hf + bf_ref[pl.ds(i, 1), :]
        hr = ar_ref[pl.ds(t, 1), :] * hr + br_ref[pl.ds(t, 1), :]
        hf_ref[pl.ds(i, 1), :] = hf
        hr_ref[pl.ds(t, 1), :] = hr
        return hf, hr

    hf, hr = lax.fori_loop(0, SEQ_T, body, (carry_ref[0], carry_ref[1]), unroll=8)
    carry_ref[0] = hf
    carry_ref[1] = hr


def _scan(a, b, dims):
    B, S, L = dims
    T = a.shape[1]
    lat = S // SEQ_T
    ctx = L // SEQ_T
    lat_base = 0
    ctx_base = B * S // SEQ_T

    def fwd(bi, s):
        return jnp.where(s < ctx, ctx_base + bi * ctx + s, lat_base + bi * lat + (s - ctx))

    def rev(bi, s):
        return jnp.where(s < ctx, ctx_base + bi * ctx + (ctx - 1 - s), lat_base + bi * lat + (lat - 1 - (s - ctx)))

    in_f = pl.BlockSpec((None, SEQ_T, LRU_WIDTH), lambda bi, s: (0, fwd(bi, s), 0))
    in_r = pl.BlockSpec((None, SEQ_T, LRU_WIDTH), lambda bi, s: (1, rev(bi, s), 0))
    out_f = pl.BlockSpec((SEQ_T, LRU_WIDTH), lambda bi, s: (fwd(bi, s), 0))
    out_r = pl.BlockSpec((SEQ_T, LRU_WIDTH), lambda bi, s: (rev(bi, s), 0))
    out = jax.ShapeDtypeStruct((T, LRU_WIDTH), F32)
    return pl.pallas_call(
        _scan_kernel,
        grid=(B, lat + ctx),
        in_specs=[in_f, in_f, in_r, in_r],
        out_specs=[out_f, out_r],
        out_shape=[out, out],
        scratch_shapes=[pltpu.VMEM((2, 1, LRU_WIDTH), F32)],
        compiler_params=_cparams(("parallel", "arbitrary")),
        name="lru_scan",
    )(a, b, a, b)


def _post_common(y, x_ref, mod_ref, nw_ref, xo_ref, h2_ref):
    x_new = x_ref[...] + mod_ref[2:3, :] * y
    xo_ref[...] = x_new
    h2_ref[...] = _rmsnorm_mod(x_new, nw_ref[...], mod_ref[3:4, :], mod_ref[4:5, :]).astype(BF16)


def _post_attn_kernel(a_ref, b_ref, x_ref, mod_ref, nw_ref, w_ref, xo_ref, h2_ref):
    half = a_ref.shape[1]
    y = (jnp.dot(a_ref[...], w_ref[:half, :], preferred_element_type=F32)
         + jnp.dot(b_ref[...], w_ref[half:, :], preferred_element_type=F32))
    _post_common(y, x_ref, mod_ref, nw_ref, xo_ref, h2_ref)


def _post_rec_kernel(pool_ref, hf_ref, hr_ref, g_ref, x_ref, mod_ref, nw_ref, w_ref, xo_ref, h2_ref):
    half = pool_ref.shape[1]
    rec = ((hf_ref[...] + hr_ref[...]) * _gelu(g_ref[...])).astype(BF16)
    y = (jnp.dot(pool_ref[...], w_ref[:half, :], preferred_element_type=F32)
         + jnp.dot(rec, w_ref[half:, :], preferred_element_type=F32))
    _post_common(y, x_ref, mod_ref, nw_ref, xo_ref, h2_ref)


def _post_mixer(kind, parts, x, mods_l, norm_w, w_out, dims, n_tiles):
    B, S, L = dims
    T = x.shape[0]
    n_lat = B * S // TM
    s_tiles = S // TM

    def mod_idx(g):
        return (jnp.where(g < n_lat, g // s_tiles, B), 0, 0)

    row = lambda w: pl.BlockSpec((TM, w), lambda g: (g, 0))
    if kind == "attn":
        kern = _post_attn_kernel
        part_specs = [row(SWA_WIDTH), row(DIFF_WIDTH)]
    else:
        kern = _post_rec_kernel
        part_specs = [row(POOL_WIDTH), row(LRU_WIDTH), row(LRU_WIDTH), row(LRU_WIDTH)]
    n_in = len(parts)
    return pl.pallas_call(
        kern,
        grid=(n_tiles,),
        in_specs=part_specs + [
            row(D_MODEL),
            pl.BlockSpec((None, N_MOD, D_MODEL), mod_idx),
            pl.BlockSpec((1, D_MODEL), lambda g: (0, 0)),
            pl.BlockSpec(w_out.shape, lambda g: (0, 0)),
        ],
        out_specs=[row(D_MODEL), row(D_MODEL)],
        out_shape=[jax.ShapeDtypeStruct((T, D_MODEL), F32), jax.ShapeDtypeStruct((T, D_MODEL), BF16)],
        input_output_aliases={n_in: 0},
        compiler_params=_cparams(("parallel",)),
        name="post_" + kind,
    )(*parts, x, mods_l, norm_w, w_out)


def _peer_cand_blocks():
    return [(i, PEER_TOPK // (i + 1)) for i in range(1, SUBLANES)]


PEER_TAG_BITS = 31


def _peer_tagged(e, rank):
    bits = pltpu.bitcast(e, jnp.int32)
    tag = PEER_TAG_BITS - rank.astype(jnp.int32)
    return pltpu.bitcast((bits & ~PEER_TAG_BITS) | tag, F32)


def _peer_routing(h2_ref, wq_ref, sk_ref, thr_ref, c_ref, e2_ref,
                  q_scr, s_scr, cur_scr, rank_scr, top_scr, cand_scr):
    neg_inf = -jnp.inf
    n_cand = cand_scr.shape[0]

    def extract_step(k, src_ref, dst_ref, rank_ref=None):
        cur = src_ref[...]
        m = jnp.max(cur, axis=0, keepdims=True)
        dst_ref[pl.ds(k, 1), :] = m
        hit = cur == m
        src_ref[...] = jnp.where(hit, neg_inf, cur)
        if rank_ref is not None:
            rank_ref[...] = jnp.where(hit, lax.convert_element_type(k + 1, F32), rank_ref[...])

    def head_body(h, _):
        q0 = pl.multiple_of(h * PEER_QDIM, PEER_QDIM)
        q_scr[...] = _nt_dot(wq_ref[pl.ds(q0, PEER_QDIM), :], h2_ref[...])
        for p in range(2):
            s = jnp.dot(sk_ref[h * 2 + p].astype(BF16), q_scr[p * PEER_HALF:(p + 1) * PEER_HALF, :].astype(BF16),
                        preferred_element_type=F32)
            s_scr[p] = s
            cur_scr[p] = s
        rank_scr[...] = jnp.full(rank_scr.shape, PEER_TOPK + 1.0, F32)

        def top_body(k, _):
            extract_step(k, cur_scr.at[0], top_scr.at[0])
            extract_step(k, cur_scr.at[1], top_scr.at[1], rank_scr)
            return 0

        lax.fori_loop(0, PEER_TOPK, top_body, 0)
        v1 = top_scr[0]
        v2 = top_scr[1]
        cand_scr[0:PEER_TOPK, :] = v1[0:1, :] + v2
        row = lax.broadcasted_iota(jnp.int32, (SUBLANES, 1), 0)
        for i, n_i in _peer_cand_blocks():
            blk = jnp.where(row < n_i, v1[i:i + 1, :] + v2[0:SUBLANES, :], neg_inf)
            cand_scr[PEER_TOPK + (i - 1) * SUBLANES: PEER_TOPK + i * SUBLANES, :] = blk
        cand_scr[n_cand - SUBLANES:, :] = v1[SUBLANES:, :] + v2[0:1, :]
        cand = cand_scr[...]

        def cand_body(k, _):
            extract_step(k, cand_scr, top_scr.at[2])
            return 0

        lax.fori_loop(0, PEER_TOPK, cand_body, 0)
        tau = top_scr[2, PEER_TOPK - 1:PEER_TOPK, :]
        top = v1[0:1, :] + v2[0:1, :]
        z = jnp.sum(jnp.where(cand >= tau, jnp.exp(cand - top), 0.0), axis=0, keepdims=True)

        s1 = s_scr[0]
        ranks = (lax.broadcasted_iota(jnp.int32, (PEER_TOPK, 1), 0) + 1).astype(F32)
        top_scr[2] = _peer_tagged(jnp.exp(v2 - v2[0:1, :]), ranks)
        thr_ref[h] = jnp.full(s1.shape, jnp.inf, F32)

        def thr_body(jj, _):
            hit = s1 + top_scr[1, pl.ds(jj, 1), :] >= tau
            thr_ref[h] = jnp.where(hit, top_scr[2, pl.ds(jj, 1), :], thr_ref[h])
            return 0

        lax.fori_loop(0, PEER_TOPK, thr_body, 0)
        c_ref[h] = jnp.exp(s1 - v1[0:1, :]) / z
        e2_ref[h] = _peer_tagged(jnp.exp(s_scr[1] - v2[0:1, :]), rank_scr[...])
        return 0

    lax.fori_loop(0, PEER_HEADS, head_body, 0)


def _peer_kernel(h2_ref, u_ref, vt_ref, wq_ref, sk_ref, x_ref, mod_ref, fw_ref, xo_ref,
                 h2t_scr, coef_scr, acc_scr, thr_scr, c_scr, e2_scr,
                 q_scr, s_scr, cur_scr, rank_scr, top_scr, cand_scr, *, final):
    j = pl.program_id(1)
    tt = h2_ref.shape[0]
    pair = PEER_UP_GROUP * PEER_NKEYS

    @pl.when(j == 0)
    def _():
        acc_scr[...] = jnp.zeros_like(acc_scr)
        h2t_scr[...] = h2_ref[...].T
        _peer_routing(h2_ref, wq_ref, sk_ref, thr_scr, c_scr, e2_scr,
                      q_scr, s_scr, cur_scr, rank_scr, top_scr, cand_scr)

    for pr in range(PEER_NA // PEER_UP_GROUP):
        act = jnp.dot(u_ref[pr * pair:(pr + 1) * pair, :], h2t_scr[...], preferred_element_type=F32)
        for half in range(PEER_UP_GROUP):
            al = PEER_UP_GROUP * pr + half
            a = j * PEER_NA + al
            thr_full = [thr_scr[h, pl.ds(a, 1), :] for h in range(PEER_HEADS)]
            c_full = [c_scr[h, pl.ds(a, 1), :] for h in range(PEER_HEADS)]
            for tc in range(tt // LANES):
                ls = slice(tc * LANES, (tc + 1) * LANES)
                thr_rows = [r[:, ls] for r in thr_full]
                c_rows = [r[:, ls] for r in c_full]
                for rb in range(PEER_NKEYS // PEER_RB):
                    rs = slice(rb * PEER_RB, (rb + 1) * PEER_RB)
                    w = jnp.zeros((PEER_RB, LANES), F32)
                    for h in range(PEER_HEADS):
                        e2_t = e2_scr[h, rs, ls]
                        w = w + jnp.where(e2_t >= thr_rows[h], c_rows[h] * e2_t, 0.0)
                    rows = slice(half * PEER_NKEYS + rb * PEER_RB, half * PEER_NKEYS + (rb + 1) * PEER_RB)
                    r0 = al * PEER_NKEYS + rb * PEER_RB
                    coef_scr[r0:r0 + PEER_RB, ls] = (w * _gelu(act[rows, ls])).astype(BF16)

    acc_scr[...] += jnp.dot(vt_ref[...], coef_scr[...], preferred_element_type=F32)

    @pl.when(j == pl.num_programs(1) - 1)
    def _():
        x_new = x_ref[...] + mod_ref[5:6, :] * acc_scr[...].T
        if final:
            x_new = x_new * lax.rsqrt(jnp.mean(x_new * x_new, axis=-1, keepdims=True) + EPS) * fw_ref[...]
        xo_ref[...] = x_new


def _peer(h2, u, v_t, wq_t, subkeys, x, mods_l, final_w, dims, n_tiles, final):
    B, S, L = dims
    T = x.shape[0]
    tt = PEER_TT
    ne = PEER_NA * PEER_NKEYS
    n_lat = B * S // tt
    s_tiles = S // tt
    n_cand = PEER_TOPK + SUBLANES * SUBLANES

    def mod_idx(i, j):
        return (jnp.where(i < n_lat, i // s_tiles, B), 0, 0)

    key_f32 = pltpu.VMEM((PEER_HEADS, PEER_NKEYS, tt), F32)
    const = pl.Buffered(1)
    return pl.pallas_call(
        functools.partial(_peer_kernel, final=final),
        grid=(n_tiles, PEER_EXPERTS // ne),
        in_specs=[
            pl.BlockSpec((tt, D_MODEL), lambda i, j: (i, 0)),
            pl.BlockSpec((ne, D_MODEL), lambda i, j: (j, 0)),
            pl.BlockSpec((None, D_MODEL, ne), lambda i, j: (j, 0, 0)),
            pl.BlockSpec(wq_t.shape, lambda i, j: (0, 0), pipeline_mode=const),
            pl.BlockSpec(subkeys.shape, lambda i, j: (0, 0, 0), pipeline_mode=const),
            pl.BlockSpec((tt, D_MODEL), lambda i, j: (i, 0)),
            pl.BlockSpec((None, N_MOD, D_MODEL), mod_idx),
            pl.BlockSpec((1, D_MODEL), lambda i, j: (0, 0)),
        ],
        out_specs=pl.BlockSpec((tt, D_MODEL), lambda i, j: (i, 0)),
        out_shape=jax.ShapeDtypeStruct((n_tiles * tt if final else T, D_MODEL), F32),
        scratch_shapes=[
            pltpu.VMEM((D_MODEL, tt), BF16),
            pltpu.VMEM((ne, tt), BF16),
            pltpu.VMEM((D_MODEL, tt), F32),
            key_f32, key_f32, key_f32,
            pltpu.VMEM((PEER_QDIM, tt), F32),
            pltpu.VMEM((2, PEER_NKEYS, tt), F32),
            pltpu.VMEM((2, PEER_NKEYS, tt), F32),
            pltpu.VMEM((PEER_NKEYS, tt), F32),
            pltpu.VMEM((3, PEER_TOPK, tt), F32),
            pltpu.VMEM((n_cand, tt), F32),
        ],
        input_output_aliases={} if final else {5: 0},
        compiler_params=_cparams(("parallel", "arbitrary")),
        name="peer",
    )(h2, u, v_t, wq_t, subkeys, x, mods_l, final_w)


def _lambda_init(layer):
    return 0.8 - 0.6 * math.exp(-0.3 * layer)


def _rope_tables(S):
    rows = S // GRID_W
    row = jnp.repeat(jnp.arange(rows), GRID_W).astype(F32)
    col = jnp.tile(jnp.arange(GRID_W), rows).astype(F32)
    inv = ROPE_THETA ** (-jnp.arange(ROPE_AXIS_FREQS, dtype=F32) / ROPE_AXIS_FREQS)
    ang = jnp.concatenate([row[:, None] * inv, col[:, None] * inv], axis=-1)
    cos, sin = jnp.cos(ang), jnp.sin(ang)
    cos_t = jnp.tile(cos, (1, LANES // ROPE_HALF))
    sin_t = jnp.tile(jnp.concatenate([-sin, sin], axis=-1), (1, LANES // HEAD_DIM))
    cos_t = jnp.concatenate([cos_t, jnp.ones((TM, LANES), F32)], axis=0)
    sin_t = jnp.concatenate([sin_t, jnp.zeros((TM, LANES), F32)], axis=0)
    return cos_t, sin_t


def _block_diag(w):
    nd, nb, bd, _ = w.shape
    eye = jnp.eye(nb, dtype=w.dtype)
    return jnp.einsum("dnij,nm->dnimj", w, eye).reshape(nd, nb * bd, nb * bd)


def kernel(x, c, ctx, c_ctx, w_mod, b_mod, norm_mix, norm_ffn, w_out, attn_w_in, swa_sink, diff_lambda, diff_subln, rec_w_in, pool_w, pool_scale, lru_conv_w, lru_conv_b, lru_wa, lru_ba, lru_wx, lru_bx, lru_lambda, peer_wq, peer_subkeys, peer_u, peer_v, final_norm):
    B, S, D = x.shape
    L = ctx.shape[1]
    depth = w_mod.shape[0]
    dims = (B, S, L)
    assert D == D_MODEL and S % PEER_TT == 0 and (B * L) % PEER_TT == 0 and L % SEQ_T == 0
    assert TM % SEQ_T == 0 and S % TM == 0 and (B * L) % TM == 0
    assert S % GRID_W == 0 and S >= 3 * BLOCK and B + 1 <= SUBLANES and S % (DIFF_GROUP * SEQ_T) == 0
    lat_rows = B * S
    T = lat_rows + B * L

    xs = jnp.concatenate([x.reshape(lat_rows, D), ctx.reshape(B * L, D)], axis=0)
    cc = jnp.zeros((SUBLANES, D), F32).at[:B].set(c).at[B].set(c_ctx)
    mods = _modulation(cc, w_mod, b_mod)
    cos_t, sin_t = _rope_tables(S)

    for l in range(depth):
        jl = l // 2
        ctx_out = l < depth - 1
        n_rows = T if ctx_out else lat_rows
        mods_l = mods[l]
        if l % 2 == 0:
            q, k, dq, dk, vt, dvt = _pre_attn(xs, mods_l, norm_mix[l][None], attn_w_in[jl].astype(BF16),
                                              cos_t, sin_t, dims)
            a = _swa(q, k, vt, swa_sink[jl], dims)
            bd = _diff_attn(dq, dk, dvt, diff_lambda[jl], diff_subln[jl][None], _lambda_init(l), dims)
            xs, h2 = _post_mixer("attn", (a, bd), xs, mods_l, norm_ffn[l][None], w_out[l].astype(BF16),
                                 dims, n_rows // TM)
        else:
            xp, xr, g = _pre_rec(xs, mods_l, norm_mix[l][None], rec_w_in[jl].astype(BF16), dims)
            pool, a_co, b_co = _rec_mid(xp, xr, lru_conv_w[jl], lru_conv_b[jl][None],
                                        _block_diag(lru_wa[jl]).astype(BF16), lru_ba[jl],
                                        _block_diag(lru_wx[jl]).astype(BF16), lru_bx[jl], lru_lambda[jl],
                                        pool_w[jl].astype(BF16), pool_scale[jl][None], dims)
            h_fwd, h_rev = _scan(a_co, b_co, dims)
            xs, h2 = _post_mixer("rec", (pool, h_fwd, h_rev, g), xs, mods_l, norm_ffn[l][None], w_out[l].astype(BF16),
                                 dims, n_rows // TM)
        n_peer = n_rows // PEER_TT
        wq_t = peer_wq[l].astype(BF16).T
        sk = peer_subkeys[l].reshape(PEER_HEADS * 2, PEER_NKEYS, PEER_HALF)
        ne = PEER_NA * PEER_NKEYS
        v_t = peer_v[l].astype(BF16).reshape(PEER_EXPERTS // ne, ne, D).transpose(0, 2, 1)
        xs = _peer(h2, peer_u[l].astype(BF16), v_t, wq_t, sk, xs, mods_l, final_norm[None], dims, n_peer,
                   final=not ctx_out)

    return xs.reshape(B, S, D)
```

```python
import functools
import math

import jax
import jax.numpy as jnp
from jax import lax
from jax.experimental import pallas as pl
from jax.experimental.pallas import tpu as pltpu

F32 = jnp.float32
BF16 = jnp.bfloat16
HIGHEST = lax.Precision.HIGHEST

D_MODEL = 1024
HEAD_DIM = 64
EPS = 1e-6
NEG = -1e30
N_MOD = 6
SCALE = HEAD_DIM ** -0.5
ROPE_HALF = HEAD_DIM // 2
ROPE_AXIS_FREQS = ROPE_HALF // 2
ROPE_THETA = 10000.0
GRID_W = 64
SWA_Q_HEADS = 8
SWA_KV_HEADS = 2
SWA_GROUP = SWA_Q_HEADS // SWA_KV_HEADS
WINDOW = 128
BLOCK = 128
SWA_WIDTH = SWA_Q_HEADS * HEAD_DIM
SWA_KV_WIDTH = SWA_KV_HEADS * HEAD_DIM
DIFF_HEADS = 4
DIFF_V_DIM = 2 * HEAD_DIM
DIFF_QK_WIDTH = DIFF_HEADS * 2 * HEAD_DIM
DIFF_WIDTH = DIFF_HEADS * DIFF_V_DIM
POOL_WINDOWS = (2, 4, 8, 16)
POOL_GROUPS = 4
POOL_WIDTH = D_MODEL // 2
POOL_GDIM = POOL_WIDTH // POOL_GROUPS
LRU_WIDTH = D_MODEL // 2
CONV_W = 4
CONV_LEFT = CONV_W // 2
LRU_C = 8.0
PEER_HEADS = 8
PEER_NKEYS = 128
PEER_EXPERTS = PEER_NKEYS * PEER_NKEYS
PEER_QDIM = 256
PEER_HALF = PEER_QDIM // 2
PEER_TOPK = 16

LANES = 128
SUBLANES = 8
VMEM_LIMIT = 56 * 1024 * 1024

TM = 1024
SEQ_T = 256
DIFF_GROUP = 32
DIFF_AHEAD = 2
DIFF_SUM_ROWS = 16
HALO = 8
PEER_TT = 512
PEER_NA = 16
PEER_UP_GROUP = 2
PEER_RB = 64


def _cparams(sem):
    return pltpu.CompilerParams(dimension_semantics=sem, vmem_limit_bytes=VMEM_LIMIT)


def _nt_dot(a, b):
    return lax.dot_general(a, b, (((1,), (1,)), ((), ())), preferred_element_type=F32)


def _rmsnorm_mod(x, w, shift, scale):
    y = x * lax.rsqrt(jnp.mean(x * x, axis=-1, keepdims=True) + EPS) * w
    return y * (1.0 + scale) + shift


def _gelu(x):
    return 0.5 * x * (1.0 + lax.erf(x * (2.0 ** -0.5)))


def _mod_kernel(cc_ref, w_ref, b_ref, o_ref):
    cc = cc_ref[...]
    sc = cc * jax.nn.sigmoid(cc)
    o_ref[...] = jnp.dot(sc, w_ref[...], precision=HIGHEST, preferred_element_type=F32) + b_ref[...]


def _modulation(cc, w_mod, b_mod):
    depth = w_mod.shape[0]
    rows = cc.shape[0]
    out = pl.pallas_call(
        _mod_kernel,
        grid=(depth, N_MOD),
        in_specs=[
            pl.BlockSpec((rows, D_MODEL), lambda l, j: (0, 0)),
            pl.BlockSpec((None, D_MODEL, D_MODEL), lambda l, j: (l, 0, j)),
            pl.BlockSpec((None, 1, D_MODEL), lambda l, j: (l, 0, j)),
        ],
        out_specs=pl.BlockSpec((None, rows, D_MODEL), lambda l, j: (l, 0, j)),
        out_shape=jax.ShapeDtypeStruct((depth, rows, N_MOD * D_MODEL), F32),
        compiler_params=_cparams(("parallel", "parallel")),
        name="modulation",
    )(cc, w_mod, b_mod.reshape(depth, 1, N_MOD * D_MODEL))
    return out.reshape(depth, rows, N_MOD, D_MODEL)


def _rope128(x, cos, sin_signed, first_half):
    partner = jnp.where(first_half, pltpu.roll(x, LANES - ROPE_HALF, 1), pltpu.roll(x, ROPE_HALF, 1))
    return x * cos + partner * sin_signed


def _pre_attn_kernel(x_ref, mod_ref, nw_ref, w_ref, cos_ref, sin_ref,
                     q_ref, k_ref, dq_ref, dk_ref, v_ref, dv_ref):
    h = _rmsnorm_mod(x_ref[...], nw_ref[...], mod_ref[0:1, :], mod_ref[1:2, :])
    p = jnp.dot(h.astype(BF16), w_ref[...], preferred_element_type=F32)
    cos = cos_ref[...]
    sin = sin_ref[...]
    lane = lax.broadcasted_iota(jnp.int32, (1, LANES), 1)
    first_half = (lane % HEAD_DIM) < ROPE_HALF

    def roped(lo, width, scale):
        outs = []
        for c in range(width // LANES):
            xc = p[:, lo + c * LANES: lo + (c + 1) * LANES]
            outs.append((_rope128(xc, cos, sin, first_half) * scale).astype(BF16))
        return outs

    o1 = SWA_WIDTH
    o2 = o1 + SWA_KV_WIDTH
    o3 = o2 + SWA_KV_WIDTH
    o4 = o3 + DIFF_QK_WIDTH
    o5 = o4 + DIFF_QK_WIDTH
    for c, val in enumerate(roped(0, SWA_WIDTH, SCALE)):
        q_ref[:, c * LANES:(c + 1) * LANES] = val
    for c, val in enumerate(roped(o1, SWA_KV_WIDTH, 1.0)):
        k_ref[:, c * LANES:(c + 1) * LANES] = val
    for c in range(TM // SEQ_T):
        v_ref[c] = p[c * SEQ_T:(c + 1) * SEQ_T, o2:o3].T.astype(BF16)
        dv_ref[c] = p[c * SEQ_T:(c + 1) * SEQ_T, o5:].T.astype(BF16)
    for c, val in enumerate(roped(o3, DIFF_QK_WIDTH, SCALE)):
        dq_ref[:, c * LANES:(c + 1) * LANES] = val
    for c, val in enumerate(roped(o4, DIFF_QK_WIDTH, 1.0)):
        dk_ref[:, c * LANES:(c + 1) * LANES] = val


def _pre_attn(x, mods_l, norm_w, w_in, cos_t, sin_t, dims):
    B, S, L = dims
    T = x.shape[0]
    n_lat = B * S // TM
    s_tiles = S // TM

    def mod_idx(g):
        return (jnp.where(g < n_lat, g // s_tiles, B), 0, 0)

    def rope_idx(g):
        return (jnp.where(g < n_lat, g % s_tiles, s_tiles), 0)

    widths = (SWA_WIDTH, SWA_KV_WIDTH, DIFF_QK_WIDTH, DIFF_QK_WIDTH)
    t_widths = (SWA_KV_WIDTH, DIFF_WIDTH)
    return pl.pallas_call(
        _pre_attn_kernel,
        grid=(T // TM,),
        in_specs=[
            pl.BlockSpec((TM, D_MODEL), lambda g: (g, 0)),
            pl.BlockSpec((None, N_MOD, D_MODEL), mod_idx),
            pl.BlockSpec((1, D_MODEL), lambda g: (0, 0)),
            pl.BlockSpec(w_in.shape, lambda g: (0, 0)),
            pl.BlockSpec((TM, LANES), rope_idx),
            pl.BlockSpec((TM, LANES), rope_idx),
        ],
        out_specs=[pl.BlockSpec((TM, w), lambda g: (g, 0)) for w in widths]
        + [pl.BlockSpec((TM // SEQ_T, w, SEQ_T), lambda g: (g, 0, 0)) for w in t_widths],
        out_shape=[jax.ShapeDtypeStruct((T, w), BF16) for w in widths]
        + [jax.ShapeDtypeStruct((T // SEQ_T, w, SEQ_T), BF16) for w in t_widths],
        compiler_params=_cparams(("parallel",)),
        name="pre_attn",
    )(x, mods_l, norm_w, w_in, cos_t, sin_t)


def _swa_kernel(q_ref, kl_ref, vtl_ref, kc_ref, vtc_ref, sink_ref, o_ref, *, n_lat_blocks, n_lat_chunks,
                n_ctx_chunks):
    j = pl.program_id(1)
    is_ctx = j >= n_lat_blocks
    jl = jnp.minimum(j, n_lat_blocks - 1)
    band_chunks = 2
    per_chunk = SEQ_T // BLOCK
    c0 = jnp.clip((jl - 1) // per_chunk, 0, n_lat_chunks - band_chunks)
    r0 = pl.multiple_of(c0 * SEQ_T, SEQ_T)
    kb = kl_ref[pl.ds(r0, band_chunks * SEQ_T), :]
    kc = kc_ref[...]
    glanes = SWA_GROUP * BLOCK
    kpos = r0 + lax.broadcasted_iota(jnp.int32, (band_chunks * SEQ_T, 1), 0)
    qpos = jl * BLOCK + lax.broadcasted_iota(jnp.int32, (1, glanes), 1) % BLOCK
    valid = jnp.logical_and(jnp.abs(qpos - kpos) <= WINDOW, jnp.logical_not(is_ctx))
    q = q_ref[...]
    for hk in range(SWA_KV_HEADS):
        ks = slice(hk * HEAD_DIM, (hk + 1) * HEAD_DIM)
        hs = [hk * SWA_GROUP + g for g in range(SWA_GROUP)]
        qg = jnp.concatenate([q[:, h * HEAD_DIM:(h + 1) * HEAD_DIM] for h in hs], axis=0)
        sink = jnp.concatenate([jnp.full((1, BLOCK), sink_ref[h], F32) for h in hs], axis=1)
        sl = jnp.where(valid, _nt_dot(kb[:, ks], qg), NEG)
        sc = _nt_dot(kc[:, ks], qg)
        m = jnp.maximum(jnp.maximum(jnp.max(sl, axis=0, keepdims=True), jnp.max(sc, axis=0, keepdims=True)), sink)
        pl_ = jnp.exp(sl - m)
        pc = jnp.exp(sc - m)
        den = jnp.sum(pl_, axis=0, keepdims=True) + jnp.sum(pc, axis=0, keepdims=True) + jnp.exp(sink - m)
        pl_b = pl_.astype(BF16)
        pc_b = pc.astype(BF16)
        o = jnp.zeros((HEAD_DIM, glanes), F32)
        for i in range(band_chunks):
            o = o + jnp.dot(vtl_ref[c0 + i, ks, :], pl_b[i * SEQ_T:(i + 1) * SEQ_T], preferred_element_type=F32)
        for i in range(n_ctx_chunks):
            o = o + jnp.dot(vtc_ref[i, ks, :], pc_b[i * SEQ_T:(i + 1) * SEQ_T], preferred_element_type=F32)
        o = o / den
        for g, h in enumerate(hs):
            o_ref[:, h * HEAD_DIM:(h + 1) * HEAD_DIM] = o[:, g * BLOCK:(g + 1) * BLOCK].T.astype(BF16)


def _swa(q, k, vt, sink, dims):
    B, S, L = dims
    T = q.shape[0]
    n_lat_blocks = S // BLOCK
    n_ctx_blocks = L // BLOCK
    n_lat_chunks = S // SEQ_T
    n_ctx_chunks = L // SEQ_T
    lat_rows = B * S

    def q_idx(b, j):
        return (jnp.where(j < n_lat_blocks, b * n_lat_blocks + j,
                          lat_rows // BLOCK + b * n_ctx_blocks + (j - n_lat_blocks)), 0)

    return pl.pallas_call(
        functools.partial(_swa_kernel, n_lat_blocks=n_lat_blocks, n_lat_chunks=n_lat_chunks,
                          n_ctx_chunks=n_ctx_chunks),
        grid=(B, n_lat_blocks + n_ctx_blocks),
        in_specs=[
            pl.BlockSpec((BLOCK, SWA_WIDTH), q_idx),
            pl.BlockSpec((S, SWA_KV_WIDTH), lambda b, j: (b, 0)),
            pl.BlockSpec((n_lat_chunks, SWA_KV_WIDTH, SEQ_T), lambda b, j: (b, 0, 0)),
            pl.BlockSpec((L, SWA_KV_WIDTH), lambda b, j: (lat_rows // L + b, 0)),
            pl.BlockSpec((n_ctx_chunks, SWA_KV_WIDTH, SEQ_T), lambda b, j: (lat_rows // L + b, 0, 0)),
            pl.BlockSpec(memory_space=pltpu.SMEM),
        ],
        out_specs=pl.BlockSpec((BLOCK, SWA_WIDTH), q_idx),
        out_shape=jax.ShapeDtypeStruct((T, SWA_WIDTH), BF16),
        compiler_params=_cparams(("parallel", "arbitrary")),
        name="swa",
    )(q, k, vt, k, vt, sink)


def _diff_kernel(q_ref, kl_ref, vtl_ref, kc_ref, vtc_ref, lam_ref, subln_ref, o_ref, qp_scr, m_scr, acc_scr, *,
                 n_lat_tiles, n_lat_chunks, n_ctx_chunks, lam_init):
    j = pl.program_id(1)
    lv = lam_ref[...]
    lam = (jnp.exp(jnp.sum(lv[0:1] * lv[1:2], axis=-1, keepdims=True))
           - jnp.exp(jnp.sum(lv[2:3] * lv[3:4], axis=-1, keepdims=True)) + lam_init)
    lane = lax.broadcasted_iota(jnp.int32, (1, DIFF_V_DIM), 1)
    heads = [slice(h * DIFF_V_DIM, (h + 1) * DIFF_V_DIM) for h in range(DIFF_HEADS)]
    ones = jnp.ones((DIFF_SUM_ROWS, SEQ_T), BF16)

    for h, vs in enumerate(heads):
        qh = q_ref[:, vs]
        zero = jnp.zeros_like(qh)
        qp_scr[2 * h] = jnp.where(lane < HEAD_DIM, qh, zero)
        qp_scr[2 * h + 1] = jnp.where(lane >= HEAD_DIM, qh, zero)
    m_scr[...] = jnp.full(m_scr.shape, -jnp.inf, F32)
    acc_scr[...] = jnp.zeros(acc_scr.shape, F32)

    def update(blocks_of_head):
        all_blocks = [blocks_of_head(vs) for vs in heads]

        def scores(ch):
            return [_nt_dot(kblk, qp_scr[ch]) for kblk, _ in all_blocks[ch // 2]]

        n_chains = 2 * DIFF_HEADS
        queue = [scores(ch) for ch in range(DIFF_AHEAD)]
        for h, vs in enumerate(heads):
            blocks = all_blocks[h]
            for m in range(2):
                ch = 2 * h + m
                ss = queue.pop(0)
                if ch + DIFF_AHEAD < n_chains:
                    queue.append(scores(ch + DIFF_AHEAD))
                m_old = m_scr[ch]
                m_blk = jnp.max(ss[0], axis=0, keepdims=True)
                for s in ss[1:]:
                    m_blk = jnp.maximum(m_blk, jnp.max(s, axis=0, keepdims=True))
                m_new = jnp.maximum(m_old, m_blk.astype(BF16).astype(F32))
                m_b = m_new.astype(BF16)
                alpha = jnp.exp(m_old - m_new)
                acc = alpha * acc_scr[ch]
                for s, (_, vtblk) in zip(ss, blocks):
                    p = jnp.exp(s.astype(BF16) - m_b)
                    acc = acc + jnp.dot(jnp.concatenate([vtblk, ones], axis=0), p, preferred_element_type=F32)
                m_scr[ch] = m_new
                acc_scr[ch] = acc

    update(lambda vs: [(kc_ref[c * SEQ_T:(c + 1) * SEQ_T, vs], vtc_ref[c, vs, :]) for c in range(n_ctx_chunks)])

    @pl.when(j < n_lat_tiles)
    def _():
        def body(g, _):
            def blocks_of_head(vs):
                blocks = []
                for i in range(DIFF_GROUP):
                    c = g * DIFF_GROUP + i
                    r0 = pl.multiple_of(c * SEQ_T, SEQ_T)
                    blocks.append((kl_ref[pl.ds(r0, SEQ_T), vs], vtl_ref[c, vs, :]))
                return blocks

            update(blocks_of_head)
            return 0

        lax.fori_loop(0, n_lat_chunks // DIFF_GROUP, body, 0)

    def normalised(ch):
        acc = acc_scr[ch]
        return acc[:DIFF_V_DIM] / acc[DIFF_V_DIM:DIFF_V_DIM + 1]

    for h, vs in enumerate(heads):
        o = (normalised(2 * h) - lam * normalised(2 * h + 1)).T
        y = o * lax.rsqrt(jnp.mean(o * o, axis=-1, keepdims=True) + EPS) * subln_ref[...]
        o_ref[:, vs] = (y * (1.0 - lam_init)).astype(BF16)


def _diff_attn(dq, dk, dvt, lam_vecs, subln, lam_init, dims):
    B, S, L = dims
    T = dq.shape[0]
    n_lat_tiles = S // SEQ_T
    n_ctx_tiles = L // SEQ_T
    lat_rows = B * S

    def q_idx(b, j):
        return (jnp.where(j < n_lat_tiles, b * n_lat_tiles + j,
                          lat_rows // SEQ_T + b * n_ctx_tiles + (j - n_lat_tiles)), 0)

    return pl.pallas_call(
        functools.partial(_diff_kernel, n_lat_tiles=n_lat_tiles, n_lat_chunks=n_lat_tiles,
                          n_ctx_chunks=n_ctx_tiles, lam_init=lam_init),
        grid=(B, n_lat_tiles + n_ctx_tiles),
        in_specs=[
            pl.BlockSpec((SEQ_T, DIFF_QK_WIDTH), q_idx),
            pl.BlockSpec((S, DIFF_QK_WIDTH), lambda b, j: (b, 0), pipeline_mode=pl.Buffered(1)),
            pl.BlockSpec((n_lat_tiles, DIFF_WIDTH, SEQ_T), lambda b, j: (b, 0, 0), pipeline_mode=pl.Buffered(1)),
            pl.BlockSpec((L, DIFF_QK_WIDTH), lambda b, j: (lat_rows // L + b, 0)),
            pl.BlockSpec((n_ctx_tiles, DIFF_WIDTH, SEQ_T), lambda b, j: (lat_rows // L + b, 0, 0)),
            pl.BlockSpec(lam_vecs.shape, lambda b, j: (0, 0)),
            pl.BlockSpec((1, DIFF_V_DIM), lambda b, j: (0, 0)),
        ],
        out_specs=pl.BlockSpec((SEQ_T, DIFF_WIDTH), q_idx),
        out_shape=jax.ShapeDtypeStruct((T, DIFF_WIDTH), BF16),
        scratch_shapes=[
            pltpu.VMEM((2 * DIFF_HEADS, SEQ_T, DIFF_V_DIM), BF16),
            pltpu.VMEM((2 * DIFF_HEADS, 1, SEQ_T), F32),
            pltpu.VMEM((2 * DIFF_HEADS, DIFF_V_DIM + DIFF_SUM_ROWS, SEQ_T), F32),
        ],
        compiler_params=_cparams(("parallel", "arbitrary")),
        name="diff_attn",
    )(dq, dk, dvt, dk, dvt, lam_vecs, subln)


def _pre_rec_kernel(x_ref, mod_ref, nw_ref, w_ref, xp_ref, xr_ref, g_ref):
    h = _rmsnorm_mod(x_ref[...], nw_ref[...], mod_ref[0:1, :], mod_ref[1:2, :])
    p = jnp.dot(h.astype(BF16), w_ref[...], preferred_element_type=F32)
    xp_ref[...] = p[:, :POOL_WIDTH]
    xr_ref[...] = p[:, POOL_WIDTH:POOL_WIDTH + LRU_WIDTH]
    g_ref[...] = p[:, POOL_WIDTH + LRU_WIDTH:]


def _pre_rec(x, mods_l, norm_w, w_in, dims):
    B, S, L = dims
    T = x.shape[0]
    n_lat = B * S // TM
    s_tiles = S // TM

    def mod_idx(g):
        return (jnp.where(g < n_lat, g // s_tiles, B), 0, 0)

    widths = (POOL_WIDTH, LRU_WIDTH, LRU_WIDTH)
    return pl.pallas_call(
        _pre_rec_kernel,
        grid=(T // TM,),
        in_specs=[
            pl.BlockSpec((TM, D_MODEL), lambda g: (g, 0)),
            pl.BlockSpec((None, N_MOD, D_MODEL), mod_idx),
            pl.BlockSpec((1, D_MODEL), lambda g: (0, 0)),
            pl.BlockSpec(w_in.shape, lambda g: (0, 0)),
        ],
        out_specs=[pl.BlockSpec((TM, w), lambda g: (g, 0)) for w in widths],
        out_shape=[jax.ShapeDtypeStruct((T, w), F32) for w in widths],
        compiler_params=_cparams(("parallel",)),
        name="pre_rec",
    )(x, mods_l, norm_w, w_in)


def _rec_mid_kernel(xp_p, xp_c, xp_n, xr_p, xr_c, xr_n, cw_ref, cb_ref, wa_ref, ba_ref, wx_ref, bx_ref,
                    lam_ref, pw_ref, ps_ref, pool_ref, a_ref, b_ref, *, n_lat_chunks, lat_chunks_per_seq,
                    ctx_chunks_per_seq, lat_len, ctx_len):
    g = pl.program_id(0)
    is_lat = g < n_lat_chunks
    cps = jnp.where(is_lat, lat_chunks_per_seq, ctx_chunks_per_seq)
    within = jnp.where(is_lat, g % lat_chunks_per_seq, (g - n_lat_chunks) % ctx_chunks_per_seq)
    has_prev = within > 0
    has_next = within < cps - 1
    seg_len = jnp.where(is_lat, lat_len, ctx_len)
    ext_rows = SEQ_T + 2 * HALO

    def extended(prev_ref, cur_ref, next_ref):
        prev = jnp.where(has_prev, prev_ref[...], 0.0)
        nxt = jnp.where(has_next, next_ref[...], 0.0)
        return jnp.concatenate([prev, cur_ref[...], nxt], axis=0)

    def shifted(ext, off):
        return pltpu.roll(ext, (-off) % ext_rows, 0)[HALO:HALO + SEQ_T, :]

    xp_ext = extended(xp_p, xp_c, xp_n)
    t = within * SEQ_T + lax.broadcasted_iota(jnp.int32, (SEQ_T, 1), 0)
    for gi, w in enumerate(POOL_WINDOWS):
        cols = slice(gi * POOL_GDIM, (gi + 1) * POOL_GDIM)
        eg = xp_ext[:, cols]
        tot = shifted(eg, -(w // 2))
        for off in range(-(w // 2) + 1, w - w // 2):
            tot = tot + shifted(eg, off)
        lo = jnp.clip(t - w // 2, 0, seg_len)
        hi = jnp.clip(t - w // 2 + w, 0, seg_len)
        cnt = (hi - lo).astype(F32)
        d = tot / cnt - xp_c[:, cols]
        y = jnp.dot(d.astype(BF16), pw_ref[gi], preferred_element_type=F32)
        pool_ref[:, cols] = (y * ps_ref[:, cols]).astype(BF16)

    xr_ext = extended(xr_p, xr_c, xr_n)
    u = cb_ref[...] + cw_ref[0:1, :] * shifted(xr_ext, -CONV_LEFT)
    for k in range(1, CONV_W):
        u = u + cw_ref[k:k + 1, :] * shifted(xr_ext, k - CONV_LEFT)
    ub = u.astype(BF16)
    for d in range(2):
        r = jax.nn.sigmoid(jnp.dot(ub, wa_ref[d], preferred_element_type=F32) + ba_ref[d:d + 1, :])
        i = jax.nn.sigmoid(jnp.dot(ub, wx_ref[d], preferred_element_type=F32) + bx_ref[d:d + 1, :])
        nl = -lam_ref[d:d + 1, :]
        softplus = jnp.maximum(nl, 0.0) + jnp.log1p(jnp.exp(-jnp.abs(nl)))
        log_a = -LRU_C * r * softplus
        a_ref[d] = jnp.exp(log_a)
        th = jnp.tanh(log_a)
        b_ref[d] = jnp.sqrt(-2.0 * th / (1.0 - th)) * (i * u)


def _rec_mid(xp, xr, conv_w, conv_b, wa_bd, ba, wx_bd, bx, lam, pool_w, pool_scale, dims):
    B, S, L = dims
    T = xp.shape[0]
    n_chunks = T // SEQ_T

    def cur(g):
        return (g, 0)

    halos_per_chunk = SEQ_T // HALO

    def prv(g):
        return (jnp.maximum(g * halos_per_chunk - 1, 0), 0)

    def nxt(g):
        return (jnp.minimum((g + 1) * halos_per_chunk, n_chunks * halos_per_chunk - 1), 0)

    tile = lambda idx: pl.BlockSpec((SEQ_T, LRU_WIDTH), idx)
    halo = lambda idx: pl.BlockSpec((HALO, LRU_WIDTH), idx)
    full = lambda arr: pl.BlockSpec(arr.shape, lambda g: (0,) * arr.ndim)
    kern = functools.partial(_rec_mid_kernel, n_lat_chunks=B * S // SEQ_T, lat_chunks_per_seq=S // SEQ_T,
                             ctx_chunks_per_seq=L // SEQ_T, lat_len=S, ctx_len=L)
    return pl.pallas_call(
        kern,
        grid=(n_chunks,),
        in_specs=[halo(prv), tile(cur), halo(nxt), halo(prv), tile(cur), halo(nxt),
                  full(conv_w), full(conv_b), full(wa_bd), full(ba), full(wx_bd), full(bx), full(lam),
                  full(pool_w), full(pool_scale)],
        out_specs=[pl.BlockSpec((SEQ_T, POOL_WIDTH), cur),
                   pl.BlockSpec((2, SEQ_T, LRU_WIDTH), lambda g: (0, g, 0)),
                   pl.BlockSpec((2, SEQ_T, LRU_WIDTH), lambda g: (0, g, 0))],
        out_shape=[jax.ShapeDtypeStruct((T, POOL_WIDTH), BF16),
                   jax.ShapeDtypeStruct((2, T, LRU_WIDTH), F32),
                   jax.ShapeDtypeStruct((2, T, LRU_WIDTH), F32)],
        compiler_params=_cparams(("parallel",)),
        name="rec_mid",
    )(xp, xp, xp, xr, xr, xr, conv_w, conv_b, wa_bd, ba, wx_bd, bx, lam, pool_w, pool_scale)


def _scan_kernel(af_ref, bf_ref, ar_ref, br_ref, hf_ref, hr_ref, carry_ref):
    s = pl.program_id(1)

    @pl.when(s == 0)
    def _():
        carry_ref[...] = jnp.zeros_like(carry_ref)

    def body(i, carry):
        hf, hr = carry
        t = SEQ_T - 1 - i
        hf = af_ref[pl.ds(i, 1), :] * hf + bf_ref[pl.ds(i, 1), :]
        hr = ar_ref[pl.ds(t, 1), :] * hr + br_ref[pl.ds(t, 1), :]
        hf_ref[pl.ds(i, 1), :] = hf
        hr_ref[pl.ds(t, 1), :] = hr
        return hf, hr

    hf, hr = lax.fori_loop(0, SEQ_T, body, (carry_ref[0], carry_ref[1]), unroll=8)
    carry_ref[0] = hf
    carry_ref[1] = hr


def _scan(a, b, dims):
    B, S, L = dims
    T = a.shape[1]
    lat = S // SEQ_T
    ctx = L // SEQ_T
    lat_base = 0
    ctx_base = B * S // SEQ_T

    def fwd(bi, s):
        return jnp.where(s < ctx, ctx_base + bi * ctx + s, lat_base + bi * lat + (s - ctx))

    def rev(bi, s):
        return jnp.where(s < ctx, ctx_base + bi * ctx + (ctx - 1 - s), lat_base + bi * lat + (lat - 1 - (s - ctx)))

    in_f = pl.BlockSpec((None, SEQ_T, LRU_WIDTH), lambda bi, s: (0, fwd(bi, s), 0))
    in_r = pl.BlockSpec((None, SEQ_T, LRU_WIDTH), lambda bi, s: (1, rev(bi, s), 0))
    out_f = pl.BlockSpec((SEQ_T, LRU_WIDTH), lambda bi, s: (fwd(bi, s), 0))
    out_r = pl.BlockSpec((SEQ_T, LRU_WIDTH), lambda bi, s: (rev(bi, s), 0))
    out = jax.ShapeDtypeStruct((T, LRU_WIDTH), F32)
    return pl.pallas_call(
        _scan_kernel,
        grid=(B, lat + ctx),
        in_specs=[in_f, in_f, in_r, in_r],
        out_specs=[out_f, out_r],
        out_shape=[out, out],
        scratch_shapes=[pltpu.VMEM((2, 1, LRU_WIDTH), F32)],
        compiler_params=_cparams(("parallel", "arbitrary")),
        name="lru_scan",
    )(a, b, a, b)


def _post_common(y, x_ref, mod_ref, nw_ref, xo_ref, h2_ref):
    x_new = x_ref[...] + mod_ref[2:3, :] * y
    xo_ref[...] = x_new
    h2_ref[...] = _rmsnorm_mod(x_new, nw_ref[...], mod_ref[3:4, :], mod_ref[4:5, :]).astype(BF16)


def _post_attn_kernel(a_ref, b_ref, x_ref, mod_ref, nw_ref, w_ref, xo_ref, h2_ref):
    half = a_ref.shape[1]
    y = (jnp.dot(a_ref[...], w_ref[:half, :], preferred_element_type=F32)
         + jnp.dot(b_ref[...], w_ref[half:, :], preferred_element_type=F32))
    _post_common(y, x_ref, mod_ref, nw_ref, xo_ref, h2_ref)


def _post_rec_kernel(pool_ref, hf_ref, hr_ref, g_ref, x_ref, mod_ref, nw_ref, w_ref, xo_ref, h2_ref):
    half = pool_ref.shape[1]
    rec = ((hf_ref[...] + hr_ref[...]) * _gelu(g_ref[...])).astype(BF16)
    y = (jnp.dot(pool_ref[...], w_ref[:half, :], preferred_element_type=F32)
         + jnp.dot(rec, w_ref[half:, :], preferred_element_type=F32))
    _post_common(y, x_ref, mod_ref, nw_ref, xo_ref, h2_ref)


def _post_mixer(kind, parts, x, mods_l, norm_w, w_out, dims, n_tiles):
    B, S, L = dims
    T = x.shape[0]
    n_lat = B * S // TM
    s_tiles = S // TM

    def mod_idx(g):
        return (jnp.where(g < n_lat, g // s_tiles, B), 0, 0)

    row = lambda w: pl.BlockSpec((TM, w), lambda g: (g, 0))
    if kind == "attn":
        kern = _post_attn_kernel
        part_specs = [row(SWA_WIDTH), row(DIFF_WIDTH)]
    else:
        kern = _post_rec_kernel
        part_specs = [row(POOL_WIDTH), row(LRU_WIDTH), row(LRU_WIDTH), row(LRU_WIDTH)]
    n_in = len(parts)
    return pl.pallas_call(
        kern,
        grid=(n_tiles,),
        in_specs=part_specs + [
            row(D_MODEL),
            pl.BlockSpec((None, N_MOD, D_MODEL), mod_idx),
            pl.BlockSpec((1, D_MODEL), lambda g: (0, 0)),
            pl.BlockSpec(w_out.shape, lambda g: (0, 0)),
        ],
        out_specs=[row(D_MODEL), row(D_MODEL)],
        out_shape=[jax.ShapeDtypeStruct((T, D_MODEL), F32), jax.ShapeDtypeStruct((T, D_MODEL), BF16)],
        input_output_aliases={n_in: 0},
        compiler_params=_cparams(("parallel",)),
        name="post_" + kind,
    )(*parts, x, mods_l, norm_w, w_out)


def _peer_cand_blocks():
    return [(i, PEER_TOPK // (i + 1)) for i in range(1, SUBLANES)]


PEER_TAG_BITS = 31


def _peer_tagged(e, rank):
    bits = pltpu.bitcast(e, jnp.int32)
    tag = PEER_TAG_BITS - rank.astype(jnp.int32)
    return pltpu.bitcast((bits & ~PEER_TAG_BITS) | tag, F32)


def _peer_routing(h2_ref, wq_ref, sk_ref, thr_ref, c_ref, e2_ref,
                  q_scr, s_scr, cur_scr, rank_scr, top_scr, cand_scr):
    neg_inf = -jnp.inf
    n_cand = cand_scr.shape[0]

    def extract_step(k, src_ref, dst_ref, rank_ref=None):
        cur = src_ref[...]
        m = jnp.max(cur, axis=0, keepdims=True)
        dst_ref[pl.ds(k, 1), :] = m
        hit = cur == m
        src_ref[...] = jnp.where(hit, neg_inf, cur)
        if rank_ref is not None:
            rank_ref[...] = jnp.where(hit, lax.convert_element_type(k + 1, F32), rank_ref[...])

    def head_body(h, _):
        q0 = pl.multiple_of(h * PEER_QDIM, PEER_QDIM)
        q_scr[...] = _nt_dot(wq_ref[pl.ds(q0, PEER_QDIM), :], h2_ref[...])
        for p in range(2):
            s = jnp.dot(sk_ref[h * 2 + p].astype(BF16), q_scr[p * PEER_HALF:(p + 1) * PEER_HALF, :].astype(BF16),
                        preferred_element_type=F32)
            s_scr[p] = s
            cur_scr[p] = s
        rank_scr[...] = jnp.full(rank_scr.shape, PEER_TOPK + 1.0, F32)

        def top_body(k, _):
            extract_step(k, cur_scr.at[0], top_scr.at[0])
            extract_step(k, cur_scr.at[1], top_scr.at[1], rank_scr)
            return 0

        lax.fori_loop(0, PEER_TOPK, top_body, 0)
        v1 = top_scr[0]
        v2 = top_scr[1]
        cand_scr[0:PEER_TOPK, :] = v1[0:1, :] + v2
        row = lax.broadcasted_iota(jnp.int32, (SUBLANES, 1), 0)
        for i, n_i in _peer_cand_blocks():
            blk = jnp.where(row < n_i, v1[i:i + 1, :] + v2[0:SUBLANES, :], neg_inf)
            cand_scr[PEER_TOPK + (i - 1) * SUBLANES: PEER_TOPK + i * SUBLANES, :] = blk
        cand_scr[n_cand - SUBLANES:, :] = v1[SUBLANES:, :] + v2[0:1, :]
        cand = cand_scr[...]

        def cand_body(k, _):
            extract_step(k, cand_scr, top_scr.at[2])
            return 0

        lax.fori_loop(0, PEER_TOPK, cand_body, 0)
        tau = top_scr[2, PEER_TOPK - 1:PEER_TOPK, :]
        top = v1[0:1, :] + v2[0:1, :]
        z = jnp.sum(jnp.where(cand >= tau, jnp.exp(cand - top), 0.0), axis=0, keepdims=True)

        s1 = s_scr[0]
        ranks = (lax.broadcasted_iota(jnp.int32, (PEER_TOPK, 1), 0) + 1).astype(F32)
        top_scr[2] = _peer_tagged(jnp.exp(v2 - v2[0:1, :]), ranks)
        thr_ref[h] = jnp.full(s1.shape, jnp.inf, F32)

        def thr_body(jj, _):
            hit = s1 + top_scr[1, pl.ds(jj, 1), :] >= tau
            thr_ref[h] = jnp.where(hit, top_scr[2, pl.ds(jj, 1), :], thr_ref[h])
            return 0

        lax.fori_loop(0, PEER_TOPK, thr_body, 0)
        c_ref[h] = jnp.exp(s1 - v1[0:1, :]) / z
        e2_ref[h] = _peer_tagged(jnp.exp(s_scr[1] - v2[0:1, :]), rank_scr[...])
        return 0

    lax.fori_loop(0, PEER_HEADS, head_body, 0)


def _peer_kernel(h2_ref, u_ref, vt_ref, wq_ref, sk_ref, x_ref, mod_ref, fw_ref, xo_ref,
                 h2t_scr, coef_scr, acc_scr, thr_scr, c_scr, e2_scr,
                 q_scr, s_scr, cur_scr, rank_scr, top_scr, cand_scr, *, final):
    j = pl.program_id(1)
    tt = h2_ref.shape[0]
    pair = PEER_UP_GROUP * PEER_NKEYS

    @pl.when(j == 0)
    def _():
        acc_scr[...] = jnp.zeros_like(acc_scr)
        h2t_scr[...] = h2_ref[...].T
        _peer_routing(h2_ref, wq_ref, sk_ref, thr_scr, c_scr, e2_scr,
                      q_scr, s_scr, cur_scr, rank_scr, top_scr, cand_scr)

    for pr in range(PEER_NA // PEER_UP_GROUP):
        act = jnp.dot(u_ref[pr * pair:(pr + 1) * pair, :], h2t_scr[...], preferred_element_type=F32)
        for half in range(PEER_UP_GROUP):
            al = PEER_UP_GROUP * pr + half
            a = j * PEER_NA + al
            thr_full = [thr_scr[h, pl.ds(a, 1), :] for h in range(PEER_HEADS)]
            c_full = [c_scr[h, pl.ds(a, 1), :] for h in range(PEER_HEADS)]
            for tc in range(tt // LANES):
                ls = slice(tc * LANES, (tc + 1) * LANES)
                thr_rows = [r[:, ls] for r in thr_full]
                c_rows = [r[:, ls] for r in c_full]
                for rb in range(PEER_NKEYS // PEER_RB):
                    rs = slice(rb * PEER_RB, (rb + 1) * PEER_RB)
                    w = jnp.zeros((PEER_RB, LANES), F32)
                    for h in range(PEER_HEADS):
                        e2_t = e2_scr[h, rs, ls]
                        w = w + jnp.where(e2_t >= thr_rows[h], c_rows[h] * e2_t, 0.0)
                    rows = slice(half * PEER_NKEYS + rb * PEER_RB, half * PEER_NKEYS + (rb + 1) * PEER_RB)
                    r0 = al * PEER_NKEYS + rb * PEER_RB
                    coef_scr[r0:r0 + PEER_RB, ls] = (w * _gelu(act[rows, ls])).astype(BF16)

    acc_scr[...] += jnp.dot(vt_ref[...], coef_scr[...], preferred_element_type=F32)

    @pl.when(j == pl.num_programs(1) - 1)
    def _():
        x_new = x_ref[...] + mod_ref[5:6, :] * acc_scr[...].T
        if final:
            x_new = x_new * lax.rsqrt(jnp.mean(x_new * x_new, axis=-1, keepdims=True) + EPS) * fw_ref[...]
        xo_ref[...] = x_new


def _peer(h2, u, v_t, wq_t, subkeys, x, mods_l, final_w, dims, n_tiles, final):
    B, S, L = dims
    T = x.shape[0]
    tt = PEER_TT
    ne = PEER_NA * PEER_NKEYS
    n_lat = B * S // tt
    s_tiles = S // tt
    n_cand = PEER_TOPK + SUBLANES * SUBLANES

    def mod_idx(i, j):
        return (jnp.where(i < n_lat, i // s_tiles, B), 0, 0)

    key_f32 = pltpu.VMEM((PEER_HEADS, PEER_NKEYS, tt), F32)
    const = pl.Buffered(1)
    return pl.pallas_call(
        functools.partial(_peer_kernel, final=final),
        grid=(n_tiles, PEER_EXPERTS // ne),
        in_specs=[
            pl.BlockSpec((tt, D_MODEL), lambda i, j: (i, 0)),
            pl.BlockSpec((ne, D_MODEL), lambda i, j: (j, 0)),
            pl.BlockSpec((None, D_MODEL, ne), lambda i, j: (j, 0, 0)),
            pl.BlockSpec(wq_t.shape, lambda i, j: (0, 0), pipeline_mode=const),
            pl.BlockSpec(subkeys.shape, lambda i, j: (0, 0, 0), pipeline_mode=const),
            pl.BlockSpec((tt, D_MODEL), lambda i, j: (i, 0)),
            pl.BlockSpec((None, N_MOD, D_MODEL), mod_idx),
            pl.BlockSpec((1, D_MODEL), lambda i, j: (0, 0)),
        ],
        out_specs=pl.BlockSpec((tt, D_MODEL), lambda i, j: (i, 0)),
        out_shape=jax.ShapeDtypeStruct((n_tiles * tt if final else T, D_MODEL), F32),
        scratch_shapes=[
            pltpu.VMEM((D_MODEL, tt), BF16),
            pltpu.VMEM((ne, tt), BF16),
            pltpu.VMEM((D_MODEL, tt), F32),
            key_f32, key_f32, key_f32,
            pltpu.VMEM((PEER_QDIM, tt), F32),
            pltpu.VMEM((2, PEER_NKEYS, tt), F32),
            pltpu.VMEM((2, PEER_NKEYS, tt), F32),
            pltpu.VMEM((PEER_NKEYS, tt), F32),
            pltpu.VMEM((3, PEER_TOPK, tt), F32),
            pltpu.VMEM((n_cand, tt), F32),
        ],
        input_output_aliases={} if final else {5: 0},
        compiler_params=_cparams(("parallel", "arbitrary")),
        name="peer",
    )(h2, u, v_t, wq_t, subkeys, x, mods_l, final_w)


def _lambda_init(layer):
    return 0.8 - 0.6 * math.exp(-0.3 * layer)


def _rope_tables(S):
    rows = S // GRID_W
    row = jnp.repeat(jnp.arange(rows), GRID_W).astype(F32)
    col = jnp.tile(jnp.arange(GRID_W), rows).astype(F32)
    inv = ROPE_THETA ** (-jnp.arange(ROPE_AXIS_FREQS, dtype=F32) / ROPE_AXIS_FREQS)
    ang = jnp.concatenate([row[:, None] * inv, col[:, None] * inv], axis=-1)
    cos, sin = jnp.cos(ang), jnp.sin(ang)
    cos_t = jnp.tile(cos, (1, LANES // ROPE_HALF))
    sin_t = jnp.tile(jnp.concatenate([-sin, sin], axis=-1), (1, LANES // HEAD_DIM))
    cos_t = jnp.concatenate([cos_t, jnp.ones((TM, LANES), F32)], axis=0)
    sin_t = jnp.concatenate([sin_t, jnp.zeros((TM, LANES), F32)], axis=0)
    return cos_t, sin_t


def _block_diag(w):
    nd, nb, bd, _ = w.shape
    eye = jnp.eye(nb, dtype=w.dtype)
    return jnp.einsum("dnij,nm->dnimj", w, eye).reshape(nd, nb * bd, nb * bd)


def kernel(x, c, ctx, c_ctx, w_mod, b_mod, norm_mix, norm_ffn, w_out, attn_w_in, swa_sink, diff_lambda, diff_subln, rec_w_in, pool_w, pool_scale, lru_conv_w, lru_conv_b, lru_wa, lru_ba, lru_wx, lru_bx, lru_lambda, peer_wq, peer_subkeys, peer_u, peer_v, final_norm):
    B, S, D = x.shape
    L = ctx.shape[1]
    depth = w_mod.shape[0]
    dims = (B, S, L)
    assert D == D_MODEL and S % PEER_TT == 0 and (B * L) % PEER_TT == 0 and L % SEQ_T == 0
    assert TM % SEQ_T == 0 and S % TM == 0 and (B * L) % TM == 0
    assert S % GRID_W == 0 and S >= 3 * BLOCK and B + 1 <= SUBLANES and S % (DIFF_GROUP * SEQ_T) == 0
    lat_rows = B * S
    T = lat_rows + B * L

    xs = jnp.concatenate([x.reshape(lat_rows, D), ctx.reshape(B * L, D)], axis=0)
    cc = jnp.zeros((SUBLANES, D), F32).at[:B].set(c).at[B].set(c_ctx)
    mods = _modulation(cc, w_mod, b_mod)
    cos_t, sin_t = _rope_tables(S)

    for l in range(depth):
        jl = l // 2
        ctx_out = l < depth - 1
        n_rows = T if ctx_out else lat_rows
        mods_l = mods[l]
        if l % 2 == 0:
            q, k, dq, dk, vt, dvt = _pre_attn(xs, mods_l, norm_mix[l][None], attn_w_in[jl].astype(BF16),
                                              cos_t, sin_t, dims)
            a = _swa(q, k, vt, swa_sink[jl], dims)
            bd = _diff_attn(dq, dk, dvt, diff_lambda[jl], diff_subln[jl][None], _lambda_init(l), dims)
            xs, h2 = _post_mixer("attn", (a, bd), xs, mods_l, norm_ffn[l][None], w_out[l].astype(BF16),
                                 dims, n_rows // TM)
        else:
            xp, xr, g = _pre_rec(xs, mods_l, norm_mix[l][None], rec_w_in[jl].astype(BF16), dims)
            pool, a_co, b_co = _rec_mid(xp, xr, lru_conv_w[jl], lru_conv_b[jl][None],
                                        _block_diag(lru_wa[jl]).astype(BF16), lru_ba[jl],
                                        _block_diag(lru_wx[jl]).astype(BF16), lru_bx[jl], lru_lambda[jl],
                                        pool_w[jl].astype(BF16), pool_scale[jl][None], dims)
            h_fwd, h_rev = _scan(a_co, b_co, dims)
            xs, h2 = _post_mixer("rec", (pool, h_fwd, h_rev, g), xs, mods_l, norm_ffn[l][None], w_out[l].astype(BF16),
                                 dims, n_rows // TM)
        n_peer = n_rows // PEER_TT
        wq_t = peer_wq[l].astype(BF16).T
        sk = peer_subkeys[l].reshape(PEER_HEADS * 2, PEER_NKEYS, PEER_HALF)
        ne = PEER_NA * PEER_NKEYS
        v_t = peer_v[l].astype(BF16).reshape(PEER_EXPERTS // ne, ne, D).transpose(0, 2, 1)
        xs = _peer(h2, peer_u[l].astype(BF16), v_t, wq_t, sk, xs, mods_l, final_norm[None], dims, n_peer,
                   final=not ctx_out)

    return xs.reshape(B, S, D)
```

```python
import functools
import math

import jax
import jax.numpy as jnp
from jax import lax
from jax.experimental import pallas as pl
from jax.experimental.pallas import tpu as pltpu

F32 = jnp.float32
BF16 = jnp.bfloat16
HIGHEST = lax.Precision.HIGHEST

D_MODEL = 1024
HEAD_DIM = 64
EPS = 1e-6
NEG = -1e30
N_MOD = 6
SCALE = HEAD_DIM ** -0.5
ROPE_HALF = HEAD_DIM // 2
ROPE_AXIS_FREQS = ROPE_HALF // 2
ROPE_THETA = 10000.0
GRID_W = 64
SWA_Q_HEADS = 8
SWA_KV_HEADS = 2
SWA_GROUP = SWA_Q_HEADS // SWA_KV_HEADS
WINDOW = 128
BLOCK = 128
SWA_WIDTH = SWA_Q_HEADS * HEAD_DIM
SWA_KV_WIDTH = SWA_KV_HEADS * HEAD_DIM
DIFF_HEADS = 4
DIFF_V_DIM = 2 * HEAD_DIM
DIFF_QK_WIDTH = DIFF_HEADS * 2 * HEAD_DIM
DIFF_WIDTH = DIFF_HEADS * DIFF_V_DIM
POOL_WINDOWS = (2, 4, 8, 16)
POOL_GROUPS = 4
POOL_WIDTH = D_MODEL // 2
POOL_GDIM = POOL_WIDTH // POOL_GROUPS
LRU_WIDTH = D_MODEL // 2
CONV_W = 4
CONV_LEFT = CONV_W // 2
LRU_C = 8.0
PEER_HEADS = 8
PEER_NKEYS = 128
PEER_EXPERTS = PEER_NKEYS * PEER_NKEYS
PEER_QDIM = 256
PEER_HALF = PEER_QDIM // 2
PEER_TOPK = 16

LANES = 128
SUBLANES = 8
VMEM_LIMIT = 56 * 1024 * 1024

TM = 1024
SEQ_T = 256
DIFF_GROUP = 32
DIFF_AHEAD = 2
DIFF_SUM_ROWS = 16
HALO = 8
PEER_TT = 512
PEER_NA = 16
PEER_UP_GROUP = 2
PEER_RB = 64
PEER_LOOP_UNROLL = 4


def _cparams(sem):
    return pltpu.CompilerParams(dimension_semantics=sem, vmem_limit_bytes=VMEM_LIMIT)


def _nt_dot(a, b):
    return lax.dot_general(a, b, (((1,), (1,)), ((), ())), preferred_element_type=F32)


def _rmsnorm_mod(x, w, shift, scale):
    y = x * lax.rsqrt(jnp.mean(x * x, axis=-1, keepdims=True) + EPS) * w
    return y * (1.0 + scale) + shift


def _gelu(x):
    return 0.5 * x * (1.0 + lax.erf(x * (2.0 ** -0.5)))


def _mod_kernel(cc_ref, w_ref, b_ref, o_ref):
    cc = cc_ref[...]
    sc = cc * jax.nn.sigmoid(cc)
    o_ref[...] = jnp.dot(sc, w_ref[...], precision=HIGHEST, preferred_element_type=F32) + b_ref[...]


def _modulation(cc, w_mod, b_mod):
    depth = w_mod.shape[0]
    rows = cc.shape[0]
    out = pl.pallas_call(
        _mod_kernel,
        grid=(depth, N_MOD),
        in_specs=[
            pl.BlockSpec((rows, D_MODEL), lambda l, j: (0, 0)),
            pl.BlockSpec((None, D_MODEL, D_MODEL), lambda l, j: (l, 0, j)),
            pl.BlockSpec((None, 1, D_MODEL), lambda l, j: (l, 0, j)),
        ],
        out_specs=pl.BlockSpec((None, rows, D_MODEL), lambda l, j: (l, 0, j)),
        out_shape=jax.ShapeDtypeStruct((depth, rows, N_MOD * D_MODEL), F32),
        compiler_params=_cparams(("parallel", "parallel")),
        name="modulation",
    )(cc, w_mod, b_mod.reshape(depth, 1, N_MOD * D_MODEL))
    return out.reshape(depth, rows, N_MOD, D_MODEL)


def _rope128(x, cos, sin_signed, first_half):
    partner = jnp.where(first_half, pltpu.roll(x, LANES - ROPE_HALF, 1), pltpu.roll(x, ROPE_HALF, 1))
    return x * cos + partner * sin_signed


def _pre_attn_kernel(x_ref, mod_ref, nw_ref, w_ref, cos_ref, sin_ref,
                     q_ref, k_ref, dq_ref, dk_ref, v_ref, dv_ref):
    h = _rmsnorm_mod(x_ref[...], nw_ref[...], mod_ref[0:1, :], mod_ref[1:2, :])
    p = jnp.dot(h.astype(BF16), w_ref[...], preferred_element_type=F32)
    cos = cos_ref[...]
    sin = sin_ref[...]
    lane = lax.broadcasted_iota(jnp.int32, (1, LANES), 1)
    first_half = (lane % HEAD_DIM) < ROPE_HALF

    def roped(lo, width, scale):
        outs = []
        for c in range(width // LANES):
            xc = p[:, lo + c * LANES: lo + (c + 1) * LANES]
            outs.append((_rope128(xc, cos, sin, first_half) * scale).astype(BF16))
        return outs

    o1 = SWA_WIDTH
    o2 = o1 + SWA_KV_WIDTH
    o3 = o2 + SWA_KV_WIDTH
    o4 = o3 + DIFF_QK_WIDTH
    o5 = o4 + DIFF_QK_WIDTH
    for c, val in enumerate(roped(0, SWA_WIDTH, SCALE)):
        q_ref[:, c * LANES:(c + 1) * LANES] = val
    for c, val in enumerate(roped(o1, SWA_KV_WIDTH, 1.0)):
        k_ref[:, c * LANES:(c + 1) * LANES] = val
    for c in range(TM // SEQ_T):
        v_ref[c] = p[c * SEQ_T:(c + 1) * SEQ_T, o2:o3].T.astype(BF16)
        dv_ref[c] = p[c * SEQ_T:(c + 1) * SEQ_T, o5:].T.astype(BF16)
    for c, val in enumerate(roped(o3, DIFF_QK_WIDTH, SCALE)):
        dq_ref[:, c * LANES:(c + 1) * LANES] = val
    for c, val in enumerate(roped(o4, DIFF_QK_WIDTH, 1.0)):
        dk_ref[:, c * LANES:(c + 1) * LANES] = val


def _pre_attn(x, mods_l, norm_w, w_in, cos_t, sin_t, dims):
    B, S, L = dims
    T = x.shape[0]
    n_lat = B * S // TM
    s_tiles = S // TM

    def mod_idx(g):
        return (jnp.where(g < n_lat, g // s_tiles, B), 0, 0)

    def rope_idx(g):
        return (jnp.where(g < n_lat, g % s_tiles, s_tiles), 0)

    widths = (SWA_WIDTH, SWA_KV_WIDTH, DIFF_QK_WIDTH, DIFF_QK_WIDTH)
    t_widths = (SWA_KV_WIDTH, DIFF_WIDTH)
    return pl.pallas_call(
        _pre_attn_kernel,
        grid=(T // TM,),
        in_specs=[
            pl.BlockSpec((TM, D_MODEL), lambda g: (g, 0)),
            pl.BlockSpec((None, N_MOD, D_MODEL), mod_idx),
            pl.BlockSpec((1, D_MODEL), lambda g: (0, 0)),
            pl.BlockSpec(w_in.shape, lambda g: (0, 0)),
            pl.BlockSpec((TM, LANES), rope_idx),
            pl.BlockSpec((TM, LANES), rope_idx),
        ],
        out_specs=[pl.BlockSpec((TM, w), lambda g: (g, 0)) for w in widths]
        + [pl.BlockSpec((TM // SEQ_T, w, SEQ_T), lambda g: (g, 0, 0)) for w in t_widths],
        out_shape=[jax.ShapeDtypeStruct((T, w), BF16) for w in widths]
        + [jax.ShapeDtypeStruct((T // SEQ_T, w, SEQ_T), BF16) for w in t_widths],
        compiler_params=_cparams(("parallel",)),
        name="pre_attn",
    )(x, mods_l, norm_w, w_in, cos_t, sin_t)


def _swa_kernel(q_ref, kl_ref, vtl_ref, kc_ref, vtc_ref, sink_ref, o_ref, *, n_lat_blocks, n_lat_chunks,
                n_ctx_chunks):
    j = pl.program_id(1)
    is_ctx = j >= n_lat_blocks
    jl = jnp.minimum(j, n_lat_blocks - 1)
    band_chunks = 2
    per_chunk = SEQ_T // BLOCK
    c0 = jnp.clip((jl - 1) // per_chunk, 0, n_lat_chunks - band_chunks)
    r0 = pl.multiple_of(c0 * SEQ_T, SEQ_T)
    kb = kl_ref[pl.ds(r0, band_chunks * SEQ_T), :]
    kc = kc_ref[...]
    glanes = SWA_GROUP * BLOCK
    kpos = r0 + lax.broadcasted_iota(jnp.int32, (band_chunks * SEQ_T, 1), 0)
    qpos = jl * BLOCK + lax.broadcasted_iota(jnp.int32, (1, glanes), 1) % BLOCK
    valid = jnp.logical_and(jnp.abs(qpos - kpos) <= WINDOW, jnp.logical_not(is_ctx))
    q = q_ref[...]
    for hk in range(SWA_KV_HEADS):
        ks = slice(hk * HEAD_DIM, (hk + 1) * HEAD_DIM)
        hs = [hk * SWA_GROUP + g for g in range(SWA_GROUP)]
        qg = jnp.concatenate([q[:, h * HEAD_DIM:(h + 1) * HEAD_DIM] for h in hs], axis=0)
        sink = jnp.concatenate([jnp.full((1, BLOCK), sink_ref[h], F32) for h in hs], axis=1)
        sl = jnp.where(valid, _nt_dot(kb[:, ks], qg), NEG)
        sc = _nt_dot(kc[:, ks], qg)
        m = jnp.maximum(jnp.maximum(jnp.max(sl, axis=0, keepdims=True), jnp.max(sc, axis=0, keepdims=True)), sink)
        pl_ = jnp.exp(sl - m)
        pc = jnp.exp(sc - m)
        den = jnp.sum(pl_, axis=0, keepdims=True) + jnp.sum(pc, axis=0, keepdims=True) + jnp.exp(sink - m)
        pl_b = pl_.astype(BF16)
        pc_b = pc.astype(BF16)
        o = jnp.zeros((HEAD_DIM, glanes), F32)
        for i in range(band_chunks):
            o = o + jnp.dot(vtl_ref[c0 + i, ks, :], pl_b[i * SEQ_T:(i + 1) * SEQ_T], preferred_element_type=F32)
        for i in range(n_ctx_chunks):
            o = o + jnp.dot(vtc_ref[i, ks, :], pc_b[i * SEQ_T:(i + 1) * SEQ_T], preferred_element_type=F32)
        o = o / den
        for g, h in enumerate(hs):
            o_ref[:, h * HEAD_DIM:(h + 1) * HEAD_DIM] = o[:, g * BLOCK:(g + 1) * BLOCK].T.astype(BF16)


def _swa(q, k, vt, sink, dims):
    B, S, L = dims
    T = q.shape[0]
    n_lat_blocks = S // BLOCK
    n_ctx_blocks = L // BLOCK
    n_lat_chunks = S // SEQ_T
    n_ctx_chunks = L // SEQ_T
    lat_rows = B * S

    def q_idx(b, j):
        return (jnp.where(j < n_lat_blocks, b * n_lat_blocks + j,
                          lat_rows // BLOCK + b * n_ctx_blocks + (j - n_lat_blocks)), 0)

    return pl.pallas_call(
        functools.partial(_swa_kernel, n_lat_blocks=n_lat_blocks, n_lat_chunks=n_lat_chunks,
                          n_ctx_chunks=n_ctx_chunks),
        grid=(B, n_lat_blocks + n_ctx_blocks),
        in_specs=[
            pl.BlockSpec((BLOCK, SWA_WIDTH), q_idx),
            pl.BlockSpec((S, SWA_KV_WIDTH), lambda b, j: (b, 0)),
            pl.BlockSpec((n_lat_chunks, SWA_KV_WIDTH, SEQ_T), lambda b, j: (b, 0, 0)),
            pl.BlockSpec((L, SWA_KV_WIDTH), lambda b, j: (lat_rows // L + b, 0)),
            pl.BlockSpec((n_ctx_chunks, SWA_KV_WIDTH, SEQ_T), lambda b, j: (lat_rows // L + b, 0, 0)),
            pl.BlockSpec(memory_space=pltpu.SMEM),
        ],
        out_specs=pl.BlockSpec((BLOCK, SWA_WIDTH), q_idx),
        out_shape=jax.ShapeDtypeStruct((T, SWA_WIDTH), BF16),
        compiler_params=_cparams(("parallel", "arbitrary")),
        name="swa",
    )(q, k, vt, k, vt, sink)


def _diff_kernel(q_ref, kl_ref, vtl_ref, kc_ref, vtc_ref, lam_ref, subln_ref, o_ref, qp_scr, m_scr, acc_scr, *,
                 n_lat_tiles, n_lat_chunks, n_ctx_chunks, lam_init):
    j = pl.program_id(1)
    lv = lam_ref[...]
    lam = (jnp.exp(jnp.sum(lv[0:1] * lv[1:2], axis=-1, keepdims=True))
           - jnp.exp(jnp.sum(lv[2:3] * lv[3:4], axis=-1, keepdims=True)) + lam_init)
    lane = lax.broadcasted_iota(jnp.int32, (1, DIFF_V_DIM), 1)
    heads = [slice(h * DIFF_V_DIM, (h + 1) * DIFF_V_DIM) for h in range(DIFF_HEADS)]
    ones = jnp.ones((DIFF_SUM_ROWS, SEQ_T), BF16)

    for h, vs in enumerate(heads):
        qh = q_ref[:, vs]
        zero = jnp.zeros_like(qh)
        qp_scr[2 * h] = jnp.where(lane < HEAD_DIM, qh, zero)
        qp_scr[2 * h + 1] = jnp.where(lane >= HEAD_DIM, qh, zero)
    m_scr[...] = jnp.full(m_scr.shape, -jnp.inf, F32)
    acc_scr[...] = jnp.zeros(acc_scr.shape, F32)

    def update(blocks_of_head):
        all_blocks = [blocks_of_head(vs) for vs in heads]

        def scores(ch):
            return [_nt_dot(kblk, qp_scr[ch]) for kblk, _ in all_blocks[ch // 2]]

        n_chains = 2 * DIFF_HEADS
        queue = [scores(ch) for ch in range(DIFF_AHEAD)]
        for h, vs in enumerate(heads):
            blocks = all_blocks[h]
            for m in range(2):
                ch = 2 * h + m
                ss = queue.pop(0)
                if ch + DIFF_AHEAD < n_chains:
                    queue.append(scores(ch + DIFF_AHEAD))
                m_old = m_scr[ch]
                m_blk = jnp.max(ss[0], axis=0, keepdims=True)
                for s in ss[1:]:
                    m_blk = jnp.maximum(m_blk, jnp.max(s, axis=0, keepdims=True))
                m_new = jnp.maximum(m_old, m_blk.astype(BF16).astype(F32))
                m_b = m_new.astype(BF16)
                alpha = jnp.exp(m_old - m_new)
                acc = alpha * acc_scr[ch]
                for s, (_, vtblk) in zip(ss, blocks):
                    p = jnp.exp(s.astype(BF16) - m_b)
                    acc = acc + jnp.dot(jnp.concatenate([vtblk, ones], axis=0), p, preferred_element_type=F32)
                m_scr[ch] = m_new
                acc_scr[ch] = acc

    update(lambda vs: [(kc_ref[c * SEQ_T:(c + 1) * SEQ_T, vs], vtc_ref[c, vs, :]) for c in range(n_ctx_chunks)])

    @pl.when(j < n_lat_tiles)
    def _():
        def body(g, _):
            def blocks_of_head(vs):
                blocks = []
                for i in range(DIFF_GROUP):
                    c = g * DIFF_GROUP + i
                    r0 = pl.multiple_of(c * SEQ_T, SEQ_T)
                    blocks.append((kl_ref[pl.ds(r0, SEQ_T), vs], vtl_ref[c, vs, :]))
                return blocks

            update(blocks_of_head)
            return 0

        lax.fori_loop(0, n_lat_chunks // DIFF_GROUP, body, 0)

    def normalised(ch):
        acc = acc_scr[ch]
        return acc[:DIFF_V_DIM] / acc[DIFF_V_DIM:DIFF_V_DIM + 1]

    for h, vs in enumerate(heads):
        o = (normalised(2 * h) - lam * normalised(2 * h + 1)).T
        y = o * lax.rsqrt(jnp.mean(o * o, axis=-1, keepdims=True) + EPS) * subln_ref[...]
        o_ref[:, vs] = (y * (1.0 - lam_init)).astype(BF16)


def _diff_attn(dq, dk, dvt, lam_vecs, subln, lam_init, dims):
    B, S, L = dims
    T = dq.shape[0]
    n_lat_tiles = S // SEQ_T
    n_ctx_tiles = L // SEQ_T
    lat_rows = B * S

    def q_idx(b, j):
        return (jnp.where(j < n_lat_tiles, b * n_lat_tiles + j,
                          lat_rows // SEQ_T + b * n_ctx_tiles + (j - n_lat_tiles)), 0)

    return pl.pallas_call(
        functools.partial(_diff_kernel, n_lat_tiles=n_lat_tiles, n_lat_chunks=n_lat_tiles,
                          n_ctx_chunks=n_ctx_tiles, lam_init=lam_init),
        grid=(B, n_lat_tiles + n_ctx_tiles),
        in_specs=[
            pl.BlockSpec((SEQ_T, DIFF_QK_WIDTH), q_idx),
            pl.BlockSpec((S, DIFF_QK_WIDTH), lambda b, j: (b, 0), pipeline_mode=pl.Buffered(1)),
            pl.BlockSpec((n_lat_tiles, DIFF_WIDTH, SEQ_T), lambda b, j: (b, 0, 0), pipeline_mode=pl.Buffered(1)),
            pl.BlockSpec((L, DIFF_QK_WIDTH), lambda b, j: (lat_rows // L + b, 0)),
            pl.BlockSpec((n_ctx_tiles, DIFF_WIDTH, SEQ_T), lambda b, j: (lat_rows // L + b, 0, 0)),
            pl.BlockSpec(lam_vecs.shape, lambda b, j: (0, 0)),
            pl.BlockSpec((1, DIFF_V_DIM), lambda b, j: (0, 0)),
        ],
        out_specs=pl.BlockSpec((SEQ_T, DIFF_WIDTH), q_idx),
        out_shape=jax.ShapeDtypeStruct((T, DIFF_WIDTH), BF16),
        scratch_shapes=[
            pltpu.VMEM((2 * DIFF_HEADS, SEQ_T, DIFF_V_DIM), BF16),
            pltpu.VMEM((2 * DIFF_HEADS, 1, SEQ_T), F32),
            pltpu.VMEM((2 * DIFF_HEADS, DIFF_V_DIM + DIFF_SUM_ROWS, SEQ_T), F32),
        ],
        compiler_params=_cparams(("parallel", "arbitrary")),
        name="diff_attn",
    )(dq, dk, dvt, dk, dvt, lam_vecs, subln)


def _pre_rec_kernel(x_ref, mod_ref, nw_ref, w_ref, xp_ref, xr_ref, g_ref):
    h = _rmsnorm_mod(x_ref[...], nw_ref[...], mod_ref[0:1, :], mod_ref[1:2, :])
    p = jnp.dot(h.astype(BF16), w_ref[...], preferred_element_type=F32)
    xp_ref[...] = p[:, :POOL_WIDTH]
    xr_ref[...] = p[:, POOL_WIDTH:POOL_WIDTH + LRU_WIDTH]
    g_ref[...] = p[:, POOL_WIDTH + LRU_WIDTH:]


def _pre_rec(x, mods_l, norm_w, w_in, dims):
    B, S, L = dims
    T = x.shape[0]
    n_lat = B * S // TM
    s_tiles = S // TM

    def mod_idx(g):
        return (jnp.where(g < n_lat, g // s_tiles, B), 0, 0)

    widths = (POOL_WIDTH, LRU_WIDTH, LRU_WIDTH)
    return pl.pallas_call(
        _pre_rec_kernel,
        grid=(T // TM,),
        in_specs=[
            pl.BlockSpec((TM, D_MODEL), lambda g: (g, 0)),
            pl.BlockSpec((None, N_MOD, D_MODEL), mod_idx),
            pl.BlockSpec((1, D_MODEL), lambda g: (0, 0)),
            pl.BlockSpec(w_in.shape, lambda g: (0, 0)),
        ],
        out_specs=[pl.BlockSpec((TM, w), lambda g: (g, 0)) for w in widths],
        out_shape=[jax.ShapeDtypeStruct((T, w), F32) for w in widths],
        compiler_params=_cparams(("parallel",)),
        name="pre_rec",
    )(x, mods_l, norm_w, w_in)


def _rec_mid_kernel(xp_p, xp_c, xp_n, xr_p, xr_c, xr_n, cw_ref, cb_ref, wa_ref, ba_ref, wx_ref, bx_ref,
                    lam_ref, pw_ref, ps_ref, pool_ref, a_ref, b_ref, *, n_lat_chunks, lat_chunks_per_seq,
                    ctx_chunks_per_seq, lat_len, ctx_len):
    g = pl.program_id(0)
    is_lat = g < n_lat_chunks
    cps = jnp.where(is_lat, lat_chunks_per_seq, ctx_chunks_per_seq)
    within = jnp.where(is_lat, g % lat_chunks_per_seq, (g - n_lat_chunks) % ctx_chunks_per_seq)
    has_prev = within > 0
    has_next = within < cps - 1
    seg_len = jnp.where(is_lat, lat_len, ctx_len)
    ext_rows = SEQ_T + 2 * HALO

    def extended(prev_ref, cur_ref, next_ref):
        prev = jnp.where(has_prev, prev_ref[...], 0.0)
        nxt = jnp.where(has_next, next_ref[...], 0.0)
        return jnp.concatenate([prev, cur_ref[...], nxt], axis=0)

    def shifted(ext, off):
        return pltpu.roll(ext, (-off) % ext_rows, 0)[HALO:HALO + SEQ_T, :]

    xp_ext = extended(xp_p, xp_c, xp_n)
    t = within * SEQ_T + lax.broadcasted_iota(jnp.int32, (SEQ_T, 1), 0)
    for gi, w in enumerate(POOL_WINDOWS):
        cols = slice(gi * POOL_GDIM, (gi + 1) * POOL_GDIM)
        eg = xp_ext[:, cols]
        tot = shifted(eg, -(w // 2))
        for off in range(-(w // 2) + 1, w - w // 2):
            tot = tot + shifted(eg, off)
        lo = jnp.clip(t - w // 2, 0, seg_len)
        hi = jnp.clip(t - w // 2 + w, 0, seg_len)
        cnt = (hi - lo).astype(F32)
        d = tot / cnt - xp_c[:, cols]
        y = jnp.dot(d.astype(BF16), pw_ref[gi], preferred_element_type=F32)
        pool_ref[:, cols] = (y * ps_ref[:, cols]).astype(BF16)

    xr_ext = extended(xr_p, xr_c, xr_n)
    u = cb_ref[...] + cw_ref[0:1, :] * shifted(xr_ext, -CONV_LEFT)
    for k in range(1, CONV_W):
        u = u + cw_ref[k:k + 1, :] * shifted(xr_ext, k - CONV_LEFT)
    ub = u.astype(BF16)
    for d in range(2):
        r = jax.nn.sigmoid(jnp.dot(ub, wa_ref[d], preferred_element_type=F32) + ba_ref[d:d + 1, :])
        i = jax.nn.sigmoid(jnp.dot(ub, wx_ref[d], preferred_element_type=F32) + bx_ref[d:d + 1, :])
        nl = -lam_ref[d:d + 1, :]
        softplus = jnp.maximum(nl, 0.0) + jnp.log1p(jnp.exp(-jnp.abs(nl)))
        log_a = -LRU_C * r * softplus
        a_ref[d] = jnp.exp(log_a)
        th = jnp.tanh(log_a)
        b_ref[d] = jnp.sqrt(-2.0 * th / (1.0 - th)) * (i * u)


def _rec_mid(xp, xr, conv_w, conv_b, wa_bd, ba, wx_bd, bx, lam, pool_w, pool_scale, dims):
    B, S, L = dims
    T = xp.shape[0]
    n_chunks = T // SEQ_T

    def cur(g):
        return (g, 0)

    halos_per_chunk = SEQ_T // HALO

    def prv(g):
        return (jnp.maximum(g * halos_per_chunk - 1, 0), 0)

    def nxt(g):
        return (jnp.minimum((g + 1) * halos_per_chunk, n_chunks * halos_per_chunk - 1), 0)

    tile = lambda idx: pl.BlockSpec((SEQ_T, LRU_WIDTH), idx)
    halo = lambda idx: pl.BlockSpec((HALO, LRU_WIDTH), idx)
    full = lambda arr: pl.BlockSpec(arr.shape, lambda g: (0,) * arr.ndim)
    kern = functools.partial(_rec_mid_kernel, n_lat_chunks=B * S // SEQ_T, lat_chunks_per_seq=S // SEQ_T,
                             ctx_chunks_per_seq=L // SEQ_T, lat_len=S, ctx_len=L)
    return pl.pallas_call(
        kern,
        grid=(n_chunks,),
        in_specs=[halo(prv), tile(cur), halo(nxt), halo(prv), tile(cur), halo(nxt),
                  full(conv_w), full(conv_b), full(wa_bd), full(ba), full(wx_bd), full(bx), full(lam),
                  full(pool_w), full(pool_scale)],
        out_specs=[pl.BlockSpec((SEQ_T, POOL_WIDTH), cur),
                   pl.BlockSpec((2, SEQ_T, LRU_WIDTH), lambda g: (0, g, 0)),
                   pl.BlockSpec((2, SEQ_T, LRU_WIDTH), lambda g: (0, g, 0))],
        out_shape=[jax.ShapeDtypeStruct((T, POOL_WIDTH), BF16),
                   jax.ShapeDtypeStruct((2, T, LRU_WIDTH), F32),
                   jax.ShapeDtypeStruct((2, T, LRU_WIDTH), F32)],
        compiler_params=_cparams(("parallel",)),
        name="rec_mid",
    )(xp, xp, xp, xr, xr, xr, conv_w, conv_b, wa_bd, ba, wx_bd, bx, lam, pool_w, pool_scale)


def _scan_kernel(af_ref, bf_ref, ar_ref, br_ref, hf_ref, hr_ref, carry_ref):
    s = pl.program_id(1)

    @pl.when(s == 0)
    def _():
        carry_ref[...] = jnp.zeros_like(carry_ref)

    def body(i, carry):
        hf, hr = carry
        t = SEQ_T - 1 - i
        hf = af_ref[pl.ds(i, 1), :] * hf + bf_ref[pl.ds(i, 1), :]
        hr = ar_ref[pl.ds(t, 1), :] * hr + br_ref[pl.ds(t, 1), :]
        hf_ref[pl.ds(i, 1), :] = hf
        hr_ref[pl.ds(t, 1), :] = hr
        return hf, hr

    hf, hr = lax.fori_loop(0, SEQ_T, body, (carry_ref[0], carry_ref[1]), unroll=8)
    carry_ref[0] = hf
    carry_ref[1] = hr


def _scan(a, b, dims):
    B, S, L = dims
    T = a.shape[1]
    lat = S // SEQ_T
    ctx = L // SEQ_T
    lat_base = 0
    ctx_base = B * S // SEQ_T

    def fwd(bi, s):
        return jnp.where(s < ctx, ctx_base + bi * ctx + s, lat_base + bi * lat + (s - ctx))

    def rev(bi, s):
        return jnp.where(s < ctx, ctx_base + bi * ctx + (ctx - 1 - s), lat_base + bi * lat + (lat - 1 - (s - ctx)))

    in_f = pl.BlockSpec((None, SEQ_T, LRU_WIDTH), lambda bi, s: (0, fwd(bi, s), 0))
    in_r = pl.BlockSpec((None, SEQ_T, LRU_WIDTH), lambda bi, s: (1, rev(bi, s), 0))
    out_f = pl.BlockSpec((SEQ_T, LRU_WIDTH), lambda bi, s: (fwd(bi, s), 0))
    out_r = pl.BlockSpec((SEQ_T, LRU_WIDTH), lambda bi, s: (rev(bi, s), 0))
    out = jax.ShapeDtypeStruct((T, LRU_WIDTH), F32)
    return pl.pallas_call(
        _scan_kernel,
        grid=(B, lat + ctx),
        in_specs=[in_f, in_f, in_r, in_r],
        out_specs=[out_f, out_r],
        out_shape=[out, out],
        scratch_shapes=[pltpu.VMEM((2, 1, LRU_WIDTH), F32)],
        compiler_params=_cparams(("parallel", "arbitrary")),
        name="lru_scan",
    )(a, b, a, b)


def _post_common(y, x_ref, mod_ref, nw_ref, xo_ref, h2_ref):
    x_new = x_ref[...] + mod_ref[2:3, :] * y
    xo_ref[...] = x_new
    h2_ref[...] = _rmsnorm_mod(x_new, nw_ref[...], mod_ref[3:4, :], mod_ref[4:5, :]).astype(BF16)


def _post_attn_kernel(a_ref, b_ref, x_ref, mod_ref, nw_ref, w_ref, xo_ref, h2_ref):
    half = a_ref.shape[1]
    y = (jnp.dot(a_ref[...], w_ref[:half, :], preferred_element_type=F32)
         + jnp.dot(b_ref[...], w_ref[half:, :], preferred_element_type=F32))
    _post_common(y, x_ref, mod_ref, nw_ref, xo_ref, h2_ref)


def _post_rec_kernel(pool_ref, hf_ref, hr_ref, g_ref, x_ref, mod_ref, nw_ref, w_ref, xo_ref, h2_ref):
    half = pool_ref.shape[1]
    rec = ((hf_ref[...] + hr_ref[...]) * _gelu(g_ref[...])).astype(BF16)
    y = (jnp.dot(pool_ref[...], w_ref[:half, :], preferred_element_type=F32)
         + jnp.dot(rec, w_ref[half:, :], preferred_element_type=F32))
    _post_common(y, x_ref, mod_ref, nw_ref, xo_ref, h2_ref)


def _post_mixer(kind, parts, x, mods_l, norm_w, w_out, dims, n_tiles):
    B, S, L = dims
    T = x.shape[0]
    n_lat = B * S // TM
    s_tiles = S // TM

    def mod_idx(g):
        return (jnp.where(g < n_lat, g // s_tiles, B), 0, 0)

    row = lambda w: pl.BlockSpec((TM, w), lambda g: (g, 0))
    if kind == "attn":
        kern = _post_attn_kernel
        part_specs = [row(SWA_WIDTH), row(DIFF_WIDTH)]
    else:
        kern = _post_rec_kernel
        part_specs = [row(POOL_WIDTH), row(LRU_WIDTH), row(LRU_WIDTH), row(LRU_WIDTH)]
    n_in = len(parts)
    return pl.pallas_call(
        kern,
        grid=(n_tiles,),
        in_specs=part_specs + [
            row(D_MODEL),
            pl.BlockSpec((None, N_MOD, D_MODEL), mod_idx),
            pl.BlockSpec((1, D_MODEL), lambda g: (0, 0)),
            pl.BlockSpec(w_out.shape, lambda g: (0, 0)),
        ],
        out_specs=[row(D_MODEL), row(D_MODEL)],
        out_shape=[jax.ShapeDtypeStruct((T, D_MODEL), F32), jax.ShapeDtypeStruct((T, D_MODEL), BF16)],
        input_output_aliases={n_in: 0},
        compiler_params=_cparams(("parallel",)),
        name="post_" + kind,
    )(*parts, x, mods_l, norm_w, w_out)


def _peer_cand_blocks():
    return [(i, PEER_TOPK // (i + 1)) for i in range(1, SUBLANES)]


PEER_TAG_BITS = 31


def _peer_tagged(e, rank):
    bits = pltpu.bitcast(e, jnp.int32)
    tag = PEER_TAG_BITS - rank.astype(jnp.int32)
    return pltpu.bitcast((bits & ~PEER_TAG_BITS) | tag, F32)


def _peer_routing(h2_ref, wq_ref, sk_ref, thr_ref, c_ref, e2_ref,
                  q_scr, s_scr, cur_scr, rank_scr, top_scr, cand_scr):
    neg_inf = -jnp.inf
    n_cand = cand_scr.shape[0]

    def extract_step(k, src_ref, dst_ref, rank_ref=None):
        cur = src_ref[...]
        m = jnp.max(cur, axis=0, keepdims=True)
        dst_ref[pl.ds(k, 1), :] = m
        hit = cur == m
        src_ref[...] = jnp.where(hit, neg_inf, cur)
        if rank_ref is not None:
            rank_ref[...] = jnp.where(hit, lax.convert_element_type(k + 1, F32), rank_ref[...])

    def head_body(h, _):
        q0 = pl.multiple_of(h * PEER_QDIM, PEER_QDIM)
        q_scr[...] = _nt_dot(wq_ref[pl.ds(q0, PEER_QDIM), :], h2_ref[...])
        for p in range(2):
            s = jnp.dot(sk_ref[h * 2 + p].astype(BF16), q_scr[p * PEER_HALF:(p + 1) * PEER_HALF, :].astype(BF16),
                        preferred_element_type=F32)
            s_scr[p] = s
            cur_scr[p] = s
        rank_scr[...] = jnp.full(rank_scr.shape, PEER_TOPK + 1.0, F32)

        def top_body(k, _):
            extract_step(k, cur_scr.at[0], top_scr.at[0])
            extract_step(k, cur_scr.at[1], top_scr.at[1], rank_scr)
            return 0

        lax.fori_loop(0, PEER_TOPK, top_body, 0, unroll=PEER_LOOP_UNROLL)
        v1 = top_scr[0]
        v2 = top_scr[1]
        cand_scr[0:PEER_TOPK, :] = v1[0:1, :] + v2
        row = lax.broadcasted_iota(jnp.int32, (SUBLANES, 1), 0)
        for i, n_i in _peer_cand_blocks():
            blk = jnp.where(row < n_i, v1[i:i + 1, :] + v2[0:SUBLANES, :], neg_inf)
            cand_scr[PEER_TOPK + (i - 1) * SUBLANES: PEER_TOPK + i * SUBLANES, :] = blk
        cand_scr[n_cand - SUBLANES:, :] = v1[SUBLANES:, :] + v2[0:1, :]
        cand = cand_scr[...]

        def cand_body(k, _):
            extract_step(k, cand_scr, top_scr.at[2])
            return 0

        lax.fori_loop(0, PEER_TOPK, cand_body, 0, unroll=PEER_LOOP_UNROLL)
        tau = top_scr[2, PEER_TOPK - 1:PEER_TOPK, :]
        top = v1[0:1, :] + v2[0:1, :]
        z = jnp.sum(jnp.where(cand >= tau, jnp.exp(cand - top), 0.0), axis=0, keepdims=True)

        s1 = s_scr[0]
        ranks = (lax.broadcasted_iota(jnp.int32, (PEER_TOPK, 1), 0) + 1).astype(F32)
        top_scr[2] = _peer_tagged(jnp.exp(v2 - v2[0:1, :]), ranks)
        thr_ref[h] = jnp.full(s1.shape, jnp.inf, F32)

        def thr_body(jj, _):
            hit = s1 + top_scr[1, pl.ds(jj, 1), :] >= tau
            thr_ref[h] = jnp.where(hit, top_scr[2, pl.ds(jj, 1), :], thr_ref[h])
            return 0

        lax.fori_loop(0, PEER_TOPK, thr_body, 0, unroll=PEER_LOOP_UNROLL)
        c_ref[h] = jnp.exp(s1 - v1[0:1, :]) / z
        e2_ref[h] = _peer_tagged(jnp.exp(s_scr[1] - v2[0:1, :]), rank_scr[...])
        return 0

    lax.fori_loop(0, PEER_HEADS, head_body, 0)


def _peer_kernel(h2_ref, u_ref, vt_ref, wq_ref, sk_ref, x_ref, mod_ref, fw_ref, xo_ref,
                 h2t_scr, coef_scr, acc_scr, thr_scr, c_scr, e2_scr,
                 q_scr, s_scr, cur_scr, rank_scr, top_scr, cand_scr, *, final):
    j = pl.program_id(1)
    tt = h2_ref.shape[0]
    pair = PEER_UP_GROUP * PEER_NKEYS

    @pl.when(j == 0)
    def _():
        acc_scr[...] = jnp.zeros_like(acc_scr)
        h2t_scr[...] = h2_ref[...].T
        _peer_routing(h2_ref, wq_ref, sk_ref, thr_scr, c_scr, e2_scr,
                      q_scr, s_scr, cur_scr, rank_scr, top_scr, cand_scr)

    for pr in range(PEER_NA // PEER_UP_GROUP):
        act = jnp.dot(u_ref[pr * pair:(pr + 1) * pair, :], h2t_scr[...], preferred_element_type=F32)
        for half in range(PEER_UP_GROUP):
            al = PEER_UP_GROUP * pr + half
            a = j * PEER_NA + al
            thr_full = [thr_scr[h, pl.ds(a, 1), :] for h in range(PEER_HEADS)]
            c_full = [c_scr[h, pl.ds(a, 1), :] for h in range(PEER_HEADS)]
            for tc in range(tt // LANES):
                ls = slice(tc * LANES, (tc + 1) * LANES)
                thr_rows = [r[:, ls] for r in thr_full]
                c_rows = [r[:, ls] for r in c_full]
                for rb in range(PEER_NKEYS // PEER_RB):
                    rs = slice(rb * PEER_RB, (rb + 1) * PEER_RB)
                    w = jnp.zeros((PEER_RB, LANES), F32)
                    for h in range(PEER_HEADS):
                        e2_t = e2_scr[h, rs, ls]
                        w = w + jnp.where(e2_t >= thr_rows[h], c_rows[h] * e2_t, 0.0)
                    rows = slice(half * PEER_NKEYS + rb * PEER_RB, half * PEER_NKEYS + (rb + 1) * PEER_RB)
                    r0 = al * PEER_NKEYS + rb * PEER_RB
                    coef_scr[r0:r0 + PEER_RB, ls] = (w * _gelu(act[rows, ls])).astype(BF16)

    acc_scr[...] += jnp.dot(vt_ref[...], coef_scr[...], preferred_element_type=F32)

    @pl.when(j == pl.num_programs(1) - 1)
    def _():
        x_new = x_ref[...] + mod_ref[5:6, :] * acc_scr[...].T
        if final:
            x_new = x_new * lax.rsqrt(jnp.mean(x_new * x_new, axis=-1, keepdims=True) + EPS) * fw_ref[...]
        xo_ref[...] = x_new


def _peer(h2, u, v_t, wq_t, subkeys, x, mods_l, final_w, dims, n_tiles, final):
    B, S, L = dims
    T = x.shape[0]
    tt = PEER_TT
    ne = PEER_NA * PEER_NKEYS
    n_lat = B * S // tt
    s_tiles = S // tt
    n_cand = PEER_TOPK + SUBLANES * SUBLANES

    def mod_idx(i, j):
        return (jnp.where(i < n_lat, i // s_tiles, B), 0, 0)

    key_f32 = pltpu.VMEM((PEER_HEADS, PEER_NKEYS, tt), F32)
    const = pl.Buffered(1)
    return pl.pallas_call(
        functools.partial(_peer_kernel, final=final),
        grid=(n_tiles, PEER_EXPERTS // ne),
        in_specs=[
            pl.BlockSpec((tt, D_MODEL), lambda i, j: (i, 0)),
            pl.BlockSpec((ne, D_MODEL), lambda i, j: (j, 0)),
            pl.BlockSpec((None, D_MODEL, ne), lambda i, j: (j, 0, 0)),
            pl.BlockSpec(wq_t.shape, lambda i, j: (0, 0), pipeline_mode=const),
            pl.BlockSpec(subkeys.shape, lambda i, j: (0, 0, 0), pipeline_mode=const),
            pl.BlockSpec((tt, D_MODEL), lambda i, j: (i, 0)),
            pl.BlockSpec((None, N_MOD, D_MODEL), mod_idx),
            pl.BlockSpec((1, D_MODEL), lambda i, j: (0, 0)),
        ],
        out_specs=pl.BlockSpec((tt, D_MODEL), lambda i, j: (i, 0)),
        out_shape=jax.ShapeDtypeStruct((n_tiles * tt if final else T, D_MODEL), F32),
        scratch_shapes=[
            pltpu.VMEM((D_MODEL, tt), BF16),
            pltpu.VMEM((ne, tt), BF16),
            pltpu.VMEM((D_MODEL, tt), F32),
            key_f32, key_f32, key_f32,
            pltpu.VMEM((PEER_QDIM, tt), F32),
            pltpu.VMEM((2, PEER_NKEYS, tt), F32),
            pltpu.VMEM((2, PEER_NKEYS, tt), F32),
            pltpu.VMEM((PEER_NKEYS, tt), F32),
            pltpu.VMEM((3, PEER_TOPK, tt), F32),
            pltpu.VMEM((n_cand, tt), F32),
        ],
        input_output_aliases={} if final else {5: 0},
        compiler_params=_cparams(("parallel", "arbitrary")),
        name="peer",
    )(h2, u, v_t, wq_t, subkeys, x, mods_l, final_w)


def _lambda_init(layer):
    return 0.8 - 0.6 * math.exp(-0.3 * layer)


def _rope_tables(S):
    rows = S // GRID_W
    row = jnp.repeat(jnp.arange(rows), GRID_W).astype(F32)
    col = jnp.tile(jnp.arange(GRID_W), rows).astype(F32)
    inv = ROPE_THETA ** (-jnp.arange(ROPE_AXIS_FREQS, dtype=F32) / ROPE_AXIS_FREQS)
    ang = jnp.concatenate([row[:, None] * inv, col[:, None] * inv], axis=-1)
    cos, sin = jnp.cos(ang), jnp.sin(ang)
    cos_t = jnp.tile(cos, (1, LANES // ROPE_HALF))
    sin_t = jnp.tile(jnp.concatenate([-sin, sin], axis=-1), (1, LANES // HEAD_DIM))
    cos_t = jnp.concatenate([cos_t, jnp.ones((TM, LANES), F32)], axis=0)
    sin_t = jnp.concatenate([sin_t, jnp.zeros((TM, LANES), F32)], axis=0)
    return cos_t, sin_t


def _block_diag(w):
    nd, nb, bd, _ = w.shape
    eye = jnp.eye(nb, dtype=w.dtype)
    return jnp.einsum("dnij,nm->dnimj", w, eye).reshape(nd, nb * bd, nb * bd)


def kernel(x, c, ctx, c_ctx, w_mod, b_mod, norm_mix, norm_ffn, w_out, attn_w_in, swa_sink, diff_lambda, diff_subln, rec_w_in, pool_w, pool_scale, lru_conv_w, lru_conv_b, lru_wa, lru_ba, lru_wx, lru_bx, lru_lambda, peer_wq, peer_subkeys, peer_u, peer_v, final_norm):
    B, S, D = x.shape
    L = ctx.shape[1]
    depth = w_mod.shape[0]
    dims = (B, S, L)
    assert D == D_MODEL and S % PEER_TT == 0 and (B * L) % PEER_TT == 0 and L % SEQ_T == 0
    assert TM % SEQ_T == 0 and S % TM == 0 and (B * L) % TM == 0
    assert S % GRID_W == 0 and S >= 3 * BLOCK and B + 1 <= SUBLANES and S % (DIFF_GROUP * SEQ_T) == 0
    lat_rows = B * S
    T = lat_rows + B * L

    xs = jnp.concatenate([x.reshape(lat_rows, D), ctx.reshape(B * L, D)], axis=0)
    cc = jnp.zeros((SUBLANES, D), F32).at[:B].set(c).at[B].set(c_ctx)
    mods = _modulation(cc, w_mod, b_mod)
    cos_t, sin_t = _rope_tables(S)

    for l in range(depth):
        jl = l // 2
        ctx_out = l < depth - 1
        n_rows = T if ctx_out else lat_rows
        mods_l = mods[l]
        if l % 2 == 0:
            q, k, dq, dk, vt, dvt = _pre_attn(xs, mods_l, norm_mix[l][None], attn_w_in[jl].astype(BF16),
                                              cos_t, sin_t, dims)
            a = _swa(q, k, vt, swa_sink[jl], dims)
            bd = _diff_attn(dq, dk, dvt, diff_lambda[jl], diff_subln[jl][None], _lambda_init(l), dims)
            xs, h2 = _post_mixer("attn", (a, bd), xs, mods_l, norm_ffn[l][None], w_out[l].astype(BF16),
                                 dims, n_rows // TM)
        else:
            xp, xr, g = _pre_rec(xs, mods_l, norm_mix[l][None], rec_w_in[jl].astype(BF16), dims)
            pool, a_co, b_co = _rec_mid(xp, xr, lru_conv_w[jl], lru_conv_b[jl][None],
                                        _block_diag(lru_wa[jl]).astype(BF16), lru_ba[jl],
                                        _block_diag(lru_wx[jl]).astype(BF16), lru_bx[jl], lru_lambda[jl],
                                        pool_w[jl].astype(BF16), pool_scale[jl][None], dims)
            h_fwd, h_rev = _scan(a_co, b_co, dims)
            xs, h2 = _post_mixer("rec", (pool, h_fwd, h_rev, g), xs, mods_l, norm_ffn[l][None], w_out[l].astype(BF16),
                                 dims, n_rows // TM)
        n_peer = n_rows // PEER_TT
        wq_t = peer_wq[l].astype(BF16).T
        sk = peer_subkeys[l].reshape(PEER_HEADS * 2, PEER_NKEYS, PEER_HALF)
        ne = PEER_NA * PEER_NKEYS
        v_t = peer_v[l].astype(BF16).reshape(PEER_EXPERTS // ne, ne, D).transpose(0, 2, 1)
        xs = _peer(h2, peer_u[l].astype(BF16), v_t, wq_t, sk, xs, mods_l, final_norm[None], dims, n_peer,
                   final=not ctx_out)

    return xs.reshape(B, S, D)
```

```python
import functools
import math

import jax
import jax.numpy as jnp
from jax import lax
from jax.experimental import pallas as pl
from jax.experimental.pallas import tpu as pltpu

F32 = jnp.float32
BF16 = jnp.bfloat16
HIGHEST = lax.Precision.HIGHEST

D_MODEL = 1024
HEAD_DIM = 64
EPS = 1e-6
NEG = -1e30
N_MOD = 6
SCALE = HEAD_DIM ** -0.5
ROPE_HALF = HEAD_DIM // 2
ROPE_AXIS_FREQS = ROPE_HALF // 2
ROPE_THETA = 10000.0
GRID_W = 64
SWA_Q_HEADS = 8
SWA_KV_HEADS = 2
SWA_GROUP = SWA_Q_HEADS // SWA_KV_HEADS
WINDOW = 128
BLOCK = 128
SWA_WIDTH = SWA_Q_HEADS * HEAD_DIM
SWA_KV_WIDTH = SWA_KV_HEADS * HEAD_DIM
DIFF_HEADS = 4
DIFF_V_DIM = 2 * HEAD_DIM
DIFF_QK_WIDTH = DIFF_HEADS * 2 * HEAD_DIM
DIFF_WIDTH = DIFF_HEADS * DIFF_V_DIM
POOL_WINDOWS = (2, 4, 8, 16)
POOL_GROUPS = 4
POOL_WIDTH = D_MODEL // 2
POOL_GDIM = POOL_WIDTH // POOL_GROUPS
LRU_WIDTH = D_MODEL // 2
CONV_W = 4
CONV_LEFT = CONV_W // 2
LRU_C = 8.0
PEER_HEADS = 8
PEER_NKEYS = 128
PEER_EXPERTS = PEER_NKEYS * PEER_NKEYS
PEER_QDIM = 256
PEER_HALF = PEER_QDIM // 2
PEER_TOPK = 16

LANES = 128
SUBLANES = 8
VMEM_LIMIT = 56 * 1024 * 1024

TM = 1024
SEQ_T = 256
DIFF_GROUP = 32
DIFF_AHEAD = 2
DIFF_SUM_ROWS = 16
HALO = 8
PEER_TT = 512
PEER_NA = 16
PEER_UP_GROUP = 2
PEER_RB = 64
PEER_LOOP_UNROLL = 4


def _cparams(sem):
    return pltpu.CompilerParams(dimension_semantics=sem, vmem_limit_bytes=VMEM_LIMIT)


def _nt_dot(a, b):
    return lax.dot_general(a, b, (((1,), (1,)), ((), ())), preferred_element_type=F32)


def _rmsnorm_mod(x, w, shift, scale):
    y = x * lax.rsqrt(jnp.mean(x * x, axis=-1, keepdims=True) + EPS) * w
    return y * (1.0 + scale) + shift


def _gelu(x):
    return 0.5 * x * (1.0 + lax.erf(x * (2.0 ** -0.5)))


def _mod_kernel(cc_ref, w_ref, b_ref, o_ref):
    cc = cc_ref[...]
    sc = cc * jax.nn.sigmoid(cc)
    o_ref[...] = jnp.dot(sc, w_ref[...], precision=HIGHEST, preferred_element_type=F32) + b_ref[...]


def _modulation(cc, w_mod, b_mod):
    depth = w_mod.shape[0]
    rows = cc.shape[0]
    out = pl.pallas_call(
        _mod_kernel,
        grid=(depth, N_MOD),
        in_specs=[
            pl.BlockSpec((rows, D_MODEL), lambda l, j: (0, 0)),
            pl.BlockSpec((None, D_MODEL, D_MODEL), lambda l, j: (l, 0, j)),
            pl.BlockSpec((None, 1, D_MODEL), lambda l, j: (l, 0, j)),
        ],
        out_specs=pl.BlockSpec((None, rows, D_MODEL), lambda l, j: (l, 0, j)),
        out_shape=jax.ShapeDtypeStruct((depth, rows, N_MOD * D_MODEL), F32),
        compiler_params=_cparams(("parallel", "parallel")),
        name="modulation",
    )(cc, w_mod, b_mod.reshape(depth, 1, N_MOD * D_MODEL))
    return out.reshape(depth, rows, N_MOD, D_MODEL)


def _rope128(x, cos, sin_signed, first_half):
    partner = jnp.where(first_half, pltpu.roll(x, LANES - ROPE_HALF, 1), pltpu.roll(x, ROPE_HALF, 1))
    return x * cos + partner * sin_signed


def _pre_attn_kernel(x_ref, mod_ref, nw_ref, w_ref, cos_ref, sin_ref,
                     q_ref, k_ref, dq_ref, dk_ref, v_ref, dv_ref):
    h = _rmsnorm_mod(x_ref[...], nw_ref[...], mod_ref[0:1, :], mod_ref[1:2, :])
    p = jnp.dot(h.astype(BF16), w_ref[...], preferred_element_type=F32)
    cos = cos_ref[...]
    sin = sin_ref[...]
    lane = lax.broadcasted_iota(jnp.int32, (1, LANES), 1)
    first_half = (lane % HEAD_DIM) < ROPE_HALF

    def roped(lo, width, scale):
        outs = []
        for c in range(width // LANES):
            xc = p[:, lo + c * LANES: lo + (c + 1) * LANES]
            outs.append((_rope128(xc, cos, sin, first_half) * scale).astype(BF16))
        return outs

    o1 = SWA_WIDTH
    o2 = o1 + SWA_KV_WIDTH
    o3 = o2 + SWA_KV_WIDTH
    o4 = o3 + DIFF_QK_WIDTH
    o5 = o4 + DIFF_QK_WIDTH
    for c, val in enumerate(roped(0, SWA_WIDTH, SCALE)):
        q_ref[:, c * LANES:(c + 1) * LANES] = val
    for c, val in enumerate(roped(o1, SWA_KV_WIDTH, 1.0)):
        k_ref[:, c * LANES:(c + 1) * LANES] = val
    for c in range(TM // SEQ_T):
        v_ref[c] = p[c * SEQ_T:(c + 1) * SEQ_T, o2:o3].T.astype(BF16)
        dv_ref[c] = p[c * SEQ_T:(c + 1) * SEQ_T, o5:].T.astype(BF16)
    for c, val in enumerate(roped(o3, DIFF_QK_WIDTH, SCALE)):
        dq_ref[:, c * LANES:(c + 1) * LANES] = val
    for c, val in enumerate(roped(o4, DIFF_QK_WIDTH, 1.0)):
        dk_ref[:, c * LANES:(c + 1) * LANES] = val


def _pre_attn(x, mods_l, norm_w, w_in, cos_t, sin_t, dims):
    B, S, L = dims
    T = x.shape[0]
    n_lat = B * S // TM
    s_tiles = S // TM

    def mod_idx(g):
        return (jnp.where(g < n_lat, g // s_tiles, B), 0, 0)

    def rope_idx(g):
        return (jnp.where(g < n_lat, g % s_tiles, s_tiles), 0)

    widths = (SWA_WIDTH, SWA_KV_WIDTH, DIFF_QK_WIDTH, DIFF_QK_WIDTH)
    t_widths = (SWA_KV_WIDTH, DIFF_WIDTH)
    return pl.pallas_call(
        _pre_attn_kernel,
        grid=(T // TM,),
        in_specs=[
            pl.BlockSpec((TM, D_MODEL), lambda g: (g, 0)),
            pl.BlockSpec((None, N_MOD, D_MODEL), mod_idx),
            pl.BlockSpec((1, D_MODEL), lambda g: (0, 0)),
            pl.BlockSpec(w_in.shape, lambda g: (0, 0)),
            pl.BlockSpec((TM, LANES), rope_idx),
            pl.BlockSpec((TM, LANES), rope_idx),
        ],
        out_specs=[pl.BlockSpec((TM, w), lambda g: (g, 0)) for w in widths]
        + [pl.BlockSpec((TM // SEQ_T, w, SEQ_T), lambda g: (g, 0, 0)) for w in t_widths],
        out_shape=[jax.ShapeDtypeStruct((T, w), BF16) for w in widths]
        + [jax.ShapeDtypeStruct((T // SEQ_T, w, SEQ_T), BF16) for w in t_widths],
        compiler_params=_cparams(("parallel",)),
        name="pre_attn",
    )(x, mods_l, norm_w, w_in, cos_t, sin_t)


def _swa_kernel(q_ref, kl_ref, vtl_ref, kc_ref, vtc_ref, sink_ref, o_ref, *, n_lat_blocks, n_lat_chunks,
                n_ctx_chunks):
    j = pl.program_id(1)
    is_ctx = j >= n_lat_blocks
    jl = jnp.minimum(j, n_lat_blocks - 1)
    band_chunks = 2
    per_chunk = SEQ_T // BLOCK
    c0 = jnp.clip((jl - 1) // per_chunk, 0, n_lat_chunks - band_chunks)
    r0 = pl.multiple_of(c0 * SEQ_T, SEQ_T)
    kb = kl_ref[pl.ds(r0, band_chunks * SEQ_T), :]
    kc = kc_ref[...]
    glanes = SWA_GROUP * BLOCK
    kpos = r0 + lax.broadcasted_iota(jnp.int32, (band_chunks * SEQ_T, 1), 0)
    qpos = jl * BLOCK + lax.broadcasted_iota(jnp.int32, (1, glanes), 1) % BLOCK
    valid = jnp.logical_and(jnp.abs(qpos - kpos) <= WINDOW, jnp.logical_not(is_ctx))
    q = q_ref[...]
    for hk in range(SWA_KV_HEADS):
        ks = slice(hk * HEAD_DIM, (hk + 1) * HEAD_DIM)
        hs = [hk * SWA_GROUP + g for g in range(SWA_GROUP)]
        qg = jnp.concatenate([q[:, h * HEAD_DIM:(h + 1) * HEAD_DIM] for h in hs], axis=0)
        sink = jnp.concatenate([jnp.full((1, BLOCK), sink_ref[h], F32) for h in hs], axis=1)
        sl = jnp.where(valid, _nt_dot(kb[:, ks], qg), NEG)
        sc = _nt_dot(kc[:, ks], qg)
        m = jnp.maximum(jnp.maximum(jnp.max(sl, axis=0, keepdims=True), jnp.max(sc, axis=0, keepdims=True)), sink)
        pl_ = jnp.exp(sl - m)
        pc = jnp.exp(sc - m)
        den = jnp.sum(pl_, axis=0, keepdims=True) + jnp.sum(pc, axis=0, keepdims=True) + jnp.exp(sink - m)
        pl_b = pl_.astype(BF16)
        pc_b = pc.astype(BF16)
        o = jnp.zeros((HEAD_DIM, glanes), F32)
        for i in range(band_chunks):
            o = o + jnp.dot(vtl_ref[c0 + i, ks, :], pl_b[i * SEQ_T:(i + 1) * SEQ_T], preferred_element_type=F32)
        for i in range(n_ctx_chunks):
            o = o + jnp.dot(vtc_ref[i, ks, :], pc_b[i * SEQ_T:(i + 1) * SEQ_T], preferred_element_type=F32)
        o = o / den
        for g, h in enumerate(hs):
            o_ref[:, h * HEAD_DIM:(h + 1) * HEAD_DIM] = o[:, g * BLOCK:(g + 1) * BLOCK].T.astype(BF16)


def _swa(q, k, vt, sink, dims):
    B, S, L = dims
    T = q.shape[0]
    n_lat_blocks = S // BLOCK
    n_ctx_blocks = L // BLOCK
    n_lat_chunks = S // SEQ_T
    n_ctx_chunks = L // SEQ_T
    lat_rows = B * S

    def q_idx(b, j):
        return (jnp.where(j < n_lat_blocks, b * n_lat_blocks + j,
                          lat_rows // BLOCK + b * n_ctx_blocks + (j - n_lat_blocks)), 0)

    return pl.pallas_call(
        functools.partial(_swa_kernel, n_lat_blocks=n_lat_blocks, n_lat_chunks=n_lat_chunks,
                          n_ctx_chunks=n_ctx_chunks),
        grid=(B, n_lat_blocks + n_ctx_blocks),
        in_specs=[
            pl.BlockSpec((BLOCK, SWA_WIDTH), q_idx),
            pl.BlockSpec((S, SWA_KV_WIDTH), lambda b, j: (b, 0)),
            pl.BlockSpec((n_lat_chunks, SWA_KV_WIDTH, SEQ_T), lambda b, j: (b, 0, 0)),
            pl.BlockSpec((L, SWA_KV_WIDTH), lambda b, j: (lat_rows // L + b, 0)),
            pl.BlockSpec((n_ctx_chunks, SWA_KV_WIDTH, SEQ_T), lambda b, j: (lat_rows // L + b, 0, 0)),
            pl.BlockSpec(memory_space=pltpu.SMEM),
        ],
        out_specs=pl.BlockSpec((BLOCK, SWA_WIDTH), q_idx),
        out_shape=jax.ShapeDtypeStruct((T, SWA_WIDTH), BF16),
        compiler_params=_cparams(("parallel", "arbitrary")),
        name="swa",
    )(q, k, vt, k, vt, sink)


def _diff_kernel(q_ref, kl_ref, vtl_ref, kc_ref, vtc_ref, lam_ref, subln_ref, o_ref, qp_scr, m_scr, acc_scr, *,
                 n_lat_tiles, n_lat_chunks, n_ctx_chunks, lam_init):
    j = pl.program_id(1)
    lv = lam_ref[...]
    lam = (jnp.exp(jnp.sum(lv[0:1] * lv[1:2], axis=-1, keepdims=True))
           - jnp.exp(jnp.sum(lv[2:3] * lv[3:4], axis=-1, keepdims=True)) + lam_init)
    lane = lax.broadcasted_iota(jnp.int32, (1, DIFF_V_DIM), 1)
    heads = [slice(h * DIFF_V_DIM, (h + 1) * DIFF_V_DIM) for h in range(DIFF_HEADS)]
    ones = jnp.ones((DIFF_SUM_ROWS, SEQ_T), BF16)

    for h, vs in enumerate(heads):
        qh = q_ref[:, vs]
        zero = jnp.zeros_like(qh)
        qp_scr[2 * h] = jnp.where(lane < HEAD_DIM, qh, zero)
        qp_scr[2 * h + 1] = jnp.where(lane >= HEAD_DIM, qh, zero)
    m_scr[...] = jnp.full(m_scr.shape, -jnp.inf, F32)
    acc_scr[...] = jnp.zeros(acc_scr.shape, F32)

    def update(blocks_of_head):
        all_blocks = [blocks_of_head(vs) for vs in heads]

        def scores(ch):
            return [_nt_dot(kblk, qp_scr[ch]) for kblk, _ in all_blocks[ch // 2]]

        n_chains = 2 * DIFF_HEADS
        queue = [scores(ch) for ch in range(DIFF_AHEAD)]
        for h, vs in enumerate(heads):
            blocks = all_blocks[h]
            for m in range(2):
                ch = 2 * h + m
                ss = queue.pop(0)
                if ch + DIFF_AHEAD < n_chains:
                    queue.append(scores(ch + DIFF_AHEAD))
                m_old = m_scr[ch]
                m_blk = jnp.max(ss[0], axis=0, keepdims=True)
                for s in ss[1:]:
                    m_blk = jnp.maximum(m_blk, jnp.max(s, axis=0, keepdims=True))
                m_new = jnp.maximum(m_old, m_blk.astype(BF16).astype(F32))
                m_b = m_new.astype(BF16)
                alpha = jnp.exp(m_old - m_new)
                acc = alpha * acc_scr[ch]
                for s, (_, vtblk) in zip(ss, blocks):
                    p = jnp.exp(s.astype(BF16) - m_b)
                    acc = acc + jnp.dot(jnp.concatenate([vtblk, ones], axis=0), p, preferred_element_type=F32)
                m_scr[ch] = m_new
                acc_scr[ch] = acc

    update(lambda vs: [(kc_ref[c * SEQ_T:(c + 1) * SEQ_T, vs], vtc_ref[c, vs, :]) for c in range(n_ctx_chunks)])

    @pl.when(j < n_lat_tiles)
    def _():
        def body(g, _):
            def blocks_of_head(vs):
                blocks = []
                for i in range(DIFF_GROUP):
                    c = g * DIFF_GROUP + i
                    r0 = pl.multiple_of(c * SEQ_T, SEQ_T)
                    blocks.append((kl_ref[pl.ds(r0, SEQ_T), vs], vtl_ref[c, vs, :]))
                return blocks

            update(blocks_of_head)
            return 0

        lax.fori_loop(0, n_lat_chunks // DIFF_GROUP, body, 0)

    def normalised(ch):
        acc = acc_scr[ch]
        return acc[:DIFF_V_DIM] / acc[DIFF_V_DIM:DIFF_V_DIM + 1]

    for h, vs in enumerate(heads):
        o = (normalised(2 * h) - lam * normalised(2 * h + 1)).T
        y = o * lax.rsqrt(jnp.mean(o * o, axis=-1, keepdims=True) + EPS) * subln_ref[...]
        o_ref[:, vs] = (y * (1.0 - lam_init)).astype(BF16)


def _diff_attn(dq, dk, dvt, lam_vecs, subln, lam_init, dims):
    B, S, L = dims
    T = dq.shape[0]
    n_lat_tiles = S // SEQ_T
    n_ctx_tiles = L // SEQ_T
    lat_rows = B * S

    def q_idx(b, j):
        return (jnp.where(j < n_lat_tiles, b * n_lat_tiles + j,
                          lat_rows // SEQ_T + b * n_ctx_tiles + (j - n_lat_tiles)), 0)

    return pl.pallas_call(
        functools.partial(_diff_kernel, n_lat_tiles=n_lat_tiles, n_lat_chunks=n_lat_tiles,
                          n_ctx_chunks=n_ctx_tiles, lam_init=lam_init),
        grid=(B, n_lat_tiles + n_ctx_tiles),
        in_specs=[
            pl.BlockSpec((SEQ_T, DIFF_QK_WIDTH), q_idx),
            pl.BlockSpec((S, DIFF_QK_WIDTH), lambda b, j: (b, 0), pipeline_mode=pl.Buffered(1)),
            pl.BlockSpec((n_lat_tiles, DIFF_WIDTH, SEQ_T), lambda b, j: (b, 0, 0), pipeline_mode=pl.Buffered(1)),
            pl.BlockSpec((L, DIFF_QK_WIDTH), lambda b, j: (lat_rows // L + b, 0)),
            pl.BlockSpec((n_ctx_tiles, DIFF_WIDTH, SEQ_T), lambda b, j: (lat_rows // L + b, 0, 0)),
            pl.BlockSpec(lam_vecs.shape, lambda b, j: (0, 0)),
            pl.BlockSpec((1, DIFF_V_DIM), lambda b, j: (0, 0)),
        ],
        out_specs=pl.BlockSpec((SEQ_T, DIFF_WIDTH), q_idx),
        out_shape=jax.ShapeDtypeStruct((T, DIFF_WIDTH), BF16),
        scratch_shapes=[
            pltpu.VMEM((2 * DIFF_HEADS, SEQ_T, DIFF_V_DIM), BF16),
            pltpu.VMEM((2 * DIFF_HEADS, 1, SEQ_T), F32),
            pltpu.VMEM((2 * DIFF_HEADS, DIFF_V_DIM + DIFF_SUM_ROWS, SEQ_T), F32),
        ],
        compiler_params=_cparams(("parallel", "arbitrary")),
        name="diff_attn",
    )(dq, dk, dvt, dk, dvt, lam_vecs, subln)


def _pre_rec_kernel(x_ref, mod_ref, nw_ref, w_ref, xp_ref, xr_ref, g_ref):
    h = _rmsnorm_mod(x_ref[...], nw_ref[...], mod_ref[0:1, :], mod_ref[1:2, :])
    p = jnp.dot(h.astype(BF16), w_ref[...], preferred_element_type=F32)
    xp_ref[...] = p[:, :POOL_WIDTH]
    xr_ref[...] = p[:, POOL_WIDTH:POOL_WIDTH + LRU_WIDTH]
    g_ref[...] = p[:, POOL_WIDTH + LRU_WIDTH:]


def _pre_rec(x, mods_l, norm_w, w_in, dims):
    B, S, L = dims
    T = x.shape[0]
    n_lat = B * S // TM
    s_tiles = S // TM

    def mod_idx(g):
        return (jnp.where(g < n_lat, g // s_tiles, B), 0, 0)

    widths = (POOL_WIDTH, LRU_WIDTH, LRU_WIDTH)
    return pl.pallas_call(
        _pre_rec_kernel,
        grid=(T // TM,),
        in_specs=[
            pl.BlockSpec((TM, D_MODEL), lambda g: (g, 0)),
            pl.BlockSpec((None, N_MOD, D_MODEL), mod_idx),
            pl.BlockSpec((1, D_MODEL), lambda g: (0, 0)),
            pl.BlockSpec(w_in.shape, lambda g: (0, 0)),
        ],
        out_specs=[pl.BlockSpec((TM, w), lambda g: (g, 0)) for w in widths],
        out_shape=[jax.ShapeDtypeStruct((T, w), F32) for w in widths],
        compiler_params=_cparams(("parallel",)),
        name="pre_rec",
    )(x, mods_l, norm_w, w_in)


def _rec_mid_kernel(xp_p, xp_c, xp_n, xr_p, xr_c, xr_n, cw_ref, cb_ref, wa_ref, ba_ref, wx_ref, bx_ref,
                    lam_ref, pw_ref, ps_ref, pool_ref, a_ref, b_ref, *, n_lat_chunks, lat_chunks_per_seq,
                    ctx_chunks_per_seq, lat_len, ctx_len):
    g = pl.program_id(0)
    is_lat = g < n_lat_chunks
    cps = jnp.where(is_lat, lat_chunks_per_seq, ctx_chunks_per_seq)
    within = jnp.where(is_lat, g % lat_chunks_per_seq, (g - n_lat_chunks) % ctx_chunks_per_seq)
    has_prev = within > 0
    has_next = within < cps - 1
    seg_len = jnp.where(is_lat, lat_len, ctx_len)
    ext_rows = SEQ_T + 2 * HALO

    def extended(prev_ref, cur_ref, next_ref):
        prev = jnp.where(has_prev, prev_ref[...], 0.0)
        nxt = jnp.where(has_next, next_ref[...], 0.0)
        return jnp.concatenate([prev, cur_ref[...], nxt], axis=0)

    def shifted(ext, off):
        return pltpu.roll(ext, (-off) % ext_rows, 0)[HALO:HALO + SEQ_T, :]

    xp_ext = extended(xp_p, xp_c, xp_n)
    t = within * SEQ_T + lax.broadcasted_iota(jnp.int32, (SEQ_T, 1), 0)
    for gi, w in enumerate(POOL_WINDOWS):
        cols = slice(gi * POOL_GDIM, (gi + 1) * POOL_GDIM)
        eg = xp_ext[:, cols]
        tot = shifted(eg, -(w // 2))
        for off in range(-(w // 2) + 1, w - w // 2):
            tot = tot + shifted(eg, off)
        lo = jnp.clip(t - w // 2, 0, seg_len)
        hi = jnp.clip(t - w // 2 + w, 0, seg_len)
        cnt = (hi - lo).astype(F32)
        d = tot / cnt - xp_c[:, cols]
        y = jnp.dot(d.astype(BF16), pw_ref[gi], preferred_element_type=F32)
        pool_ref[:, cols] = (y * ps_ref[:, cols]).astype(BF16)

    xr_ext = extended(xr_p, xr_c, xr_n)
    u = cb_ref[...] + cw_ref[0:1, :] * shifted(xr_ext, -CONV_LEFT)
    for k in range(1, CONV_W):
        u = u + cw_ref[k:k + 1, :] * shifted(xr_ext, k - CONV_LEFT)
    ub = u.astype(BF16)
    for d in range(2):
        r = jax.nn.sigmoid(jnp.dot(ub, wa_ref[d], preferred_element_type=F32) + ba_ref[d:d + 1, :])
        i = jax.nn.sigmoid(jnp.dot(ub, wx_ref[d], preferred_element_type=F32) + bx_ref[d:d + 1, :])
        nl = -lam_ref[d:d + 1, :]
        softplus = jnp.maximum(nl, 0.0) + jnp.log1p(jnp.exp(-jnp.abs(nl)))
        log_a = -LRU_C * r * softplus
        a_ref[d] = jnp.exp(log_a)
        th = jnp.tanh(log_a)
        b_ref[d] = jnp.sqrt(-2.0 * th / (1.0 - th)) * (i * u)


def _rec_mid(xp, xr, conv_w, conv_b, wa_bd, ba, wx_bd, bx, lam, pool_w, pool_scale, dims):
    B, S, L = dims
    T = xp.shape[0]
    n_chunks = T // SEQ_T

    def cur(g):
        return (g, 0)

    halos_per_chunk = SEQ_T // HALO

    def prv(g):
        return (jnp.maximum(g * halos_per_chunk - 1, 0), 0)

    def nxt(g):
        return (jnp.minimum((g + 1) * halos_per_chunk, n_chunks * halos_per_chunk - 1), 0)

    tile = lambda idx: pl.BlockSpec((SEQ_T, LRU_WIDTH), idx)
    halo = lambda idx: pl.BlockSpec((HALO, LRU_WIDTH), idx)
    full = lambda arr: pl.BlockSpec(arr.shape, lambda g: (0,) * arr.ndim)
    kern = functools.partial(_rec_mid_kernel, n_lat_chunks=B * S // SEQ_T, lat_chunks_per_seq=S // SEQ_T,
                             ctx_chunks_per_seq=L // SEQ_T, lat_len=S, ctx_len=L)
    return pl.pallas_call(
        kern,
        grid=(n_chunks,),
        in_specs=[halo(prv), tile(cur), halo(nxt), halo(prv), tile(cur), halo(nxt),
                  full(conv_w), full(conv_b), full(wa_bd), full(ba), full(wx_bd), full(bx), full(lam),
                  full(pool_w), full(pool_scale)],
        out_specs=[pl.BlockSpec((SEQ_T, POOL_WIDTH), cur),
                   pl.BlockSpec((2, SEQ_T, LRU_WIDTH), lambda g: (0, g, 0)),
                   pl.BlockSpec((2, SEQ_T, LRU_WIDTH), lambda g: (0, g, 0))],
        out_shape=[jax.ShapeDtypeStruct((T, POOL_WIDTH), BF16),
                   jax.ShapeDtypeStruct((2, T, LRU_WIDTH), F32),
                   jax.ShapeDtypeStruct((2, T, LRU_WIDTH), F32)],
        compiler_params=_cparams(("parallel",)),
        name="rec_mid",
    )(xp, xp, xp, xr, xr, xr, conv_w, conv_b, wa_bd, ba, wx_bd, bx, lam, pool_w, pool_scale)


def _scan_kernel(af_ref, bf_ref, ar_ref, br_ref, hf_ref, hr_ref, carry_ref):
    s = pl.program_id(1)

    @pl.when(s == 0)
    def _():
        carry_ref[...] = jnp.zeros_like(carry_ref)

    def body(i, carry):
        hf, hr = carry
        t = SEQ_T - 1 - i
        hf = af_ref[pl.ds(i, 1), :] * hf + bf_ref[pl.ds(i, 1), :]
        hr = ar_ref[pl.ds(t, 1), :] * hr + br_ref[pl.ds(t, 1), :]
        hf_ref[pl.ds(i, 1), :] = hf
        hr_ref[pl.ds(t, 1), :] = hr
        return hf, hr

    hf, hr = lax.fori_loop(0, SEQ_T, body, (carry_ref[0], carry_ref[1]), unroll=8)
    carry_ref[0] = hf
    carry_ref[1] = hr


def _scan(a, b, dims):
    B, S, L = dims
    T = a.shape[1]
    lat = S // SEQ_T
    ctx = L // SEQ_T
    lat_base = 0
    ctx_base = B * S // SEQ_T

    def fwd(bi, s):
        return jnp.where(s < ctx, ctx_base + bi * ctx + s, lat_base + bi * lat + (s - ctx))

    def rev(bi, s):
        return jnp.where(s < ctx, ctx_base + bi * ctx + (ctx - 1 - s), lat_base + bi * lat + (lat - 1 - (s - ctx)))

    in_f = pl.BlockSpec((None, SEQ_T, LRU_WIDTH), lambda bi, s: (0, fwd(bi, s), 0))
    in_r = pl.BlockSpec((None, SEQ_T, LRU_WIDTH), lambda bi, s: (1, rev(bi, s), 0))
    out_f = pl.BlockSpec((SEQ_T, LRU_WIDTH), lambda bi, s: (fwd(bi, s), 0))
    out_r = pl.BlockSpec((SEQ_T, LRU_WIDTH), lambda bi, s: (rev(bi, s), 0))
    out = jax.ShapeDtypeStruct((T, LRU_WIDTH), F32)
    return pl.pallas_call(
        _scan_kernel,
        grid=(B, lat + ctx),
        in_specs=[in_f, in_f, in_r, in_r],
        out_specs=[out_f, out_r],
        out_shape=[out, out],
        scratch_shapes=[pltpu.VMEM((2, 1, LRU_WIDTH), F32)],
        compiler_params=_cparams(("parallel", "arbitrary")),
        name="lru_scan",
    )(a, b, a, b)


def _post_common(y, x_ref, mod_ref, nw_ref, xo_ref, h2_ref):
    x_new = x_ref[...] + mod_ref[2:3, :] * y
    xo_ref[...] = x_new
    h2_ref[...] = _rmsnorm_mod(x_new, nw_ref[...], mod_ref[3:4, :], mod_ref[4:5, :]).astype(BF16)


def _post_attn_kernel(a_ref, b_ref, x_ref, mod_ref, nw_ref, w_ref, xo_ref, h2_ref):
    half = a_ref.shape[1]
    y = (jnp.dot(a_ref[...], w_ref[:half, :], preferred_element_type=F32)
         + jnp.dot(b_ref[...], w_ref[half:, :], preferred_element_type=F32))
    _post_common(y, x_ref, mod_ref, nw_ref, xo_ref, h2_ref)


def _post_rec_kernel(pool_ref, hf_ref, hr_ref, g_ref, x_ref, mod_ref, nw_ref, w_ref, xo_ref, h2_ref):
    half = pool_ref.shape[1]
    rec = ((hf_ref[...] + hr_ref[...]) * _gelu(g_ref[...])).astype(BF16)
    y = (jnp.dot(pool_ref[...], w_ref[:half, :], preferred_element_type=F32)
         + jnp.dot(rec, w_ref[half:, :], preferred_element_type=F32))
    _post_common(y, x_ref, mod_ref, nw_ref, xo_ref, h2_ref)


def _post_mixer(kind, parts, x, mods_l, norm_w, w_out, dims, n_tiles):
    B, S, L = dims
    T = x.shape[0]
    n_lat = B * S // TM
    s_tiles = S // TM

    def mod_idx(g):
        return (jnp.where(g < n_lat, g // s_tiles, B), 0, 0)

    row = lambda w: pl.BlockSpec((TM, w), lambda g: (g, 0))
    if kind == "attn":
        kern = _post_attn_kernel
        part_specs = [row(SWA_WIDTH), row(DIFF_WIDTH)]
    else:
        kern = _post_rec_kernel
        part_specs = [row(POOL_WIDTH), row(LRU_WIDTH), row(LRU_WIDTH), row(LRU_WIDTH)]
    n_in = len(parts)
    return pl.pallas_call(
        kern,
        grid=(n_tiles,),
        in_specs=part_specs + [
            row(D_MODEL),
            pl.BlockSpec((None, N_MOD, D_MODEL), mod_idx),
            pl.BlockSpec((1, D_MODEL), lambda g: (0, 0)),
            pl.BlockSpec(w_out.shape, lambda g: (0, 0)),
        ],
        out_specs=[row(D_MODEL), row(D_MODEL)],
        out_shape=[jax.ShapeDtypeStruct((T, D_MODEL), F32), jax.ShapeDtypeStruct((T, D_MODEL), BF16)],
        input_output_aliases={n_in: 0},
        compiler_params=_cparams(("parallel",)),
        name="post_" + kind,
    )(*parts, x, mods_l, norm_w, w_out)


def _peer_cand_blocks():
    return [(i, PEER_TOPK // (i + 1)) for i in range(1, SUBLANES)]


PEER_TAG_BITS = 31


def _peer_tagged(e, rank):
    bits = pltpu.bitcast(e, jnp.int32)
    tag = PEER_TAG_BITS - rank.astype(jnp.int32)
    return pltpu.bitcast((bits & ~PEER_TAG_BITS) | tag, F32)


def _peer_routing(h2t_ref, wq_ref, sk_ref, thr_ref, c_ref, e2_ref,
                  q_scr, s_scr, cur_scr, rank_scr, top_scr, cand_scr):
    neg_inf = -jnp.inf
    n_cand = cand_scr.shape[0]

    def extract_step(k, src_ref, dst_ref, rank_ref=None):
        cur = src_ref[...]
        m = jnp.max(cur, axis=0, keepdims=True)
        dst_ref[pl.ds(k, 1), :] = m
        hit = cur == m
        src_ref[...] = jnp.where(hit, neg_inf, cur)
        if rank_ref is not None:
            rank_ref[...] = jnp.where(hit, lax.convert_element_type(k + 1, F32), rank_ref[...])

    def head_body(h, _):
        q0 = pl.multiple_of(h * PEER_QDIM, PEER_QDIM)
        q_scr[...] = jnp.dot(wq_ref[pl.ds(q0, PEER_QDIM), :], h2t_ref[...],
                             preferred_element_type=F32)
        for p in range(2):
            s = jnp.dot(sk_ref[h * 2 + p].astype(BF16), q_scr[p * PEER_HALF:(p + 1) * PEER_HALF, :].astype(BF16),
                        preferred_element_type=F32)
            s_scr[p] = s
            cur_scr[p] = s
        rank_scr[...] = jnp.full(rank_scr.shape, PEER_TOPK + 1.0, F32)

        def top_body(k, _):
            extract_step(k, cur_scr.at[0], top_scr.at[0])
            extract_step(k, cur_scr.at[1], top_scr.at[1], rank_scr)
            return 0

        lax.fori_loop(0, PEER_TOPK, top_body, 0, unroll=PEER_LOOP_UNROLL)
        v1 = top_scr[0]
        v2 = top_scr[1]
        cand_scr[0:PEER_TOPK, :] = v1[0:1, :] + v2
        row = lax.broadcasted_iota(jnp.int32, (SUBLANES, 1), 0)
        for i, n_i in _peer_cand_blocks():
            blk = jnp.where(row < n_i, v1[i:i + 1, :] + v2[0:SUBLANES, :], neg_inf)
            cand_scr[PEER_TOPK + (i - 1) * SUBLANES: PEER_TOPK + i * SUBLANES, :] = blk
        cand_scr[n_cand - SUBLANES:, :] = v1[SUBLANES:, :] + v2[0:1, :]
        cand = cand_scr[...]

        def cand_body(k, _):
            extract_step(k, cand_scr, top_scr.at[2])
            return 0

        lax.fori_loop(0, PEER_TOPK, cand_body, 0, unroll=PEER_LOOP_UNROLL)
        tau = top_scr[2, PEER_TOPK - 1:PEER_TOPK, :]
        top = v1[0:1, :] + v2[0:1, :]
        z = jnp.sum(jnp.where(cand >= tau, jnp.exp(cand - top), 0.0), axis=0, keepdims=True)

        s1 = s_scr[0]
        ranks = (lax.broadcasted_iota(jnp.int32, (PEER_TOPK, 1), 0) + 1).astype(F32)
        top_scr[2] = _peer_tagged(jnp.exp(v2 - v2[0:1, :]), ranks)
        thr_ref[h] = jnp.full(s1.shape, jnp.inf, F32)

        def thr_body(jj, _):
            hit = s1 + top_scr[1, pl.ds(jj, 1), :] >= tau
            thr_ref[h] = jnp.where(hit, top_scr[2, pl.ds(jj, 1), :], thr_ref[h])
            return 0

        lax.fori_loop(0, PEER_TOPK, thr_body, 0, unroll=PEER_LOOP_UNROLL)
        c_ref[h] = jnp.exp(s1 - v1[0:1, :]) / z
        e2_ref[h] = _peer_tagged(jnp.exp(s_scr[1] - v2[0:1, :]), rank_scr[...])
        return 0

    lax.fori_loop(0, PEER_HEADS, head_body, 0)


def _peer_kernel(h2_ref, u_ref, vt_ref, wq_ref, sk_ref, x_ref, mod_ref, fw_ref, xo_ref,
                 h2t_scr, coef_scr, acc_scr, thr_scr, c_scr, e2_scr,
                 q_scr, s_scr, cur_scr, rank_scr, top_scr, cand_scr, *, final):
    j = pl.program_id(1)
    tt = h2_ref.shape[0]
    pair = PEER_UP_GROUP * PEER_NKEYS

    @pl.when(j == 0)
    def _():
        acc_scr[...] = jnp.zeros_like(acc_scr)
        h2t_scr[...] = h2_ref[...].T
        _peer_routing(h2t_scr, wq_ref, sk_ref, thr_scr, c_scr, e2_scr,
                      q_scr, s_scr, cur_scr, rank_scr, top_scr, cand_scr)

    for pr in range(PEER_NA // PEER_UP_GROUP):
        act = jnp.dot(u_ref[pr * pair:(pr + 1) * pair, :], h2t_scr[...], preferred_element_type=F32)
        for half in range(PEER_UP_GROUP):
            al = PEER_UP_GROUP * pr + half
            a = j * PEER_NA + al
            thr_full = [thr_scr[h, pl.ds(a, 1), :] for h in range(PEER_HEADS)]
            c_full = [c_scr[h, pl.ds(a, 1), :] for h in range(PEER_HEADS)]
            for tc in range(tt // LANES):
                ls = slice(tc * LANES, (tc + 1) * LANES)
                thr_rows = [r[:, ls] for r in thr_full]
                c_rows = [r[:, ls] for r in c_full]
                for rb in range(PEER_NKEYS // PEER_RB):
                    rs = slice(rb * PEER_RB, (rb + 1) * PEER_RB)
                    w = jnp.zeros((PEER_RB, LANES), F32)
                    for h in range(PEER_HEADS):
                        e2_t = e2_scr[h, rs, ls]
                        w = w + jnp.where(e2_t >= thr_rows[h], c_rows[h] * e2_t, 0.0)
                    rows = slice(half * PEER_NKEYS + rb * PEER_RB, half * PEER_NKEYS + (rb + 1) * PEER_RB)
                    r0 = al * PEER_NKEYS + rb * PEER_RB
                    coef_scr[r0:r0 + PEER_RB, ls] = (w * _gelu(act[rows, ls])).astype(BF16)

    acc_scr[...] += jnp.dot(vt_ref[...], coef_scr[...], preferred_element_type=F32)

    @pl.when(j == pl.num_programs(1) - 1)
    def _():
        x_new = x_ref[...] + mod_ref[5:6, :] * acc_scr[...].T
        if final:
            x_new = x_new * lax.rsqrt(jnp.mean(x_new * x_new, axis=-1, keepdims=True) + EPS) * fw_ref[...]
        xo_ref[...] = x_new


def _peer(h2, u, v_t, wq_t, subkeys, x, mods_l, final_w, dims, n_tiles, final):
    B, S, L = dims
    T = x.shape[0]
    tt = PEER_TT
    ne = PEER_NA * PEER_NKEYS
    n_lat = B * S // tt
    s_tiles = S // tt
    n_cand = PEER_TOPK + SUBLANES * SUBLANES

    def mod_idx(i, j):
        return (jnp.where(i < n_lat, i // s_tiles, B), 0, 0)

    key_f32 = pltpu.VMEM((PEER_HEADS, PEER_NKEYS, tt), F32)
    const = pl.Buffered(1)
    return pl.pallas_call(
        functools.partial(_peer_kernel, final=final),
        grid=(n_tiles, PEER_EXPERTS // ne),
        in_specs=[
            pl.BlockSpec((tt, D_MODEL), lambda i, j: (i, 0)),
            pl.BlockSpec((ne, D_MODEL), lambda i, j: (j, 0)),
            pl.BlockSpec((None, D_MODEL, ne), lambda i, j: (j, 0, 0)),
            pl.BlockSpec(wq_t.shape, lambda i, j: (0, 0), pipeline_mode=const),
            pl.BlockSpec(subkeys.shape, lambda i, j: (0, 0, 0), pipeline_mode=const),
            pl.BlockSpec((tt, D_MODEL), lambda i, j: (i, 0)),
            pl.BlockSpec((None, N_MOD, D_MODEL), mod_idx),
            pl.BlockSpec((1, D_MODEL), lambda i, j: (0, 0)),
        ],
        out_specs=pl.BlockSpec((tt, D_MODEL), lambda i, j: (i, 0)),
        out_shape=jax.ShapeDtypeStruct((n_tiles * tt if final else T, D_MODEL), F32),
        scratch_shapes=[
            pltpu.VMEM((D_MODEL, tt), BF16),
            pltpu.VMEM((ne, tt), BF16),
            pltpu.VMEM((D_MODEL, tt), F32),
            key_f32, key_f32, key_f32,
            pltpu.VMEM((PEER_QDIM, tt), F32),
            pltpu.VMEM((2, PEER_NKEYS, tt), F32),
            pltpu.VMEM((2, PEER_NKEYS, tt), F32),
            pltpu.VMEM((PEER_NKEYS, tt), F32),
            pltpu.VMEM((3, PEER_TOPK, tt), F32),
            pltpu.VMEM((n_cand, tt), F32),
        ],
        input_output_aliases={} if final else {5: 0},
        compiler_params=_cparams(("parallel", "arbitrary")),
        name="peer",
    )(h2, u, v_t, wq_t, subkeys, x, mods_l, final_w)


def _lambda_init(layer):
    return 0.8 - 0.6 * math.exp(-0.3 * layer)


def _rope_tables(S):
    rows = S // GRID_W
    row = jnp.repeat(jnp.arange(rows), GRID_W).astype(F32)
    col = jnp.tile(jnp.arange(GRID_W), rows).astype(F32)
    inv = ROPE_THETA ** (-jnp.arange(ROPE_AXIS_FREQS, dtype=F32) / ROPE_AXIS_FREQS)
    ang = jnp.concatenate([row[:, None] * inv, col[:, None] * inv], axis=-1)
    cos, sin = jnp.cos(ang), jnp.sin(ang)
    cos_t = jnp.tile(cos, (1, LANES // ROPE_HALF))
    sin_t = jnp.tile(jnp.concatenate([-sin, sin], axis=-1), (1, LANES // HEAD_DIM))
    cos_t = jnp.concatenate([cos_t, jnp.ones((TM, LANES), F32)], axis=0)
    sin_t = jnp.concatenate([sin_t, jnp.zeros((TM, LANES), F32)], axis=0)
    return cos_t, sin_t


def _block_diag(w):
    nd, nb, bd, _ = w.shape
    eye = jnp.eye(nb, dtype=w.dtype)
    return jnp.einsum("dnij,nm->dnimj", w, eye).reshape(nd, nb * bd, nb * bd)


def kernel(x, c, ctx, c_ctx, w_mod, b_mod, norm_mix, norm_ffn, w_out, attn_w_in, swa_sink, diff_lambda, diff_subln, rec_w_in, pool_w, pool_scale, lru_conv_w, lru_conv_b, lru_wa, lru_ba, lru_wx, lru_bx, lru_lambda, peer_wq, peer_subkeys, peer_u, peer_v, final_norm):
    B, S, D = x.shape
    L = ctx.shape[1]
    depth = w_mod.shape[0]
    dims = (B, S, L)
    assert D == D_MODEL and S % PEER_TT == 0 and (B * L) % PEER_TT == 0 and L % SEQ_T == 0
    assert TM % SEQ_T == 0 and S % TM == 0 and (B * L) % TM == 0
    assert S % GRID_W == 0 and S >= 3 * BLOCK and B + 1 <= SUBLANES and S % (DIFF_GROUP * SEQ_T) == 0
    lat_rows = B * S
    T = lat_rows + B * L

    xs = jnp.concatenate([x.reshape(lat_rows, D), ctx.reshape(B * L, D)], axis=0)
    cc = jnp.zeros((SUBLANES, D), F32).at[:B].set(c).at[B].set(c_ctx)
    mods = _modulation(cc, w_mod, b_mod)
    cos_t, sin_t = _rope_tables(S)

    for l in range(depth):
        jl = l // 2
        ctx_out = l < depth - 1
        n_rows = T if ctx_out else lat_rows
        mods_l = mods[l]
        if l % 2 == 0:
            q, k, dq, dk, vt, dvt = _pre_attn(xs, mods_l, norm_mix[l][None], attn_w_in[jl].astype(BF16),
                                              cos_t, sin_t, dims)
            a = _swa(q, k, vt, swa_sink[jl], dims)
            bd = _diff_attn(dq, dk, dvt, diff_lambda[jl], diff_subln[jl][None], _lambda_init(l), dims)
            xs, h2 = _post_mixer("attn", (a, bd), xs, mods_l, norm_ffn[l][None], w_out[l].astype(BF16),
                                 dims, n_rows // TM)
        else:
            xp, xr, g = _pre_rec(xs, mods_l, norm_mix[l][None], rec_w_in[jl].astype(BF16), dims)
            pool, a_co, b_co = _rec_mid(xp, xr, lru_conv_w[jl], lru_conv_b[jl][None],
                                        _block_diag(lru_wa[jl]).astype(BF16), lru_ba[jl],
                                        _block_diag(lru_wx[jl]).astype(BF16), lru_bx[jl], lru_lambda[jl],
                                        pool_w[jl].astype(BF16), pool_scale[jl][None], dims)
            h_fwd, h_rev = _scan(a_co, b_co, dims)
            xs, h2 = _post_mixer("rec", (pool, h_fwd, h_rev, g), xs, mods_l, norm_ffn[l][None], w_out[l].astype(BF16),
                                 dims, n_rows // TM)
        n_peer = n_rows // PEER_TT
        wq_t = peer_wq[l].astype(BF16).T
        sk = peer_subkeys[l].reshape(PEER_HEADS * 2, PEER_NKEYS, PEER_HALF)
        ne = PEER_NA * PEER_NKEYS
        v_t = peer_v[l].astype(BF16).reshape(PEER_EXPERTS // ne, ne, D).transpose(0, 2, 1)
        xs = _peer(h2, peer_u[l].astype(BF16), v_t, wq_t, sk, xs, mods_l, final_norm[None], dims, n_peer,
                   final=not ctx_out)

    return xs.reshape(B, S, D)
```
